```python
import jax, jax.numpy as jnp
from jax import lax
import numpy as np

D_MODEL = 1024
BATCH = 8
SEQ = 8192
DEPTH = 4

CHUNK = 64
EPS = 1e-6
N_BRANCH = 2
CONV_DIM = D_MODEL
CONV_WIDTH = 3
SSM_EXPAND = 2
D_SSM = SSM_EXPAND * D_MODEL
SSM_HEAD_DIM = 64
SSM_HEADS = D_SSM // SSM_HEAD_DIM
SSM_GROUPS = 8
SSM_STATE = 128
SSM_CONV_WIDTH = 4
SSM_CONV_DIM = D_SSM + 2 * SSM_GROUPS * SSM_STATE
DT_MIN = 1e-3
DT_MAX = 1e-1
D_FF = 4 * D_MODEL
N_MOD = 6

kernel_name = "hybrid_shortconv_ssd_gated_trunk"


def proj_sizes():
    return (N_BRANCH * D_MODEL,
            CONV_DIM, CONV_DIM, CONV_DIM,
            D_SSM,
            SSM_CONV_DIM,
            SSM_HEADS)


def rms_norm(x, w):
    xf = x.astype(jnp.float32)
    y = xf * lax.rsqrt(jnp.mean(xf * xf, axis=-1, keepdims=True) + EPS)
    return (y * w.astype(jnp.float32)).astype(x.dtype)


def causal_depthwise_conv(x, w):
    k = w.shape[0]
    return lax.conv_general_dilated(
        x, w[:, None, :].astype(x.dtype), window_strides=(1,), padding=((k - 1, 0),),
        dimension_numbers=('NWC', 'WIO', 'NWC'), feature_group_count=x.shape[-1])


def ssd_scan(x, a, b, c):
    bsz, seqlen, h, p = x.shape
    g, n = b.shape[-2:]
    r = h // g
    nc = seqlen // CHUNK

    def to_chunks(t):
        return jnp.moveaxis(t.reshape(bsz, nc, CHUNK, *t.shape[2:]), 1, 0)

    xs = (to_chunks(x.reshape(bsz, seqlen, g, r, p)),
          to_chunks(a.reshape(bsz, seqlen, g, r)),
          to_chunks(b), to_chunks(c))
    tril = jnp.tril(jnp.ones((CHUNK, CHUNK), dtype=bool))[None, :, :, None, None]

    def step(state, inp):
        xq, aq, bq, cq = inp
        a_cum = jnp.cumsum(aq, axis=1)
        seg = a_cum[:, :, None] - a_cum[:, None, :]
        decay = jnp.exp(jnp.where(tril, seg, -jnp.inf))
        scores = jnp.einsum('btgn,bsgn->btsg', cq, bq)
        y_diag = jnp.einsum('btsg,btsgr,bsgrp->btgrp', scores, decay, xq)
        y_off = jnp.einsum('btgn,bgrpn->btgrp', cq, state) * jnp.exp(a_cum)[..., None]
        to_end = jnp.exp(a_cum[:, -1:] - a_cum)
        new_state = (state * jnp.exp(a_cum[:, -1])[..., None, None]
                     + jnp.einsum('bsgn,bsgr,bsgrp->bgrpn', bq, to_end, xq))
        return new_state, y_diag + y_off

    state0 = jnp.zeros((bsz, g, r, p, n), jnp.float32)
    _, y = lax.scan(step, state0, xs)
    return jnp.moveaxis(y, 0, 1).reshape(bsz, seqlen, h, p)


def mixer_sublayer(u, w_in, conv_w, ssm_conv_w, ssm_conv_b, dt_bias, a_log, d_skip,
                   ssm_norm_w, w_conv_out, w_ssm_out, w_o):
    bsz, seqlen, _ = u.shape
    proj = u @ w_in
    split_at = [int(v) for v in np.cumsum(proj_sizes())[:-1]]
    gl, cb, cc, cx, z, xbc, dt = jnp.split(proj, split_at, axis=-1)

    y_conv = cb * causal_depthwise_conv(cc * cx, conv_w)
    p_conv = y_conv @ w_conv_out

    xbc = jax.nn.silu(causal_depthwise_conv(xbc, ssm_conv_w) + ssm_conv_b.astype(xbc.dtype))
    xbc = xbc.astype(jnp.float32)
    xs, bs, cs = jnp.split(xbc, [D_SSM, D_SSM + SSM_GROUPS * SSM_STATE], axis=-1)
    xs = xs.reshape(bsz, seqlen, SSM_HEADS, SSM_HEAD_DIM)
    bs = bs.reshape(bsz, seqlen, SSM_GROUPS, SSM_STATE)
    cs = cs.reshape(bsz, seqlen, SSM_GROUPS, SSM_STATE)
    dt = jax.nn.softplus(dt.astype(jnp.float32) + dt_bias.astype(jnp.float32))
    a = -jnp.exp(a_log.astype(jnp.float32))
    y = ssd_scan(xs * dt[..., None], dt * a, bs, cs)
    y = y + d_skip.astype(jnp.float32)[:, None] * xs
    y = y.reshape(bsz, seqlen, D_SSM) * jax.nn.silu(z.astype(jnp.float32))
    yg = y.reshape(bsz, seqlen, SSM_GROUPS, D_SSM // SSM_GROUPS)
    yg = yg * lax.rsqrt(jnp.mean(yg * yg, axis=-1, keepdims=True) + EPS)
    y = (yg.reshape(bsz, seqlen, D_SSM) * ssm_norm_w.astype(jnp.float32)).astype(u.dtype)
    p_ssm = y @ w_ssm_out

    g_conv, g_ssm = jnp.split(jax.nn.sigmoid(gl), 2, axis=-1)
    merged = g_conv * p_conv + g_ssm * p_ssm
    return merged @ w_o


def _fwd_setup_inputs(seed: int = 0) -> dict:
    key = jax.random.key(seed)
    ks = jax.random.split(key, 24)
    d_proj = sum(proj_sizes())
    f32 = jnp.float32

    def nrm(k, shape, scale):
        return jax.random.normal(k, shape, f32) * scale

    dt0 = jnp.exp(jax.random.uniform(ks[10], (DEPTH, SSM_HEADS), f32,
                                     np.log(DT_MIN), np.log(DT_MAX)))
    dt_bias = dt0 + jnp.log(-jnp.expm1(-dt0))
    return {
        "x": nrm(ks[0], (BATCH, SEQ, D_MODEL), 1.0),
        "c": nrm(ks[1], (BATCH, D_MODEL), 1.0),
        "w_ada": nrm(ks[2], (DEPTH, D_MODEL, N_MOD * D_MODEL), 0.5 * D_MODEL ** -0.5),
        "b_ada": nrm(ks[3], (DEPTH, N_MOD * D_MODEL), 0.02),
        "ln1": 1.0 + nrm(ks[4], (DEPTH, D_MODEL), 0.1),
        "ln2": 1.0 + nrm(ks[5], (DEPTH, D_MODEL), 0.1),
        "w_in": nrm(ks[6], (DEPTH, D_MODEL, d_proj), D_MODEL ** -0.5),
        "conv_w": nrm(ks[7], (DEPTH, CONV_WIDTH, CONV_DIM), CONV_WIDTH ** -0.5),
        "ssm_conv_w": nrm(ks[8], (DEPTH, SSM_CONV_WIDTH, SSM_CONV_DIM), SSM_CONV_WIDTH ** -0.5),
        "ssm_conv_b": nrm(ks[9], (DEPTH, SSM_CONV_DIM), 0.01),
        "dt_bias": dt_bias,
        "a_log": jnp.log(jax.random.uniform(ks[11], (DEPTH, SSM_HEADS), f32, 1.0, 16.0)),
        "d_skip": 1.0 + nrm(ks[12], (DEPTH, SSM_HEADS), 0.1),
        "ssm_norm_w": 1.0 + nrm(ks[13], (DEPTH, D_SSM), 0.1),
        "w_conv_out": nrm(ks[14], (DEPTH, CONV_DIM, D_MODEL), CONV_DIM ** -0.5),
        "w_ssm_out": nrm(ks[15], (DEPTH, D_SSM, D_MODEL), D_SSM ** -0.5),
        "w_o": nrm(ks[16], (DEPTH, D_MODEL, D_MODEL), D_MODEL ** -0.5),
        "w_up": nrm(ks[17], (DEPTH, D_MODEL, D_FF), D_MODEL ** -0.5),
        "w_down": nrm(ks[18], (DEPTH, D_FF, D_MODEL), D_FF ** -0.5),
        "final_norm": 1.0 + nrm(ks[19], (D_MODEL,), 0.1),
    }


def _fwd_reference(x, c, w_ada, b_ada, ln1, ln2, w_in, conv_w, ssm_conv_w, ssm_conv_b,
              dt_bias, a_log, d_skip, ssm_norm_w, w_conv_out, w_ssm_out, w_o,
              w_up, w_down, final_norm):
    bsz = x.shape[0]
    c_act = jax.nn.silu(c)
    for i in range(DEPTH):
        mod = (c_act @ w_ada[i] + b_ada[i]).reshape(bsz, N_MOD, D_MODEL)[:, :, None, :]
        shift1, scale1, gate1 = mod[:, 0], mod[:, 1], mod[:, 2]
        shift2, scale2, gate2 = mod[:, 3], mod[:, 4], mod[:, 5]

        u = rms_norm(x, ln1[i]) * (1.0 + scale1) + shift1
        mix = mixer_sublayer(u, w_in[i], conv_w[i], ssm_conv_w[i], ssm_conv_b[i], dt_bias[i],
                             a_log[i], d_skip[i], ssm_norm_w[i], w_conv_out[i],
                             w_ssm_out[i], w_o[i])
        x = x + gate1 * mix

        u2 = rms_norm(x, ln2[i]) * (1.0 + scale2) + shift2
        hid = jnp.square(jax.nn.relu(u2 @ w_up[i]))
        x = x + gate2 * (hid @ w_down[i])
    return rms_norm(x, final_norm)


import jax as _jax
import jax.numpy as _jnp

TWIN_FORMAT = 'train_step'
FWD_PARAMS = ['x', 'c', 'w_ada', 'b_ada', 'ln1', 'ln2', 'w_in', 'conv_w', 'ssm_conv_w', 'ssm_conv_b', 'dt_bias', 'a_log', 'd_skip', 'ssm_norm_w', 'w_conv_out', 'w_ssm_out', 'w_o', 'w_up', 'w_down', 'final_norm']
TWIN_WEIGHTS = ['w_ada', 'b_ada', 'ln1', 'ln2', 'w_in', 'conv_w', 'ssm_conv_w', 'ssm_conv_b', 'dt_bias', 'a_log', 'd_skip', 'ssm_norm_w', 'w_conv_out', 'w_ssm_out', 'w_o', 'w_up', 'w_down', 'final_norm']
TWIN_DIFF_INPUT = 'x'
TWIN_INPUTS = ['x', 'c', 'w_ada', 'b_ada', 'ln1', 'ln2', 'w_in', 'conv_w', 'ssm_conv_w', 'ssm_conv_b', 'dt_bias', 'a_log', 'd_skip', 'ssm_norm_w', 'w_conv_out', 'w_ssm_out', 'w_o', 'w_up', 'w_down', 'final_norm', 'loss_target', 'm_w_ada', 'm_b_ada', 'm_ln1', 'm_ln2', 'm_w_in', 'm_conv_w', 'm_ssm_conv_w', 'm_ssm_conv_b', 'm_dt_bias', 'm_a_log', 'm_d_skip', 'm_ssm_norm_w', 'm_w_conv_out', 'm_w_ssm_out', 'm_w_o', 'm_w_up', 'm_w_down', 'm_final_norm', 'v_w_ada', 'v_b_ada', 'v_ln1', 'v_ln2', 'v_w_in', 'v_conv_w', 'v_ssm_conv_w', 'v_ssm_conv_b', 'v_dt_bias', 'v_a_log', 'v_d_skip', 'v_ssm_norm_w', 'v_w_conv_out', 'v_w_ssm_out', 'v_w_o', 'v_w_up', 'v_w_down', 'v_final_norm']
TWIN_OUTPUTS = ['loss', 'grad_x', 'grad_w_ada', 'grad_b_ada', 'grad_ln1', 'grad_ln2', 'grad_w_in', 'grad_conv_w', 'grad_ssm_conv_w', 'grad_ssm_conv_b', 'grad_dt_bias', 'grad_a_log', 'grad_d_skip', 'grad_ssm_norm_w', 'grad_w_conv_out', 'grad_w_ssm_out', 'grad_w_o', 'grad_w_up', 'grad_w_down', 'grad_final_norm', 'delta_w_ada', 'delta_b_ada', 'delta_ln1', 'delta_ln2', 'delta_w_in', 'delta_conv_w', 'delta_ssm_conv_w', 'delta_ssm_conv_b', 'delta_dt_bias', 'delta_a_log', 'delta_d_skip', 'delta_ssm_norm_w', 'delta_w_conv_out', 'delta_w_ssm_out', 'delta_w_o', 'delta_w_up', 'delta_w_down', 'delta_final_norm', 'new_m_w_ada', 'new_m_b_ada', 'new_m_ln1', 'new_m_ln2', 'new_m_w_in', 'new_m_conv_w', 'new_m_ssm_conv_w', 'new_m_ssm_conv_b', 'new_m_dt_bias', 'new_m_a_log', 'new_m_d_skip', 'new_m_ssm_norm_w', 'new_m_w_conv_out', 'new_m_w_ssm_out', 'new_m_w_o', 'new_m_w_up', 'new_m_w_down', 'new_m_final_norm', 'new_v_w_ada', 'new_v_b_ada', 'new_v_ln1', 'new_v_ln2', 'new_v_w_in', 'new_v_conv_w', 'new_v_ssm_conv_w', 'new_v_ssm_conv_b', 'new_v_dt_bias', 'new_v_a_log', 'new_v_d_skip', 'new_v_ssm_norm_w', 'new_v_w_conv_out', 'new_v_w_ssm_out', 'new_v_w_o', 'new_v_w_up', 'new_v_w_down', 'new_v_final_norm']
TWIN_LEAF_KINDS = {'loss': 'loss', 'grad_x': 'grad_x', 'grad_w_ada': 'grad_w', 'grad_b_ada': 'grad_w', 'grad_ln1': 'grad_w', 'grad_ln2': 'grad_w', 'grad_w_in': 'grad_w', 'grad_conv_w': 'grad_w', 'grad_ssm_conv_w': 'grad_w', 'grad_ssm_conv_b': 'grad_w', 'grad_dt_bias': 'grad_w', 'grad_a_log': 'grad_w', 'grad_d_skip': 'grad_w', 'grad_ssm_norm_w': 'grad_w', 'grad_w_conv_out': 'grad_w', 'grad_w_ssm_out': 'grad_w', 'grad_w_o': 'grad_w', 'grad_w_up': 'grad_w', 'grad_w_down': 'grad_w', 'grad_final_norm': 'grad_w', 'delta_w_ada': 'delta_w', 'delta_b_ada': 'delta_w', 'delta_ln1': 'delta_w', 'delta_ln2': 'delta_w', 'delta_w_in': 'delta_w', 'delta_conv_w': 'delta_w', 'delta_ssm_conv_w': 'delta_w', 'delta_ssm_conv_b': 'delta_w', 'delta_dt_bias': 'delta_w', 'delta_a_log': 'delta_w', 'delta_d_skip': 'delta_w', 'delta_ssm_norm_w': 'delta_w', 'delta_w_conv_out': 'delta_w', 'delta_w_ssm_out': 'delta_w', 'delta_w_o': 'delta_w', 'delta_w_up': 'delta_w', 'delta_w_down': 'delta_w', 'delta_final_norm': 'delta_w', 'new_m_w_ada': 'new_m', 'new_m_b_ada': 'new_m', 'new_m_ln1': 'new_m', 'new_m_ln2': 'new_m', 'new_m_w_in': 'new_m', 'new_m_conv_w': 'new_m', 'new_m_ssm_conv_w': 'new_m', 'new_m_ssm_conv_b': 'new_m', 'new_m_dt_bias': 'new_m', 'new_m_a_log': 'new_m', 'new_m_d_skip': 'new_m', 'new_m_ssm_norm_w': 'new_m', 'new_m_w_conv_out': 'new_m', 'new_m_w_ssm_out': 'new_m', 'new_m_w_o': 'new_m', 'new_m_w_up': 'new_m', 'new_m_w_down': 'new_m', 'new_m_final_norm': 'new_m', 'new_v_w_ada': 'new_v', 'new_v_b_ada': 'new_v', 'new_v_ln1': 'new_v', 'new_v_ln2': 'new_v', 'new_v_w_in': 'new_v', 'new_v_conv_w': 'new_v', 'new_v_ssm_conv_w': 'new_v', 'new_v_ssm_conv_b': 'new_v', 'new_v_dt_bias': 'new_v', 'new_v_a_log': 'new_v', 'new_v_d_skip': 'new_v', 'new_v_ssm_norm_w': 'new_v', 'new_v_w_conv_out': 'new_v', 'new_v_w_ssm_out': 'new_v', 'new_v_w_o': 'new_v', 'new_v_w_up': 'new_v', 'new_v_w_down': 'new_v', 'new_v_final_norm': 'new_v'}


def _forward(args):
    return _fwd_reference(*[args[k] for k in FWD_PARAMS])


def _output_shape():
    def fwd():
        inp = _fwd_setup_inputs(0)
        return _fwd_reference(*[inp[k] for k in FWD_PARAMS])
    out = _jax.eval_shape(fwd)
    return out.shape, out.dtype

N_MICROBATCH = 1
ADAM_LR = 0.001
ADAM_B1 = 0.9
ADAM_B2 = 0.999
ADAM_EPS = 1e-08
ADAM_WD = 0.01
ADAM_STEP = 10
PER_EXAMPLE_BATCH_AXIS = {'x': 0, 'c': 0, 'loss_target': 0}
SHARED_INPUTS = []
_WEIGHT_DTYPES = {'w_ada': _jnp.float32, 'b_ada': _jnp.float32, 'ln1': _jnp.float32, 'ln2': _jnp.float32, 'w_in': _jnp.float32, 'conv_w': _jnp.float32, 'ssm_conv_w': _jnp.float32, 'ssm_conv_b': _jnp.float32, 'dt_bias': _jnp.float32, 'a_log': _jnp.float32, 'd_skip': _jnp.float32, 'ssm_norm_w': _jnp.float32, 'w_conv_out': _jnp.float32, 'w_ssm_out': _jnp.float32, 'w_o': _jnp.float32, 'w_up': _jnp.float32, 'w_down': _jnp.float32, 'final_norm': _jnp.float32}
MOMENT_SCALE = {'w_ada': 3.178154e-01, 'b_ada': 7.225870e-01, 'ln1': 1.040601e-01, 'ln2': 1.227056e-01, 'w_in': 3.382497e-02, 'conv_w': 5.314255e-02, 'ssm_conv_w': 2.397686e-02, 'ssm_conv_b': 3.898002e-02, 'dt_bias': 5.673990e-02, 'a_log': 1.100332e-01, 'd_skip': 1.355268e-01, 'ssm_norm_w': 3.513869e-02, 'w_conv_out': 5.149784e-02, 'w_ssm_out': 4.827632e-02, 'w_o': 7.100142e-02, 'w_up': 6.651968e-02, 'w_down': 2.172912e-01, 'final_norm': 6.468031e+01}


def _to_microbatches(a, axis):
    t = _jnp.moveaxis(a, axis, 0)
    t = t.reshape((N_MICROBATCH, t.shape[0] // N_MICROBATCH) + t.shape[1:])
    return _jnp.moveaxis(t, 1, axis + 1)


def setup_inputs(seed: int = 0) -> dict:
    inp = _fwd_setup_inputs(seed)
    key = _jax.random.fold_in(_jax.random.key(seed), 7919)
    shape, _ = _output_shape()
    out = dict(inp)
    out["loss_target"] = _jax.random.normal(_jax.random.fold_in(key, 0), shape, _jnp.float32)
    for i, name in enumerate(TWIN_WEIGHTS):
        w = inp[name].astype(_jnp.float32)
        if MOMENT_SCALE is None:
            s = _jnp.sqrt(_jnp.mean(_jnp.square(w)) + 1e-30)
        else:
            s = MOMENT_SCALE[name]
        km, kv = _jax.random.split(_jax.random.fold_in(key, i + 1))
        out[name] = w
        out["m_" + name] = s * _jax.random.normal(km, w.shape, _jnp.float32)
        out["v_" + name] = (s * s) * _jax.random.uniform(kv, w.shape, _jnp.float32, 0.5, 1.5)
    if N_MICROBATCH > 1:
        for name, axis in PER_EXAMPLE_BATCH_AXIS.items():
            out[name] = _to_microbatches(out[name], axis)
    return {'x': out['x'], 'c': out['c'], 'w_ada': out['w_ada'], 'b_ada': out['b_ada'], 'ln1': out['ln1'], 'ln2': out['ln2'], 'w_in': out['w_in'], 'conv_w': out['conv_w'], 'ssm_conv_w': out['ssm_conv_w'], 'ssm_conv_b': out['ssm_conv_b'], 'dt_bias': out['dt_bias'], 'a_log': out['a_log'], 'd_skip': out['d_skip'], 'ssm_norm_w': out['ssm_norm_w'], 'w_conv_out': out['w_conv_out'], 'w_ssm_out': out['w_ssm_out'], 'w_o': out['w_o'], 'w_up': out['w_up'], 'w_down': out['w_down'], 'final_norm': out['final_norm'], 'loss_target': out['loss_target'], 'm_w_ada': out['m_w_ada'], 'm_b_ada': out['m_b_ada'], 'm_ln1': out['m_ln1'], 'm_ln2': out['m_ln2'], 'm_w_in': out['m_w_in'], 'm_conv_w': out['m_conv_w'], 'm_ssm_conv_w': out['m_ssm_conv_w'], 'm_ssm_conv_b': out['m_ssm_conv_b'], 'm_dt_bias': out['m_dt_bias'], 'm_a_log': out['m_a_log'], 'm_d_skip': out['m_d_skip'], 'm_ssm_norm_w': out['m_ssm_norm_w'], 'm_w_conv_out': out['m_w_conv_out'], 'm_w_ssm_out': out['m_w_ssm_out'], 'm_w_o': out['m_w_o'], 'm_w_up': out['m_w_up'], 'm_w_down': out['m_w_down'], 'm_final_norm': out['m_final_norm'], 'v_w_ada': out['v_w_ada'], 'v_b_ada': out['v_b_ada'], 'v_ln1': out['v_ln1'], 'v_ln2': out['v_ln2'], 'v_w_in': out['v_w_in'], 'v_conv_w': out['v_conv_w'], 'v_ssm_conv_w': out['v_ssm_conv_w'], 'v_ssm_conv_b': out['v_ssm_conv_b'], 'v_dt_bias': out['v_dt_bias'], 'v_a_log': out['v_a_log'], 'v_d_skip': out['v_d_skip'], 'v_ssm_norm_w': out['v_ssm_norm_w'], 'v_w_conv_out': out['v_w_conv_out'], 'v_w_ssm_out': out['v_w_ssm_out'], 'v_w_o': out['v_w_o'], 'v_w_up': out['v_w_up'], 'v_w_down': out['v_w_down'], 'v_final_norm': out['v_final_norm']}


def _loss(weights, diff, rest, loss_target):
    with _jax.named_scope("forward"):
        args = {**rest, TWIN_DIFF_INPUT: diff, **{k: w.astype(_WEIGHT_DTYPES[k]) for k, w in weights.items()}}
        y = _forward(args)
    with _jax.named_scope("loss_head"):
        err = _jnp.square(y.astype(_jnp.float32) - loss_target)
        return 0.5 * _jnp.sum(_jnp.mean(err, axis=-1)) if err.ndim else 0.5 * err


def _adamw(w, g, m, v):
    m = ADAM_B1 * m + (1.0 - ADAM_B1) * g
    v = ADAM_B2 * v + (1.0 - ADAM_B2) * _jnp.square(g)
    m_hat = m / (1.0 - ADAM_B1 ** ADAM_STEP)
    v_hat = v / (1.0 - ADAM_B2 ** ADAM_STEP)
    delta = -ADAM_LR * (m_hat / (_jnp.sqrt(v_hat) + ADAM_EPS) + ADAM_WD * w)
    return delta, m, v


def reference(x, c, w_ada, b_ada, ln1, ln2, w_in, conv_w, ssm_conv_w, ssm_conv_b, dt_bias, a_log, d_skip, ssm_norm_w, w_conv_out, w_ssm_out, w_o, w_up, w_down, final_norm, loss_target, m_w_ada, m_b_ada, m_ln1, m_ln2, m_w_in, m_conv_w, m_ssm_conv_w, m_ssm_conv_b, m_dt_bias, m_a_log, m_d_skip, m_ssm_norm_w, m_w_conv_out, m_w_ssm_out, m_w_o, m_w_up, m_w_down, m_final_norm, v_w_ada, v_b_ada, v_ln1, v_ln2, v_w_in, v_conv_w, v_ssm_conv_w, v_ssm_conv_b, v_dt_bias, v_a_log, v_d_skip, v_ssm_norm_w, v_w_conv_out, v_w_ssm_out, v_w_o, v_w_up, v_w_down, v_final_norm):
    given = dict(x=x, c=c, w_ada=w_ada, b_ada=b_ada, ln1=ln1, ln2=ln2, w_in=w_in, conv_w=conv_w, ssm_conv_w=ssm_conv_w, ssm_conv_b=ssm_conv_b, dt_bias=dt_bias, a_log=a_log, d_skip=d_skip, ssm_norm_w=ssm_norm_w, w_conv_out=w_conv_out, w_ssm_out=w_ssm_out, w_o=w_o, w_up=w_up, w_down=w_down, final_norm=final_norm, loss_target=loss_target, m_w_ada=m_w_ada, m_b_ada=m_b_ada, m_ln1=m_ln1, m_ln2=m_ln2, m_w_in=m_w_in, m_conv_w=m_conv_w, m_ssm_conv_w=m_ssm_conv_w, m_ssm_conv_b=m_ssm_conv_b, m_dt_bias=m_dt_bias, m_a_log=m_a_log, m_d_skip=m_d_skip, m_ssm_norm_w=m_ssm_norm_w, m_w_conv_out=m_w_conv_out, m_w_ssm_out=m_w_ssm_out, m_w_o=m_w_o, m_w_up=m_w_up, m_w_down=m_w_down, m_final_norm=m_final_norm, v_w_ada=v_w_ada, v_b_ada=v_b_ada, v_ln1=v_ln1, v_ln2=v_ln2, v_w_in=v_w_in, v_conv_w=v_conv_w, v_ssm_conv_w=v_ssm_conv_w, v_ssm_conv_b=v_ssm_conv_b, v_dt_bias=v_dt_bias, v_a_log=v_a_log, v_d_skip=v_d_skip, v_ssm_norm_w=v_ssm_norm_w, v_w_conv_out=v_w_conv_out, v_w_ssm_out=v_w_ssm_out, v_w_o=v_w_o, v_w_up=v_w_up, v_w_down=v_w_down, v_final_norm=v_final_norm)
    weights = {n: given[n] for n in TWIN_WEIGHTS}
    shared = {n: given[n] for n in SHARED_INPUTS}
    per_example = {n: given[n] for n in ['x', 'c']}
    grad_fn = _jax.value_and_grad(_loss, argnums=(0, 1))

    def one_microbatch(ex, loss_target):
        ex = dict(ex)
        diff = ex.pop(TWIN_DIFF_INPUT)
        return grad_fn(weights, diff, {**shared, **ex}, loss_target)

    if N_MICROBATCH == 1:
        loss, (grad_w, grad_x) = one_microbatch(per_example, given["loss_target"])
    else:
        def body(carry, xs):
            loss_sum, grad_sum = carry
            l_k, (gw_k, gx_k) = one_microbatch(xs[0], xs[1])
            with _jax.named_scope("update"):
                return (loss_sum + l_k, _jax.tree.map(_jnp.add, grad_sum, gw_k)), gx_k

        init = (_jnp.zeros((), _jnp.float32), _jax.tree.map(_jnp.zeros_like, weights))
        (loss, grad_w), grad_x = _jax.lax.scan(body, init, (per_example, given["loss_target"]))
    with _jax.named_scope("update"):
        delta_w, new_m, new_v = {}, {}, {}
        for n in TWIN_WEIGHTS:
            delta_w[n], new_m[n], new_v[n] = _adamw(weights[n], grad_w[n], given["m_" + n], given["v_" + n])
    return (loss, grad_x, *[grad_w[n] for n in TWIN_WEIGHTS], *[delta_w[n] for n in TWIN_WEIGHTS],
            *[new_m[n] for n in TWIN_WEIGHTS], *[new_v[n] for n in TWIN_WEIGHTS])
```

```python
import functools

import numpy as np
import jax
import jax.numpy as jnp
from jax import lax
from jax.experimental import pallas as pl
from jax.experimental.pallas import tpu as pltpu

F32 = jnp.float32
BF16 = jnp.bfloat16
EPS = 1e-6
N_STATE = 128
HEAD_DIM = 64
HEADS_PER_GROUP = 4
GROUP_W = HEAD_DIM * HEADS_PER_GROUP
N_DEV = 8
ROW_TILE = 256
LANE = 128
VMEM_LIMIT = 56 * 1024 * 1024

ADAM_LR, ADAM_B1, ADAM_B2, ADAM_EPS, ADAM_WD, ADAM_STEP = 0.001, 0.9, 0.999, 1e-08, 0.01, 10

MESH = pl.DeviceIdType.MESH


def _params(sem=None):
    return pltpu.CompilerParams(dimension_semantics=sem, vmem_limit_bytes=VMEM_LIMIT)


def _tile(n, target):
    if n <= target:
        return n
    best = None
    for d in range(LANE, target + 1, LANE):
        if n % d == 0:
            best = d
    assert best is not None, (n, target)
    return best


def _slot(p):
    return 4 * p[0] + 2 * p[1] + p[2]


def _all_gather(xs, name):
    n = len(xs)

    def body(*refs):
        x_refs, o_refs = refs[:n], refs[n:2 * n]
        send, recv, loc = refs[2 * n:]
        x, y, c = lax.axis_index("x"), lax.axis_index("y"), lax.axis_index("c")
        me, sib = (x, y, c), (x, y, 1 - c)
        chips = [(1 - x, y), (x, 1 - y), (1 - x, 1 - y)]

        def cp(a, k, block, to, src=None):
            dst = o_refs[a].at[_slot(block)]
            return pltpu.make_async_remote_copy(
                src_ref=dst if src is None else src, dst_ref=dst,
                send_sem=send.at[7 * a + k], recv_sem=recv.at[7 * a + k],
                device_id=to, device_id_type=MESH)

        mine = [pltpu.make_async_copy(x_refs[a], o_refs[a].at[_slot(me)], loc.at[a]) for a in range(n)]
        for m in mine:
            m.start()
        first = []
        for a in range(n):
            first.append(cp(a, 0, me, sib, src=x_refs[a]))
            first += [cp(a, 1 + j, me, (*chip, c), src=x_refs[a]) for j, chip in enumerate(chips)]
        for f in first:
            f.start()
        passed = []
        for j, chip in enumerate(chips):
            for a in range(n):
                cp(a, 1 + j, (*chip, c), me).wait_recv()
                p = cp(a, 4 + j, (*chip, c), sib)
                p.start()
                passed.append(p)
        for a in range(n):
            cp(a, 0, sib, me).wait_recv()
            for j, chip in enumerate(chips):
                cp(a, 4 + j, (*chip, 1 - c), me).wait_recv()
        for f in first + passed:
            f.wait_send()
        for m in mine:
            m.wait()

    any_spec = pl.BlockSpec(memory_space=pl.ANY)
    return pl.pallas_call(
        body, name=name,
        out_shape=[jax.ShapeDtypeStruct((N_DEV,) + x.shape, x.dtype) for x in xs],
        in_specs=[any_spec] * n, out_specs=[any_spec] * n,
        scratch_shapes=[pltpu.SemaphoreType.DMA((7 * n,)), pltpu.SemaphoreType.DMA((7 * n,)),
                        pltpu.SemaphoreType.DMA((n,))],
    )(*xs)


def _all_to_all(xs, name):
    n = len(xs)

    def body(*refs):
        x_refs, o_refs = refs[:n], refs[n:2 * n]
        send, recv, loc = refs[2 * n:]
        x, y, c = lax.axis_index("x"), lax.axis_index("y"), lax.axis_index("c")
        me = (x, y, c)
        peers = []
        for k in range(1, 8):
            kx, ky, kc = (k >> 2) & 1, (k >> 1) & 1, k & 1
            peers.append((x + kx - 2 * x * kx, y + ky - 2 * y * ky, c + kc - 2 * c * kc))

        def cp(a, k, src_slot, dst_slot, to):
            return pltpu.make_async_remote_copy(
                src_ref=x_refs[a].at[src_slot], dst_ref=o_refs[a].at[dst_slot],
                send_sem=send.at[7 * a + k], recv_sem=recv.at[7 * a + k],
                device_id=to, device_id_type=MESH)

        mine = [pltpu.make_async_copy(x_refs[a].at[_slot(me)], o_refs[a].at[_slot(me)], loc.at[a])
                for a in range(n)]
        for m in mine:
            m.start()
        sends = [cp(a, k, _slot(p), _slot(me), p) for a in range(n) for k, p in enumerate(peers)]
        for s in sends:
            s.start()
        for a in range(n):
            for k, p in enumerate(peers):
                cp(a, k, _slot(me), _slot(p), me).wait_recv()
        for s in sends:
            s.wait_send()
        for m in mine:
            m.wait()

    any_spec = pl.BlockSpec(memory_space=pl.ANY)
    return pl.pallas_call(
        body, name=name,
        out_shape=[jax.ShapeDtypeStruct(x.shape, x.dtype) for x in xs],
        in_specs=[any_spec] * n, out_specs=[any_spec] * n,
        scratch_shapes=[pltpu.SemaphoreType.DMA((7 * n,)), pltpu.SemaphoreType.DMA((7 * n,)),
                        pltpu.SemaphoreType.DMA((n,))],
    )(*xs)


_DIMS = {"nn": (((1,), (0,)), ((), ())), "nt": (((1,), (1,)), ((), ())), "tn": (((0,), (0,)), ((), ()))}


def _matmul(a, b, mode, out_dtype, name):
    if mode == "nn":
        (m, k), (_, n) = a.shape, b.shape
    elif mode == "nt":
        (m, k), (n, _) = a.shape, b.shape
    else:
        (k, m), (_, n) = a.shape, b.shape
    tm, tn, tk = _tile(m, 1024), _tile(n, 1024), _tile(k, 1024)
    nk = k // tk
    dn = _DIMS[mode]

    def body(a_ref, b_ref, o_ref, *scratch):
        part = lax.dot_general(a_ref[...], b_ref[...], dn, preferred_element_type=F32)
        if nk == 1:
            o_ref[...] = part.astype(o_ref.dtype)
            return
        acc_ref, = scratch
        kk = pl.program_id(2)

        @pl.when(kk == 0)
        def _():
            acc_ref[...] = part

        @pl.when(kk > 0)
        def _():
            acc_ref[...] += part

        @pl.when(kk == nk - 1)
        def _():
            o_ref[...] = acc_ref[...].astype(o_ref.dtype)

    a_spec = pl.BlockSpec((tk, tm), lambda i, j, kk: (kk, i)) if mode == "tn" else \
        pl.BlockSpec((tm, tk), lambda i, j, kk: (i, kk))
    b_spec = pl.BlockSpec((tn, tk), lambda i, j, kk: (j, kk)) if mode == "nt" else \
        pl.BlockSpec((tk, tn), lambda i, j, kk: (kk, j))
    return pl.pallas_call(
        body, name=name,
        out_shape=jax.ShapeDtypeStruct((m, n), out_dtype),
        grid=(m // tm, n // tn, nk),
        in_specs=[a_spec, b_spec],
        out_specs=pl.BlockSpec((tm, tn), lambda i, j, kk: (i, j)),
        scratch_shapes=[] if nk == 1 else [pltpu.VMEM((tm, tn), F32)],
        compiler_params=_params(("parallel", "parallel", "arbitrary")),
    )(a, b)


def _dot(a, b, mode):
    return lax.dot_general(a.astype(BF16), b.astype(BF16), _DIMS[mode], preferred_element_type=F32)


@functools.partial(jax.custom_vjp, nondiff_argnums=(2,))
def _bdot(a, b, mode):
    return _dot(a, b, mode)


def _bdot_fwd(a, b, mode):
    return _dot(a, b, mode), (a, b)


def _bdot_bwd(mode, res, g):
    a, b = res
    if mode == "nn":
        return _dot(g, b, "nt"), _dot(a, g, "tn")
    if mode == "nt":
        return _dot(g, b, "nn"), _dot(g, a, "tn")
    return _dot(b, g, "nt"), _dot(a, g, "nn")


_bdot.defvjp(_bdot_fwd, _bdot_bwd)


def _xdot01(m01, x):
    hi = x.astype(BF16)
    r1 = x - hi.astype(F32)
    mid = r1.astype(BF16)
    lo = (r1 - mid.astype(F32)).astype(BF16)
    dn = _DIMS["nn"]
    return (lax.dot_general(m01, hi, dn, preferred_element_type=F32)
            + lax.dot_general(m01, mid, dn, preferred_element_type=F32)
            + lax.dot_general(m01, lo, dn, preferred_element_type=F32))


def _chunk_tri(n, lower):
    r = lax.broadcasted_iota(jnp.int32, (n, n), 0)
    c = lax.broadcasted_iota(jnp.int32, (n, n), 1)
    return jnp.where((r >= c) if lower else (r <= c), 1.0, 0.0).astype(BF16)


def _shift_down(x, prev8, k):
    if k == 0:
        return x
    n = x.shape[0]
    r = pltpu.roll(x, k, 0)
    rp = pltpu.roll(prev8, k, 0)
    rows = lax.broadcasted_iota(jnp.int32, (8, x.shape[1]), 0)
    head = jnp.where(rows < k, rp, r[:8])
    return head if n == 8 else jnp.concatenate([head, r[8:]], axis=0)


def _shift_up(x, next8, k):
    if k == 0:
        return x
    n = x.shape[0]
    r = pltpu.roll(x, n - k, 0)
    rn = pltpu.roll(next8, 8 - k, 0)
    rows = lax.broadcasted_iota(jnp.int32, (8, x.shape[1]), 0)
    tail = jnp.where(rows >= 8 - k, rn, r[n - 8:])
    return tail if n == 8 else jnp.concatenate([r[:n - 8], tail], axis=0)


def _sigmoid(x):
    return 1.0 / (1.0 + jnp.exp(-x))


def _rows(ref, i):
    return ref[i:i + 1, :]


def _norm_parts(x, mp_ref, which):
    ln, sh, sc = _rows(mp_ref, 6 + which), _rows(mp_ref, 3 * which), _rows(mp_ref, 3 * which + 1)
    r = lax.rsqrt(jnp.mean(x * x, axis=1, keepdims=True) + EPS)
    return r, ln, sh, sc


def _norm_fwd(x, modp, which, name):
    t, d = x.shape

    def body(x_ref, mp_ref, u_ref):
        xv = x_ref[...]
        r, ln, sh, sc = _norm_parts(xv, mp_ref, which)
        u_ref[...] = (((xv * r) * ln) * (1.0 + sc) + sh).astype(BF16)

    row = pl.BlockSpec((ROW_TILE, d), lambda i: (i, 0))
    return pl.pallas_call(
        body, name=name, out_shape=jax.ShapeDtypeStruct((t, d), BF16), grid=(t // ROW_TILE,),
        in_specs=[row, pl.BlockSpec((8, d), lambda i: (0, 0))], out_specs=row,
        compiler_params=_params(("parallel",)))(x, modp)


def _resid_norm_fwd(x, br, mp_gate, gate_row, mp_norm, which, name):
    t, d = x.shape

    def body(x_ref, br_ref, mg_ref, mn_ref, xn_ref, u_ref):
        xv = x_ref[...] + _rows(mg_ref, gate_row) * br_ref[...]
        xn_ref[...] = xv
        r, ln, sh, sc = _norm_parts(xv, mn_ref, which)
        u_ref[...] = (((xv * r) * ln) * (1.0 + sc) + sh).astype(BF16)

    row = pl.BlockSpec((ROW_TILE, d), lambda i: (i, 0))
    mp = pl.BlockSpec((8, d), lambda i: (0, 0))
    return pl.pallas_call(
        body, name=name,
        out_shape=[jax.ShapeDtypeStruct((t, d), F32), jax.ShapeDtypeStruct((t, d), BF16)],
        grid=(t // ROW_TILE,), in_specs=[row, row, mp, mp], out_specs=[row, row],
        compiler_params=_params(("parallel",)))(x, br, mp_gate, mp_norm)


def _final_fwd_bwd(x, br, mp_gate, fnorm8, target, name):
    t, d = x.shape

    def body(x_ref, br_ref, mg_ref, fn_ref, tg_ref, dx_ref, dbr_ref, acc_ref):
        gate = _rows(mg_ref, 5)
        brv = br_ref[...]
        xv = x_ref[...] + gate * brv
        fn = _rows(fn_ref, 0)
        r = lax.rsqrt(jnp.mean(xv * xv, axis=1, keepdims=True) + EPS)
        nrm = xv * r
        err = nrm * fn - tg_ref[...]
        loss = 0.5 * jnp.sum(jnp.mean(err * err, axis=1, keepdims=True), axis=0, keepdims=True)
        dy = err * (1.0 / d)
        dn = dy * fn
        dx = r * (dn - nrm * jnp.mean(dn * nrm, axis=1, keepdims=True))
        dx_ref[...] = dx
        dbr_ref[...] = (dx * gate).astype(BF16)
        upd = jnp.concatenate([
            jnp.broadcast_to(loss, (1, d)),
            jnp.sum(dy * nrm, axis=0, keepdims=True),
            jnp.sum(dx * brv, axis=0, keepdims=True),
            jnp.zeros((5, d), F32)], axis=0)

        @pl.when(pl.program_id(0) == 0)
        def _():
            acc_ref[...] = upd

        @pl.when(pl.program_id(0) > 0)
        def _():
            acc_ref[...] += upd

    row = pl.BlockSpec((ROW_TILE, d), lambda i: (i, 0))
    mp = pl.BlockSpec((8, d), lambda i: (0, 0))
    return pl.pallas_call(
        body, name=name,
        out_shape=[jax.ShapeDtypeStruct((t, d), F32), jax.ShapeDtypeStruct((t, d), BF16),
                   jax.ShapeDtypeStruct((8, d), F32)],
        grid=(t // ROW_TILE,), in_specs=[row, row, mp, mp, row], out_specs=[row, row, mp],
        compiler_params=_params(("arbitrary",)))(x, br, mp_gate, fnorm8, target)


def _resid_norm_bwd(dx, du, x, mp_norm, which, name, br=None, mp_gate=None, gate_row=None):
    t, d = x.shape
    has_gate = br is not None

    def body(*refs):
        if has_gate:
            dx_ref, du_ref, x_ref, mn_ref, br_ref, mg_ref, dxn_ref, dbr_ref, acc_ref = refs
        else:
            dx_ref, du_ref, x_ref, mn_ref, dxn_ref, acc_ref = refs
        xv, duv = x_ref[...], du_ref[...]
        r, ln, _, sc = _norm_parts(xv, mn_ref, which)
        nrm = xv * r
        dn = duv * (ln * (1.0 + sc))
        dxn = dx_ref[...] + r * (dn - nrm * jnp.mean(dn * nrm, axis=1, keepdims=True))
        dxn_ref[...] = dxn
        rows = [jnp.sum(duv, axis=0, keepdims=True), jnp.sum(duv * nrm, axis=0, keepdims=True)]
        if has_gate:
            dbr_ref[...] = (dxn * _rows(mg_ref, gate_row)).astype(BF16)
            rows.append(jnp.sum(dxn * br_ref[...], axis=0, keepdims=True))
        upd = jnp.concatenate(rows + [jnp.zeros((8 - len(rows), d), F32)], axis=0)

        @pl.when(pl.program_id(0) == 0)
        def _():
            acc_ref[...] = upd

        @pl.when(pl.program_id(0) > 0)
        def _():
            acc_ref[...] += upd

    row = pl.BlockSpec((ROW_TILE, d), lambda i: (i, 0))
    mp = pl.BlockSpec((8, d), lambda i: (0, 0))
    ins, in_specs = [dx, du, x, mp_norm], [row, row, row, mp]
    outs = [jax.ShapeDtypeStruct((t, d), F32)]
    out_specs = [row]
    if has_gate:
        ins += [br, mp_gate]
        in_specs += [row, mp]
        outs.append(jax.ShapeDtypeStruct((t, d), BF16))
        out_specs.append(row)
    outs.append(jax.ShapeDtypeStruct((8, d), F32))
    out_specs.append(mp)
    return pl.pallas_call(
        body, name=name, out_shape=outs, grid=(t // ROW_TILE,), in_specs=in_specs, out_specs=out_specs,
        compiler_params=_params(("arbitrary",)))(*ins)


def _relu2_fwd(h, name):
    t, f = h.shape
    tf = _tile(f, 2048)

    def body(h_ref, o_ref):
        hv = jnp.maximum(h_ref[...].astype(F32), 0.0)
        o_ref[...] = (hv * hv).astype(BF16)

    blk = pl.BlockSpec((ROW_TILE, tf), lambda i, j: (i, j))
    return pl.pallas_call(
        body, name=name, out_shape=jax.ShapeDtypeStruct((t, f), BF16), grid=(t // ROW_TILE, f // tf),
        in_specs=[blk], out_specs=blk, compiler_params=_params(("parallel", "parallel")))(h)


def _relu2_bwd(dhid, h, name):
    t, f = h.shape
    tf = _tile(f, 2048)

    def body(d_ref, h_ref, o_ref):
        o_ref[...] = (d_ref[...] * (2.0 * jnp.maximum(h_ref[...].astype(F32), 0.0))).astype(BF16)

    blk = pl.BlockSpec((ROW_TILE, tf), lambda i, j: (i, j))
    return pl.pallas_call(
        body, name=name, out_shape=jax.ShapeDtypeStruct((t, f), BF16), grid=(t // ROW_TILE, f // tf),
        in_specs=[blk, blk], out_specs=blk, compiler_params=_params(("parallel", "parallel")))(dhid, h)


def _merge_fwd(proj, p_conv, p_ssm, name):
    t, d = p_conv.shape

    def body(gl_ref, pc_ref, ps_ref, o_ref):
        g = _sigmoid(gl_ref[...].astype(F32))
        o_ref[...] = (g[:, :d] * pc_ref[...] + g[:, d:] * ps_ref[...]).astype(BF16)

    row = pl.BlockSpec((ROW_TILE, d), lambda i: (i, 0))
    return pl.pallas_call(
        body, name=name, out_shape=jax.ShapeDtypeStruct((t, d), BF16), grid=(t // ROW_TILE,),
        in_specs=[pl.BlockSpec((ROW_TILE, 2 * d), lambda i: (i, 0)), row, row], out_specs=row,
        compiler_params=_params(("parallel",)))(proj, p_conv, p_ssm)


def _merge_bwd(dmerged, proj, p_conv, p_ssm, name):
    t, d = p_conv.shape
    pw = proj.shape[1]

    def body(dm_ref, gl_ref, pc_ref, ps_ref, dpc_ref, dps_ref, dgl_ref):
        g = _sigmoid(gl_ref[...].astype(F32))
        gc, gs = g[:, :d], g[:, d:]
        dm = dm_ref[...]
        dpc_ref[...] = (dm * gc).astype(BF16)
        dps_ref[...] = (dm * gs).astype(BF16)
        dgl_ref[...] = jnp.concatenate(
            [dm * pc_ref[...] * gc * (1.0 - gc), dm * ps_ref[...] * gs * (1.0 - gs)], axis=1).astype(BF16)

    row = pl.BlockSpec((ROW_TILE, d), lambda i: (i, 0))
    wide = pl.BlockSpec((ROW_TILE, 2 * d), lambda i: (i, 0))
    return pl.pallas_call(
        body, name=name,
        out_shape=[jax.ShapeDtypeStruct((t, d), BF16), jax.ShapeDtypeStruct((t, d), BF16),
                   jax.ShapeDtypeStruct((t, pw), BF16)],
        grid=(t // ROW_TILE,), in_specs=[row, wide, row, row], out_specs=[row, row, wide],
        compiler_params=_params(("parallel",)))(dmerged, proj, p_conv, p_ssm)


def _halo_specs(t, width, col):
    nb = t // 8
    step = ROW_TILE // 8
    prev = pl.BlockSpec((8, width), lambda i: (jnp.maximum(i * step - 1, 0), col))
    nxt = pl.BlockSpec((8, width), lambda i: (jnp.minimum((i + 1) * step, nb - 1), col))
    return prev, nxt


def _gconv_fwd(proj, conv_w8, d, name):
    t = proj.shape[0]

    def body(cb_ref, cc_ref, cx_ref, ccp_ref, cxp_ref, w_ref, o_ref):
        first = pl.program_id(0) == 0
        v = cc_ref[...].astype(F32) * cx_ref[...].astype(F32)
        vp = jnp.where(first, 0.0, ccp_ref[...].astype(F32) * cxp_ref[...].astype(F32))
        cv = sum(_rows(w_ref, k) * _shift_down(v, vp, 2 - k) for k in range(3))
        o_ref[...] = (cb_ref[...].astype(F32) * cv).astype(BF16)

    def win(col):
        return pl.BlockSpec((ROW_TILE, d), lambda i: (i, col))

    return pl.pallas_call(
        body, name=name, out_shape=jax.ShapeDtypeStruct((t, d), BF16), grid=(t // ROW_TILE,),
        in_specs=[win(2), win(3), win(4), _halo_specs(t, d, 3)[0], _halo_specs(t, d, 4)[0],
                  pl.BlockSpec((8, d), lambda i: (0, 0))],
        out_specs=pl.BlockSpec((ROW_TILE, d), lambda i: (i, 0)),
        compiler_params=_params(("parallel",)))(proj, proj, proj, proj, proj, conv_w8)


def _gconv_bwd(dy, proj, conv_w8, dproj, d, name):
    t = proj.shape[0]
    nt = t // ROW_TILE

    def body(dy_ref, dyn_ref, cb_ref, cc_ref, cx_ref, ccp_ref, cxp_ref, cbn_ref, w_ref, _, dp_ref, dw_ref, st_ref):
        i, j = pl.program_id(0), pl.program_id(1)

        @pl.when(j == 0)
        def _():
            cb, cc, cx = cb_ref[...].astype(F32), cc_ref[...].astype(F32), cx_ref[...].astype(F32)
            v = cc * cx
            vp = jnp.where(i == 0, 0.0, ccp_ref[...].astype(F32) * cxp_ref[...].astype(F32))
            sh = [_shift_down(v, vp, 2 - k) for k in range(3)]
            cv = sum(_rows(w_ref, k) * sh[k] for k in range(3))
            dyv = dy_ref[...]
            dcv = dyv * cb
            dcvn = jnp.where(i == nt - 1, 0.0, dyn_ref[...] * cbn_ref[...].astype(F32))
            dv = sum(_rows(w_ref, k) * _shift_up(dcv, dcvn, 2 - k) for k in range(3))
            st_ref[0] = (dyv * cv).astype(BF16)
            st_ref[1] = (dv * cx).astype(BF16)
            st_ref[2] = (dv * cc).astype(BF16)
            upd = jnp.concatenate([jnp.sum(dcv * sh[k], axis=0, keepdims=True) for k in range(3)]
                                  + [jnp.zeros((5, d), F32)], axis=0)

            @pl.when(i == 0)
            def _():
                dw_ref[...] = upd

            @pl.when(i > 0)
            def _():
                dw_ref[...] += upd

        dp_ref[...] = st_ref[j]

    def win(col):
        return pl.BlockSpec((ROW_TILE, d), lambda i, j: (i, col))

    def halo(col, which):
        nb, step = t // 8, ROW_TILE // 8
        if which == 0:
            return pl.BlockSpec((8, d), lambda i, j: (jnp.maximum(i * step - 1, 0), col))
        return pl.BlockSpec((8, d), lambda i, j: (jnp.minimum((i + 1) * step, nb - 1), col))

    row = pl.BlockSpec((ROW_TILE, d), lambda i, j: (i, 0))
    w8 = pl.BlockSpec((8, d), lambda i, j: (0, 0))
    return pl.pallas_call(
        body, name=name,
        out_shape=[jax.ShapeDtypeStruct(dproj.shape, BF16), jax.ShapeDtypeStruct((8, d), F32)],
        grid=(nt, 3),
        in_specs=[row, halo(0, 1), win(2), win(3), win(4), halo(3, 0), halo(4, 0), halo(2, 1), w8,
                  pl.BlockSpec(memory_space=pl.ANY)],
        out_specs=[pl.BlockSpec((ROW_TILE, d), lambda i, j: (i, 2 + j)), w8],
        scratch_shapes=[pltpu.VMEM((3, ROW_TILE, d), BF16)],
        input_output_aliases={9: 0},
        compiler_params=_params(("arbitrary", "arbitrary")))(
            dy, dy, proj, proj, proj, proj, proj, proj, conv_w8, dproj)


def _dsilu(p):
    s = _sigmoid(p)
    return s * (1.0 + p * (1.0 - s))


def _sconv_specs(t, d, col0, nt):
    nb, step = t // 8, ROW_TILE // 8
    win = pl.BlockSpec((ROW_TILE, d), lambda j, i: (i, col0 + j))
    prev = pl.BlockSpec((8, d), lambda j, i: (jnp.maximum(i * step - 1, 0), col0 + j))
    nxt = pl.BlockSpec((8, d), lambda j, i: (jnp.minimum((i + 1) * step, nb - 1), col0 + j))
    return win, prev, nxt


def _sconv_fwd(proj, w8, b8, d, col0, ncol, name):
    t = proj.shape[0]
    nt = t // ROW_TILE

    def body(x_ref, xp_ref, w_ref, b_ref, o_ref):
        xv = x_ref[...].astype(F32)
        xp = jnp.where(pl.program_id(1) == 0, 0.0, xp_ref[...].astype(F32))
        pre = _rows(b_ref, 0) + sum(_rows(w_ref, k) * _shift_down(xv, xp, 3 - k) for k in range(4))
        o_ref[...] = (pre * _sigmoid(pre)).astype(BF16)

    win, prev, _ = _sconv_specs(t, d, col0, nt)
    par = pl.BlockSpec((8, d), lambda j, i: (0, j))
    return pl.pallas_call(
        body, name=name, out_shape=jax.ShapeDtypeStruct((t, ncol * d), BF16), grid=(ncol, nt),
        in_specs=[win, prev, par, par], out_specs=pl.BlockSpec((ROW_TILE, d), lambda j, i: (i, j)),
        compiler_params=_params(("parallel", "parallel")))(proj, proj, w8, b8)


def _sconv_bwd(dxc, proj, w8, b8, dproj, d, col0, ncol, name):
    t = proj.shape[0]
    nt = t // ROW_TILE

    def body(d_ref, dn_ref, x_ref, xp_ref, xn_ref, w_ref, b_ref, _, dp_ref, dw_ref):
        i = pl.program_id(1)
        xv = x_ref[...].astype(F32)
        xp = jnp.where(i == 0, 0.0, xp_ref[...].astype(F32))
        bias = _rows(b_ref, 0)
        sh = [_shift_down(xv, xp, 3 - k) for k in range(4)]
        pre = bias + sum(_rows(w_ref, k) * sh[k] for k in range(4))
        dpre = d_ref[...] * _dsilu(pre)
        xn = xn_ref[...].astype(F32)
        pre_n = bias + sum(_rows(w_ref, k) * _shift_down(xn, xv[ROW_TILE - 8:], 3 - k) for k in range(4))
        dpre_n = jnp.where(i == nt - 1, 0.0, dn_ref[...] * _dsilu(pre_n))
        dx = sum(_rows(w_ref, k) * _shift_up(dpre, dpre_n, 3 - k) for k in range(4))
        dp_ref[...] = dx.astype(BF16)
        upd = jnp.concatenate([jnp.sum(dpre * sh[k], axis=0, keepdims=True) for k in range(4)]
                              + [jnp.sum(dpre, axis=0, keepdims=True), jnp.zeros((3, d), F32)], axis=0)

        @pl.when(i == 0)
        def _():
            dw_ref[...] = upd

        @pl.when(i > 0)
        def _():
            dw_ref[...] += upd

    win, prev, nxt = _sconv_specs(t, d, col0, nt)
    nb, step = t // 8, ROW_TILE // 8
    dwin = pl.BlockSpec((ROW_TILE, d), lambda j, i: (i, j))
    dnxt = pl.BlockSpec((8, d), lambda j, i: (jnp.minimum((i + 1) * step, nb - 1), j))
    par = pl.BlockSpec((8, d), lambda j, i: (0, j))
    return pl.pallas_call(
        body, name=name,
        out_shape=[jax.ShapeDtypeStruct(dproj.shape, BF16), jax.ShapeDtypeStruct((8, ncol * d), F32)],
        grid=(ncol, nt),
        in_specs=[dwin, dnxt, win, prev, nxt, par, par, pl.BlockSpec(memory_space=pl.ANY)],
        out_specs=[win, par], input_output_aliases={7: 0},
        compiler_params=_params(("arbitrary", "arbitrary")))(dxc, dxc, proj, proj, proj, w8, b8, dproj)


def _softplus(x):
    return jnp.maximum(x, 0.0) + jnp.log(1.0 + jnp.exp(-jnp.abs(x)))


def _dt_fwd(proj, par8, nheads, dt_col, name):
    t = proj.shape[0]

    def body(r_ref, p_ref, dt_ref, ac_ref):
        lane = lax.broadcasted_iota(jnp.int32, (1, LANE), 1)
        valid = lane < nheads
        raw = r_ref[...][:, :LANE].astype(F32)
        dt = jnp.where(valid, _softplus(raw + _rows(p_ref, 0)), 0.0)
        a = jnp.where(valid, -jnp.exp(_rows(p_ref, 1)), 0.0)
        dt_ref[...] = dt
        ac_ref[...] = _xdot01(_chunk_tri(ROW_TILE, True), dt * a)

    out = pl.BlockSpec((ROW_TILE, LANE), lambda i: (i, 0))
    return pl.pallas_call(
        body, name=name,
        out_shape=[jax.ShapeDtypeStruct((t, LANE), F32), jax.ShapeDtypeStruct((t, LANE), F32)],
        grid=(t // ROW_TILE,),
        in_specs=[pl.BlockSpec((ROW_TILE, 2 * LANE), lambda i: (i, dt_col)), pl.BlockSpec((8, LANE), lambda i: (0, 0))],
        out_specs=[out, out], compiler_params=_params(("parallel",)))(proj, par8)


def _dt_bwd(ddt_g, dac_g, dt, proj, par8, dproj, nheads, dt_col, name):
    t = proj.shape[0]
    ng = ddt_g.shape[0]

    def body(dd_ref, da_ref, dt_ref, r_ref, p_ref, _, dp_ref, acc_ref):
        lane = lax.broadcasted_iota(jnp.int32, (1, LANE), 1)
        valid = lane < nheads
        ddt = sum(dd_ref[g] for g in range(ng))
        dac = sum(da_ref[g] for g in range(ng))
        a = jnp.where(valid, -jnp.exp(_rows(p_ref, 1)), 0.0)
        d_a = _xdot01(_chunk_tri(ROW_TILE, False), dac)
        ddt = ddt + d_a * a
        raw = r_ref[...][:, :LANE].astype(F32)
        draw = jnp.where(valid, ddt * _sigmoid(raw + _rows(p_ref, 0)), 0.0)
        dp_ref[...] = jnp.concatenate([draw, jnp.zeros_like(draw)], axis=1).astype(BF16)
        upd = jnp.concatenate([jnp.sum(draw, axis=0, keepdims=True),
                               jnp.sum(d_a * dt_ref[...], axis=0, keepdims=True) * a,
                               jnp.zeros((6, LANE), F32)], axis=0)

        @pl.when(pl.program_id(0) == 0)
        def _():
            acc_ref[...] = upd

        @pl.when(pl.program_id(0) > 0)
        def _():
            acc_ref[...] += upd

    row = pl.BlockSpec((ROW_TILE, LANE), lambda i: (i, 0))
    grp = pl.BlockSpec((ng, ROW_TILE, LANE), lambda i: (0, i, 0))
    win = pl.BlockSpec((ROW_TILE, 2 * LANE), lambda i: (i, dt_col))
    par = pl.BlockSpec((8, LANE), lambda i: (0, 0))
    return pl.pallas_call(
        body, name=name,
        out_shape=[jax.ShapeDtypeStruct(dproj.shape, BF16), jax.ShapeDtypeStruct((8, LANE), F32)],
        grid=(t // ROW_TILE,),
        in_specs=[grp, grp, row, win, par, pl.BlockSpec(memory_space=pl.ANY)],
        out_specs=[win, par], input_output_aliases={5: 0},
        compiler_params=_params(("arbitrary",)))(ddt_g, dac_g, dt, proj, par8, dproj)


def _ssd_chunk(xs, bm, cm, z, dt, ac, st, dsk, nw, g):
    q = xs.shape[0]
    lane_h = lax.broadcasted_iota(jnp.int32, (1, LANE), 1)
    sub_h = lax.broadcasted_iota(jnp.int32, (LANE, 1), 0)
    lane_c = lax.broadcasted_iota(jnp.int32, (1, GROUP_W), 1) // HEAD_DIM
    rr = lax.broadcasted_iota(jnp.int32, (q, q), 0)
    cc = lax.broadcasted_iota(jnp.int32, (q, q), 1)
    tril = rr >= cc
    last_row = jnp.where(lax.broadcasted_iota(jnp.int32, (q, 1), 0) == q - 1, 1.0, 0.0)
    act = ac.T
    col, row, dtc, mk = [], [], [], []
    for r in range(HEADS_PER_GROUP):
        h = g * HEADS_PER_GROUP + r
        sel = jnp.where(lane_h == h, 1.0, 0.0)
        col.append(jnp.sum(ac * sel, axis=1, keepdims=True))
        dtc.append(jnp.sum(dt * sel, axis=1, keepdims=True))
        row.append(jnp.sum(act * jnp.where(sub_h == h, 1.0, 0.0), axis=0, keepdims=True))
        mk.append(jnp.where(lane_c == r, 1.0, 0.0))
    acx = sum(mk[r] * col[r] for r in range(HEADS_PER_GROUP))
    dtx = sum(mk[r] * dtc[r] for r in range(HEADS_PER_GROUP))
    x = xs * dtx
    s = _bdot(cm, bm, "nt")
    y = jnp.zeros_like(xs)
    for r in range(HEADS_PER_GROUP):
        decay = jnp.exp(jnp.where(tril, col[r] - row[r], -jnp.inf))
        y = y + mk[r] * _bdot(s * decay, x, "nn")
    y = y + jnp.exp(acx) * _bdot(cm, st, "nn")
    last = jnp.sum(acx * last_row, axis=0, keepdims=True)
    new_st = st * jnp.exp(last) + _bdot(bm, x * jnp.exp(last - acx), "tn")
    y = y + dsk * xs
    y = y * (z * _sigmoid(z))
    yn = y * lax.rsqrt(jnp.mean(y * y, axis=1, keepdims=True) + EPS) * nw
    return yn, new_st


def _ssd_specs(nc, z_col, rev):
    def ci(c):
        return nc - 1 - c if rev else c

    xc = pl.BlockSpec((ROW_TILE, 2 * GROUP_W), lambda g, c: (ci(c), g))
    z = pl.BlockSpec((ROW_TILE, GROUP_W), lambda g, c: (ci(c), z_col + g))
    hd = pl.BlockSpec((ROW_TILE, LANE), lambda g, c: (ci(c), 0))
    par = pl.BlockSpec((8, GROUP_W), lambda g, c: (0, g))
    yb = pl.BlockSpec((ROW_TILE, GROUP_W), lambda g, c: (ci(c), g))
    stb = pl.BlockSpec((None, N_STATE, GROUP_W), lambda g, c: (ci(c), 0, g))
    return xc, z, hd, par, yb, stb


def _ssd_fwd(xc, proj, dt, ac, par8, ngroups, z_col, name):
    t = xc.shape[0]
    nc = t // ROW_TILE

    def body(xc_ref, z_ref, dt_ref, ac_ref, p_ref, y_ref, so_ref, st_ref):
        @pl.when(pl.program_id(1) == 0)
        def _():
            st_ref[...] = jnp.zeros_like(st_ref)

        st = st_ref[...]
        so_ref[...] = st
        blk = xc_ref[...].astype(F32)
        yn, new_st = _ssd_chunk(blk[:, :GROUP_W], blk[:, GROUP_W:GROUP_W + N_STATE], blk[:, GROUP_W + N_STATE:],
                                z_ref[...].astype(F32), dt_ref[...], ac_ref[...], st,
                                _rows(p_ref, 0), _rows(p_ref, 1), pl.program_id(0))
        y_ref[...] = yn.astype(BF16)
        st_ref[...] = new_st

    xcs, zs, hd, par, yb, stb = _ssd_specs(nc, z_col, False)
    return pl.pallas_call(
        body, name=name,
        out_shape=[jax.ShapeDtypeStruct((t, ngroups * GROUP_W), BF16),
                   jax.ShapeDtypeStruct((nc, N_STATE, ngroups * GROUP_W), F32)],
        grid=(ngroups, nc), in_specs=[xcs, zs, hd, hd, par], out_specs=[yb, stb],
        scratch_shapes=[pltpu.VMEM((N_STATE, GROUP_W), F32)],
        compiler_params=_params(("parallel", "arbitrary")))(xc, proj, dt, ac, par8)


def _ssd_bwd(dyn, xc, proj, dt, ac, par8, states, dproj, ngroups, z_col, name):
    t = xc.shape[0]
    nc = t // ROW_TILE

    def body(dy_ref, xc_ref, z_ref, dt_ref, ac_ref, p_ref, s_ref, _, dz_ref, dxc_ref, ddt_ref, dac_ref, dp_ref,
             ds_ref):
        c = pl.program_id(1)

        @pl.when(c == 0)
        def _():
            ds_ref[...] = jnp.zeros_like(ds_ref)

        g = pl.program_id(0)
        blk = xc_ref[...].astype(F32)
        args = (blk[:, :GROUP_W], blk[:, GROUP_W:GROUP_W + N_STATE], blk[:, GROUP_W + N_STATE:],
                z_ref[...].astype(F32), dt_ref[...], ac_ref[...], s_ref[...], _rows(p_ref, 0), _rows(p_ref, 1))
        _, vjp = jax.vjp(lambda *a: _ssd_chunk(*a, g), *args)
        dxs, dbm, dcm, dz, ddt, dac, dst, ddsk, dnw = vjp((dy_ref[...], ds_ref[...]))
        ds_ref[...] = dst
        dz_ref[...] = dz.astype(BF16)
        dxc_ref[...] = jnp.concatenate([dxs, dbm, dcm], axis=1)
        ddt_ref[...] = ddt
        dac_ref[...] = dac
        upd = jnp.concatenate([ddsk, dnw, jnp.zeros((6, GROUP_W), F32)], axis=0)

        @pl.when(c == 0)
        def _():
            dp_ref[...] = upd

        @pl.when(c > 0)
        def _():
            dp_ref[...] += upd

    xcs, zs, hd, par, yb, stb = _ssd_specs(nc, z_col, True)
    dxcs = pl.BlockSpec((ROW_TILE, 2 * GROUP_W), lambda g, c: (nc - 1 - c, g))
    hg = pl.BlockSpec((None, ROW_TILE, LANE), lambda g, c: (g, nc - 1 - c, 0))
    return pl.pallas_call(
        body, name=name,
        out_shape=[jax.ShapeDtypeStruct(dproj.shape, BF16),
                   jax.ShapeDtypeStruct((t, ngroups * 2 * GROUP_W), F32),
                   jax.ShapeDtypeStruct((ngroups, t, LANE), F32), jax.ShapeDtypeStruct((ngroups, t, LANE), F32),
                   jax.ShapeDtypeStruct((8, ngroups * GROUP_W), F32)],
        grid=(ngroups, nc),
        in_specs=[yb, xcs, zs, hd, hd, par, stb, pl.BlockSpec(memory_space=pl.ANY)],
        out_specs=[zs, dxcs, hg, hg, par], input_output_aliases={7: 0},
        scratch_shapes=[pltpu.VMEM((N_STATE, GROUP_W), F32)],
        compiler_params=_params(("arbitrary", "arbitrary")))(dyn, xc, proj, dt, ac, par8, states, dproj)


def _small_matmul(a, b, mode, name):
    if mode == "nn":
        shape = (a.shape[0], b.shape[1])
    else:
        shape = (a.shape[1], b.shape[1])

    def body(a_ref, b_ref, o_ref):
        o_ref[...] = _dot(a_ref[...], b_ref[...], mode)

    return pl.pallas_call(body, name=name, out_shape=jax.ShapeDtypeStruct(shape, F32),
                          compiler_params=_params())(a, b)


def _adamw(parts, w, m, v, name):
    npart, rows, cols = parts.shape
    tr = rows if rows <= 128 else _tile8(rows, 128)

    def body(p_ref, w_ref, m_ref, v_ref, g_ref, d_ref, nm_ref, nv_ref):
        g = p_ref[0]
        for j in range(1, npart):
            g = g + p_ref[j]
        mm = ADAM_B1 * m_ref[...] + (1.0 - ADAM_B1) * g
        vv = ADAM_B2 * v_ref[...] + (1.0 - ADAM_B2) * (g * g)
        m_hat = mm / (1.0 - ADAM_B1 ** ADAM_STEP)
        v_hat = vv / (1.0 - ADAM_B2 ** ADAM_STEP)
        g_ref[...] = g
        d_ref[...] = -ADAM_LR * (m_hat / (jnp.sqrt(v_hat) + ADAM_EPS) + ADAM_WD * w_ref[...])
        nm_ref[...] = mm
        nv_ref[...] = vv

    blk = pl.BlockSpec((tr, cols), lambda i: (i, 0))
    out = jax.ShapeDtypeStruct((rows, cols), F32)
    return pl.pallas_call(
        body, name=name, out_shape=[out] * 4, grid=(rows // tr,),
        in_specs=[pl.BlockSpec((npart, tr, cols), lambda i: (0, i, 0)), blk, blk, blk], out_specs=[blk] * 4,
        compiler_params=_params(("parallel",)))(parts, w, m, v)


def _tile8(n, target):
    best = 8
    for d in range(8, target + 1, 8):
        if n % d == 0:
            best = d
    return best


def _pad_rows8(a):
    return jnp.concatenate([a, jnp.zeros((8 - a.shape[0], a.shape[1]), a.dtype)], axis=0)


def _pad_lanes(a, n):
    return jnp.concatenate([a, jnp.zeros(a.shape[:-1] + (n - a.shape[-1],), a.dtype)], axis=-1)


def _pack(vecs, mult):
    flat = jnp.concatenate([v.reshape(-1) for v in vecs])
    pad = (-flat.shape[0]) % mult
    flat = jnp.concatenate([flat, jnp.zeros((pad,), flat.dtype)])
    return flat.reshape(-1, LANE)


def _unpack(flat, shapes):
    out, o = [], 0
    for s in shapes:
        n = int(np.prod(s))
        out.append(flat[o:o + n].reshape(s))
        o += n
    return out


def _cols_to_blocks(a):
    w = a.shape[-1] // N_DEV
    return jnp.moveaxis(a.reshape(a.shape[:-1] + (N_DEV, w)), -2, 0)


def _rows_to_blocks(a):
    l, rows, n = a.shape
    return jnp.moveaxis(a.reshape(l, N_DEV, rows // N_DEV, n), 1, 0)


def kernel(x, c, w_ada, b_ada, ln1, ln2, w_in, conv_w, ssm_conv_w, ssm_conv_b, dt_bias, a_log, d_skip, ssm_norm_w, w_conv_out, w_ssm_out, w_o, w_up, w_down, final_norm, loss_target, m_w_ada, m_b_ada, m_ln1, m_ln2, m_w_in, m_conv_w, m_ssm_conv_w, m_ssm_conv_b, m_dt_bias, m_a_log, m_d_skip, m_ssm_norm_w, m_w_conv_out, m_w_ssm_out, m_w_o, m_w_up, m_w_down, m_final_norm, v_w_ada, v_b_ada, v_ln1, v_ln2, v_w_in, v_conv_w, v_ssm_conv_w, v_ssm_conv_b, v_dt_bias, v_a_log, v_d_skip, v_ssm_norm_w, v_w_conv_out, v_w_ssm_out, v_w_o, v_w_up, v_w_down, v_final_norm):
    names = ["w_ada", "b_ada", "ln1", "ln2", "w_in", "conv_w", "ssm_conv_w", "ssm_conv_b", "dt_bias", "a_log",
             "d_skip", "ssm_norm_w", "w_conv_out", "w_ssm_out", "w_o", "w_up", "w_down", "final_norm"]
    w_of = dict(zip(names, [w_ada, b_ada, ln1, ln2, w_in, conv_w, ssm_conv_w, ssm_conv_b, dt_bias, a_log, d_skip,
                            ssm_norm_w, w_conv_out, w_ssm_out, w_o, w_up, w_down, final_norm]))
    m_of = dict(zip(names, [m_w_ada, m_b_ada, m_ln1, m_ln2, m_w_in, m_conv_w, m_ssm_conv_w, m_ssm_conv_b, m_dt_bias,
                            m_a_log, m_d_skip, m_ssm_norm_w, m_w_conv_out, m_w_ssm_out, m_w_o, m_w_up, m_w_down,
                            m_final_norm]))
    v_of = dict(zip(names, [v_w_ada, v_b_ada, v_ln1, v_ln2, v_w_in, v_conv_w, v_ssm_conv_w, v_ssm_conv_b, v_dt_bias,
                            v_a_log, v_d_skip, v_ssm_norm_w, v_w_conv_out, v_w_ssm_out, v_w_o, v_w_up, v_w_down,
                            v_final_norm]))

    _, t, d = x.shape
    nl = w_ada.shape[0]
    ada_w = w_ada.shape[2]
    ds = ssm_norm_w.shape[1]
    nh = dt_bias.shape[1]
    ng = nh // HEADS_PER_GROUP
    gn = ng * N_STATE
    xbc_w = ds + 2 * gn
    z_off, xbc_off = 5 * d, 5 * d + ds
    dt_off = xbc_off + xbc_w
    proj_w = dt_off + nh
    pw = dt_off + 2 * LANE
    me = 4 * lax.axis_index("x") + 2 * lax.axis_index("y") + lax.axis_index("c")

    perm = []
    for g in range(ng):
        perm += [2 * g, 2 * g + 1, ds // LANE + g, ds // LANE + ng + g]
    perm = np.array(perm)
    inv_perm = np.argsort(perm)

    def to_group_major(a):
        return a.reshape(a.shape[:-1] + (xbc_w // LANE, LANE))[..., perm, :].reshape(a.shape)

    def from_group_major(a):
        return a.reshape(a.shape[:-1] + (xbc_w // LANE, LANE))[..., inv_perm, :].reshape(a.shape)

    x2, tgt = x[0], loss_target[0]

    c_act = c * jax.nn.sigmoid(c)
    sizes1 = [(1, d), conv_w.shape, ssm_conv_w.shape]
    (g1,) = _all_gather([_pack([c_act, conv_w, ssm_conv_w], LANE)], "ag_small_in")
    parts1 = [_unpack(g1[j].reshape(-1), sizes1) for j in range(N_DEV)]
    c_act_all = jnp.concatenate([p[0] for p in parts1], axis=0)
    conv_w_full = jnp.concatenate([p[1] for p in parts1], axis=-1)
    sconv_w_full = to_group_major(jnp.concatenate([p[2] for p in parts1], axis=-1))
    sconv_b_gm = to_group_major(ssm_conv_b)

    mod_part = jnp.stack([_small_matmul(c_act_all, w_ada[l], "nn", f"ada_fwd{l}") for l in range(nl)])
    (gmod,) = _all_gather([mod_part], "ag_mod")
    mod = lax.dynamic_index_in_dim(gmod, me, axis=2, keepdims=False)
    mod = jnp.moveaxis(mod, 0, 1).reshape(nl, N_DEV * ada_w) + b_ada
    mod = mod.reshape(nl, 6, d)
    modp = [jnp.concatenate([mod[l], ln1[l][None], ln2[l][None]], axis=0) for l in range(nl)]

    big = ["w_in", "w_up", "w_conv_out", "w_ssm_out", "w_o", "w_down"]
    gathered = dict(zip(big, _all_gather([w_of[k].astype(BF16) for k in big], "ag_weights")))

    def full_cols(g):
        return jnp.moveaxis(g, 0, 2).reshape(g.shape[1], g.shape[2], N_DEV * g.shape[3])

    def full_rows(g):
        return jnp.moveaxis(g, 0, 1).reshape(g.shape[1], N_DEV * g.shape[2], g.shape[3])

    wi = full_cols(gathered["w_in"])
    wi = jnp.concatenate([wi[..., :xbc_off], to_group_major(wi[..., xbc_off:dt_off]), wi[..., dt_off:],
                          jnp.zeros((nl, d, pw - proj_w), BF16)], axis=-1)
    wup = full_cols(gathered["w_up"])
    wco, wso = full_rows(gathered["w_conv_out"]), full_rows(gathered["w_ssm_out"])
    wo, wdown = full_rows(gathered["w_o"]), full_rows(gathered["w_down"])

    cw8 = [_pad_rows8(conv_w_full[l]) for l in range(nl)]
    sw8 = [_pad_rows8(sconv_w_full[l]) for l in range(nl)]
    sb8 = [_pad_rows8(sconv_b_gm[l][None]) for l in range(nl)]
    dtp8 = [_pad_rows8(_pad_lanes(jnp.stack([dt_bias[l], a_log[l]]), LANE)) for l in range(nl)]
    sp8 = [_pad_rows8(jnp.stack([jnp.repeat(d_skip[l], HEAD_DIM), ssm_norm_w[l]])) for l in range(nl)]
    fn8 = _pad_rows8(final_norm[None])
    xbc_col, ncol, dt_col, z_col = xbc_off // d, xbc_w // d, dt_off // (2 * LANE), z_off // GROUP_W

    saved = []
    x_cur, br_prev = x2, None
    for l in range(nl):
        if l == 0:
            x_in, u1 = x_cur, _norm_fwd(x_cur, modp[0], 0, "norm_first")
        else:
            x_in, u1 = _resid_norm_fwd(x_cur, br_prev, modp[l - 1], 5, modp[l], 0, f"resid_norm_a{l}")
        proj = _matmul(u1, wi[l], "nn", BF16, f"mm_in{l}")
        y_conv = _gconv_fwd(proj, cw8[l], d, f"gconv_fwd{l}")
        xc = _sconv_fwd(proj, sw8[l], sb8[l], d, xbc_col, ncol, f"sconv_fwd{l}")
        dt, ac = _dt_fwd(proj, dtp8[l], nh, dt_col, f"dt_fwd{l}")
        yn, states = _ssd_fwd(xc, proj, dt, ac, sp8[l], ng, z_col, f"ssd_fwd{l}")
        p_conv = _matmul(y_conv, wco[l], "nn", F32, f"mm_conv_out{l}")
        p_ssm = _matmul(yn, wso[l], "nn", F32, f"mm_ssm_out{l}")
        merged = _merge_fwd(proj, p_conv, p_ssm, f"merge_fwd{l}")
        mix = _matmul(merged, wo[l], "nn", F32, f"mm_o{l}")
        x_mid, u2 = _resid_norm_fwd(x_in, mix, modp[l], 2, modp[l], 1, f"resid_norm_b{l}")
        h = _matmul(u2, wup[l], "nn", BF16, f"mm_up{l}")
        hid = _relu2_fwd(h, f"relu2_fwd{l}")
        mlp = _matmul(hid, wdown[l], "nn", F32, f"mm_down{l}")
        saved.append(dict(x_in=x_in, u1=u1, proj=proj, y_conv=y_conv, xc=xc, dt=dt, ac=ac, yn=yn, states=states,
                          p_conv=p_conv, p_ssm=p_ssm, merged=merged, mix=mix, x_mid=x_mid, u2=u2, h=h, hid=hid,
                          mlp=mlp))
        x_cur, br_prev = x_mid, mlp

    dx, dbr, acc = _final_fwd_bwd(x_cur, br_prev, modp[nl - 1], fn8, tgt, "final")
    loss = lax.psum(acc[0, 0], ("x", "y", "c"))
    g_final_norm = acc[1]
    dgate2 = acc[2]

    gw = {k: [None] * nl for k in big}
    small = {k: [None] * nl for k in ["mod", "ln1", "ln2", "conv_w", "ssm_conv_w", "ssm_conv_b", "dt_bias",
                                      "a_log", "d_skip", "ssm_norm_w"]}
    for l in reversed(range(nl)):
        s = saved[l]
        dhid = _matmul(dbr, wdown[l], "nt", F32, f"mm_down_dx{l}")
        gw["w_down"][l] = _matmul(s["hid"], dbr, "tn", F32, f"mm_down_dw{l}")
        dh = _relu2_bwd(dhid, s["h"], f"relu2_bwd{l}")
        du2 = _matmul(dh, wup[l], "nt", F32, f"mm_up_dx{l}")
        gw["w_up"][l] = _matmul(s["u2"], dh, "tn", F32, f"mm_up_dw{l}")
        dx_mid, dmix, acc2 = _resid_norm_bwd(dx, du2, s["x_mid"], modp[l], 1, f"resid_norm_b_bwd{l}",
                                             br=s["mix"], mp_gate=modp[l], gate_row=2)
        dmerged = _matmul(dmix, wo[l], "nt", F32, f"mm_o_dx{l}")
        gw["w_o"][l] = _matmul(s["merged"], dmix, "tn", F32, f"mm_o_dw{l}")
        dpc, dps, dproj = _merge_bwd(dmerged, s["proj"], s["p_conv"], s["p_ssm"], f"merge_bwd{l}")
        dyc = _matmul(dpc, wco[l], "nt", F32, f"mm_conv_out_dx{l}")
        gw["w_conv_out"][l] = _matmul(s["y_conv"], dpc, "tn", F32, f"mm_conv_out_dw{l}")
        dyn = _matmul(dps, wso[l], "nt", F32, f"mm_ssm_out_dx{l}")
        gw["w_ssm_out"][l] = _matmul(s["yn"], dps, "tn", F32, f"mm_ssm_out_dw{l}")
        dproj, dxc, ddt_g, dac_g, dsp = _ssd_bwd(dyn, s["xc"], s["proj"], s["dt"], s["ac"], sp8[l], s["states"],
                                                 dproj, ng, z_col, f"ssd_bwd{l}")
        dproj, dcw = _gconv_bwd(dyc, s["proj"], cw8[l], dproj, d, f"gconv_bwd{l}")
        dproj, dsw = _sconv_bwd(dxc, s["proj"], sw8[l], sb8[l], dproj, d, xbc_col, ncol, f"sconv_bwd{l}")
        dproj, ddtp = _dt_bwd(ddt_g, dac_g, s["dt"], s["proj"], dtp8[l], dproj, nh, dt_col, f"dt_bwd{l}")
        du1 = _matmul(dproj, wi[l], "nt", F32, f"mm_in_dx{l}")
        gw["w_in"][l] = _matmul(s["u1"], dproj, "tn", F32, f"mm_in_dw{l}")
        if l > 0:
            dx, dbr, acc1 = _resid_norm_bwd(dx_mid, du1, s["x_in"], modp[l], 0, f"resid_norm_a_bwd{l}",
                                            br=saved[l - 1]["mlp"], mp_gate=modp[l - 1], gate_row=5)
        else:
            dx, acc1 = _resid_norm_bwd(dx_mid, du1, s["x_in"], modp[0], 0, "norm_first_bwd")
        sc1, sc2 = mod[l, 1], mod[l, 4]
        small["mod"][l] = jnp.stack([acc1[0], acc1[1] * ln1[l], acc2[2], acc2[0], acc2[1] * ln2[l], dgate2])
        small["ln1"][l] = acc1[1] * (1.0 + sc1)
        small["ln2"][l] = acc2[1] * (1.0 + sc2)
        small["conv_w"][l] = dcw[:3]
        small["ssm_conv_w"][l] = from_group_major(dsw[:4])
        small["ssm_conv_b"][l] = from_group_major(dsw[4])
        small["dt_bias"][l] = ddtp[0, :nh]
        small["a_log"][l] = ddtp[1, :nh]
        small["d_skip"][l] = dsp[0].reshape(nh, HEAD_DIM).sum(axis=-1)
        small["ssm_norm_w"][l] = dsp[1]
        if l > 0:
            dgate2 = acc1[2]
    grad_x = dx[None]

    sm = {k: jnp.stack(v) for k, v in small.items()}
    rep_names = ["b_ada", "ln1", "ln2", "ssm_conv_b", "dt_bias", "a_log", "d_skip", "ssm_norm_w", "final_norm"]
    rep_grads = [sm["mod"].reshape(nl, 6 * d), sm["ln1"], sm["ln2"], sm["ssm_conv_b"], sm["dt_bias"], sm["a_log"],
                 sm["d_skip"], sm["ssm_norm_w"], g_final_norm]
    rep_pack = _pack(rep_grads, 8 * LANE)
    conv_pack = _pack([sm["conv_w"], sm["ssm_conv_w"]], 8 * LANE)
    g_rep, g_conv = _all_gather([rep_pack, conv_pack], "ag_small_grads")

    outs = {}

    def run_adamw(name, parts, tag):
        w, m, v = w_of[name], m_of[name], v_of[name]
        shp = w.shape
        r2 = (int(np.prod(shp[:-1])), shp[-1])
        res = _adamw(parts.reshape((parts.shape[0],) + r2), w.reshape(r2), m.reshape(r2), v.reshape(r2), tag)
        outs[name] = [a.reshape(shp) for a in res]

    rep_shapes = [w_of[k].shape for k in rep_names]
    res = _adamw(g_rep, _pack([w_of[k] for k in rep_names], 8 * LANE), _pack([m_of[k] for k in rep_names], 8 * LANE),
                 _pack([v_of[k] for k in rep_names], 8 * LANE), "adamw_replicated")
    for k, vals in zip(rep_names, zip(*[_unpack(a.reshape(-1), rep_shapes) for a in res])):
        outs[k] = list(vals)

    conv_parts = [_unpack(g_conv[j].reshape(-1), [sm["conv_w"].shape, sm["ssm_conv_w"].shape]) for j in range(N_DEV)]
    for idx, name in enumerate(["conv_w", "ssm_conv_w"]):
        wsh = w_of[name].shape[-1]
        full = jnp.stack([p[idx] for p in conv_parts])
        run_adamw(name, lax.dynamic_slice_in_dim(full, me * wsh, wsh, axis=3), "adamw_" + name)

    dmod_all = g_rep.reshape(N_DEV, -1)[:, :nl * 6 * d].reshape(N_DEV, nl, 6 * d)
    dmod_mine = lax.dynamic_slice_in_dim(dmod_all, me * ada_w, ada_w, axis=2)
    g_ada = jnp.stack([_small_matmul(c_act_all, dmod_mine[:, l], "tn", f"ada_bwd{l}") for l in range(nl)])
    run_adamw("w_ada", g_ada[None], "adamw_w_ada")

    g_in = jnp.stack(gw["w_in"])
    g_in = jnp.concatenate([g_in[..., :xbc_off], from_group_major(g_in[..., xbc_off:dt_off]),
                            g_in[..., dt_off:proj_w]], axis=-1)
    send = [_cols_to_blocks(g_in), _cols_to_blocks(jnp.stack(gw["w_up"]))]
    send += [_rows_to_blocks(jnp.stack(gw[k])) for k in ["w_conv_out", "w_ssm_out", "w_o", "w_down"]]
    recv = _all_to_all(send, "a2a_grads")
    for name, parts in zip(big, recv):
        run_adamw(name, parts, "adamw_" + name)

    result = [loss, grad_x]
    for i in range(4):
        result += [outs[k][i] for k in names]
    return tuple(result)
```

```python
import functools

import numpy as np
import jax
import jax.numpy as jnp
from jax import lax
from jax.experimental import pallas as pl
from jax.experimental.pallas import tpu as pltpu

F32 = jnp.float32
BF16 = jnp.bfloat16
EPS = 1e-6
N_STATE = 128
HEAD_DIM = 64
HEADS_PER_GROUP = 4
GROUP_W = HEAD_DIM * HEADS_PER_GROUP
N_DEV = 8
ROW_TILE = 256
LANE = 128
VMEM_LIMIT = 56 * 1024 * 1024

ADAM_LR, ADAM_B1, ADAM_B2, ADAM_EPS, ADAM_WD, ADAM_STEP = 0.001, 0.9, 0.999, 1e-08, 0.01, 10

MESH = pl.DeviceIdType.MESH


def _params(sem=None):
    return pltpu.CompilerParams(dimension_semantics=sem, vmem_limit_bytes=VMEM_LIMIT)


def _tile(n, target):
    if n <= target:
        return n
    best = None
    for d in range(LANE, target + 1, LANE):
        if n % d == 0:
            best = d
    assert best is not None, (n, target)
    return best


def _slot(p):
    return 4 * p[0] + 2 * p[1] + p[2]


def _all_gather(xs, name):
    comm = _GatherComm(xs)

    def body(*refs):
        begin, end = comm.ops(refs[:comm.n_in], refs[comm.n_in:comm.n_in + comm.n_out], refs[-3:])
        begin()
        end()

    any_spec = pl.BlockSpec(memory_space=pl.ANY)
    return pl.pallas_call(
        body, name=name, out_shape=comm.out_shape,
        in_specs=[any_spec] * comm.n_in, out_specs=[any_spec] * comm.n_out, scratch_shapes=comm.sems,
    )(*comm.operands)


class _GatherComm:
    def __init__(self, xs):
        n = len(xs)
        self.operands = list(xs)
        self.n_in = self.n_out = n
        self.aliases = {}
        self.out_shape = [jax.ShapeDtypeStruct((N_DEV,) + x.shape, x.dtype) for x in xs]
        self.sems = [pltpu.SemaphoreType.DMA((7 * n,)), pltpu.SemaphoreType.DMA((7 * n,)),
                     pltpu.SemaphoreType.DMA((n,))]

    def ops(self, x_refs, o_refs, sem_refs):
        n = self.n_in
        send, recv, loc = sem_refs
        x, y, c = lax.axis_index("x"), lax.axis_index("y"), lax.axis_index("c")
        me, sib = (x, y, c), (x, y, 1 - c)
        chips = [(1 - x, y), (x, 1 - y), (1 - x, 1 - y)]

        def cp(a, k, block, to, src=None):
            dst = o_refs[a].at[_slot(block)]
            return pltpu.make_async_remote_copy(
                src_ref=dst if src is None else src, dst_ref=dst,
                send_sem=send.at[7 * a + k], recv_sem=recv.at[7 * a + k],
                device_id=to, device_id_type=MESH)

        mine = [pltpu.make_async_copy(x_refs[a], o_refs[a].at[_slot(me)], loc.at[a]) for a in range(n)]
        first = []
        for a in range(n):
            first.append(cp(a, 0, me, sib, src=x_refs[a]))
            first += [cp(a, 1 + j, me, (*chip, c), src=x_refs[a]) for j, chip in enumerate(chips)]

        def begin():
            for m in mine:
                m.start()
            for f in first:
                f.start()

        def end():
            passed = []
            for j, chip in enumerate(chips):
                for a in range(n):
                    cp(a, 1 + j, (*chip, c), me).wait_recv()
                    p = cp(a, 4 + j, (*chip, c), sib)
                    p.start()
                    passed.append(p)
            for a in range(n):
                cp(a, 0, sib, me).wait_recv()
                for j, chip in enumerate(chips):
                    cp(a, 4 + j, (*chip, 1 - c), me).wait_recv()
            for f in first + passed:
                f.wait_send()
            for m in mine:
                m.wait()

        return begin, end


class _ExchangeComm:
    def __init__(self, xs, bufs, layer):
        n = len(xs)
        self.layer = layer
        self.operands = list(xs) + list(bufs)
        self.n_in, self.n_out = 2 * n, n
        self.aliases = {n + a: a for a in range(n)}
        self.out_shape = [jax.ShapeDtypeStruct(b.shape, b.dtype) for b in bufs]
        self.sems = [pltpu.SemaphoreType.DMA((7 * n,)), pltpu.SemaphoreType.DMA((7 * n,)),
                     pltpu.SemaphoreType.DMA((n,))]

    def ops(self, in_refs, o_refs, sem_refs):
        n = self.n_out
        x_refs = in_refs[:n]
        send, recv, loc = sem_refs
        layer = self.layer
        x, y, c = lax.axis_index("x"), lax.axis_index("y"), lax.axis_index("c")
        me = (x, y, c)
        peers = []
        for k in range(1, 8):
            kx, ky, kc = (k >> 2) & 1, (k >> 1) & 1, k & 1
            peers.append((x + kx - 2 * x * kx, y + ky - 2 * y * ky, c + kc - 2 * c * kc))

        def cp(a, k, src_slot, dst_slot, to):
            return pltpu.make_async_remote_copy(
                src_ref=x_refs[a].at[src_slot], dst_ref=o_refs[a].at[dst_slot, layer],
                send_sem=send.at[7 * a + k], recv_sem=recv.at[7 * a + k],
                device_id=to, device_id_type=MESH)

        mine = [pltpu.make_async_copy(x_refs[a].at[_slot(me)], o_refs[a].at[_slot(me), layer], loc.at[a])
                for a in range(n)]
        sends = [cp(a, k, _slot(p), _slot(me), p) for a in range(n) for k, p in enumerate(peers)]

        def begin():
            for m in mine:
                m.start()
            for s in sends:
                s.start()

        def end():
            for a in range(n):
                for k, p in enumerate(peers):
                    cp(a, k, _slot(me), _slot(p), me).wait_recv()
            for s in sends:
                s.wait_send()
            for m in mine:
                m.wait()

        return begin, end


def _exchange(xs, bufs, layer, name):
    comm = _ExchangeComm(xs, bufs, layer)

    def body(*refs):
        begin, end = comm.ops(refs[:comm.n_in], refs[comm.n_in:comm.n_in + comm.n_out], refs[-3:])
        begin()
        end()

    any_spec = pl.BlockSpec(memory_space=pl.ANY)
    return pl.pallas_call(
        body, name=name, out_shape=comm.out_shape,
        in_specs=[any_spec] * comm.n_in, out_specs=[any_spec] * comm.n_out, scratch_shapes=comm.sems,
        input_output_aliases=dict(comm.aliases),
    )(*comm.operands)


_DIMS = {"nn": (((1,), (0,)), ((), ())), "nt": (((1,), (1,)), ((), ())), "tn": (((0,), (0,)), ((), ()))}


def _matmul(a, b, mode, out_dtype, name, comm=None):
    if mode == "nn":
        (m, k), (_, n) = a.shape, b.shape
    elif mode == "nt":
        (m, k), (n, _) = a.shape, b.shape
    else:
        (k, m), (_, n) = a.shape, b.shape
    tm, tn, tk = _tile(m, 1024), _tile(n, 1024), _tile(k, 1024)
    nk = k // tk
    grid = (m // tm, n // tn, nk)
    dn = _DIMS[mode]
    n_cin = comm.n_in if comm else 0
    n_cout = comm.n_out if comm else 0
    n_acc = 0 if nk == 1 else 1

    def body(*refs):
        a_ref, b_ref = refs[:2]
        o_ref = refs[2 + n_cin]
        scratch = refs[3 + n_cin + n_cout:]
        pid = [pl.program_id(ax) for ax in range(3)]
        if comm:
            begin, end = comm.ops(refs[2:2 + n_cin], refs[3 + n_cin:3 + n_cin + n_cout], scratch[n_acc:])
            pl.when((pid[0] == 0) & (pid[1] == 0) & (pid[2] == 0))(begin)
        part = lax.dot_general(a_ref[...], b_ref[...], dn, preferred_element_type=F32)
        if nk == 1:
            o_ref[...] = part.astype(o_ref.dtype)
        else:
            acc_ref = scratch[0]

            @pl.when(pid[2] == 0)
            def _():
                acc_ref[...] = part

            @pl.when(pid[2] > 0)
            def _():
                acc_ref[...] += part

            @pl.when(pid[2] == nk - 1)
            def _():
                o_ref[...] = acc_ref[...].astype(o_ref.dtype)
        if comm:
            pl.when((pid[0] == grid[0] - 1) & (pid[1] == grid[1] - 1) & (pid[2] == grid[2] - 1))(end)

    a_spec = pl.BlockSpec((tk, tm), lambda i, j, kk: (kk, i)) if mode == "tn" else \
        pl.BlockSpec((tm, tk), lambda i, j, kk: (i, kk))
    b_spec = pl.BlockSpec((tn, tk), lambda i, j, kk: (j, kk)) if mode == "nt" else \
        pl.BlockSpec((tk, tn), lambda i, j, kk: (kk, j))
    any_spec = pl.BlockSpec(memory_space=pl.ANY)
    res = pl.pallas_call(
        body, name=name,
        out_shape=[jax.ShapeDtypeStruct((m, n), out_dtype)] + (comm.out_shape if comm else []),
        grid=grid,
        in_specs=[a_spec, b_spec] + [any_spec] * n_cin,
        out_specs=[pl.BlockSpec((tm, tn), lambda i, j, kk: (i, j))] + [any_spec] * n_cout,
        scratch_shapes=([] if nk == 1 else [pltpu.VMEM((tm, tn), F32)]) + (comm.sems if comm else []),
        input_output_aliases={2 + i: 1 + o for i, o in comm.aliases.items()} if comm else {},
        compiler_params=_params(("arbitrary",) * 3 if comm else ("parallel", "parallel", "arbitrary")),
    )(a, b, *(comm.operands if comm else []))
    return (res[0], list(res[1:])) if comm else res[0]


def _dot(a, b, mode):
    return lax.dot_general(a.astype(BF16), b.astype(BF16), _DIMS[mode], preferred_element_type=F32)


@functools.partial(jax.custom_vjp, nondiff_argnums=(2,))
def _bdot(a, b, mode):
    return _dot(a, b, mode)


def _bdot_fwd(a, b, mode):
    return _dot(a, b, mode), (a, b)


def _bdot_bwd(mode, res, g):
    a, b = res
    if mode == "nn":
        return _dot(g, b, "nt"), _dot(a, g, "tn")
    if mode == "nt":
        return _dot(g, b, "nn"), _dot(g, a, "tn")
    return _dot(b, g, "nt"), _dot(a, g, "nn")


_bdot.defvjp(_bdot_fwd, _bdot_bwd)


def _xdot01(m01, x):
    hi = x.astype(BF16)
    r1 = x - hi.astype(F32)
    mid = r1.astype(BF16)
    lo = (r1 - mid.astype(F32)).astype(BF16)
    dn = _DIMS["nn"]
    return (lax.dot_general(m01, hi, dn, preferred_element_type=F32)
            + lax.dot_general(m01, mid, dn, preferred_element_type=F32)
            + lax.dot_general(m01, lo, dn, preferred_element_type=F32))


def _chunk_tri(n, lower):
    r = lax.broadcasted_iota(jnp.int32, (n, n), 0)
    c = lax.broadcasted_iota(jnp.int32, (n, n), 1)
    return jnp.where((r >= c) if lower else (r <= c), 1.0, 0.0).astype(BF16)


def _shift_down(x, prev8, k):
    if k == 0:
        return x
    n = x.shape[0]
    r = pltpu.roll(x, k, 0)
    rp = pltpu.roll(prev8, k, 0)
    rows = lax.broadcasted_iota(jnp.int32, (8, x.shape[1]), 0)
    head = jnp.where(rows < k, rp, r[:8])
    return head if n == 8 else jnp.concatenate([head, r[8:]], axis=0)


def _shift_up(x, next8, k):
    if k == 0:
        return x
    n = x.shape[0]
    r = pltpu.roll(x, n - k, 0)
    rn = pltpu.roll(next8, 8 - k, 0)
    rows = lax.broadcasted_iota(jnp.int32, (8, x.shape[1]), 0)
    tail = jnp.where(rows >= 8 - k, rn, r[n - 8:])
    return tail if n == 8 else jnp.concatenate([r[:n - 8], tail], axis=0)


def _sigmoid(x):
    return 0.5 * jnp.tanh(0.5 * x) + 0.5


def _rows(ref, i):
    return ref[i:i + 1, :]


def _norm_parts(x, mp_ref, which):
    ln, sh, sc = _rows(mp_ref, 6 + which), _rows(mp_ref, 3 * which), _rows(mp_ref, 3 * which + 1)
    r = lax.rsqrt(jnp.mean(x * x, axis=1, keepdims=True) + EPS)
    return r, ln, sh, sc


def _norm_fwd(x, modp, which, name):
    t, d = x.shape

    def body(x_ref, mp_ref, u_ref):
        xv = x_ref[...]
        r, ln, sh, sc = _norm_parts(xv, mp_ref, which)
        u_ref[...] = (((xv * r) * ln) * (1.0 + sc) + sh).astype(BF16)

    row = pl.BlockSpec((ROW_TILE, d), lambda i: (i, 0))
    return pl.pallas_call(
        body, name=name, out_shape=jax.ShapeDtypeStruct((t, d), BF16), grid=(t // ROW_TILE,),
        in_specs=[row, pl.BlockSpec((8, d), lambda i: (0, 0))], out_specs=row,
        compiler_params=_params(("parallel",)))(x, modp)


def _resid_norm_fwd(x, br, mp_gate, gate_row, mp_norm, which, name):
    t, d = x.shape

    def body(x_ref, br_ref, mg_ref, mn_ref, xn_ref, u_ref):
        xv = x_ref[...] + _rows(mg_ref, gate_row) * br_ref[...]
        xn_ref[...] = xv
        r, ln, sh, sc = _norm_parts(xv, mn_ref, which)
        u_ref[...] = (((xv * r) * ln) * (1.0 + sc) + sh).astype(BF16)

    row = pl.BlockSpec((ROW_TILE, d), lambda i: (i, 0))
    mp = pl.BlockSpec((8, d), lambda i: (0, 0))
    return pl.pallas_call(
        body, name=name,
        out_shape=[jax.ShapeDtypeStruct((t, d), F32), jax.ShapeDtypeStruct((t, d), BF16)],
        grid=(t // ROW_TILE,), in_specs=[row, row, mp, mp], out_specs=[row, row],
        compiler_params=_params(("parallel",)))(x, br, mp_gate, mp_norm)


def _final_fwd_bwd(x, br, mp_gate, fnorm8, target, name):
    t, d = x.shape

    def body(x_ref, br_ref, mg_ref, fn_ref, tg_ref, dx_ref, dbr_ref, acc_ref):
        gate = _rows(mg_ref, 5)
        brv = br_ref[...]
        xv = x_ref[...] + gate * brv
        fn = _rows(fn_ref, 0)
        r = lax.rsqrt(jnp.mean(xv * xv, axis=1, keepdims=True) + EPS)
        nrm = xv * r
        err = nrm * fn - tg_ref[...]
        loss = 0.5 * jnp.sum(jnp.mean(err * err, axis=1, keepdims=True), axis=0, keepdims=True)
        dy = err * (1.0 / d)
        dn = dy * fn
        dx = r * (dn - nrm * jnp.mean(dn * nrm, axis=1, keepdims=True))
        dx_ref[...] = dx
        dbr_ref[...] = (dx * gate).astype(BF16)
        upd = jnp.concatenate([
            jnp.broadcast_to(loss, (1, d)),
            jnp.sum(dy * nrm, axis=0, keepdims=True),
            jnp.sum(dx * brv, axis=0, keepdims=True),
            jnp.zeros((5, d), F32)], axis=0)

        @pl.when(pl.program_id(0) == 0)
        def _():
            acc_ref[...] = upd

        @pl.when(pl.program_id(0) > 0)
        def _():
            acc_ref[...] += upd

    row = pl.BlockSpec((ROW_TILE, d), lambda i: (i, 0))
    mp = pl.BlockSpec((8, d), lambda i: (0, 0))
    return pl.pallas_call(
        body, name=name,
        out_shape=[jax.ShapeDtypeStruct((t, d), F32), jax.ShapeDtypeStruct((t, d), BF16),
                   jax.ShapeDtypeStruct((8, d), F32)],
        grid=(t // ROW_TILE,), in_specs=[row, row, mp, mp, row], out_specs=[row, row, mp],
        compiler_params=_params(("arbitrary",)))(x, br, mp_gate, fnorm8, target)


def _resid_norm_bwd(dx, du, x, mp_norm, which, name, br=None, mp_gate=None, gate_row=None):
    t, d = x.shape
    has_gate = br is not None

    def body(*refs):
        if has_gate:
            dx_ref, du_ref, x_ref, mn_ref, br_ref, mg_ref, dxn_ref, dbr_ref, acc_ref = refs
        else:
            dx_ref, du_ref, x_ref, mn_ref, dxn_ref, acc_ref = refs
        xv, duv = x_ref[...], du_ref[...]
        r, ln, _, sc = _norm_parts(xv, mn_ref, which)
        nrm = xv * r
        dn = duv * (ln * (1.0 + sc))
        dxn = dx_ref[...] + r * (dn - nrm * jnp.mean(dn * nrm, axis=1, keepdims=True))
        dxn_ref[...] = dxn
        rows = [jnp.sum(duv, axis=0, keepdims=True), jnp.sum(duv * nrm, axis=0, keepdims=True)]
        if has_gate:
            dbr_ref[...] = (dxn * _rows(mg_ref, gate_row)).astype(BF16)
            rows.append(jnp.sum(dxn * br_ref[...], axis=0, keepdims=True))
        upd = jnp.concatenate(rows + [jnp.zeros((8 - len(rows), d), F32)], axis=0)

        @pl.when(pl.program_id(0) == 0)
        def _():
            acc_ref[...] = upd

        @pl.when(pl.program_id(0) > 0)
        def _():
            acc_ref[...] += upd

    row = pl.BlockSpec((ROW_TILE, d), lambda i: (i, 0))
    mp = pl.BlockSpec((8, d), lambda i: (0, 0))
    ins, in_specs = [dx, du, x, mp_norm], [row, row, row, mp]
    outs = [jax.ShapeDtypeStruct((t, d), F32)]
    out_specs = [row]
    if has_gate:
        ins += [br, mp_gate]
        in_specs += [row, mp]
        outs.append(jax.ShapeDtypeStruct((t, d), BF16))
        out_specs.append(row)
    outs.append(jax.ShapeDtypeStruct((8, d), F32))
    out_specs.append(mp)
    return pl.pallas_call(
        body, name=name, out_shape=outs, grid=(t // ROW_TILE,), in_specs=in_specs, out_specs=out_specs,
        compiler_params=_params(("arbitrary",)))(*ins)


def _relu2_fwd(h, name):
    t, f = h.shape
    tf = _tile(f, 2048)

    def body(h_ref, o_ref):
        hv = jnp.maximum(h_ref[...].astype(F32), 0.0)
        o_ref[...] = (hv * hv).astype(BF16)

    blk = pl.BlockSpec((ROW_TILE, tf), lambda i, j: (i, j))
    return pl.pallas_call(
        body, name=name, out_shape=jax.ShapeDtypeStruct((t, f), BF16), grid=(t // ROW_TILE, f // tf),
        in_specs=[blk], out_specs=blk, compiler_params=_params(("parallel", "parallel")))(h)


def _relu2_bwd(dhid, h, name):
    t, f = h.shape
    tf = _tile(f, 2048)

    def body(d_ref, h_ref, o_ref):
        o_ref[...] = (d_ref[...] * (2.0 * jnp.maximum(h_ref[...].astype(F32), 0.0))).astype(BF16)

    blk = pl.BlockSpec((ROW_TILE, tf), lambda i, j: (i, j))
    return pl.pallas_call(
        body, name=name, out_shape=jax.ShapeDtypeStruct((t, f), BF16), grid=(t // ROW_TILE, f // tf),
        in_specs=[blk, blk], out_specs=blk, compiler_params=_params(("parallel", "parallel")))(dhid, h)


def _merge_fwd(proj, p_conv, p_ssm, name):
    t, d = p_conv.shape

    def body(gl_ref, pc_ref, ps_ref, o_ref):
        g = _sigmoid(gl_ref[...].astype(F32))
        o_ref[...] = (g[:, :d] * pc_ref[...] + g[:, d:] * ps_ref[...]).astype(BF16)

    row = pl.BlockSpec((ROW_TILE, d), lambda i: (i, 0))
    return pl.pallas_call(
        body, name=name, out_shape=jax.ShapeDtypeStruct((t, d), BF16), grid=(t // ROW_TILE,),
        in_specs=[pl.BlockSpec((ROW_TILE, 2 * d), lambda i: (i, 0)), row, row], out_specs=row,
        compiler_params=_params(("parallel",)))(proj, p_conv, p_ssm)


def _merge_bwd(dmerged, proj, p_conv, p_ssm, name):
    t, d = p_conv.shape
    pw = proj.shape[1]

    def body(dm_ref, gl_ref, pc_ref, ps_ref, dpc_ref, dps_ref, dgl_ref):
        g = _sigmoid(gl_ref[...].astype(F32))
        gc, gs = g[:, :d], g[:, d:]
        dm = dm_ref[...]
        dpc_ref[...] = (dm * gc).astype(BF16)
        dps_ref[...] = (dm * gs).astype(BF16)
        dgl_ref[...] = jnp.concatenate(
            [dm * pc_ref[...] * gc * (1.0 - gc), dm * ps_ref[...] * gs * (1.0 - gs)], axis=1).astype(BF16)

    row = pl.BlockSpec((ROW_TILE, d), lambda i: (i, 0))
    wide = pl.BlockSpec((ROW_TILE, 2 * d), lambda i: (i, 0))
    return pl.pallas_call(
        body, name=name,
        out_shape=[jax.ShapeDtypeStruct((t, d), BF16), jax.ShapeDtypeStruct((t, d), BF16),
                   jax.ShapeDtypeStruct((t, pw), BF16)],
        grid=(t // ROW_TILE,), in_specs=[row, wide, row, row], out_specs=[row, row, wide],
        compiler_params=_params(("parallel",)))(dmerged, proj, p_conv, p_ssm)


def _halo_specs(t, width, col):
    nb = t // 8
    step = ROW_TILE // 8
    prev = pl.BlockSpec((8, width), lambda i: (jnp.maximum(i * step - 1, 0), col))
    nxt = pl.BlockSpec((8, width), lambda i: (jnp.minimum((i + 1) * step, nb - 1), col))
    return prev, nxt


def _gconv_fwd(proj, conv_w8, d, name):
    t = proj.shape[0]

    def body(cb_ref, cc_ref, cx_ref, ccp_ref, cxp_ref, w_ref, o_ref):
        first = pl.program_id(0) == 0
        v = cc_ref[...].astype(F32) * cx_ref[...].astype(F32)
        vp = jnp.where(first, 0.0, ccp_ref[...].astype(F32) * cxp_ref[...].astype(F32))
        cv = sum(_rows(w_ref, k) * _shift_down(v, vp, 2 - k) for k in range(3))
        o_ref[...] = (cb_ref[...].astype(F32) * cv).astype(BF16)

    def win(col):
        return pl.BlockSpec((ROW_TILE, d), lambda i: (i, col))

    return pl.pallas_call(
        body, name=name, out_shape=jax.ShapeDtypeStruct((t, d), BF16), grid=(t // ROW_TILE,),
        in_specs=[win(2), win(3), win(4), _halo_specs(t, d, 3)[0], _halo_specs(t, d, 4)[0],
                  pl.BlockSpec((8, d), lambda i: (0, 0))],
        out_specs=pl.BlockSpec((ROW_TILE, d), lambda i: (i, 0)),
        compiler_params=_params(("parallel",)))(proj, proj, proj, proj, proj, conv_w8)


def _gconv_bwd(dy, proj, conv_w8, dproj, d, name):
    t = proj.shape[0]
    nt = t // ROW_TILE

    def body(dy_ref, dyn_ref, cb_ref, cc_ref, cx_ref, ccp_ref, cxp_ref, cbn_ref, w_ref, _, dp_ref, dw_ref, st_ref):
        i, j = pl.program_id(0), pl.program_id(1)

        @pl.when(j == 0)
        def _():
            cb, cc, cx = cb_ref[...].astype(F32), cc_ref[...].astype(F32), cx_ref[...].astype(F32)
            v = cc * cx
            vp = jnp.where(i == 0, 0.0, ccp_ref[...].astype(F32) * cxp_ref[...].astype(F32))
            sh = [_shift_down(v, vp, 2 - k) for k in range(3)]
            cv = sum(_rows(w_ref, k) * sh[k] for k in range(3))
            dyv = dy_ref[...]
            dcv = dyv * cb
            dcvn = jnp.where(i == nt - 1, 0.0, dyn_ref[...] * cbn_ref[...].astype(F32))
            dv = sum(_rows(w_ref, k) * _shift_up(dcv, dcvn, 2 - k) for k in range(3))
            st_ref[0] = (dyv * cv).astype(BF16)
            st_ref[1] = (dv * cx).astype(BF16)
            st_ref[2] = (dv * cc).astype(BF16)
            upd = jnp.concatenate([jnp.sum(dcv * sh[k], axis=0, keepdims=True) for k in range(3)]
                                  + [jnp.zeros((5, d), F32)], axis=0)

            @pl.when(i == 0)
            def _():
                dw_ref[...] = upd

            @pl.when(i > 0)
            def _():
                dw_ref[...] += upd

        dp_ref[...] = st_ref[j]

    def win(col):
        return pl.BlockSpec((ROW_TILE, d), lambda i, j: (i, col))

    def halo(col, which):
        nb, step = t // 8, ROW_TILE // 8
        if which == 0:
            return pl.BlockSpec((8, d), lambda i, j: (jnp.maximum(i * step - 1, 0), col))
        return pl.BlockSpec((8, d), lambda i, j: (jnp.minimum((i + 1) * step, nb - 1), col))

    row = pl.BlockSpec((ROW_TILE, d), lambda i, j: (i, 0))
    w8 = pl.BlockSpec((8, d), lambda i, j: (0, 0))
    return pl.pallas_call(
        body, name=name,
        out_shape=[jax.ShapeDtypeStruct(dproj.shape, BF16), jax.ShapeDtypeStruct((8, d), F32)],
        grid=(nt, 3),
        in_specs=[row, halo(0, 1), win(2), win(3), win(4), halo(3, 0), halo(4, 0), halo(2, 1), w8,
                  pl.BlockSpec(memory_space=pl.ANY)],
        out_specs=[pl.BlockSpec((ROW_TILE, d), lambda i, j: (i, 2 + j)), w8],
        scratch_shapes=[pltpu.VMEM((3, ROW_TILE, d), BF16)],
        input_output_aliases={9: 0},
        compiler_params=_params(("arbitrary", "arbitrary")))(
            dy, dy, proj, proj, proj, proj, proj, proj, conv_w8, dproj)


def _dsilu(p):
    s = _sigmoid(p)
    return s * (1.0 + p * (1.0 - s))


def _sconv_specs(t, d, col0, nt):
    nb, step = t // 8, ROW_TILE // 8
    win = pl.BlockSpec((ROW_TILE, d), lambda j, i: (i, col0 + j))
    prev = pl.BlockSpec((8, d), lambda j, i: (jnp.maximum(i * step - 1, 0), col0 + j))
    nxt = pl.BlockSpec((8, d), lambda j, i: (jnp.minimum((i + 1) * step, nb - 1), col0 + j))
    return win, prev, nxt


def _sconv_fwd(proj, w8, b8, d, col0, ncol, name):
    t = proj.shape[0]
    nt = t // ROW_TILE

    def body(x_ref, xp_ref, w_ref, b_ref, o_ref):
        xv = x_ref[...].astype(F32)
        xp = jnp.where(pl.program_id(1) == 0, 0.0, xp_ref[...].astype(F32))
        pre = _rows(b_ref, 0) + sum(_rows(w_ref, k) * _shift_down(xv, xp, 3 - k) for k in range(4))
        o_ref[...] = (pre * _sigmoid(pre)).astype(BF16)

    win, prev, _ = _sconv_specs(t, d, col0, nt)
    par = pl.BlockSpec((8, d), lambda j, i: (0, j))
    return pl.pallas_call(
        body, name=name, out_shape=jax.ShapeDtypeStruct((t, ncol * d), BF16), grid=(ncol, nt),
        in_specs=[win, prev, par, par], out_specs=pl.BlockSpec((ROW_TILE, d), lambda j, i: (i, j)),
        compiler_params=_params(("parallel", "parallel")))(proj, proj, w8, b8)


def _sconv_bwd(dxc, proj, w8, b8, dproj, d, col0, ncol, name):
    t = proj.shape[0]
    nt = t // ROW_TILE

    def body(d_ref, dn_ref, x_ref, xp_ref, xn_ref, w_ref, b_ref, _, dp_ref, dw_ref):
        i = pl.program_id(1)
        xv = x_ref[...].astype(F32)
        xp = jnp.where(i == 0, 0.0, xp_ref[...].astype(F32))
        bias = _rows(b_ref, 0)
        sh = [_shift_down(xv, xp, 3 - k) for k in range(4)]
        pre = bias + sum(_rows(w_ref, k) * sh[k] for k in range(4))
        dpre = d_ref[...] * _dsilu(pre)
        xn = xn_ref[...].astype(F32)
        pre_n = bias + sum(_rows(w_ref, k) * _shift_down(xn, xv[ROW_TILE - 8:], 3 - k) for k in range(4))
        dpre_n = jnp.where(i == nt - 1, 0.0, dn_ref[...] * _dsilu(pre_n))
        dx = sum(_rows(w_ref, k) * _shift_up(dpre, dpre_n, 3 - k) for k in range(4))
        dp_ref[...] = dx.astype(BF16)
        upd = jnp.concatenate([jnp.sum(dpre * sh[k], axis=0, keepdims=True) for k in range(4)]
                              + [jnp.sum(dpre, axis=0, keepdims=True), jnp.zeros((3, d), F32)], axis=0)

        @pl.when(i == 0)
        def _():
            dw_ref[...] = upd

        @pl.when(i > 0)
        def _():
            dw_ref[...] += upd

    win, prev, nxt = _sconv_specs(t, d, col0, nt)
    nb, step = t // 8, ROW_TILE // 8
    dwin = pl.BlockSpec((ROW_TILE, d), lambda j, i: (i, j))
    dnxt = pl.BlockSpec((8, d), lambda j, i: (jnp.minimum((i + 1) * step, nb - 1), j))
    par = pl.BlockSpec((8, d), lambda j, i: (0, j))
    return pl.pallas_call(
        body, name=name,
        out_shape=[jax.ShapeDtypeStruct(dproj.shape, BF16), jax.ShapeDtypeStruct((8, ncol * d), F32)],
        grid=(ncol, nt),
        in_specs=[dwin, dnxt, win, prev, nxt, par, par, pl.BlockSpec(memory_space=pl.ANY)],
        out_specs=[win, par], input_output_aliases={7: 0},
        compiler_params=_params(("arbitrary", "arbitrary")))(dxc, dxc, proj, proj, proj, w8, b8, dproj)


def _softplus(x):
    return jnp.maximum(x, 0.0) + jnp.log(1.0 + jnp.exp(-jnp.abs(x)))


def _dt_fwd(proj, par8, nheads, dt_col, name):
    t = proj.shape[0]

    def body(r_ref, p_ref, dt_ref, ac_ref):
        lane = lax.broadcasted_iota(jnp.int32, (1, LANE), 1)
        valid = lane < nheads
        raw = r_ref[...][:, :LANE].astype(F32)
        dt = jnp.where(valid, _softplus(raw + _rows(p_ref, 0)), 0.0)
        a = jnp.where(valid, -jnp.exp(_rows(p_ref, 1)), 0.0)
        dt_ref[...] = dt
        ac_ref[...] = _xdot01(_chunk_tri(ROW_TILE, True), dt * a)

    out = pl.BlockSpec((ROW_TILE, LANE), lambda i: (i, 0))
    return pl.pallas_call(
        body, name=name,
        out_shape=[jax.ShapeDtypeStruct((t, LANE), F32), jax.ShapeDtypeStruct((t, LANE), F32)],
        grid=(t // ROW_TILE,),
        in_specs=[pl.BlockSpec((ROW_TILE, 2 * LANE), lambda i: (i, dt_col)), pl.BlockSpec((8, LANE), lambda i: (0, 0))],
        out_specs=[out, out], compiler_params=_params(("parallel",)))(proj, par8)


def _dt_bwd(ddt_g, dac_g, dt, proj, par8, dproj, nheads, dt_col, name):
    t = proj.shape[0]
    ng = ddt_g.shape[0]

    def body(dd_ref, da_ref, dt_ref, r_ref, p_ref, _, dp_ref, acc_ref):
        lane = lax.broadcasted_iota(jnp.int32, (1, LANE), 1)
        valid = lane < nheads
        ddt = sum(dd_ref[g] for g in range(ng))
        dac = sum(da_ref[g] for g in range(ng))
        a = jnp.where(valid, -jnp.exp(_rows(p_ref, 1)), 0.0)
        d_a = _xdot01(_chunk_tri(ROW_TILE, False), dac)
        ddt = ddt + d_a * a
        raw = r_ref[...][:, :LANE].astype(F32)
        draw = jnp.where(valid, ddt * _sigmoid(raw + _rows(p_ref, 0)), 0.0)
        dp_ref[...] = jnp.concatenate([draw, jnp.zeros_like(draw)], axis=1).astype(BF16)
        upd = jnp.concatenate([jnp.sum(draw, axis=0, keepdims=True),
                               jnp.sum(d_a * dt_ref[...], axis=0, keepdims=True) * a,
                               jnp.zeros((6, LANE), F32)], axis=0)

        @pl.when(pl.program_id(0) == 0)
        def _():
            acc_ref[...] = upd

        @pl.when(pl.program_id(0) > 0)
        def _():
            acc_ref[...] += upd

    row = pl.BlockSpec((ROW_TILE, LANE), lambda i: (i, 0))
    grp = pl.BlockSpec((ng, ROW_TILE, LANE), lambda i: (0, i, 0))
    win = pl.BlockSpec((ROW_TILE, 2 * LANE), lambda i: (i, dt_col))
    par = pl.BlockSpec((8, LANE), lambda i: (0, 0))
    return pl.pallas_call(
        body, name=name,
        out_shape=[jax.ShapeDtypeStruct(dproj.shape, BF16), jax.ShapeDtypeStruct((8, LANE), F32)],
        grid=(t // ROW_TILE,),
        in_specs=[grp, grp, row, win, par, pl.BlockSpec(memory_space=pl.ANY)],
        out_specs=[win, par], input_output_aliases={5: 0},
        compiler_params=_params(("arbitrary",)))(ddt_g, dac_g, dt, proj, par8, dproj)


def _ssd_chunk(xs, bm, cm, z, dt, ac, st, dsk, nw, g):
    q = xs.shape[0]
    lane_h = lax.broadcasted_iota(jnp.int32, (1, LANE), 1)
    sub_h = lax.broadcasted_iota(jnp.int32, (LANE, 1), 0)
    lane_c = lax.broadcasted_iota(jnp.int32, (1, GROUP_W), 1) // HEAD_DIM
    rr = lax.broadcasted_iota(jnp.int32, (q, q), 0)
    cc = lax.broadcasted_iota(jnp.int32, (q, q), 1)
    tril = rr >= cc
    last_row = jnp.where(lax.broadcasted_iota(jnp.int32, (q, 1), 0) == q - 1, 1.0, 0.0)
    act = ac.T
    col, row, dtc = [], [], []
    for r in range(HEADS_PER_GROUP):
        h = g * HEADS_PER_GROUP + r
        sel = jnp.where(lane_h == h, 1.0, 0.0)
        col.append(jnp.sum(ac * sel, axis=1, keepdims=True))
        dtc.append(jnp.sum(dt * sel, axis=1, keepdims=True))
        row.append(jnp.sum(act * jnp.where(sub_h == h, 1.0, 0.0), axis=0, keepdims=True))

    def per_head(vals):
        out = jnp.broadcast_to(vals[0], (q, GROUP_W))
        for r in range(1, HEADS_PER_GROUP):
            out = jnp.where(lane_c == r, vals[r], out)
        return out

    acx = per_head(col)
    x = xs * per_head(dtc)
    s = _bdot(cm, bm, "nt")
    y = per_head([_bdot(s * jnp.exp(jnp.where(tril, col[r] - row[r], -jnp.inf)), x, "nn")
                  for r in range(HEADS_PER_GROUP)])
    y = y + jnp.exp(acx) * _bdot(cm, st, "nn")
    last = jnp.sum(acx * last_row, axis=0, keepdims=True)
    new_st = st * jnp.exp(last) + _bdot(bm, x * jnp.exp(last - acx), "tn")
    y = y + dsk * xs
    y = y * (z * _sigmoid(z))
    yn = y * lax.rsqrt(jnp.mean(y * y, axis=1, keepdims=True) + EPS) * nw
    return yn, new_st


def _ssd_specs(nc, z_col, rev):
    def ci(c):
        return nc - 1 - c if rev else c

    xc = pl.BlockSpec((ROW_TILE, 2 * GROUP_W), lambda g, c: (ci(c), g))
    z = pl.BlockSpec((ROW_TILE, GROUP_W), lambda g, c: (ci(c), z_col + g))
    hd = pl.BlockSpec((ROW_TILE, LANE), lambda g, c: (ci(c), 0))
    par = pl.BlockSpec((8, GROUP_W), lambda g, c: (0, g))
    yb = pl.BlockSpec((ROW_TILE, GROUP_W), lambda g, c: (ci(c), g))
    stb = pl.BlockSpec((None, N_STATE, GROUP_W), lambda g, c: (ci(c), 0, g))
    return xc, z, hd, par, yb, stb


def _ssd_fwd(xc, proj, dt, ac, par8, ngroups, z_col, name):
    t = xc.shape[0]
    nc = t // ROW_TILE

    def body(xc_ref, z_ref, dt_ref, ac_ref, p_ref, y_ref, so_ref, st_ref):
        @pl.when(pl.program_id(1) == 0)
        def _():
            st_ref[...] = jnp.zeros_like(st_ref)

        st = st_ref[...]
        so_ref[...] = st
        blk = xc_ref[...].astype(F32)
        yn, new_st = _ssd_chunk(blk[:, :GROUP_W], blk[:, GROUP_W:GROUP_W + N_STATE], blk[:, GROUP_W + N_STATE:],
                                z_ref[...].astype(F32), dt_ref[...], ac_ref[...], st,
                                _rows(p_ref, 0), _rows(p_ref, 1), pl.program_id(0))
        y_ref[...] = yn.astype(BF16)
        st_ref[...] = new_st

    xcs, zs, hd, par, yb, stb = _ssd_specs(nc, z_col, False)
    return pl.pallas_call(
        body, name=name,
        out_shape=[jax.ShapeDtypeStruct((t, ngroups * GROUP_W), BF16),
                   jax.ShapeDtypeStruct((nc, N_STATE, ngroups * GROUP_W), F32)],
        grid=(ngroups, nc), in_specs=[xcs, zs, hd, hd, par], out_specs=[yb, stb],
        scratch_shapes=[pltpu.VMEM((N_STATE, GROUP_W), F32)],
        compiler_params=_params(("parallel", "arbitrary")))(xc, proj, dt, ac, par8)


def _ssd_bwd(dyn, xc, proj, dt, ac, par8, states, dproj, ngroups, z_col, name):
    t = xc.shape[0]
    nc = t // ROW_TILE

    def body(dy_ref, xc_ref, z_ref, dt_ref, ac_ref, p_ref, s_ref, _, dz_ref, dxc_ref, ddt_ref, dac_ref, dp_ref,
             ds_ref):
        c = pl.program_id(1)

        @pl.when(c == 0)
        def _():
            ds_ref[...] = jnp.zeros_like(ds_ref)

        g = pl.program_id(0)
        blk = xc_ref[...].astype(F32)
        args = (blk[:, :GROUP_W], blk[:, GROUP_W:GROUP_W + N_STATE], blk[:, GROUP_W + N_STATE:],
                z_ref[...].astype(F32), dt_ref[...], ac_ref[...], s_ref[...], _rows(p_ref, 0), _rows(p_ref, 1))
        _, vjp = jax.vjp(lambda *a: _ssd_chunk(*a, g), *args)
        dxs, dbm, dcm, dz, ddt, dac, dst, ddsk, dnw = vjp((dy_ref[...], ds_ref[...]))
        ds_ref[...] = dst
        dz_ref[...] = dz.astype(BF16)
        dxc_ref[...] = jnp.concatenate([dxs, dbm, dcm], axis=1)
        ddt_ref[...] = ddt
        dac_ref[...] = dac
        upd = jnp.concatenate([ddsk, dnw, jnp.zeros((6, GROUP_W), F32)], axis=0)

        @pl.when(c == 0)
        def _():
            dp_ref[...] = upd

        @pl.when(c > 0)
        def _():
            dp_ref[...] += upd

    xcs, zs, hd, par, yb, stb = _ssd_specs(nc, z_col, True)
    dxcs = pl.BlockSpec((ROW_TILE, 2 * GROUP_W), lambda g, c: (nc - 1 - c, g))
    hg = pl.BlockSpec((None, ROW_TILE, LANE), lambda g, c: (g, nc - 1 - c, 0))
    return pl.pallas_call(
        body, name=name,
        out_shape=[jax.ShapeDtypeStruct(dproj.shape, BF16),
                   jax.ShapeDtypeStruct((t, ngroups * 2 * GROUP_W), F32),
                   jax.ShapeDtypeStruct((ngroups, t, LANE), F32), jax.ShapeDtypeStruct((ngroups, t, LANE), F32),
                   jax.ShapeDtypeStruct((8, ngroups * GROUP_W), F32)],
        grid=(ngroups, nc),
        in_specs=[yb, xcs, zs, hd, hd, par, stb, pl.BlockSpec(memory_space=pl.ANY)],
        out_specs=[zs, dxcs, hg, hg, par], input_output_aliases={7: 0},
        scratch_shapes=[pltpu.VMEM((N_STATE, GROUP_W), F32)],
        compiler_params=_params(("arbitrary", "arbitrary")))(dyn, xc, proj, dt, ac, par8, states, dproj)


def _small_matmul(a, b, mode, name):
    if mode == "nn":
        shape = (a.shape[0], b.shape[1])
    else:
        shape = (a.shape[1], b.shape[1])

    def body(a_ref, b_ref, o_ref):
        o_ref[...] = _dot(a_ref[...], b_ref[...], mode)

    return pl.pallas_call(body, name=name, out_shape=jax.ShapeDtypeStruct(shape, F32),
                          compiler_params=_params())(a, b)


def _adamw(parts, w, m, v, name):
    npart, rows, cols = parts.shape
    tr = rows if rows <= 128 else _tile8(rows, 128)

    def body(p_ref, w_ref, m_ref, v_ref, g_ref, d_ref, nm_ref, nv_ref):
        g = p_ref[0].astype(F32)
        for j in range(1, npart):
            g = g + p_ref[j].astype(F32)
        mm = ADAM_B1 * m_ref[...] + (1.0 - ADAM_B1) * g
        vv = ADAM_B2 * v_ref[...] + (1.0 - ADAM_B2) * (g * g)
        m_hat = mm / (1.0 - ADAM_B1 ** ADAM_STEP)
        v_hat = vv / (1.0 - ADAM_B2 ** ADAM_STEP)
        g_ref[...] = g
        d_ref[...] = -ADAM_LR * (m_hat / (jnp.sqrt(v_hat) + ADAM_EPS) + ADAM_WD * w_ref[...])
        nm_ref[...] = mm
        nv_ref[...] = vv

    blk = pl.BlockSpec((tr, cols), lambda i: (i, 0))
    out = jax.ShapeDtypeStruct((rows, cols), F32)
    return pl.pallas_call(
        body, name=name, out_shape=[out] * 4, grid=(rows // tr,),
        in_specs=[pl.BlockSpec((npart, tr, cols), lambda i: (0, i, 0)), blk, blk, blk], out_specs=[blk] * 4,
        compiler_params=_params(("parallel",)))(parts, w, m, v)


def _tile8(n, target):
    best = 8
    for d in range(8, target + 1, 8):
        if n % d == 0:
            best = d
    return best


def _pad_rows8(a):
    return jnp.concatenate([a, jnp.zeros((8 - a.shape[0], a.shape[1]), a.dtype)], axis=0)


def _pad_lanes(a, n):
    return jnp.concatenate([a, jnp.zeros(a.shape[:-1] + (n - a.shape[-1],), a.dtype)], axis=-1)


def _pack(vecs, mult):
    flat = jnp.concatenate([v.reshape(-1) for v in vecs])
    pad = (-flat.shape[0]) % mult
    flat = jnp.concatenate([flat, jnp.zeros((pad,), flat.dtype)])
    return flat.reshape(-1, LANE)


def _unpack(flat, shapes):
    out, o = [], 0
    for s in shapes:
        n = int(np.prod(s))
        out.append(flat[o:o + n].reshape(s))
        o += n
    return out


def _cols_to_blocks(a):
    w = a.shape[-1] // N_DEV
    return jnp.moveaxis(a.reshape(a.shape[:-1] + (N_DEV, w)), -2, 0)


def _rows_to_blocks(a):
    l, rows, n = a.shape
    return jnp.moveaxis(a.reshape(l, N_DEV, rows // N_DEV, n), 1, 0)


def kernel(x, c, w_ada, b_ada, ln1, ln2, w_in, conv_w, ssm_conv_w, ssm_conv_b, dt_bias, a_log, d_skip, ssm_norm_w, w_conv_out, w_ssm_out, w_o, w_up, w_down, final_norm, loss_target, m_w_ada, m_b_ada, m_ln1, m_ln2, m_w_in, m_conv_w, m_ssm_conv_w, m_ssm_conv_b, m_dt_bias, m_a_log, m_d_skip, m_ssm_norm_w, m_w_conv_out, m_w_ssm_out, m_w_o, m_w_up, m_w_down, m_final_norm, v_w_ada, v_b_ada, v_ln1, v_ln2, v_w_in, v_conv_w, v_ssm_conv_w, v_ssm_conv_b, v_dt_bias, v_a_log, v_d_skip, v_ssm_norm_w, v_w_conv_out, v_w_ssm_out, v_w_o, v_w_up, v_w_down, v_final_norm):
    names = ["w_ada", "b_ada", "ln1", "ln2", "w_in", "conv_w", "ssm_conv_w", "ssm_conv_b", "dt_bias", "a_log",
             "d_skip", "ssm_norm_w", "w_conv_out", "w_ssm_out", "w_o", "w_up", "w_down", "final_norm"]
    w_of = dict(zip(names, [w_ada, b_ada, ln1, ln2, w_in, conv_w, ssm_conv_w, ssm_conv_b, dt_bias, a_log, d_skip,
                            ssm_norm_w, w_conv_out, w_ssm_out, w_o, w_up, w_down, final_norm]))
    m_of = dict(zip(names, [m_w_ada, m_b_ada, m_ln1, m_ln2, m_w_in, m_conv_w, m_ssm_conv_w, m_ssm_conv_b, m_dt_bias,
                            m_a_log, m_d_skip, m_ssm_norm_w, m_w_conv_out, m_w_ssm_out, m_w_o, m_w_up, m_w_down,
                            m_final_norm]))
    v_of = dict(zip(names, [v_w_ada, v_b_ada, v_ln1, v_ln2, v_w_in, v_conv_w, v_ssm_conv_w, v_ssm_conv_b, v_dt_bias,
                            v_a_log, v_d_skip, v_ssm_norm_w, v_w_conv_out, v_w_ssm_out, v_w_o, v_w_up, v_w_down,
                            v_final_norm]))

    _, t, d = x.shape
    nl = w_ada.shape[0]
    ada_w = w_ada.shape[2]
    ds = ssm_norm_w.shape[1]
    nh = dt_bias.shape[1]
    ng = nh // HEADS_PER_GROUP
    gn = ng * N_STATE
    xbc_w = ds + 2 * gn
    z_off, xbc_off = 5 * d, 5 * d + ds
    dt_off = xbc_off + xbc_w
    proj_w = dt_off + nh
    pw = dt_off + 2 * LANE
    me = 4 * lax.axis_index("x") + 2 * lax.axis_index("y") + lax.axis_index("c")

    perm = []
    for g in range(ng):
        perm += [2 * g, 2 * g + 1, ds // LANE + g, ds // LANE + ng + g]
    perm = np.array(perm)
    inv_perm = np.argsort(perm)

    def reorder(a, order):
        blocks = a.reshape(a.shape[:-1] + (xbc_w // LANE, LANE))
        return jnp.concatenate([blocks[..., j:j + 1, :] for j in order], axis=-2).reshape(a.shape)

    def to_group_major(a):
        return reorder(a, perm)

    def from_group_major(a):
        return reorder(a, inv_perm)

    x2, tgt = x[0], loss_target[0]

    c_act = c * jax.nn.sigmoid(c)
    sizes1 = [(1, d), conv_w.shape, ssm_conv_w.shape]
    (g1,) = _all_gather([_pack([c_act, conv_w, ssm_conv_w], LANE)], "ag_small_in")
    parts1 = [_unpack(g1[j].reshape(-1), sizes1) for j in range(N_DEV)]
    c_act_all = jnp.concatenate([p[0] for p in parts1], axis=0)
    conv_w_full = jnp.concatenate([p[1] for p in parts1], axis=-1)
    sconv_w_full = to_group_major(jnp.concatenate([p[2] for p in parts1], axis=-1))
    sconv_b_gm = to_group_major(ssm_conv_b)

    mod_part = jnp.stack([_small_matmul(c_act_all, w_ada[l], "nn", f"ada_fwd{l}") for l in range(nl)])
    (gmod,) = _all_gather([mod_part], "ag_mod")
    mod = lax.dynamic_index_in_dim(gmod, me, axis=2, keepdims=False)
    mod = jnp.moveaxis(mod, 0, 1).reshape(nl, N_DEV * ada_w) + b_ada
    mod = mod.reshape(nl, 6, d)
    modp = [jnp.concatenate([mod[l], ln1[l][None], ln2[l][None]], axis=0) for l in range(nl)]

    big = ["w_in", "w_up", "w_conv_out", "w_ssm_out", "w_o", "w_down"]
    def blocks_of(l, keys):
        return [w_of[k][l].astype(BF16) for k in keys]

    def full_cols(g):
        return jnp.moveaxis(g, 0, 1).reshape(g.shape[1], N_DEV * g.shape[2])

    def full_rows(g):
        return g.reshape(N_DEV * g.shape[1], g.shape[2])

    def pad_w_in(g):
        w = full_cols(g)
        return jnp.concatenate([w[:, :xbc_off], to_group_major(w[:, xbc_off:dt_off]), w[:, dt_off:],
                                jnp.zeros((d, pw - proj_w), BF16)], axis=-1)

    ag_sets = [["w_in"], ["w_up", "w_conv_out", "w_o"], ["w_down", "w_ssm_out"]]
    got = dict(zip(big, _all_gather(blocks_of(0, big), "ag_weights0")))
    wi, wup, wco, wso, wo, wdown = ([None] * nl for _ in range(6))

    def assemble(l, g):
        wi[l], wup[l] = pad_w_in(g["w_in"]), full_cols(g["w_up"])
        wco[l], wso[l] = full_rows(g["w_conv_out"]), full_rows(g["w_ssm_out"])
        wo[l], wdown[l] = full_rows(g["w_o"]), full_rows(g["w_down"])

    assemble(0, got)

    cw8 = [_pad_rows8(conv_w_full[l]) for l in range(nl)]
    sw8 = [_pad_rows8(sconv_w_full[l]) for l in range(nl)]
    sb8 = [_pad_rows8(sconv_b_gm[l][None]) for l in range(nl)]
    dtp8 = [_pad_rows8(_pad_lanes(jnp.stack([dt_bias[l], a_log[l]]), LANE)) for l in range(nl)]
    sp8 = [_pad_rows8(jnp.stack([jnp.repeat(d_skip[l], HEAD_DIM), ssm_norm_w[l]])) for l in range(nl)]
    fn8 = _pad_rows8(final_norm[None])
    xbc_col, ncol, dt_col, z_col = xbc_off // d, xbc_w // d, dt_off // (2 * LANE), z_off // GROUP_W

    saved = []
    x_cur, br_prev = x2, None
    for l in range(nl):
        if l == 0:
            x_in, u1 = x_cur, _norm_fwd(x_cur, modp[0], 0, "norm_first")
        else:
            x_in, u1 = _resid_norm_fwd(x_cur, br_prev, modp[l - 1], 5, modp[l], 0, f"resid_norm_a{l}")
        nxt = {}

        def mm_carry(a, b, out_dtype, name, keys):
            if l + 1 == nl:
                return _matmul(a, b, "nn", out_dtype, name)
            out, res = _matmul(a, b, "nn", out_dtype, name, comm=_GatherComm(blocks_of(l + 1, keys)))
            nxt.update(zip(keys, res))
            return out

        proj = mm_carry(u1, wi[l], BF16, f"mm_in{l}", ag_sets[0])
        y_conv = _gconv_fwd(proj, cw8[l], d, f"gconv_fwd{l}")
        xc = _sconv_fwd(proj, sw8[l], sb8[l], d, xbc_col, ncol, f"sconv_fwd{l}")
        dt, ac = _dt_fwd(proj, dtp8[l], nh, dt_col, f"dt_fwd{l}")
        yn, states = _ssd_fwd(xc, proj, dt, ac, sp8[l], ng, z_col, f"ssd_fwd{l}")
        p_conv = _matmul(y_conv, wco[l], "nn", F32, f"mm_conv_out{l}")
        p_ssm = _matmul(yn, wso[l], "nn", F32, f"mm_ssm_out{l}")
        merged = _merge_fwd(proj, p_conv, p_ssm, f"merge_fwd{l}")
        mix = _matmul(merged, wo[l], "nn", F32, f"mm_o{l}")
        x_mid, u2 = _resid_norm_fwd(x_in, mix, modp[l], 2, modp[l], 1, f"resid_norm_b{l}")
        h = mm_carry(u2, wup[l], BF16, f"mm_up{l}", ag_sets[1])
        hid = _relu2_fwd(h, f"relu2_fwd{l}")
        mlp = mm_carry(hid, wdown[l], F32, f"mm_down{l}", ag_sets[2])
        if l + 1 < nl:
            assemble(l + 1, nxt)
        saved.append(dict(x_in=x_in, u1=u1, proj=proj, y_conv=y_conv, xc=xc, dt=dt, ac=ac, yn=yn, states=states,
                          p_conv=p_conv, p_ssm=p_ssm, merged=merged, mix=mix, x_mid=x_mid, u2=u2, h=h, hid=hid,
                          mlp=mlp))
        x_cur, br_prev = x_mid, mlp

    dx, dbr, acc = _final_fwd_bwd(x_cur, br_prev, modp[nl - 1], fn8, tgt, "final")
    loss = lax.psum(acc[0, 0], ("x", "y", "c"))
    g_final_norm = acc[1]
    dgate2 = acc[2]

    gw = {k: [None] * nl for k in big}
    small = {k: [None] * nl for k in ["mod", "ln1", "ln2", "conv_w", "ssm_conv_w", "ssm_conv_b", "dt_bias",
                                      "a_log", "d_skip", "ssm_norm_w"]}

    def grad_cols(a):
        return jnp.moveaxis(a.reshape(a.shape[0], N_DEV, a.shape[1] // N_DEV), 1, 0)

    def grad_rows(a):
        return a.reshape(N_DEV, a.shape[0] // N_DEV, a.shape[1])

    def recv_buf(name):
        return jnp.zeros((N_DEV, nl) + w_of[name].shape[1:], BF16)

    bufs_a, bufs_b = [recv_buf("w_in")], [recv_buf(k) for k in big[1:]]
    pending = None
    for l in reversed(range(nl)):
        s = saved[l]
        dhid = _matmul(dbr, wdown[l], "nt", F32, f"mm_down_dx{l}")
        gw["w_down"][l] = _matmul(s["hid"], dbr, "tn", BF16, f"mm_down_dw{l}")
        dh = _relu2_bwd(dhid, s["h"], f"relu2_bwd{l}")
        du2 = _matmul(dh, wup[l], "nt", F32, f"mm_up_dx{l}")
        gw["w_up"][l] = _matmul(s["u2"], dh, "tn", BF16, f"mm_up_dw{l}")
        dx_mid, dmix, acc2 = _resid_norm_bwd(dx, du2, s["x_mid"], modp[l], 1, f"resid_norm_b_bwd{l}",
                                             br=s["mix"], mp_gate=modp[l], gate_row=2)
        dmerged = _matmul(dmix, wo[l], "nt", F32, f"mm_o_dx{l}")
        gw["w_o"][l] = _matmul(s["merged"], dmix, "tn", BF16, f"mm_o_dw{l}")
        dpc, dps, dproj = _merge_bwd(dmerged, s["proj"], s["p_conv"], s["p_ssm"], f"merge_bwd{l}")
        dyc = _matmul(dpc, wco[l], "nt", F32, f"mm_conv_out_dx{l}")
        gw["w_conv_out"][l] = _matmul(s["y_conv"], dpc, "tn", BF16, f"mm_conv_out_dw{l}")
        dyn = _matmul(dps, wso[l], "nt", F32, f"mm_ssm_out_dx{l}")
        gw["w_ssm_out"][l] = _matmul(s["yn"], dps, "tn", BF16, f"mm_ssm_out_dw{l}")
        dproj, dxc, ddt_g, dac_g, dsp = _ssd_bwd(dyn, s["xc"], s["proj"], s["dt"], s["ac"], sp8[l], s["states"],
                                                 dproj, ng, z_col, f"ssd_bwd{l}")
        dproj, dcw = _gconv_bwd(dyc, s["proj"], cw8[l], dproj, d, f"gconv_bwd{l}")
        dproj, dsw = _sconv_bwd(dxc, s["proj"], sw8[l], sb8[l], dproj, d, xbc_col, ncol, f"sconv_bwd{l}")
        dproj, ddtp = _dt_bwd(ddt_g, dac_g, s["dt"], s["proj"], dtp8[l], dproj, nh, dt_col, f"dt_bwd{l}")
        if pending is None:
            du1 = _matmul(dproj, wi[l], "nt", F32, f"mm_in_dx{l}")
            g_in = _matmul(s["u1"], dproj, "tn", BF16, f"mm_in_dw{l}")
        else:
            du1, bufs_a = _matmul(dproj, wi[l], "nt", F32, f"mm_in_dx{l}",
                                  comm=_ExchangeComm(pending[0], bufs_a, l + 1))
            g_in, bufs_b = _matmul(s["u1"], dproj, "tn", BF16, f"mm_in_dw{l}",
                                   comm=_ExchangeComm(pending[1], bufs_b, l + 1))
        g_in = jnp.concatenate([g_in[:, :xbc_off], from_group_major(g_in[:, xbc_off:dt_off]),
                                g_in[:, dt_off:proj_w]], axis=-1)
        pending = ([grad_cols(g_in)],
                   [grad_cols(gw["w_up"][l])] + [grad_rows(gw[k][l]) for k in big[2:]])
        if l > 0:
            dx, dbr, acc1 = _resid_norm_bwd(dx_mid, du1, s["x_in"], modp[l], 0, f"resid_norm_a_bwd{l}",
                                            br=saved[l - 1]["mlp"], mp_gate=modp[l - 1], gate_row=5)
        else:
            dx, acc1 = _resid_norm_bwd(dx_mid, du1, s["x_in"], modp[0], 0, "norm_first_bwd")
        sc1, sc2 = mod[l, 1], mod[l, 4]
        small["mod"][l] = jnp.stack([acc1[0], acc1[1] * ln1[l], acc2[2], acc2[0], acc2[1] * ln2[l], dgate2])
        small["ln1"][l] = acc1[1] * (1.0 + sc1)
        small["ln2"][l] = acc2[1] * (1.0 + sc2)
        small["conv_w"][l] = dcw[:3]
        small["ssm_conv_w"][l] = from_group_major(dsw[:4])
        small["ssm_conv_b"][l] = from_group_major(dsw[4])
        small["dt_bias"][l] = ddtp[0, :nh]
        small["a_log"][l] = ddtp[1, :nh]
        small["d_skip"][l] = dsp[0].reshape(nh, HEAD_DIM).sum(axis=-1)
        small["ssm_norm_w"][l] = dsp[1]
        if l > 0:
            dgate2 = acc1[2]
    grad_x = dx[None]

    sm = {k: jnp.stack(v) for k, v in small.items()}
    rep_names = ["b_ada", "ln1", "ln2", "ssm_conv_b", "dt_bias", "a_log", "d_skip", "ssm_norm_w", "final_norm"]
    rep_grads = [sm["mod"].reshape(nl, 6 * d), sm["ln1"], sm["ln2"], sm["ssm_conv_b"], sm["dt_bias"], sm["a_log"],
                 sm["d_skip"], sm["ssm_norm_w"], g_final_norm]
    rep_pack = _pack(rep_grads, 8 * LANE)
    conv_pack = _pack([sm["conv_w"], sm["ssm_conv_w"]], 8 * LANE)
    g_rep, g_conv = _all_gather([rep_pack, conv_pack], "ag_small_grads")

    outs = {}

    def run_adamw(name, parts, tag):
        w, m, v = w_of[name], m_of[name], v_of[name]
        shp = w.shape
        r2 = (int(np.prod(shp[:-1])), shp[-1])
        res = _adamw(parts.reshape((parts.shape[0],) + r2), w.reshape(r2), m.reshape(r2), v.reshape(r2), tag)
        outs[name] = [a.reshape(shp) for a in res]

    rep_shapes = [w_of[k].shape for k in rep_names]
    res = _adamw(g_rep, _pack([w_of[k] for k in rep_names], 8 * LANE), _pack([m_of[k] for k in rep_names], 8 * LANE),
                 _pack([v_of[k] for k in rep_names], 8 * LANE), "adamw_replicated")
    for k, vals in zip(rep_names, zip(*[_unpack(a.reshape(-1), rep_shapes) for a in res])):
        outs[k] = list(vals)

    conv_parts = [_unpack(g_conv[j].reshape(-1), [sm["conv_w"].shape, sm["ssm_conv_w"].shape]) for j in range(N_DEV)]
    for idx, name in enumerate(["conv_w", "ssm_conv_w"]):
        wsh = w_of[name].shape[-1]
        full = jnp.stack([p[idx] for p in conv_parts])
        run_adamw(name, lax.dynamic_slice_in_dim(full, me * wsh, wsh, axis=3), "adamw_" + name)

    dmod_all = g_rep.reshape(N_DEV, -1)[:, :nl * 6 * d].reshape(N_DEV, nl, 6 * d)
    dmod_mine = lax.dynamic_slice_in_dim(dmod_all, me * ada_w, ada_w, axis=2)
    g_ada = jnp.stack([_small_matmul(c_act_all, dmod_mine[:, l], "tn", f"ada_bwd{l}") for l in range(nl)])
    run_adamw("w_ada", g_ada[None], "adamw_w_ada")

    recv = _exchange(pending[0] + pending[1], bufs_a + bufs_b, 0, "a2a_grads0")
    for name, parts in zip(big, recv):
        run_adamw(name, parts, "adamw_" + name)

    result = [loss, grad_x]
    for i in range(4):
        result += [outs[k][i] for k in names]
    return tuple(result)
```

```python
import functools

import numpy as np
import jax
import jax.numpy as jnp
from jax import lax
from jax.experimental import pallas as pl
from jax.experimental.pallas import tpu as pltpu

F32 = jnp.float32
BF16 = jnp.bfloat16
EPS = 1e-6
N_STATE = 128
HEAD_DIM = 64
HEADS_PER_GROUP = 4
GROUP_W = HEAD_DIM * HEADS_PER_GROUP
N_DEV = 8
ROW_TILE = 256
LANE = 128
VMEM_LIMIT = 56 * 1024 * 1024

ADAM_LR, ADAM_B1, ADAM_B2, ADAM_EPS, ADAM_WD, ADAM_STEP = 0.001, 0.9, 0.999, 1e-08, 0.01, 10

MESH = pl.DeviceIdType.MESH


def _params(sem=None):
    return pltpu.CompilerParams(dimension_semantics=sem, vmem_limit_bytes=VMEM_LIMIT)


def _tile(n, target):
    if n <= target:
        return n
    best = None
    for d in range(LANE, target + 1, LANE):
        if n % d == 0:
            best = d
    assert best is not None, (n, target)
    return best


def _slot(p):
    return 4 * p[0] + 2 * p[1] + p[2]


def _all_gather(xs, name):
    comm = _GatherComm(xs)

    def body(*refs):
        begin, end = comm.ops(refs[:comm.n_in], refs[comm.n_in:comm.n_in + comm.n_out], refs[-3:])
        begin()
        end()

    any_spec = pl.BlockSpec(memory_space=pl.ANY)
    return pl.pallas_call(
        body, name=name, out_shape=comm.out_shape,
        in_specs=[any_spec] * comm.n_in, out_specs=[any_spec] * comm.n_out, scratch_shapes=comm.sems,
    )(*comm.operands)


class _GatherComm:
    def __init__(self, xs):
        n = len(xs)
        self.operands = list(xs)
        self.n_in = self.n_out = n
        self.aliases = {}
        self.out_shape = [jax.ShapeDtypeStruct((N_DEV,) + x.shape, x.dtype) for x in xs]
        self.sems = [pltpu.SemaphoreType.DMA((7 * n,)), pltpu.SemaphoreType.DMA((7 * n,)),
                     pltpu.SemaphoreType.DMA((n,))]

    def ops(self, x_refs, o_refs, sem_refs):
        n = self.n_in
        send, recv, loc = sem_refs
        x, y, c = lax.axis_index("x"), lax.axis_index("y"), lax.axis_index("c")
        me, sib = (x, y, c), (x, y, 1 - c)
        chips = [(1 - x, y), (x, 1 - y), (1 - x, 1 - y)]

        def cp(a, k, block, to, src=None):
            dst = o_refs[a].at[_slot(block)]
            return pltpu.make_async_remote_copy(
                src_ref=dst if src is None else src, dst_ref=dst,
                send_sem=send.at[7 * a + k], recv_sem=recv.at[7 * a + k],
                device_id=to, device_id_type=MESH)

        mine = [pltpu.make_async_copy(x_refs[a], o_refs[a].at[_slot(me)], loc.at[a]) for a in range(n)]
        first = []
        for a in range(n):
            first.append(cp(a, 0, me, sib, src=x_refs[a]))
            first += [cp(a, 1 + j, me, (*chip, c), src=x_refs[a]) for j, chip in enumerate(chips)]

        def begin():
            for m in mine:
                m.start()
            for f in first:
                f.start()

        def end():
            passed = []
            for j, chip in enumerate(chips):
                for a in range(n):
                    cp(a, 1 + j, (*chip, c), me).wait_recv()
                    p = cp(a, 4 + j, (*chip, c), sib)
                    p.start()
                    passed.append(p)
            for a in range(n):
                cp(a, 0, sib, me).wait_recv()
                for j, chip in enumerate(chips):
                    cp(a, 4 + j, (*chip, 1 - c), me).wait_recv()
            for f in first + passed:
                f.wait_send()
            for m in mine:
                m.wait()

        return begin, end


class _ExchangeComm:
    def __init__(self, xs, bufs, layer, row0=0):
        n = len(xs)
        self.layer = layer
        self.row0 = row0
        self.nrows = [x.shape[1] for x in xs]
        self.operands = list(xs) + list(bufs)
        self.n_in, self.n_out = 2 * n, n
        self.aliases = {n + a: a for a in range(n)}
        self.out_shape = [jax.ShapeDtypeStruct(b.shape, b.dtype) for b in bufs]
        self.sems = [pltpu.SemaphoreType.DMA((7 * n,)), pltpu.SemaphoreType.DMA((7 * n,)),
                     pltpu.SemaphoreType.DMA((n,))]

    def ops(self, in_refs, o_refs, sem_refs):
        n = self.n_out
        x_refs = in_refs[:n]
        send, recv, loc = sem_refs
        layer = self.layer
        x, y, c = lax.axis_index("x"), lax.axis_index("y"), lax.axis_index("c")
        me = (x, y, c)
        peers = []
        for k in range(1, 8):
            kx, ky, kc = (k >> 2) & 1, (k >> 1) & 1, k & 1
            peers.append((x + kx - 2 * x * kx, y + ky - 2 * y * ky, c + kc - 2 * c * kc))

        def land(a, slot):
            return o_refs[a].at[slot, layer, pl.ds(self.row0, self.nrows[a])]

        def cp(a, k, src_slot, dst_slot, to):
            return pltpu.make_async_remote_copy(
                src_ref=x_refs[a].at[src_slot], dst_ref=land(a, dst_slot),
                send_sem=send.at[7 * a + k], recv_sem=recv.at[7 * a + k],
                device_id=to, device_id_type=MESH)

        mine = [pltpu.make_async_copy(x_refs[a].at[_slot(me)], land(a, _slot(me)), loc.at[a])
                for a in range(n)]
        sends = [cp(a, k, _slot(p), _slot(me), p) for a in range(n) for k, p in enumerate(peers)]

        def begin():
            for m in mine:
                m.start()
            for s in sends:
                s.start()

        def end():
            for a in range(n):
                for k, p in enumerate(peers):
                    cp(a, k, _slot(me), _slot(p), me).wait_recv()
            for s in sends:
                s.wait_send()
            for m in mine:
                m.wait()

        return begin, end


def _exchange(xs, bufs, layer, name, row0=0):
    comm = _ExchangeComm(xs, bufs, layer, row0)

    def body(*refs):
        begin, end = comm.ops(refs[:comm.n_in], refs[comm.n_in:comm.n_in + comm.n_out], refs[-3:])
        begin()
        end()

    any_spec = pl.BlockSpec(memory_space=pl.ANY)
    return pl.pallas_call(
        body, name=name, out_shape=comm.out_shape,
        in_specs=[any_spec] * comm.n_in, out_specs=[any_spec] * comm.n_out, scratch_shapes=comm.sems,
        input_output_aliases=dict(comm.aliases),
    )(*comm.operands)


_DIMS = {"nn": (((1,), (0,)), ((), ())), "nt": (((1,), (1,)), ((), ())), "tn": (((0,), (0,)), ((), ()))}


def _matmul(a, b, mode, out_dtype, name, comm=None, epi=None, extra=None):
    if mode == "nn":
        (m, k), (_, n) = a.shape, b.shape
    elif mode == "nt":
        (m, k), (n, _) = a.shape, b.shape
    else:
        (k, m), (_, n) = a.shape, b.shape
    tm, tn, tk = _tile(m, 1024), _tile(n, 1024), _tile(k, 1024)
    nk = k // tk
    grid = (m // tm, n // tn, nk)
    dn = _DIMS[mode]
    n_cin = comm.n_in if comm else 0
    n_cout = comm.n_out if comm else 0
    n_acc = 0 if nk == 1 else 1
    n_ext = 1 if epi == "relu2_bwd" else 0
    n_main = 2 if epi == "relu2" else 1
    first_in, first_out = 2 + n_ext, 2 + n_ext + n_cin

    def finish(acc, refs):
        outs = refs[first_out:first_out + n_main]
        if epi == "relu2":
            outs[0][...] = acc.astype(outs[0].dtype)
            r = jnp.maximum(acc, 0.0)
            outs[1][...] = (r * r).astype(outs[1].dtype)
        elif epi == "relu2_bwd":
            outs[0][...] = (acc * (2.0 * jnp.maximum(refs[2][...].astype(F32), 0.0))).astype(outs[0].dtype)
        else:
            outs[0][...] = acc.astype(outs[0].dtype)

    def body(*refs):
        a_ref, b_ref = refs[:2]
        scratch = refs[first_out + n_main + n_cout:]
        pid = [pl.program_id(ax) for ax in range(3)]
        if comm:
            begin, end = comm.ops(refs[first_in:first_in + n_cin],
                                  refs[first_out + n_main:first_out + n_main + n_cout], scratch[n_acc:])
            pl.when((pid[0] == 0) & (pid[1] == 0) & (pid[2] == 0))(begin)
        part = lax.dot_general(a_ref[...], b_ref[...], dn, preferred_element_type=F32)
        if nk == 1:
            finish(part, refs)
        else:
            acc_ref = scratch[0]

            @pl.when(pid[2] == 0)
            def _():
                acc_ref[...] = part

            @pl.when(pid[2] > 0)
            def _():
                acc_ref[...] += part

            @pl.when(pid[2] == nk - 1)
            def _():
                finish(acc_ref[...], refs)
        if comm:
            pl.when((pid[0] == grid[0] - 1) & (pid[1] == grid[1] - 1) & (pid[2] == grid[2] - 1))(end)

    a_spec = pl.BlockSpec((tk, tm), lambda i, j, kk: (kk, i)) if mode == "tn" else \
        pl.BlockSpec((tm, tk), lambda i, j, kk: (i, kk))
    b_spec = pl.BlockSpec((tn, tk), lambda i, j, kk: (j, kk)) if mode == "nt" else \
        pl.BlockSpec((tk, tn), lambda i, j, kk: (kk, j))
    o_spec = pl.BlockSpec((tm, tn), lambda i, j, kk: (i, j))
    any_spec = pl.BlockSpec(memory_space=pl.ANY)
    res = pl.pallas_call(
        body, name=name,
        out_shape=[jax.ShapeDtypeStruct((m, n), out_dtype)] * n_main + (comm.out_shape if comm else []),
        grid=grid,
        in_specs=[a_spec, b_spec] + [o_spec] * n_ext + [any_spec] * n_cin,
        out_specs=[o_spec] * n_main + [any_spec] * n_cout,
        scratch_shapes=([] if nk == 1 else [pltpu.VMEM((tm, tn), F32)]) + (comm.sems if comm else []),
        input_output_aliases={first_in + i: n_main + o for i, o in comm.aliases.items()} if comm else {},
        compiler_params=_params(("arbitrary",) * 3 if comm else ("parallel", "parallel", "arbitrary")),
    )(a, b, *([extra] if n_ext else []), *(comm.operands if comm else []))
    main = res[0] if n_main == 1 else tuple(res[:n_main])
    return (main, list(res[n_main:])) if comm else main


def _dot(a, b, mode):
    return lax.dot_general(a.astype(BF16), b.astype(BF16), _DIMS[mode], preferred_element_type=F32)


@functools.partial(jax.custom_vjp, nondiff_argnums=(2,))
def _bdot(a, b, mode):
    return _dot(a, b, mode)


def _bdot_fwd(a, b, mode):
    return _dot(a, b, mode), (a, b)


def _bdot_bwd(mode, res, g):
    a, b = res
    if mode == "nn":
        return _dot(g, b, "nt"), _dot(a, g, "tn")
    if mode == "nt":
        return _dot(g, b, "nn"), _dot(g, a, "tn")
    return _dot(b, g, "nt"), _dot(a, g, "nn")


_bdot.defvjp(_bdot_fwd, _bdot_bwd)


def _xdot01(m01, x):
    hi = x.astype(BF16)
    r1 = x - hi.astype(F32)
    mid = r1.astype(BF16)
    lo = (r1 - mid.astype(F32)).astype(BF16)
    dn = _DIMS["nn"]
    return (lax.dot_general(m01, hi, dn, preferred_element_type=F32)
            + lax.dot_general(m01, mid, dn, preferred_element_type=F32)
            + lax.dot_general(m01, lo, dn, preferred_element_type=F32))


def _chunk_tri(n, lower):
    r = lax.broadcasted_iota(jnp.int32, (n, n), 0)
    c = lax.broadcasted_iota(jnp.int32, (n, n), 1)
    return jnp.where((r >= c) if lower else (r <= c), 1.0, 0.0).astype(BF16)


def _shift_down(x, prev8, k):
    if k == 0:
        return x
    n = x.shape[0]
    r = pltpu.roll(x, k, 0)
    rp = pltpu.roll(prev8, k, 0)
    rows = lax.broadcasted_iota(jnp.int32, (8, x.shape[1]), 0)
    head = jnp.where(rows < k, rp, r[:8])
    return head if n == 8 else jnp.concatenate([head, r[8:]], axis=0)


def _shift_up(x, next8, k):
    if k == 0:
        return x
    n = x.shape[0]
    r = pltpu.roll(x, n - k, 0)
    rn = pltpu.roll(next8, 8 - k, 0)
    rows = lax.broadcasted_iota(jnp.int32, (8, x.shape[1]), 0)
    tail = jnp.where(rows >= 8 - k, rn, r[n - 8:])
    return tail if n == 8 else jnp.concatenate([r[:n - 8], tail], axis=0)


def _sigmoid(x):
    return 0.5 * jnp.tanh(0.5 * x) + 0.5


def _rows(ref, i):
    return ref[i:i + 1, :]


def _norm_parts(x, mp_ref, which):
    ln, sh, sc = _rows(mp_ref, 6 + which), _rows(mp_ref, 3 * which), _rows(mp_ref, 3 * which + 1)
    r = lax.rsqrt(jnp.mean(x * x, axis=1, keepdims=True) + EPS)
    return r, ln, sh, sc


def _norm_fwd(x, modp, which, name):
    t, d = x.shape

    def body(x_ref, mp_ref, u_ref):
        xv = x_ref[...]
        r, ln, sh, sc = _norm_parts(xv, mp_ref, which)
        u_ref[...] = (((xv * r) * ln) * (1.0 + sc) + sh).astype(BF16)

    row = pl.BlockSpec((ROW_TILE, d), lambda i: (i, 0))
    return pl.pallas_call(
        body, name=name, out_shape=jax.ShapeDtypeStruct((t, d), BF16), grid=(t // ROW_TILE,),
        in_specs=[row, pl.BlockSpec((8, d), lambda i: (0, 0))], out_specs=row,
        compiler_params=_params(("parallel",)))(x, modp)


def _resid_norm_fwd(x, br, mp_gate, gate_row, mp_norm, which, name):
    t, d = x.shape

    def body(x_ref, br_ref, mg_ref, mn_ref, xn_ref, u_ref):
        xv = x_ref[...] + _rows(mg_ref, gate_row) * br_ref[...]
        xn_ref[...] = xv
        r, ln, sh, sc = _norm_parts(xv, mn_ref, which)
        u_ref[...] = (((xv * r) * ln) * (1.0 + sc) + sh).astype(BF16)

    row = pl.BlockSpec((ROW_TILE, d), lambda i: (i, 0))
    mp = pl.BlockSpec((8, d), lambda i: (0, 0))
    return pl.pallas_call(
        body, name=name,
        out_shape=[jax.ShapeDtypeStruct((t, d), F32), jax.ShapeDtypeStruct((t, d), BF16)],
        grid=(t // ROW_TILE,), in_specs=[row, row, mp, mp], out_specs=[row, row],
        compiler_params=_params(("parallel",)))(x, br, mp_gate, mp_norm)


def _final_fwd_bwd(x, br, mp_gate, fnorm8, target, name):
    t, d = x.shape

    def body(x_ref, br_ref, mg_ref, fn_ref, tg_ref, dx_ref, dbr_ref, acc_ref):
        gate = _rows(mg_ref, 5)
        brv = br_ref[...]
        xv = x_ref[...] + gate * brv
        fn = _rows(fn_ref, 0)
        r = lax.rsqrt(jnp.mean(xv * xv, axis=1, keepdims=True) + EPS)
        nrm = xv * r
        err = nrm * fn - tg_ref[...]
        loss = 0.5 * jnp.sum(jnp.mean(err * err, axis=1, keepdims=True), axis=0, keepdims=True)
        dy = err * (1.0 / d)
        dn = dy * fn
        dx = r * (dn - nrm * jnp.mean(dn * nrm, axis=1, keepdims=True))
        dx_ref[...] = dx
        dbr_ref[...] = (dx * gate).astype(BF16)
        upd = jnp.concatenate([
            jnp.broadcast_to(loss, (1, d)),
            jnp.sum(dy * nrm, axis=0, keepdims=True),
            jnp.sum(dx * brv, axis=0, keepdims=True),
            jnp.zeros((5, d), F32)], axis=0)

        @pl.when(pl.program_id(0) == 0)
        def _():
            acc_ref[...] = upd

        @pl.when(pl.program_id(0) > 0)
        def _():
            acc_ref[...] += upd

    row = pl.BlockSpec((ROW_TILE, d), lambda i: (i, 0))
    mp = pl.BlockSpec((8, d), lambda i: (0, 0))
    return pl.pallas_call(
        body, name=name,
        out_shape=[jax.ShapeDtypeStruct((t, d), F32), jax.ShapeDtypeStruct((t, d), BF16),
                   jax.ShapeDtypeStruct((8, d), F32)],
        grid=(t // ROW_TILE,), in_specs=[row, row, mp, mp, row], out_specs=[row, row, mp],
        compiler_params=_params(("arbitrary",)))(x, br, mp_gate, fnorm8, target)


def _resid_norm_bwd(dx, du, x, mp_norm, which, name, br=None, mp_gate=None, gate_row=None):
    t, d = x.shape
    has_gate = br is not None

    def body(*refs):
        if has_gate:
            dx_ref, du_ref, x_ref, mn_ref, br_ref, mg_ref, dxn_ref, dbr_ref, acc_ref = refs
        else:
            dx_ref, du_ref, x_ref, mn_ref, dxn_ref, acc_ref = refs
        xv, duv = x_ref[...], du_ref[...]
        r, ln, _, sc = _norm_parts(xv, mn_ref, which)
        nrm = xv * r
        dn = duv * (ln * (1.0 + sc))
        dxn = dx_ref[...] + r * (dn - nrm * jnp.mean(dn * nrm, axis=1, keepdims=True))
        dxn_ref[...] = dxn
        rows = [jnp.sum(duv, axis=0, keepdims=True), jnp.sum(duv * nrm, axis=0, keepdims=True)]
        if has_gate:
            dbr_ref[...] = (dxn * _rows(mg_ref, gate_row)).astype(BF16)
            rows.append(jnp.sum(dxn * br_ref[...], axis=0, keepdims=True))
        upd = jnp.concatenate(rows + [jnp.zeros((8 - len(rows), d), F32)], axis=0)

        @pl.when(pl.program_id(0) == 0)
        def _():
            acc_ref[...] = upd

        @pl.when(pl.program_id(0) > 0)
        def _():
            acc_ref[...] += upd

    row = pl.BlockSpec((ROW_TILE, d), lambda i: (i, 0))
    mp = pl.BlockSpec((8, d), lambda i: (0, 0))
    ins, in_specs = [dx, du, x, mp_norm], [row, row, row, mp]
    outs = [jax.ShapeDtypeStruct((t, d), F32)]
    out_specs = [row]
    if has_gate:
        ins += [br, mp_gate]
        in_specs += [row, mp]
        outs.append(jax.ShapeDtypeStruct((t, d), BF16))
        out_specs.append(row)
    outs.append(jax.ShapeDtypeStruct((8, d), F32))
    out_specs.append(mp)
    return pl.pallas_call(
        body, name=name, out_shape=outs, grid=(t // ROW_TILE,), in_specs=in_specs, out_specs=out_specs,
        compiler_params=_params(("arbitrary",)))(*ins)


def _merge_fwd(proj, p_conv, p_ssm, name):
    t, d = p_conv.shape

    def body(gl_ref, pc_ref, ps_ref, o_ref):
        g = _sigmoid(gl_ref[...].astype(F32))
        o_ref[...] = (g[:, :d] * pc_ref[...] + g[:, d:] * ps_ref[...]).astype(BF16)

    row = pl.BlockSpec((ROW_TILE, d), lambda i: (i, 0))
    return pl.pallas_call(
        body, name=name, out_shape=jax.ShapeDtypeStruct((t, d), BF16), grid=(t // ROW_TILE,),
        in_specs=[pl.BlockSpec((ROW_TILE, 2 * d), lambda i: (i, 0)), row, row], out_specs=row,
        compiler_params=_params(("parallel",)))(proj, p_conv, p_ssm)


def _merge_bwd(dmerged, proj, p_conv, p_ssm, name):
    t, d = p_conv.shape
    pw = proj.shape[1]

    def body(dm_ref, gl_ref, pc_ref, ps_ref, dpc_ref, dps_ref, dgl_ref):
        g = _sigmoid(gl_ref[...].astype(F32))
        gc, gs = g[:, :d], g[:, d:]
        dm = dm_ref[...]
        dpc_ref[...] = (dm * gc).astype(BF16)
        dps_ref[...] = (dm * gs).astype(BF16)
        dgl_ref[...] = jnp.concatenate(
            [dm * pc_ref[...] * gc * (1.0 - gc), dm * ps_ref[...] * gs * (1.0 - gs)], axis=1).astype(BF16)

    row = pl.BlockSpec((ROW_TILE, d), lambda i: (i, 0))
    wide = pl.BlockSpec((ROW_TILE, 2 * d), lambda i: (i, 0))
    return pl.pallas_call(
        body, name=name,
        out_shape=[jax.ShapeDtypeStruct((t, d), BF16), jax.ShapeDtypeStruct((t, d), BF16),
                   jax.ShapeDtypeStruct((t, pw), BF16)],
        grid=(t // ROW_TILE,), in_specs=[row, wide, row, row], out_specs=[row, row, wide],
        compiler_params=_params(("parallel",)))(dmerged, proj, p_conv, p_ssm)


def _halo_specs(t, width, col):
    nb = t // 8
    step = ROW_TILE // 8
    prev = pl.BlockSpec((8, width), lambda i: (jnp.maximum(i * step - 1, 0), col))
    nxt = pl.BlockSpec((8, width), lambda i: (jnp.minimum((i + 1) * step, nb - 1), col))
    return prev, nxt


def _gconv_fwd(proj, conv_w8, d, name):
    t = proj.shape[0]

    def body(cb_ref, cc_ref, cx_ref, ccp_ref, cxp_ref, w_ref, o_ref):
        first = pl.program_id(0) == 0
        v = cc_ref[...].astype(F32) * cx_ref[...].astype(F32)
        vp = jnp.where(first, 0.0, ccp_ref[...].astype(F32) * cxp_ref[...].astype(F32))
        cv = sum(_rows(w_ref, k) * _shift_down(v, vp, 2 - k) for k in range(3))
        o_ref[...] = (cb_ref[...].astype(F32) * cv).astype(BF16)

    def win(col):
        return pl.BlockSpec((ROW_TILE, d), lambda i: (i, col))

    return pl.pallas_call(
        body, name=name, out_shape=jax.ShapeDtypeStruct((t, d), BF16), grid=(t // ROW_TILE,),
        in_specs=[win(2), win(3), win(4), _halo_specs(t, d, 3)[0], _halo_specs(t, d, 4)[0],
                  pl.BlockSpec((8, d), lambda i: (0, 0))],
        out_specs=pl.BlockSpec((ROW_TILE, d), lambda i: (i, 0)),
        compiler_params=_params(("parallel",)))(proj, proj, proj, proj, proj, conv_w8)


def _gconv_bwd(dy, proj, conv_w8, dproj, d, name):
    t = proj.shape[0]
    nt = t // ROW_TILE

    def body(dy_ref, dyn_ref, cb_ref, cc_ref, cx_ref, ccp_ref, cxp_ref, cbn_ref, w_ref, _, dp_ref, dw_ref, st_ref):
        i, j = pl.program_id(0), pl.program_id(1)

        @pl.when(j == 0)
        def _():
            cb, cc, cx = cb_ref[...].astype(F32), cc_ref[...].astype(F32), cx_ref[...].astype(F32)
            v = cc * cx
            vp = jnp.where(i == 0, 0.0, ccp_ref[...].astype(F32) * cxp_ref[...].astype(F32))
            sh = [_shift_down(v, vp, 2 - k) for k in range(3)]
            cv = sum(_rows(w_ref, k) * sh[k] for k in range(3))
            dyv = dy_ref[...]
            dcv = dyv * cb
            dcvn = jnp.where(i == nt - 1, 0.0, dyn_ref[...] * cbn_ref[...].astype(F32))
            dv = sum(_rows(w_ref, k) * _shift_up(dcv, dcvn, 2 - k) for k in range(3))
            st_ref[0] = (dyv * cv).astype(BF16)
            st_ref[1] = (dv * cx).astype(BF16)
            st_ref[2] = (dv * cc).astype(BF16)
            upd = jnp.concatenate([jnp.sum(dcv * sh[k], axis=0, keepdims=True) for k in range(3)]
                                  + [jnp.zeros((5, d), F32)], axis=0)

            @pl.when(i == 0)
            def _():
                dw_ref[...] = upd

            @pl.when(i > 0)
            def _():
                dw_ref[...] += upd

        dp_ref[...] = st_ref[j]

    def win(col):
        return pl.BlockSpec((ROW_TILE, d), lambda i, j: (i, col))

    def halo(col, which):
        nb, step = t // 8, ROW_TILE // 8
        if which == 0:
            return pl.BlockSpec((8, d), lambda i, j: (jnp.maximum(i * step - 1, 0), col))
        return pl.BlockSpec((8, d), lambda i, j: (jnp.minimum((i + 1) * step, nb - 1), col))

    row = pl.BlockSpec((ROW_TILE, d), lambda i, j: (i, 0))
    w8 = pl.BlockSpec((8, d), lambda i, j: (0, 0))
    return pl.pallas_call(
        body, name=name,
        out_shape=[jax.ShapeDtypeStruct(dproj.shape, BF16), jax.ShapeDtypeStruct((8, d), F32)],
        grid=(nt, 3),
        in_specs=[row, halo(0, 1), win(2), win(3), win(4), halo(3, 0), halo(4, 0), halo(2, 1), w8,
                  pl.BlockSpec(memory_space=pl.ANY)],
        out_specs=[pl.BlockSpec((ROW_TILE, d), lambda i, j: (i, 2 + j)), w8],
        scratch_shapes=[pltpu.VMEM((3, ROW_TILE, d), BF16)],
        input_output_aliases={9: 0},
        compiler_params=_params(("arbitrary", "arbitrary")))(
            dy, dy, proj, proj, proj, proj, proj, proj, conv_w8, dproj)


def _dsilu(p):
    s = _sigmoid(p)
    return s * (1.0 + p * (1.0 - s))


def _sconv_specs(t, d, col0, nt):
    nb, step = t // 8, ROW_TILE // 8
    win = pl.BlockSpec((ROW_TILE, d), lambda j, i: (i, col0 + j))
    prev = pl.BlockSpec((8, d), lambda j, i: (jnp.maximum(i * step - 1, 0), col0 + j))
    nxt = pl.BlockSpec((8, d), lambda j, i: (jnp.minimum((i + 1) * step, nb - 1), col0 + j))
    return win, prev, nxt


def _sconv_fwd(proj, w8, b8, d, col0, ncol, name):
    t = proj.shape[0]
    nt = t // ROW_TILE

    def body(x_ref, xp_ref, w_ref, b_ref, o_ref):
        xv = x_ref[...].astype(F32)
        xp = jnp.where(pl.program_id(1) == 0, 0.0, xp_ref[...].astype(F32))
        pre = _rows(b_ref, 0) + sum(_rows(w_ref, k) * _shift_down(xv, xp, 3 - k) for k in range(4))
        o_ref[...] = (pre * _sigmoid(pre)).astype(BF16)

    win, prev, _ = _sconv_specs(t, d, col0, nt)
    par = pl.BlockSpec((8, d), lambda j, i: (0, j))
    return pl.pallas_call(
        body, name=name, out_shape=jax.ShapeDtypeStruct((t, ncol * d), BF16), grid=(ncol, nt),
        in_specs=[win, prev, par, par], out_specs=pl.BlockSpec((ROW_TILE, d), lambda j, i: (i, j)),
        compiler_params=_params(("parallel", "parallel")))(proj, proj, w8, b8)


def _sconv_bwd(dxc, proj, w8, b8, dproj, d, col0, ncol, name):
    t = proj.shape[0]
    nt = t // ROW_TILE

    def body(d_ref, dn_ref, x_ref, xp_ref, xn_ref, w_ref, b_ref, _, dp_ref, dw_ref):
        i = pl.program_id(1)
        xv = x_ref[...].astype(F32)
        xp = jnp.where(i == 0, 0.0, xp_ref[...].astype(F32))
        bias = _rows(b_ref, 0)
        sh = [_shift_down(xv, xp, 3 - k) for k in range(4)]
        pre = bias + sum(_rows(w_ref, k) * sh[k] for k in range(4))
        dpre = d_ref[...] * _dsilu(pre)
        xn = xn_ref[...].astype(F32)
        pre_n = bias + sum(_rows(w_ref, k) * _shift_down(xn, xv[ROW_TILE - 8:], 3 - k) for k in range(4))
        dpre_n = jnp.where(i == nt - 1, 0.0, dn_ref[...] * _dsilu(pre_n))
        dx = sum(_rows(w_ref, k) * _shift_up(dpre, dpre_n, 3 - k) for k in range(4))
        dp_ref[...] = dx.astype(BF16)
        upd = jnp.concatenate([jnp.sum(dpre * sh[k], axis=0, keepdims=True) for k in range(4)]
                              + [jnp.sum(dpre, axis=0, keepdims=True), jnp.zeros((3, d), F32)], axis=0)

        @pl.when(i == 0)
        def _():
            dw_ref[...] = upd

        @pl.when(i > 0)
        def _():
            dw_ref[...] += upd

    win, prev, nxt = _sconv_specs(t, d, col0, nt)
    nb, step = t // 8, ROW_TILE // 8
    dwin = pl.BlockSpec((ROW_TILE, d), lambda j, i: (i, j))
    dnxt = pl.BlockSpec((8, d), lambda j, i: (jnp.minimum((i + 1) * step, nb - 1), j))
    par = pl.BlockSpec((8, d), lambda j, i: (0, j))
    return pl.pallas_call(
        body, name=name,
        out_shape=[jax.ShapeDtypeStruct(dproj.shape, BF16), jax.ShapeDtypeStruct((8, ncol * d), F32)],
        grid=(ncol, nt),
        in_specs=[dwin, dnxt, win, prev, nxt, par, par, pl.BlockSpec(memory_space=pl.ANY)],
        out_specs=[win, par], input_output_aliases={7: 0},
        compiler_params=_params(("arbitrary", "arbitrary")))(dxc, dxc, proj, proj, proj, w8, b8, dproj)


def _softplus(x):
    return jnp.maximum(x, 0.0) + jnp.log(1.0 + jnp.exp(-jnp.abs(x)))


def _dt_fwd(proj, par8, nheads, dt_col, name):
    t = proj.shape[0]

    def body(r_ref, p_ref, dt_ref, ac_ref):
        lane = lax.broadcasted_iota(jnp.int32, (1, LANE), 1)
        valid = lane < nheads
        raw = r_ref[...][:, :LANE].astype(F32)
        dt = jnp.where(valid, _softplus(raw + _rows(p_ref, 0)), 0.0)
        a = jnp.where(valid, -jnp.exp(_rows(p_ref, 1)), 0.0)
        dt_ref[...] = dt
        ac_ref[...] = _xdot01(_chunk_tri(ROW_TILE, True), dt * a)

    out = pl.BlockSpec((ROW_TILE, LANE), lambda i: (i, 0))
    return pl.pallas_call(
        body, name=name,
        out_shape=[jax.ShapeDtypeStruct((t, LANE), F32), jax.ShapeDtypeStruct((t, LANE), F32)],
        grid=(t // ROW_TILE,),
        in_specs=[pl.BlockSpec((ROW_TILE, 2 * LANE), lambda i: (i, dt_col)), pl.BlockSpec((8, LANE), lambda i: (0, 0))],
        out_specs=[out, out], compiler_params=_params(("parallel",)))(proj, par8)


def _dt_bwd(ddt_g, dac_g, dt, proj, par8, dproj, nheads, dt_col, name):
    t = proj.shape[0]
    ng = ddt_g.shape[0]

    def body(dd_ref, da_ref, dt_ref, r_ref, p_ref, _, dp_ref, acc_ref):
        lane = lax.broadcasted_iota(jnp.int32, (1, LANE), 1)
        valid = lane < nheads
        ddt = sum(dd_ref[g] for g in range(ng))
        dac = sum(da_ref[g] for g in range(ng))
        a = jnp.where(valid, -jnp.exp(_rows(p_ref, 1)), 0.0)
        d_a = _xdot01(_chunk_tri(ROW_TILE, False), dac)
        ddt = ddt + d_a * a
        raw = r_ref[...][:, :LANE].astype(F32)
        draw = jnp.where(valid, ddt * _sigmoid(raw + _rows(p_ref, 0)), 0.0)
        dp_ref[...] = jnp.concatenate([draw, jnp.zeros_like(draw)], axis=1).astype(BF16)
        upd = jnp.concatenate([jnp.sum(draw, axis=0, keepdims=True),
                               jnp.sum(d_a * dt_ref[...], axis=0, keepdims=True) * a,
                               jnp.zeros((6, LANE), F32)], axis=0)

        @pl.when(pl.program_id(0) == 0)
        def _():
            acc_ref[...] = upd

        @pl.when(pl.program_id(0) > 0)
        def _():
            acc_ref[...] += upd

    row = pl.BlockSpec((ROW_TILE, LANE), lambda i: (i, 0))
    grp = pl.BlockSpec((ng, ROW_TILE, LANE), lambda i: (0, i, 0))
    win = pl.BlockSpec((ROW_TILE, 2 * LANE), lambda i: (i, dt_col))
    par = pl.BlockSpec((8, LANE), lambda i: (0, 0))
    return pl.pallas_call(
        body, name=name,
        out_shape=[jax.ShapeDtypeStruct(dproj.shape, BF16), jax.ShapeDtypeStruct((8, LANE), F32)],
        grid=(t // ROW_TILE,),
        in_specs=[grp, grp, row, win, par, pl.BlockSpec(memory_space=pl.ANY)],
        out_specs=[win, par], input_output_aliases={5: 0},
        compiler_params=_params(("arbitrary",)))(ddt_g, dac_g, dt, proj, par8, dproj)


def _ssd_chunk(xs, bm, cm, z, dt, ac, st, dsk, nw, g):
    q = xs.shape[0]
    lane_h = lax.broadcasted_iota(jnp.int32, (1, LANE), 1)
    sub_h = lax.broadcasted_iota(jnp.int32, (LANE, 1), 0)
    lane_c = lax.broadcasted_iota(jnp.int32, (1, GROUP_W), 1) // HEAD_DIM
    rr = lax.broadcasted_iota(jnp.int32, (q, q), 0)
    cc = lax.broadcasted_iota(jnp.int32, (q, q), 1)
    tril = rr >= cc
    last_row = jnp.where(lax.broadcasted_iota(jnp.int32, (q, 1), 0) == q - 1, 1.0, 0.0)
    act = ac.T
    col, row, dtc = [], [], []
    for r in range(HEADS_PER_GROUP):
        h = g * HEADS_PER_GROUP + r
        sel = jnp.where(lane_h == h, 1.0, 0.0)
        col.append(jnp.sum(ac * sel, axis=1, keepdims=True))
        dtc.append(jnp.sum(dt * sel, axis=1, keepdims=True))
        row.append(jnp.sum(act * jnp.where(sub_h == h, 1.0, 0.0), axis=0, keepdims=True))

    def per_head(vals):
        out = jnp.broadcast_to(vals[0], (q, GROUP_W))
        for r in range(1, HEADS_PER_GROUP):
            out = jnp.where(lane_c == r, vals[r], out)
        return out

    acx = per_head(col)
    x = xs * per_head(dtc)
    s = _bdot(cm, bm, "nt")
    y = per_head([_bdot(s * jnp.exp(jnp.where(tril, col[r] - row[r], -jnp.inf)), x, "nn")
                  for r in range(HEADS_PER_GROUP)])
    y = y + jnp.exp(acx) * _bdot(cm, st, "nn")
    last = jnp.sum(acx * last_row, axis=0, keepdims=True)
    new_st = st * jnp.exp(last) + _bdot(bm, x * jnp.exp(last - acx), "tn")
    y = y + dsk * xs
    y = y * (z * _sigmoid(z))
    yn = y * lax.rsqrt(jnp.mean(y * y, axis=1, keepdims=True) + EPS) * nw
    return yn, new_st


def _ssd_specs(nc, z_col, rev):
    def ci(c):
        return nc - 1 - c if rev else c

    xc = pl.BlockSpec((ROW_TILE, 2 * GROUP_W), lambda g, c: (ci(c), g))
    z = pl.BlockSpec((ROW_TILE, GROUP_W), lambda g, c: (ci(c), z_col + g))
    hd = pl.BlockSpec((ROW_TILE, LANE), lambda g, c: (ci(c), 0))
    par = pl.BlockSpec((8, GROUP_W), lambda g, c: (0, g))
    yb = pl.BlockSpec((ROW_TILE, GROUP_W), lambda g, c: (ci(c), g))
    stb = pl.BlockSpec((None, N_STATE, GROUP_W), lambda g, c: (ci(c), 0, g))
    return xc, z, hd, par, yb, stb


def _ssd_fwd(xc, proj, dt, ac, par8, ngroups, z_col, name):
    t = xc.shape[0]
    nc = t // ROW_TILE

    def body(xc_ref, z_ref, dt_ref, ac_ref, p_ref, y_ref, so_ref, st_ref):
        @pl.when(pl.program_id(1) == 0)
        def _():
            st_ref[...] = jnp.zeros_like(st_ref)

        st = st_ref[...]
        so_ref[...] = st
        blk = xc_ref[...].astype(F32)
        yn, new_st = _ssd_chunk(blk[:, :GROUP_W], blk[:, GROUP_W:GROUP_W + N_STATE], blk[:, GROUP_W + N_STATE:],
                                z_ref[...].astype(F32), dt_ref[...], ac_ref[...], st,
                                _rows(p_ref, 0), _rows(p_ref, 1), pl.program_id(0))
        y_ref[...] = yn.astype(BF16)
        st_ref[...] = new_st

    xcs, zs, hd, par, yb, stb = _ssd_specs(nc, z_col, False)
    return pl.pallas_call(
        body, name=name,
        out_shape=[jax.ShapeDtypeStruct((t, ngroups * GROUP_W), BF16),
                   jax.ShapeDtypeStruct((nc, N_STATE, ngroups * GROUP_W), F32)],
        grid=(ngroups, nc), in_specs=[xcs, zs, hd, hd, par], out_specs=[yb, stb],
        scratch_shapes=[pltpu.VMEM((N_STATE, GROUP_W), F32)],
        compiler_params=_params(("parallel", "arbitrary")))(xc, proj, dt, ac, par8)


def _ssd_bwd(dyn, xc, proj, dt, ac, par8, states, dproj, ngroups, z_col, name):
    t = xc.shape[0]
    nc = t // ROW_TILE

    def body(dy_ref, xc_ref, z_ref, dt_ref, ac_ref, p_ref, s_ref, _, dz_ref, dxc_ref, ddt_ref, dac_ref, dp_ref,
             ds_ref):
        c = pl.program_id(1)

        @pl.when(c == 0)
        def _():
            ds_ref[...] = jnp.zeros_like(ds_ref)

        g = pl.program_id(0)
        blk = xc_ref[...].astype(F32)
        args = (blk[:, :GROUP_W], blk[:, GROUP_W:GROUP_W + N_STATE], blk[:, GROUP_W + N_STATE:],
                z_ref[...].astype(F32), dt_ref[...], ac_ref[...], s_ref[...], _rows(p_ref, 0), _rows(p_ref, 1))
        _, vjp = jax.vjp(lambda *a: _ssd_chunk(*a, g), *args)
        dxs, dbm, dcm, dz, ddt, dac, dst, ddsk, dnw = vjp((dy_ref[...], ds_ref[...]))
        ds_ref[...] = dst
        dz_ref[...] = dz.astype(BF16)
        dxc_ref[...] = jnp.concatenate([dxs, dbm, dcm], axis=1)
        ddt_ref[...] = ddt
        dac_ref[...] = dac
        upd = jnp.concatenate([ddsk, dnw, jnp.zeros((6, GROUP_W), F32)], axis=0)

        @pl.when(c == 0)
        def _():
            dp_ref[...] = upd

        @pl.when(c > 0)
        def _():
            dp_ref[...] += upd

    xcs, zs, hd, par, yb, stb = _ssd_specs(nc, z_col, True)
    dxcs = pl.BlockSpec((ROW_TILE, 2 * GROUP_W), lambda g, c: (nc - 1 - c, g))
    hg = pl.BlockSpec((None, ROW_TILE, LANE), lambda g, c: (g, nc - 1 - c, 0))
    return pl.pallas_call(
        body, name=name,
        out_shape=[jax.ShapeDtypeStruct(dproj.shape, BF16),
                   jax.ShapeDtypeStruct((t, ngroups * 2 * GROUP_W), F32),
                   jax.ShapeDtypeStruct((ngroups, t, LANE), F32), jax.ShapeDtypeStruct((ngroups, t, LANE), F32),
                   jax.ShapeDtypeStruct((8, ngroups * GROUP_W), F32)],
        grid=(ngroups, nc),
        in_specs=[yb, xcs, zs, hd, hd, par, stb, pl.BlockSpec(memory_space=pl.ANY)],
        out_specs=[zs, dxcs, hg, hg, par], input_output_aliases={7: 0},
        scratch_shapes=[pltpu.VMEM((N_STATE, GROUP_W), F32)],
        compiler_params=_params(("arbitrary", "arbitrary")))(dyn, xc, proj, dt, ac, par8, states, dproj)


def _small_matmul(a, b, mode, name):
    if mode == "nn":
        shape = (a.shape[0], b.shape[1])
    else:
        shape = (a.shape[1], b.shape[1])

    def body(a_ref, b_ref, o_ref):
        o_ref[...] = _dot(a_ref[...], b_ref[...], mode)

    return pl.pallas_call(body, name=name, out_shape=jax.ShapeDtypeStruct(shape, F32),
                          compiler_params=_params())(a, b)


def _adamw(parts, w, m, v, name):
    npart, rows, cols = parts.shape
    tr = rows if rows <= 128 else _tile8(rows, 128)

    def body(p_ref, w_ref, m_ref, v_ref, g_ref, d_ref, nm_ref, nv_ref):
        g = p_ref[0].astype(F32)
        for j in range(1, npart):
            g = g + p_ref[j].astype(F32)
        mm = ADAM_B1 * m_ref[...] + (1.0 - ADAM_B1) * g
        vv = ADAM_B2 * v_ref[...] + (1.0 - ADAM_B2) * (g * g)
        m_hat = mm / (1.0 - ADAM_B1 ** ADAM_STEP)
        v_hat = vv / (1.0 - ADAM_B2 ** ADAM_STEP)
        g_ref[...] = g
        d_ref[...] = -ADAM_LR * (m_hat / (jnp.sqrt(v_hat) + ADAM_EPS) + ADAM_WD * w_ref[...])
        nm_ref[...] = mm
        nv_ref[...] = vv

    blk = pl.BlockSpec((tr, cols), lambda i: (i, 0))
    out = jax.ShapeDtypeStruct((rows, cols), F32)
    return pl.pallas_call(
        body, name=name, out_shape=[out] * 4, grid=(rows // tr,),
        in_specs=[pl.BlockSpec((npart, tr, cols), lambda i: (0, i, 0)), blk, blk, blk], out_specs=[blk] * 4,
        compiler_params=_params(("parallel",)))(parts, w, m, v)


def _tile8(n, target):
    best = 8
    for d in range(8, target + 1, 8):
        if n % d == 0:
            best = d
    return best


def _pad_rows8(a):
    return jnp.concatenate([a, jnp.zeros((8 - a.shape[0], a.shape[1]), a.dtype)], axis=0)


def _pad_lanes(a, n):
    return jnp.concatenate([a, jnp.zeros(a.shape[:-1] + (n - a.shape[-1],), a.dtype)], axis=-1)


def _pack(vecs, mult):
    flat = jnp.concatenate([v.reshape(-1) for v in vecs])
    pad = (-flat.shape[0]) % mult
    flat = jnp.concatenate([flat, jnp.zeros((pad,), flat.dtype)])
    return flat.reshape(-1, LANE)


def _unpack(flat, shapes):
    out, o = [], 0
    for s in shapes:
        n = int(np.prod(s))
        out.append(flat[o:o + n].reshape(s))
        o += n
    return out


def kernel(x, c, w_ada, b_ada, ln1, ln2, w_in, conv_w, ssm_conv_w, ssm_conv_b, dt_bias, a_log, d_skip, ssm_norm_w, w_conv_out, w_ssm_out, w_o, w_up, w_down, final_norm, loss_target, m_w_ada, m_b_ada, m_ln1, m_ln2, m_w_in, m_conv_w, m_ssm_conv_w, m_ssm_conv_b, m_dt_bias, m_a_log, m_d_skip, m_ssm_norm_w, m_w_conv_out, m_w_ssm_out, m_w_o, m_w_up, m_w_down, m_final_norm, v_w_ada, v_b_ada, v_ln1, v_ln2, v_w_in, v_conv_w, v_ssm_conv_w, v_ssm_conv_b, v_dt_bias, v_a_log, v_d_skip, v_ssm_norm_w, v_w_conv_out, v_w_ssm_out, v_w_o, v_w_up, v_w_down, v_final_norm):
    names = ["w_ada", "b_ada", "ln1", "ln2", "w_in", "conv_w", "ssm_conv_w", "ssm_conv_b", "dt_bias", "a_log",
             "d_skip", "ssm_norm_w", "w_conv_out", "w_ssm_out", "w_o", "w_up", "w_down", "final_norm"]
    w_of = dict(zip(names, [w_ada, b_ada, ln1, ln2, w_in, conv_w, ssm_conv_w, ssm_conv_b, dt_bias, a_log, d_skip,
                            ssm_norm_w, w_conv_out, w_ssm_out, w_o, w_up, w_down, final_norm]))
    m_of = dict(zip(names, [m_w_ada, m_b_ada, m_ln1, m_ln2, m_w_in, m_conv_w, m_ssm_conv_w, m_ssm_conv_b, m_dt_bias,
                            m_a_log, m_d_skip, m_ssm_norm_w, m_w_conv_out, m_w_ssm_out, m_w_o, m_w_up, m_w_down,
                            m_final_norm]))
    v_of = dict(zip(names, [v_w_ada, v_b_ada, v_ln1, v_ln2, v_w_in, v_conv_w, v_ssm_conv_w, v_ssm_conv_b, v_dt_bias,
                            v_a_log, v_d_skip, v_ssm_norm_w, v_w_conv_out, v_w_ssm_out, v_w_o, v_w_up, v_w_down,
                            v_final_norm]))

    _, t, d = x.shape
    nl = w_ada.shape[0]
    ada_w = w_ada.shape[2]
    ds = ssm_norm_w.shape[1]
    nh = dt_bias.shape[1]
    ng = nh // HEADS_PER_GROUP
    gn = ng * N_STATE
    xbc_w = ds + 2 * gn
    z_off, xbc_off = 5 * d, 5 * d + ds
    dt_off = xbc_off + xbc_w
    proj_w = dt_off + nh
    pw = dt_off + 2 * LANE
    me = 4 * lax.axis_index("x") + 2 * lax.axis_index("y") + lax.axis_index("c")

    perm = []
    for g in range(ng):
        perm += [2 * g, 2 * g + 1, ds // LANE + g, ds // LANE + ng + g]
    perm = np.array(perm)
    inv_perm = np.argsort(perm)

    def reorder(a, order):
        blocks = a.reshape(a.shape[:-1] + (xbc_w // LANE, LANE))
        return jnp.concatenate([blocks[..., j:j + 1, :] for j in order], axis=-2).reshape(a.shape)

    def to_group_major(a):
        return reorder(a, perm)

    def from_group_major(a):
        return reorder(a, inv_perm)

    x2, tgt = x[0], loss_target[0]

    c_act = c * jax.nn.sigmoid(c)
    sizes1 = [(1, d), conv_w.shape, ssm_conv_w.shape]
    (g1,) = _all_gather([_pack([c_act, conv_w, ssm_conv_w], LANE)], "ag_small_in")
    parts1 = [_unpack(g1[j].reshape(-1), sizes1) for j in range(N_DEV)]
    c_act_all = jnp.concatenate([p[0] for p in parts1], axis=0)
    conv_w_full = jnp.concatenate([p[1] for p in parts1], axis=-1)
    sconv_w_full = to_group_major(jnp.concatenate([p[2] for p in parts1], axis=-1))
    sconv_b_gm = to_group_major(ssm_conv_b)

    mod_part = jnp.stack([_small_matmul(c_act_all, w_ada[l], "nn", f"ada_fwd{l}") for l in range(nl)])
    (gmod,) = _all_gather([mod_part], "ag_mod")
    mod = lax.dynamic_index_in_dim(gmod, me, axis=2, keepdims=False)
    mod = jnp.moveaxis(mod, 0, 1).reshape(nl, N_DEV * ada_w) + b_ada
    mod = mod.reshape(nl, 6, d)
    modp = [jnp.concatenate([mod[l], ln1[l][None], ln2[l][None]], axis=0) for l in range(nl)]

    big = ["w_in", "w_up", "w_conv_out", "w_ssm_out", "w_o", "w_down"]
    rest = big[1:]

    def blocks_of(l, keys):
        return [w_of[k][l].astype(BF16) for k in keys]

    def w_in_halves(l):
        w = w_of["w_in"][l].astype(BF16)
        return [w[:d // 2]], [w[d // 2:]]

    def full_cols(g):
        return jnp.moveaxis(g, 0, 1).reshape(g.shape[1], N_DEV * g.shape[2])

    def full_rows(g):
        return g.reshape(N_DEV * g.shape[1], g.shape[2])

    def pad_w_in(pieces):
        w = jnp.concatenate([full_cols(g) for g in pieces], axis=0)
        return jnp.concatenate([w[:, :xbc_off], to_group_major(w[:, xbc_off:dt_off]), w[:, dt_off:],
                                jnp.zeros((d, pw - proj_w), BF16)], axis=-1)

    wi = [None] * nl
    wi[0] = pad_w_in(_all_gather(blocks_of(0, ["w_in"]), "ag_w_in0"))

    cw8 = [_pad_rows8(conv_w_full[l]) for l in range(nl)]
    sw8 = [_pad_rows8(sconv_w_full[l]) for l in range(nl)]
    sb8 = [_pad_rows8(sconv_b_gm[l][None]) for l in range(nl)]
    dtp8 = [_pad_rows8(_pad_lanes(jnp.stack([dt_bias[l], a_log[l]]), LANE)) for l in range(nl)]
    sp8 = [_pad_rows8(jnp.stack([jnp.repeat(d_skip[l], HEAD_DIM), ssm_norm_w[l]])) for l in range(nl)]
    fn8 = _pad_rows8(final_norm[None])
    xbc_col, ncol, dt_col, z_col = xbc_off // d, xbc_w // d, dt_off // (2 * LANE), z_off // GROUP_W

    saved = []
    x_cur, br_prev = x2, None
    for l in range(nl):
        if l == 0:
            x_in, u1 = x_cur, _norm_fwd(x_cur, modp[0], 0, "norm_first")
        else:
            x_in, u1 = _resid_norm_fwd(x_cur, br_prev, modp[l - 1], 5, modp[l], 0, f"resid_norm_a{l}")
        proj, got = _matmul(u1, wi[l], "nn", BF16, f"mm_in{l}", comm=_GatherComm(blocks_of(l, rest)))
        got = dict(zip(rest, got))
        wup, wdown = full_cols(got["w_up"]), full_rows(got["w_down"])
        wco, wso, wo = full_rows(got["w_conv_out"]), full_rows(got["w_ssm_out"]), full_rows(got["w_o"])
        y_conv = _gconv_fwd(proj, cw8[l], d, f"gconv_fwd{l}")
        xc = _sconv_fwd(proj, sw8[l], sb8[l], d, xbc_col, ncol, f"sconv_fwd{l}")
        dt, ac = _dt_fwd(proj, dtp8[l], nh, dt_col, f"dt_fwd{l}")
        yn, states = _ssd_fwd(xc, proj, dt, ac, sp8[l], ng, z_col, f"ssd_fwd{l}")
        p_conv = _matmul(y_conv, wco, "nn", F32, f"mm_conv_out{l}")
        p_ssm = _matmul(yn, wso, "nn", F32, f"mm_ssm_out{l}")
        merged = _merge_fwd(proj, p_conv, p_ssm, f"merge_fwd{l}")
        mix = _matmul(merged, wo, "nn", F32, f"mm_o{l}")
        x_mid, u2 = _resid_norm_fwd(x_in, mix, modp[l], 2, modp[l], 1, f"resid_norm_b{l}")
        if l + 1 < nl:
            top, bot = w_in_halves(l + 1)
            (h, hid), g_top = _matmul(u2, wup, "nn", BF16, f"mm_up{l}", epi="relu2", comm=_GatherComm(top))
            mlp, g_bot = _matmul(hid, wdown, "nn", F32, f"mm_down{l}", comm=_GatherComm(bot))
            wi[l + 1] = pad_w_in(g_top + g_bot)
        else:
            h, hid = _matmul(u2, wup, "nn", BF16, f"mm_up{l}", epi="relu2")
            mlp = _matmul(hid, wdown, "nn", F32, f"mm_down{l}")
        saved.append(dict(x_in=x_in, u1=u1, proj=proj, y_conv=y_conv, xc=xc, dt=dt, ac=ac, yn=yn, states=states,
                          p_conv=p_conv, p_ssm=p_ssm, merged=merged, mix=mix, x_mid=x_mid, u2=u2, h=h, hid=hid,
                          mlp=mlp, wup=wup, wdown=wdown, wco=wco, wso=wso, wo=wo))
        x_cur, br_prev = x_mid, mlp

    dx, dbr, acc = _final_fwd_bwd(x_cur, br_prev, modp[nl - 1], fn8, tgt, "final")
    loss = lax.psum(acc[0, 0], ("x", "y", "c"))
    g_final_norm = acc[1]
    dgate2 = acc[2]

    gw = {k: [None] * nl for k in big}
    small = {k: [None] * nl for k in ["mod", "ln1", "ln2", "conv_w", "ssm_conv_w", "ssm_conv_b", "dt_bias",
                                      "a_log", "d_skip", "ssm_norm_w"]}

    def grad_cols(a):
        return jnp.moveaxis(a.reshape(a.shape[0], N_DEV, a.shape[1] // N_DEV), 1, 0)

    def grad_rows(a):
        return a.reshape(N_DEV, a.shape[0] // N_DEV, a.shape[1])

    def recv_buf(name):
        return jnp.zeros((N_DEV, nl) + w_of[name].shape[1:], BF16)

    bufs_a, bufs_b = [recv_buf("w_in")], [recv_buf(k) for k in big[1:]]
    pending = None
    for l in reversed(range(nl)):
        s = saved[l]
        dh = _matmul(dbr, s["wdown"], "nt", BF16, f"mm_down_dx{l}", epi="relu2_bwd", extra=s["h"])
        gw["w_down"][l] = _matmul(s["hid"], dbr, "tn", BF16, f"mm_down_dw{l}")
        du2 = _matmul(dh, s["wup"], "nt", F32, f"mm_up_dx{l}")
        gw["w_up"][l] = _matmul(s["u2"], dh, "tn", BF16, f"mm_up_dw{l}")
        dx_mid, dmix, acc2 = _resid_norm_bwd(dx, du2, s["x_mid"], modp[l], 1, f"resid_norm_b_bwd{l}",
                                             br=s["mix"], mp_gate=modp[l], gate_row=2)
        dmerged = _matmul(dmix, s["wo"], "nt", F32, f"mm_o_dx{l}")
        gw["w_o"][l] = _matmul(s["merged"], dmix, "tn", BF16, f"mm_o_dw{l}")
        dpc, dps, dproj = _merge_bwd(dmerged, s["proj"], s["p_conv"], s["p_ssm"], f"merge_bwd{l}")
        dyc = _matmul(dpc, s["wco"], "nt", F32, f"mm_conv_out_dx{l}")
        gw["w_conv_out"][l] = _matmul(s["y_conv"], dpc, "tn", BF16, f"mm_conv_out_dw{l}")
        dyn = _matmul(dps, s["wso"], "nt", F32, f"mm_ssm_out_dx{l}")
        gw["w_ssm_out"][l] = _matmul(s["yn"], dps, "tn", BF16, f"mm_ssm_out_dw{l}")
        dproj, dxc, ddt_g, dac_g, dsp = _ssd_bwd(dyn, s["xc"], s["proj"], s["dt"], s["ac"], sp8[l], s["states"],
                                                 dproj, ng, z_col, f"ssd_bwd{l}")
        dproj, dcw = _gconv_bwd(dyc, s["proj"], cw8[l], dproj, d, f"gconv_bwd{l}")
        dproj, dsw = _sconv_bwd(dxc, s["proj"], sw8[l], sb8[l], dproj, d, xbc_col, ncol, f"sconv_bwd{l}")
        dproj, ddtp = _dt_bwd(ddt_g, dac_g, s["dt"], s["proj"], dtp8[l], dproj, nh, dt_col, f"dt_bwd{l}")
        rest_grads = [grad_cols(gw["w_up"][l])] + [grad_rows(gw[k][l]) for k in big[2:]]
        du1, bufs_b = _matmul(dproj, wi[l], "nt", F32, f"mm_in_dx{l}", comm=_ExchangeComm(rest_grads, bufs_b, l))

        def mm_in_dw(u1, name, comm):
            if comm is None:
                return _matmul(u1, dproj, "tn", BF16, name), None
            return _matmul(u1, dproj, "tn", BF16, name, comm=comm)

        def w_in_blocks(g):
            return grad_cols(jnp.concatenate([g[:, :xbc_off], from_group_major(g[:, xbc_off:dt_off]),
                                              g[:, dt_off:proj_w]], axis=-1))

        prev = None if pending is None else _ExchangeComm(pending, bufs_a, l + 1)
        if l > 0:
            g_in, got_bufs = mm_in_dw(s["u1"], f"mm_in_dw{l}", prev)
            bufs_a = bufs_a if got_bufs is None else got_bufs
            pending = [w_in_blocks(g_in)]
        else:
            g_top, got_bufs = mm_in_dw(s["u1"][:, :d // 2], "mm_in_dw0_top", prev)
            bufs_a = bufs_a if got_bufs is None else got_bufs
            g_bot, bufs_a = mm_in_dw(s["u1"][:, d // 2:], "mm_in_dw0_bot",
                                     _ExchangeComm([w_in_blocks(g_top)], bufs_a, 0, row0=0))
            pending = [w_in_blocks(g_bot)]
        if l > 0:
            dx, dbr, acc1 = _resid_norm_bwd(dx_mid, du1, s["x_in"], modp[l], 0, f"resid_norm_a_bwd{l}",
                                            br=saved[l - 1]["mlp"], mp_gate=modp[l - 1], gate_row=5)
        else:
            dx, acc1 = _resid_norm_bwd(dx_mid, du1, s["x_in"], modp[0], 0, "norm_first_bwd")
        sc1, sc2 = mod[l, 1], mod[l, 4]
        small["mod"][l] = jnp.stack([acc1[0], acc1[1] * ln1[l], acc2[2], acc2[0], acc2[1] * ln2[l], dgate2])
        small["ln1"][l] = acc1[1] * (1.0 + sc1)
        small["ln2"][l] = acc2[1] * (1.0 + sc2)
        small["conv_w"][l] = dcw[:3]
        small["ssm_conv_w"][l] = from_group_major(dsw[:4])
        small["ssm_conv_b"][l] = from_group_major(dsw[4])
        small["dt_bias"][l] = ddtp[0, :nh]
        small["a_log"][l] = ddtp[1, :nh]
        small["d_skip"][l] = dsp[0].reshape(nh, HEAD_DIM).sum(axis=-1)
        small["ssm_norm_w"][l] = dsp[1]
        if l > 0:
            dgate2 = acc1[2]
    grad_x = dx[None]

    sm = {k: jnp.stack(v) for k, v in small.items()}
    rep_names = ["b_ada", "ln1", "ln2", "ssm_conv_b", "dt_bias", "a_log", "d_skip", "ssm_norm_w", "final_norm"]
    rep_grads = [sm["mod"].reshape(nl, 6 * d), sm["ln1"], sm["ln2"], sm["ssm_conv_b"], sm["dt_bias"], sm["a_log"],
                 sm["d_skip"], sm["ssm_norm_w"], g_final_norm]
    rep_pack = _pack(rep_grads, 8 * LANE)
    conv_pack = _pack([sm["conv_w"], sm["ssm_conv_w"]], 8 * LANE)
    g_rep, g_conv = _all_gather([rep_pack, conv_pack], "ag_small_grads")

    outs = {}

    def run_adamw(name, parts, tag):
        w, m, v = w_of[name], m_of[name], v_of[name]
        shp = w.shape
        r2 = (int(np.prod(shp[:-1])), shp[-1])
        res = _adamw(parts.reshape((parts.shape[0],) + r2), w.reshape(r2), m.reshape(r2), v.reshape(r2), tag)
        outs[name] = [a.reshape(shp) for a in res]

    rep_shapes = [w_of[k].shape for k in rep_names]
    res = _adamw(g_rep, _pack([w_of[k] for k in rep_names], 8 * LANE), _pack([m_of[k] for k in rep_names], 8 * LANE),
                 _pack([v_of[k] for k in rep_names], 8 * LANE), "adamw_replicated")
    for k, vals in zip(rep_names, zip(*[_unpack(a.reshape(-1), rep_shapes) for a in res])):
        outs[k] = list(vals)

    conv_parts = [_unpack(g_conv[j].reshape(-1), [sm["conv_w"].shape, sm["ssm_conv_w"].shape]) for j in range(N_DEV)]
    for idx, name in enumerate(["conv_w", "ssm_conv_w"]):
        wsh = w_of[name].shape[-1]
        full = jnp.stack([p[idx] for p in conv_parts])
        run_adamw(name, lax.dynamic_slice_in_dim(full, me * wsh, wsh, axis=3), "adamw_" + name)

    dmod_all = g_rep.reshape(N_DEV, -1)[:, :nl * 6 * d].reshape(N_DEV, nl, 6 * d)
    dmod_mine = lax.dynamic_slice_in_dim(dmod_all, me * ada_w, ada_w, axis=2)
    g_ada = jnp.stack([_small_matmul(c_act_all, dmod_mine[:, l], "tn", f"ada_bwd{l}") for l in range(nl)])
    run_adamw("w_ada", g_ada[None], "adamw_w_ada")

    bufs_a = _exchange(pending, bufs_a, 0, "a2a_w_in0_bot", row0=d // 2)
    for name, parts in zip(big, list(bufs_a) + list(bufs_b)):
        run_adamw(name, parts, "adamw_" + name)

    result = [loss, grad_x]
    for i in range(4):
        result += [outs[k][i] for k in names]
    return tuple(result)
```

```python
import functools

import numpy as np
import jax
import jax.numpy as jnp
from jax import lax
from jax.experimental import pallas as pl
from jax.experimental.pallas import tpu as pltpu

F32 = jnp.float32
BF16 = jnp.bfloat16
EPS = 1e-6
N_STATE = 128
HEAD_DIM = 64
HEADS_PER_GROUP = 4
GROUP_W = HEAD_DIM * HEADS_PER_GROUP
SSD_GROUPS = 2
SSD_Q = 256
N_DEV = 8
ROW_TILE = 256
LANE = 128
VMEM_LIMIT = 56 * 1024 * 1024

ADAM_LR, ADAM_B1, ADAM_B2, ADAM_EPS, ADAM_WD, ADAM_STEP = 0.001, 0.9, 0.999, 1e-08, 0.01, 10

MESH = pl.DeviceIdType.MESH


def _params(sem=None):
    return pltpu.CompilerParams(dimension_semantics=sem, vmem_limit_bytes=VMEM_LIMIT)


def _tile(n, target):
    if n <= target:
        return n
    best = None
    for d in range(LANE, target + 1, LANE):
        if n % d == 0:
            best = d
    assert best is not None, (n, target)
    return best


def _slot(p):
    return 4 * p[0] + 2 * p[1] + p[2]


def _all_gather(xs, name):
    comm = _GatherComm(xs)

    def body(*refs):
        begin, end = comm.ops(refs[:comm.n_in], refs[comm.n_in:comm.n_in + comm.n_out], refs[-3:])
        begin()
        end()

    any_spec = pl.BlockSpec(memory_space=pl.ANY)
    return pl.pallas_call(
        body, name=name, out_shape=comm.out_shape,
        in_specs=[any_spec] * comm.n_in, out_specs=[any_spec] * comm.n_out, scratch_shapes=comm.sems,
    )(*comm.operands)


class _GatherComm:
    def __init__(self, xs):
        n = len(xs)
        self.operands = list(xs)
        self.n_in = self.n_out = n
        self.aliases = {}
        self.out_shape = [jax.ShapeDtypeStruct((N_DEV,) + x.shape, x.dtype) for x in xs]
        self.sems = [pltpu.SemaphoreType.DMA((7 * n,)), pltpu.SemaphoreType.DMA((7 * n,)),
                     pltpu.SemaphoreType.DMA((n,))]

    def ops(self, x_refs, o_refs, sem_refs):
        n = self.n_in
        send, recv, loc = sem_refs
        x, y, c = lax.axis_index("x"), lax.axis_index("y"), lax.axis_index("c")
        me, sib = (x, y, c), (x, y, 1 - c)
        chips = [(1 - x, y), (x, 1 - y), (1 - x, 1 - y)]

        def cp(a, k, block, to, src=None):
            dst = o_refs[a].at[_slot(block)]
            return pltpu.make_async_remote_copy(
                src_ref=dst if src is None else src, dst_ref=dst,
                send_sem=send.at[7 * a + k], recv_sem=recv.at[7 * a + k],
                device_id=to, device_id_type=MESH)

        mine = [pltpu.make_async_copy(x_refs[a], o_refs[a].at[_slot(me)], loc.at[a]) for a in range(n)]
        first = []
        for a in range(n):
            first.append(cp(a, 0, me, sib, src=x_refs[a]))
            first += [cp(a, 1 + j, me, (*chip, c), src=x_refs[a]) for j, chip in enumerate(chips)]

        def begin():
            for m in mine:
                m.start()
            for f in first:
                f.start()

        def end():
            passed = []
            for j, chip in enumerate(chips):
                for a in range(n):
                    cp(a, 1 + j, (*chip, c), me).wait_recv()
                    p = cp(a, 4 + j, (*chip, c), sib)
                    p.start()
                    passed.append(p)
            for a in range(n):
                cp(a, 0, sib, me).wait_recv()
                for j, chip in enumerate(chips):
                    cp(a, 4 + j, (*chip, 1 - c), me).wait_recv()
            for f in first + passed:
                f.wait_send()
            for m in mine:
                m.wait()

        return begin, end


class _ExchangeComm:
    def __init__(self, xs, bufs, layer, row0=0):
        n = len(xs)
        self.layer = layer
        self.row0 = row0
        self.nrows = [x.shape[1] for x in xs]
        self.operands = list(xs) + list(bufs)
        self.n_in, self.n_out = 2 * n, n
        self.aliases = {n + a: a for a in range(n)}
        self.out_shape = [jax.ShapeDtypeStruct(b.shape, b.dtype) for b in bufs]
        self.sems = [pltpu.SemaphoreType.DMA((7 * n,)), pltpu.SemaphoreType.DMA((7 * n,)),
                     pltpu.SemaphoreType.DMA((n,))]

    def ops(self, in_refs, o_refs, sem_refs):
        n = self.n_out
        x_refs = in_refs[:n]
        send, recv, loc = sem_refs
        layer = self.layer
        x, y, c = lax.axis_index("x"), lax.axis_index("y"), lax.axis_index("c")
        me = (x, y, c)
        peers = []
        for k in range(1, 8):
            kx, ky, kc = (k >> 2) & 1, (k >> 1) & 1, k & 1
            peers.append((x + kx - 2 * x * kx, y + ky - 2 * y * ky, c + kc - 2 * c * kc))

        def land(a, slot):
            return o_refs[a].at[slot, layer, pl.ds(self.row0, self.nrows[a])]

        def cp(a, k, src_slot, dst_slot, to):
            return pltpu.make_async_remote_copy(
                src_ref=x_refs[a].at[src_slot], dst_ref=land(a, dst_slot),
                send_sem=send.at[7 * a + k], recv_sem=recv.at[7 * a + k],
                device_id=to, device_id_type=MESH)

        mine = [pltpu.make_async_copy(x_refs[a].at[_slot(me)], land(a, _slot(me)), loc.at[a])
                for a in range(n)]
        sends = [cp(a, k, _slot(p), _slot(me), p) for a in range(n) for k, p in enumerate(peers)]

        def begin():
            for m in mine:
                m.start()
            for s in sends:
                s.start()

        def end():
            for a in range(n):
                for k, p in enumerate(peers):
                    cp(a, k, _slot(me), _slot(p), me).wait_recv()
            for s in sends:
                s.wait_send()
            for m in mine:
                m.wait()

        return begin, end


def _exchange(xs, bufs, layer, name, row0=0):
    comm = _ExchangeComm(xs, bufs, layer, row0)

    def body(*refs):
        begin, end = comm.ops(refs[:comm.n_in], refs[comm.n_in:comm.n_in + comm.n_out], refs[-3:])
        begin()
        end()

    any_spec = pl.BlockSpec(memory_space=pl.ANY)
    return pl.pallas_call(
        body, name=name, out_shape=comm.out_shape,
        in_specs=[any_spec] * comm.n_in, out_specs=[any_spec] * comm.n_out, scratch_shapes=comm.sems,
        input_output_aliases=dict(comm.aliases),
    )(*comm.operands)


_DIMS = {"nn": (((1,), (0,)), ((), ())), "nt": (((1,), (1,)), ((), ())), "tn": (((0,), (0,)), ((), ()))}


def _matmul(a, b, mode, out_dtype, name, comm=None, epi=None, extra=None):
    if mode == "nn":
        (m, k), (_, n) = a.shape, b.shape
    elif mode == "nt":
        (m, k), (n, _) = a.shape, b.shape
    else:
        (k, m), (_, n) = a.shape, b.shape
    tm, tn, tk = _tile(m, 1024), _tile(n, 1024), _tile(k, 2048)
    nk = k // tk
    grid = (m // tm, n // tn, nk)
    dn = _DIMS[mode]
    n_cin = comm.n_in if comm else 0
    n_cout = comm.n_out if comm else 0
    n_acc = 0 if nk == 1 else 1
    n_ext = 1 if epi == "relu2_bwd" else 0
    n_main = 2 if epi == "relu2" else 1
    first_in, first_out = 2 + n_ext, 2 + n_ext + n_cin

    def finish(acc, refs):
        outs = refs[first_out:first_out + n_main]
        if epi == "relu2":
            outs[0][...] = acc.astype(outs[0].dtype)
            r = jnp.maximum(acc, 0.0)
            outs[1][...] = (r * r).astype(outs[1].dtype)
        elif epi == "relu2_bwd":
            outs[0][...] = (acc * (2.0 * jnp.maximum(refs[2][...].astype(F32), 0.0))).astype(outs[0].dtype)
        else:
            outs[0][...] = acc.astype(outs[0].dtype)

    def body(*refs):
        a_ref, b_ref = refs[:2]
        scratch = refs[first_out + n_main + n_cout:]
        pid = [pl.program_id(ax) for ax in range(3)]
        if comm:
            begin, end = comm.ops(refs[first_in:first_in + n_cin],
                                  refs[first_out + n_main:first_out + n_main + n_cout], scratch[n_acc:])
            pl.when((pid[0] == 0) & (pid[1] == 0) & (pid[2] == 0))(begin)
        part = lax.dot_general(a_ref[...], b_ref[...], dn, preferred_element_type=F32)
        if nk == 1:
            finish(part, refs)
        else:
            acc_ref = scratch[0]

            @pl.when(pid[2] == 0)
            def _():
                acc_ref[...] = part

            @pl.when(pid[2] > 0)
            def _():
                acc_ref[...] += part

            @pl.when(pid[2] == nk - 1)
            def _():
                finish(acc_ref[...], refs)
        if comm:
            pl.when((pid[0] == grid[0] - 1) & (pid[1] == grid[1] - 1) & (pid[2] == grid[2] - 1))(end)

    a_spec = pl.BlockSpec((tk, tm), lambda i, j, kk: (kk, i)) if mode == "tn" else \
        pl.BlockSpec((tm, tk), lambda i, j, kk: (i, kk))
    b_spec = pl.BlockSpec((tn, tk), lambda i, j, kk: (j, kk)) if mode == "nt" else \
        pl.BlockSpec((tk, tn), lambda i, j, kk: (kk, j))
    o_spec = pl.BlockSpec((tm, tn), lambda i, j, kk: (i, j))
    any_spec = pl.BlockSpec(memory_space=pl.ANY)
    res = pl.pallas_call(
        body, name=name,
        out_shape=[jax.ShapeDtypeStruct((m, n), out_dtype)] * n_main + (comm.out_shape if comm else []),
        grid=grid,
        in_specs=[a_spec, b_spec] + [o_spec] * n_ext + [any_spec] * n_cin,
        out_specs=[o_spec] * n_main + [any_spec] * n_cout,
        scratch_shapes=([] if nk == 1 else [pltpu.VMEM((tm, tn), F32)]) + (comm.sems if comm else []),
        input_output_aliases={first_in + i: n_main + o for i, o in comm.aliases.items()} if comm else {},
        compiler_params=_params(("arbitrary",) * 3 if comm else ("parallel", "parallel", "arbitrary")),
    )(a, b, *([extra] if n_ext else []), *(comm.operands if comm else []))
    main = res[0] if n_main == 1 else tuple(res[:n_main])
    return (main, list(res[n_main:])) if comm else main


def _dot(a, b, mode):
    return lax.dot_general(a.astype(BF16), b.astype(BF16), _DIMS[mode], preferred_element_type=F32)


@functools.partial(jax.custom_vjp, nondiff_argnums=(2,))
def _bdot(a, b, mode):
    return _dot(a, b, mode)


def _bdot_fwd(a, b, mode):
    return _dot(a, b, mode), (a, b)


def _bdot_bwd(mode, res, g):
    a, b = res
    if mode == "nn":
        return _dot(g, b, "nt"), _dot(a, g, "tn")
    if mode == "nt":
        return _dot(g, b, "nn"), _dot(g, a, "tn")
    return _dot(b, g, "nt"), _dot(a, g, "nn")


_bdot.defvjp(_bdot_fwd, _bdot_bwd)


def _split3(x):
    hi = x.astype(BF16)
    r1 = x - hi.astype(F32)
    mid = r1.astype(BF16)
    return hi, mid, (r1 - mid.astype(F32)).astype(BF16)


def _xdot01(m01, x):
    return sum(lax.dot_general(m01, p, _DIMS["nn"], preferred_element_type=F32) for p in _split3(x))


@jax.custom_vjp
def _spread(v, m01):
    return sum(lax.dot_general(p, m01, _DIMS["nn"], preferred_element_type=F32) for p in _split3(v))


def _spread_fwd(v, m01):
    return _spread(v, m01), m01


def _spread_bwd(m01, g):
    dv = sum(lax.dot_general(p, m01, _DIMS["nt"], preferred_element_type=F32) for p in _split3(g)[:2])
    return dv, jnp.zeros_like(m01)


_spread.defvjp(_spread_fwd, _spread_bwd)


def _chunk_tri(n, lower):
    r = lax.broadcasted_iota(jnp.int32, (n, n), 0)
    c = lax.broadcasted_iota(jnp.int32, (n, n), 1)
    tri = (r >= c) if lower else (r <= c)
    return jnp.where(tri & (r // SSD_Q == c // SSD_Q), 1.0, 0.0).astype(BF16)


def _shift_down(x, prev8, k):
    if k == 0:
        return x
    n = x.shape[0]
    r = pltpu.roll(x, k, 0)
    rp = pltpu.roll(prev8, k, 0)
    rows = lax.broadcasted_iota(jnp.int32, (8, x.shape[1]), 0)
    head = jnp.where(rows < k, rp, r[:8])
    return head if n == 8 else jnp.concatenate([head, r[8:]], axis=0)


def _shift_up(x, next8, k):
    if k == 0:
        return x
    n = x.shape[0]
    r = pltpu.roll(x, n - k, 0)
    rn = pltpu.roll(next8, 8 - k, 0)
    rows = lax.broadcasted_iota(jnp.int32, (8, x.shape[1]), 0)
    tail = jnp.where(rows >= 8 - k, rn, r[n - 8:])
    return tail if n == 8 else jnp.concatenate([r[:n - 8], tail], axis=0)


def _sigmoid(x):
    return 0.5 * jnp.tanh(0.5 * x) + 0.5


def _rows(ref, i):
    return ref[i:i + 1, :]


def _norm_parts(x, mp_ref, which):
    ln, sh, sc = _rows(mp_ref, 6 + which), _rows(mp_ref, 3 * which), _rows(mp_ref, 3 * which + 1)
    r = lax.rsqrt(jnp.mean(x * x, axis=1, keepdims=True) + EPS)
    return r, ln, sh, sc


def _norm_fwd(x, modp, which, name):
    t, d = x.shape

    def body(x_ref, mp_ref, u_ref):
        xv = x_ref[...]
        r, ln, sh, sc = _norm_parts(xv, mp_ref, which)
        u_ref[...] = (((xv * r) * ln) * (1.0 + sc) + sh).astype(BF16)

    row = pl.BlockSpec((ROW_TILE, d), lambda i: (i, 0))
    return pl.pallas_call(
        body, name=name, out_shape=jax.ShapeDtypeStruct((t, d), BF16), grid=(t // ROW_TILE,),
        in_specs=[row, pl.BlockSpec((8, d), lambda i: (0, 0))], out_specs=row,
        compiler_params=_params(("parallel",)))(x, modp)


def _resid_norm_fwd(x, br, mp_gate, gate_row, mp_norm, which, name):
    t, d = x.shape

    def body(x_ref, br_ref, mg_ref, mn_ref, xn_ref, u_ref):
        xv = x_ref[...] + _rows(mg_ref, gate_row) * br_ref[...]
        xn_ref[...] = xv
        r, ln, sh, sc = _norm_parts(xv, mn_ref, which)
        u_ref[...] = (((xv * r) * ln) * (1.0 + sc) + sh).astype(BF16)

    row = pl.BlockSpec((ROW_TILE, d), lambda i: (i, 0))
    mp = pl.BlockSpec((8, d), lambda i: (0, 0))
    return pl.pallas_call(
        body, name=name,
        out_shape=[jax.ShapeDtypeStruct((t, d), F32), jax.ShapeDtypeStruct((t, d), BF16)],
        grid=(t // ROW_TILE,), in_specs=[row, row, mp, mp], out_specs=[row, row],
        compiler_params=_params(("parallel",)))(x, br, mp_gate, mp_norm)


def _final_fwd_bwd(x, br, mp_gate, fnorm8, target, name):
    t, d = x.shape

    def body(x_ref, br_ref, mg_ref, fn_ref, tg_ref, dx_ref, dbr_ref, acc_ref):
        gate = _rows(mg_ref, 5)
        brv = br_ref[...]
        xv = x_ref[...] + gate * brv
        fn = _rows(fn_ref, 0)
        r = lax.rsqrt(jnp.mean(xv * xv, axis=1, keepdims=True) + EPS)
        nrm = xv * r
        err = nrm * fn - tg_ref[...]
        loss = 0.5 * jnp.sum(jnp.mean(err * err, axis=1, keepdims=True), axis=0, keepdims=True)
        dy = err * (1.0 / d)
        dn = dy * fn
        dx = r * (dn - nrm * jnp.mean(dn * nrm, axis=1, keepdims=True))
        dx_ref[...] = dx
        dbr_ref[...] = (dx * gate).astype(BF16)
        upd = jnp.concatenate([
            jnp.broadcast_to(loss, (1, d)),
            jnp.sum(dy * nrm, axis=0, keepdims=True),
            jnp.sum(dx * brv, axis=0, keepdims=True),
            jnp.zeros((5, d), F32)], axis=0)

        @pl.when(pl.program_id(0) == 0)
        def _():
            acc_ref[...] = upd

        @pl.when(pl.program_id(0) > 0)
        def _():
            acc_ref[...] += upd

    row = pl.BlockSpec((ROW_TILE, d), lambda i: (i, 0))
    mp = pl.BlockSpec((8, d), lambda i: (0, 0))
    return pl.pallas_call(
        body, name=name,
        out_shape=[jax.ShapeDtypeStruct((t, d), F32), jax.ShapeDtypeStruct((t, d), BF16),
                   jax.ShapeDtypeStruct((8, d), F32)],
        grid=(t // ROW_TILE,), in_specs=[row, row, mp, mp, row], out_specs=[row, row, mp],
        compiler_params=_params(("arbitrary",)))(x, br, mp_gate, fnorm8, target)


def _resid_norm_bwd(dx, du, x, mp_norm, which, name, br=None, mp_gate=None, gate_row=None):
    t, d = x.shape
    has_gate = br is not None

    def body(*refs):
        if has_gate:
            dx_ref, du_ref, x_ref, mn_ref, br_ref, mg_ref, dxn_ref, dbr_ref, acc_ref = refs
        else:
            dx_ref, du_ref, x_ref, mn_ref, dxn_ref, acc_ref = refs
        xv, duv = x_ref[...], du_ref[...]
        r, ln, _, sc = _norm_parts(xv, mn_ref, which)
        nrm = xv * r
        dn = duv * (ln * (1.0 + sc))
        dxn = dx_ref[...] + r * (dn - nrm * jnp.mean(dn * nrm, axis=1, keepdims=True))
        dxn_ref[...] = dxn
        rows = [jnp.sum(duv, axis=0, keepdims=True), jnp.sum(duv * nrm, axis=0, keepdims=True)]
        if has_gate:
            dbr_ref[...] = (dxn * _rows(mg_ref, gate_row)).astype(BF16)
            rows.append(jnp.sum(dxn * br_ref[...], axis=0, keepdims=True))
        upd = jnp.concatenate(rows + [jnp.zeros((8 - len(rows), d), F32)], axis=0)

        @pl.when(pl.program_id(0) == 0)
        def _():
            acc_ref[...] = upd

        @pl.when(pl.program_id(0) > 0)
        def _():
            acc_ref[...] += upd

    row = pl.BlockSpec((ROW_TILE, d), lambda i: (i, 0))
    mp = pl.BlockSpec((8, d), lambda i: (0, 0))
    ins, in_specs = [dx, du, x, mp_norm], [row, row, row, mp]
    outs = [jax.ShapeDtypeStruct((t, d), F32)]
    out_specs = [row]
    if has_gate:
        ins += [br, mp_gate]
        in_specs += [row, mp]
        outs.append(jax.ShapeDtypeStruct((t, d), BF16))
        out_specs.append(row)
    outs.append(jax.ShapeDtypeStruct((8, d), F32))
    out_specs.append(mp)
    return pl.pallas_call(
        body, name=name, out_shape=outs, grid=(t // ROW_TILE,), in_specs=in_specs, out_specs=out_specs,
        compiler_params=_params(("arbitrary",)))(*ins)


def _merge_fwd(proj, p_conv, p_ssm, name):
    t, d = p_conv.shape

    def body(gl_ref, pc_ref, ps_ref, o_ref):
        g = _sigmoid(gl_ref[...].astype(F32))
        o_ref[...] = (g[:, :d] * pc_ref[...] + g[:, d:] * ps_ref[...]).astype(BF16)

    row = pl.BlockSpec((ROW_TILE, d), lambda i: (i, 0))
    return pl.pallas_call(
        body, name=name, out_shape=jax.ShapeDtypeStruct((t, d), BF16), grid=(t // ROW_TILE,),
        in_specs=[pl.BlockSpec((ROW_TILE, 2 * d), lambda i: (i, 0)), row, row], out_specs=row,
        compiler_params=_params(("parallel",)))(proj, p_conv, p_ssm)


def _merge_bwd(dmerged, proj, p_conv, p_ssm, name):
    t, d = p_conv.shape
    pw = proj.shape[1]

    def body(dm_ref, gl_ref, pc_ref, ps_ref, dpc_ref, dps_ref, dgl_ref):
        g = _sigmoid(gl_ref[...].astype(F32))
        gc, gs = g[:, :d], g[:, d:]
        dm = dm_ref[...]
        dpc_ref[...] = (dm * gc).astype(BF16)
        dps_ref[...] = (dm * gs).astype(BF16)
        dgl_ref[...] = jnp.concatenate(
            [dm * pc_ref[...] * gc * (1.0 - gc), dm * ps_ref[...] * gs * (1.0 - gs)], axis=1).astype(BF16)

    row = pl.BlockSpec((ROW_TILE, d), lambda i: (i, 0))
    wide = pl.BlockSpec((ROW_TILE, 2 * d), lambda i: (i, 0))
    return pl.pallas_call(
        body, name=name,
        out_shape=[jax.ShapeDtypeStruct((t, d), BF16), jax.ShapeDtypeStruct((t, d), BF16),
                   jax.ShapeDtypeStruct((t, pw), BF16)],
        grid=(t // ROW_TILE,), in_specs=[row, wide, row, row], out_specs=[row, row, wide],
        compiler_params=_params(("parallel",)))(dmerged, proj, p_conv, p_ssm)


def _halo_specs(t, width, col):
    nb = t // 8
    step = ROW_TILE // 8
    prev = pl.BlockSpec((8, width), lambda i: (jnp.maximum(i * step - 1, 0), col))
    nxt = pl.BlockSpec((8, width), lambda i: (jnp.minimum((i + 1) * step, nb - 1), col))
    return prev, nxt


def _gconv_fwd(proj, conv_w8, d, name):
    t = proj.shape[0]

    def body(cb_ref, cc_ref, cx_ref, ccp_ref, cxp_ref, w_ref, o_ref):
        first = pl.program_id(0) == 0
        v = cc_ref[...].astype(F32) * cx_ref[...].astype(F32)
        vp = jnp.where(first, 0.0, ccp_ref[...].astype(F32) * cxp_ref[...].astype(F32))
        cv = sum(_rows(w_ref, k) * _shift_down(v, vp, 2 - k) for k in range(3))
        o_ref[...] = (cb_ref[...].astype(F32) * cv).astype(BF16)

    def win(col):
        return pl.BlockSpec((ROW_TILE, d), lambda i: (i, col))

    return pl.pallas_call(
        body, name=name, out_shape=jax.ShapeDtypeStruct((t, d), BF16), grid=(t // ROW_TILE,),
        in_specs=[win(2), win(3), win(4), _halo_specs(t, d, 3)[0], _halo_specs(t, d, 4)[0],
                  pl.BlockSpec((8, d), lambda i: (0, 0))],
        out_specs=pl.BlockSpec((ROW_TILE, d), lambda i: (i, 0)),
        compiler_params=_params(("parallel",)))(proj, proj, proj, proj, proj, conv_w8)


def _gconv_bwd(dy, proj, conv_w8, dproj, d, name):
    t = proj.shape[0]
    nt = t // ROW_TILE

    def body(dy_ref, dyn_ref, cb_ref, cc_ref, cx_ref, ccp_ref, cxp_ref, cbn_ref, w_ref, _, dp_ref, dw_ref, st_ref):
        i, j = pl.program_id(0), pl.program_id(1)

        @pl.when(j == 0)
        def _():
            cb, cc, cx = cb_ref[...].astype(F32), cc_ref[...].astype(F32), cx_ref[...].astype(F32)
            v = cc * cx
            vp = jnp.where(i == 0, 0.0, ccp_ref[...].astype(F32) * cxp_ref[...].astype(F32))
            sh = [_shift_down(v, vp, 2 - k) for k in range(3)]
            cv = sum(_rows(w_ref, k) * sh[k] for k in range(3))
            dyv = dy_ref[...]
            dcv = dyv * cb
            dcvn = jnp.where(i == nt - 1, 0.0, dyn_ref[...] * cbn_ref[...].astype(F32))
            dv = sum(_rows(w_ref, k) * _shift_up(dcv, dcvn, 2 - k) for k in range(3))
            st_ref[0] = (dyv * cv).astype(BF16)
            st_ref[1] = (dv * cx).astype(BF16)
            st_ref[2] = (dv * cc).astype(BF16)
            upd = jnp.concatenate([jnp.sum(dcv * sh[k], axis=0, keepdims=True) for k in range(3)]
                                  + [jnp.zeros((5, d), F32)], axis=0)

            @pl.when(i == 0)
            def _():
                dw_ref[...] = upd

            @pl.when(i > 0)
            def _():
                dw_ref[...] += upd

        dp_ref[...] = st_ref[j]

    def win(col):
        return pl.BlockSpec((ROW_TILE, d), lambda i, j: (i, col))

    def halo(col, which):
        nb, step = t // 8, ROW_TILE // 8
        if which == 0:
            return pl.BlockSpec((8, d), lambda i, j: (jnp.maximum(i * step - 1, 0), col))
        return pl.BlockSpec((8, d), lambda i, j: (jnp.minimum((i + 1) * step, nb - 1), col))

    row = pl.BlockSpec((ROW_TILE, d), lambda i, j: (i, 0))
    w8 = pl.BlockSpec((8, d), lambda i, j: (0, 0))
    return pl.pallas_call(
        body, name=name,
        out_shape=[jax.ShapeDtypeStruct(dproj.shape, BF16), jax.ShapeDtypeStruct((8, d), F32)],
        grid=(nt, 3),
        in_specs=[row, halo(0, 1), win(2), win(3), win(4), halo(3, 0), halo(4, 0), halo(2, 1), w8,
                  pl.BlockSpec(memory_space=pl.ANY)],
        out_specs=[pl.BlockSpec((ROW_TILE, d), lambda i, j: (i, 2 + j)), w8],
        scratch_shapes=[pltpu.VMEM((3, ROW_TILE, d), BF16)],
        input_output_aliases={9: 0},
        compiler_params=_params(("arbitrary", "arbitrary")))(
            dy, dy, proj, proj, proj, proj, proj, proj, conv_w8, dproj)


def _dsilu(p):
    s = _sigmoid(p)
    return s * (1.0 + p * (1.0 - s))


def _sconv_specs(t, d, col0, nt):
    nb, step = t // 8, ROW_TILE // 8
    win = pl.BlockSpec((ROW_TILE, d), lambda j, i: (i, col0 + j))
    prev = pl.BlockSpec((8, d), lambda j, i: (jnp.maximum(i * step - 1, 0), col0 + j))
    nxt = pl.BlockSpec((8, d), lambda j, i: (jnp.minimum((i + 1) * step, nb - 1), col0 + j))
    return win, prev, nxt


def _sconv_fwd(proj, w8, b8, d, col0, ncol, name):
    t = proj.shape[0]
    nt = t // ROW_TILE

    def body(x_ref, xp_ref, w_ref, b_ref, o_ref):
        xv = x_ref[...].astype(F32)
        xp = jnp.where(pl.program_id(1) == 0, 0.0, xp_ref[...].astype(F32))
        pre = _rows(b_ref, 0) + sum(_rows(w_ref, k) * _shift_down(xv, xp, 3 - k) for k in range(4))
        o_ref[...] = (pre * _sigmoid(pre)).astype(BF16)

    win, prev, _ = _sconv_specs(t, d, col0, nt)
    par = pl.BlockSpec((8, d), lambda j, i: (0, j))
    return pl.pallas_call(
        body, name=name, out_shape=jax.ShapeDtypeStruct((t, ncol * d), BF16), grid=(ncol, nt),
        in_specs=[win, prev, par, par], out_specs=pl.BlockSpec((ROW_TILE, d), lambda j, i: (i, j)),
        compiler_params=_params(("parallel", "parallel")))(proj, proj, w8, b8)


def _sconv_bwd(dxc, proj, w8, b8, dproj, d, col0, ncol, name):
    t = proj.shape[0]
    nt = t // ROW_TILE

    def body(d_ref, dn_ref, x_ref, xp_ref, xn_ref, w_ref, b_ref, _, dp_ref, dw_ref):
        i = pl.program_id(1)
        xv = x_ref[...].astype(F32)
        xp = jnp.where(i == 0, 0.0, xp_ref[...].astype(F32))
        bias = _rows(b_ref, 0)
        sh = [_shift_down(xv, xp, 3 - k) for k in range(4)]
        pre = bias + sum(_rows(w_ref, k) * sh[k] for k in range(4))
        dpre = d_ref[...] * _dsilu(pre)
        xn = xn_ref[...].astype(F32)
        pre_n = bias + sum(_rows(w_ref, k) * _shift_down(xn, xv[ROW_TILE - 8:], 3 - k) for k in range(4))
        dpre_n = jnp.where(i == nt - 1, 0.0, dn_ref[...] * _dsilu(pre_n))
        dx = sum(_rows(w_ref, k) * _shift_up(dpre, dpre_n, 3 - k) for k in range(4))
        dp_ref[...] = dx.astype(BF16)
        upd = jnp.concatenate([jnp.sum(dpre * sh[k], axis=0, keepdims=True) for k in range(4)]
                              + [jnp.sum(dpre, axis=0, keepdims=True), jnp.zeros((3, d), F32)], axis=0)

        @pl.when(i == 0)
        def _():
            dw_ref[...] = upd

        @pl.when(i > 0)
        def _():
            dw_ref[...] += upd

    win, prev, nxt = _sconv_specs(t, d, col0, nt)
    nb, step = t // 8, ROW_TILE // 8
    dwin = pl.BlockSpec((ROW_TILE, d), lambda j, i: (i, j))
    dnxt = pl.BlockSpec((8, d), lambda j, i: (jnp.minimum((i + 1) * step, nb - 1), j))
    par = pl.BlockSpec((8, d), lambda j, i: (0, j))
    return pl.pallas_call(
        body, name=name,
        out_shape=[jax.ShapeDtypeStruct(dproj.shape, BF16), jax.ShapeDtypeStruct((8, ncol * d), F32)],
        grid=(ncol, nt),
        in_specs=[dwin, dnxt, win, prev, nxt, par, par, pl.BlockSpec(memory_space=pl.ANY)],
        out_specs=[win, par], input_output_aliases={7: 0},
        compiler_params=_params(("arbitrary", "arbitrary")))(dxc, dxc, proj, proj, proj, w8, b8, dproj)


def _softplus(x):
    return jnp.maximum(x, 0.0) + jnp.log(1.0 + jnp.exp(-jnp.abs(x)))


def _dt_fwd(proj, par8, nheads, dt_col, name):
    t = proj.shape[0]

    def body(r_ref, p_ref, dt_ref, ac_ref):
        lane = lax.broadcasted_iota(jnp.int32, (1, LANE), 1)
        valid = lane < nheads
        raw = r_ref[...][:, :LANE].astype(F32)
        dt = jnp.where(valid, _softplus(raw + _rows(p_ref, 0)), 0.0)
        a = jnp.where(valid, -jnp.exp(_rows(p_ref, 1)), 0.0)
        dt_ref[...] = dt
        ac_ref[...] = _xdot01(_chunk_tri(ROW_TILE, True), dt * a)

    out = pl.BlockSpec((ROW_TILE, LANE), lambda i: (i, 0))
    return pl.pallas_call(
        body, name=name,
        out_shape=[jax.ShapeDtypeStruct((t, LANE), F32), jax.ShapeDtypeStruct((t, LANE), F32)],
        grid=(t // ROW_TILE,),
        in_specs=[pl.BlockSpec((ROW_TILE, 2 * LANE), lambda i: (i, dt_col)), pl.BlockSpec((8, LANE), lambda i: (0, 0))],
        out_specs=[out, out], compiler_params=_params(("parallel",)))(proj, par8)


def _dt_bwd(ddt_g, dac_g, dt, proj, par8, dproj, nheads, dt_col, name):
    t = proj.shape[0]
    ng = ddt_g.shape[0]

    def body(dd_ref, da_ref, dt_ref, r_ref, p_ref, _, dp_ref, acc_ref):
        lane = lax.broadcasted_iota(jnp.int32, (1, LANE), 1)
        valid = lane < nheads
        ddt = sum(dd_ref[g] for g in range(ng))
        dac = sum(da_ref[g] for g in range(ng))
        a = jnp.where(valid, -jnp.exp(_rows(p_ref, 1)), 0.0)
        d_a = _xdot01(_chunk_tri(ROW_TILE, False), dac)
        ddt = ddt + d_a * a
        raw = r_ref[...][:, :LANE].astype(F32)
        draw = jnp.where(valid, ddt * _sigmoid(raw + _rows(p_ref, 0)), 0.0)
        dp_ref[...] = jnp.concatenate([draw, jnp.zeros_like(draw)], axis=1).astype(BF16)
        upd = jnp.concatenate([jnp.sum(draw, axis=0, keepdims=True),
                               jnp.sum(d_a * dt_ref[...], axis=0, keepdims=True) * a,
                               jnp.zeros((6, LANE), F32)], axis=0)

        @pl.when(pl.program_id(0) == 0)
        def _():
            acc_ref[...] = upd

        @pl.when(pl.program_id(0) > 0)
        def _():
            acc_ref[...] += upd

    row = pl.BlockSpec((ROW_TILE, LANE), lambda i: (i, 0))
    grp = pl.BlockSpec((ng, ROW_TILE, LANE), lambda i: (0, i, 0))
    win = pl.BlockSpec((ROW_TILE, 2 * LANE), lambda i: (i, dt_col))
    par = pl.BlockSpec((8, LANE), lambda i: (0, 0))
    return pl.pallas_call(
        body, name=name,
        out_shape=[jax.ShapeDtypeStruct(dproj.shape, BF16), jax.ShapeDtypeStruct((8, LANE), F32)],
        grid=(t // ROW_TILE,),
        in_specs=[grp, grp, row, win, par, pl.BlockSpec(memory_space=pl.ANY)],
        out_specs=[win, par], input_output_aliases={5: 0},
        compiler_params=_params(("arbitrary",)))(ddt_g, dac_g, dt, proj, par8, dproj)


def _ssd_chunk(xs, bm, cm, z, dt, ac, st, dsk, nw, g):
    q = xs.shape[0]
    lane_h = lax.broadcasted_iota(jnp.int32, (1, LANE), 1)
    sub_h = lax.broadcasted_iota(jnp.int32, (LANE, 1), 0)
    lane_c = lax.broadcasted_iota(jnp.int32, (1, GROUP_W), 1) // HEAD_DIM
    rr = lax.broadcasted_iota(jnp.int32, (q, q), 0)
    cc = lax.broadcasted_iota(jnp.int32, (q, q), 1)
    tril = rr >= cc
    last_row = jnp.where(lax.broadcasted_iota(jnp.int32, (q, 1), 0) == q - 1, 1.0, 0.0)
    act = ac.T
    col, row = [], []
    for r in range(HEADS_PER_GROUP):
        h = g * HEADS_PER_GROUP + r
        col.append(jnp.sum(ac * jnp.where(lane_h == h, 1.0, 0.0), axis=1, keepdims=True))
        row.append(jnp.sum(act * jnp.where(sub_h == h, 1.0, 0.0), axis=0, keepdims=True))

    def per_head(vals):
        out = jnp.broadcast_to(vals[0], (q, GROUP_W))
        for r in range(1, HEADS_PER_GROUP):
            out = jnp.where(lane_c == r, vals[r], out)
        return out

    spread = jnp.where(lax.broadcasted_iota(jnp.int32, (LANE, GROUP_W), 0)
                       == g * HEADS_PER_GROUP + lax.broadcasted_iota(jnp.int32, (LANE, GROUP_W), 1) // HEAD_DIM,
                       1.0, 0.0).astype(BF16)
    acx = _spread(ac, spread)
    x = xs * _spread(dt, spread)
    s = _bdot(cm, bm, "nt")
    y = per_head([_bdot(s * jnp.exp(jnp.where(tril, col[r] - row[r], -jnp.inf)), x, "nn")
                  for r in range(HEADS_PER_GROUP)])
    y = y + jnp.exp(acx) * _bdot(cm, st, "nn")
    last = jnp.sum(acx * last_row, axis=0, keepdims=True)
    new_st = st * jnp.exp(last) + _bdot(bm, x * jnp.exp(last - acx), "tn")
    y = y + dsk * xs
    y = y * (z * _sigmoid(z))
    yn = y * lax.rsqrt(jnp.mean(y * y, axis=1, keepdims=True) + EPS) * nw
    return yn, new_st


def _ssd_specs(nc, z_col, ngroups, rev):
    assert ngroups % SSD_GROUPS == 0 and z_col % SSD_GROUPS == 0
    z_col //= SSD_GROUPS
    b_col, c_col = 2 * ngroups // SSD_GROUPS, 3 * ngroups // SSD_GROUPS

    def ci(c):
        return nc - 1 - c if rev else c

    nx, nb = SSD_GROUPS * GROUP_W, SSD_GROUPS * N_STATE
    xs = pl.BlockSpec((SSD_Q, nx), lambda g, c: (ci(c), g))
    bm = pl.BlockSpec((SSD_Q, nb), lambda g, c: (ci(c), b_col + g))
    cm = pl.BlockSpec((SSD_Q, nb), lambda g, c: (ci(c), c_col + g))
    z = pl.BlockSpec((SSD_Q, nx), lambda g, c: (ci(c), z_col + g))
    hd = pl.BlockSpec((SSD_Q, LANE), lambda g, c: (ci(c), 0))
    par = pl.BlockSpec((8, nx), lambda g, c: (0, g))
    stb = pl.BlockSpec((None, N_STATE, nx), lambda g, c: (ci(c), 0, g))
    db = pl.BlockSpec((SSD_Q, nb), lambda g, c: (ci(c), g))
    return xs, bm, cm, z, hd, par, stb, db


def _ssd_fwd(xc, proj, dt, ac, par8, ngroups, z_col, name):
    t = xc.shape[0]
    nc = t // SSD_Q
    steps = ngroups // SSD_GROUPS

    def body(x_ref, b_ref, c_ref, z_ref, dt_ref, ac_ref, p_ref, y_ref, so_ref, st_ref):
        @pl.when(pl.program_id(1) == 0)
        def _():
            st_ref[...] = jnp.zeros_like(st_ref)

        so_ref[...] = st_ref[...]
        dtv, acv = dt_ref[...], ac_ref[...]
        for i in range(SSD_GROUPS):
            cx, cb = slice(i * GROUP_W, (i + 1) * GROUP_W), slice(i * N_STATE, (i + 1) * N_STATE)
            yn, new_st = _ssd_chunk(x_ref[:, cx].astype(F32), b_ref[:, cb].astype(F32), c_ref[:, cb].astype(F32),
                                    z_ref[:, cx].astype(F32), dtv, acv, st_ref[:, cx],
                                    p_ref[0:1, cx], p_ref[1:2, cx], pl.program_id(0) * SSD_GROUPS + i)
            y_ref[:, cx] = yn.astype(BF16)
            st_ref[:, cx] = new_st

    xs, bm, cm, zs, hd, par, stb, _ = _ssd_specs(nc, z_col, ngroups, False)
    return pl.pallas_call(
        body, name=name,
        out_shape=[jax.ShapeDtypeStruct((t, ngroups * GROUP_W), BF16),
                   jax.ShapeDtypeStruct((nc, N_STATE, ngroups * GROUP_W), F32)],
        grid=(steps, nc), in_specs=[xs, bm, cm, zs, hd, hd, par], out_specs=[xs, stb],
        scratch_shapes=[pltpu.VMEM((N_STATE, SSD_GROUPS * GROUP_W), F32)],
        compiler_params=_params(("parallel", "arbitrary")))(xc, xc, xc, proj, dt, ac, par8)


def _ssd_bwd(dyn, xc, proj, dt, ac, par8, states, dproj, ngroups, z_col, name):
    t = xc.shape[0]
    nc = t // SSD_Q
    steps = ngroups // SSD_GROUPS

    def body(dy_ref, x_ref, b_ref, c_ref, z_ref, dt_ref, ac_ref, p_ref, s_ref, _,
             dz_ref, dx_ref, db_ref, dc_ref, ddt_ref, dac_ref, dp_ref, ds_ref):
        c = pl.program_id(1)

        @pl.when(c == 0)
        def _():
            ds_ref[...] = jnp.zeros_like(ds_ref)

        dtv, acv = dt_ref[...], ac_ref[...]
        ddt_sum, dac_sum, upds = None, None, []
        for i in range(SSD_GROUPS):
            cx, cb = slice(i * GROUP_W, (i + 1) * GROUP_W), slice(i * N_STATE, (i + 1) * N_STATE)
            g = pl.program_id(0) * SSD_GROUPS + i
            args = (x_ref[:, cx].astype(F32), b_ref[:, cb].astype(F32), c_ref[:, cb].astype(F32),
                    z_ref[:, cx].astype(F32), dtv, acv, s_ref[:, cx], p_ref[0:1, cx], p_ref[1:2, cx])
            _, vjp = jax.vjp(lambda *a, g=g: _ssd_chunk(*a, g), *args)
            dxs, dbm, dcm, dz, ddt, dac, dst, ddsk, dnw = vjp((dy_ref[:, cx], ds_ref[:, cx]))
            ds_ref[:, cx] = dst
            dz_ref[:, cx] = dz.astype(BF16)
            dx_ref[:, cx] = dxs
            db_ref[:, cb] = dbm
            dc_ref[:, cb] = dcm
            ddt_sum = ddt if ddt_sum is None else ddt_sum + ddt
            dac_sum = dac if dac_sum is None else dac_sum + dac
            upds.append(jnp.concatenate([ddsk, dnw, jnp.zeros((6, GROUP_W), F32)], axis=0))
        ddt_ref[...] = ddt_sum
        dac_ref[...] = dac_sum
        upd = jnp.concatenate(upds, axis=1)

        @pl.when(c == 0)
        def _():
            dp_ref[...] = upd

        @pl.when(c > 0)
        def _():
            dp_ref[...] += upd

    xs, bm, cm, zs, hd, par, stb, db = _ssd_specs(nc, z_col, ngroups, True)
    hg = pl.BlockSpec((None, SSD_Q, LANE), lambda g, c: (g, nc - 1 - c, 0))
    return pl.pallas_call(
        body, name=name,
        out_shape=[jax.ShapeDtypeStruct(dproj.shape, BF16),
                   jax.ShapeDtypeStruct((t, ngroups * GROUP_W), F32),
                   jax.ShapeDtypeStruct((t, ngroups * N_STATE), F32), jax.ShapeDtypeStruct((t, ngroups * N_STATE), F32),
                   jax.ShapeDtypeStruct((steps, t, LANE), F32), jax.ShapeDtypeStruct((steps, t, LANE), F32),
                   jax.ShapeDtypeStruct((8, ngroups * GROUP_W), F32)],
        grid=(steps, nc),
        in_specs=[xs, xs, bm, cm, zs, hd, hd, par, stb, pl.BlockSpec(memory_space=pl.ANY)],
        out_specs=[zs, xs, db, db, hg, hg, par], input_output_aliases={9: 0},
        scratch_shapes=[pltpu.VMEM((N_STATE, SSD_GROUPS * GROUP_W), F32)],
        compiler_params=_params(("arbitrary", "arbitrary")))(dyn, xc, xc, xc, proj, dt, ac, par8, states, dproj)


def _small_matmul(a, b, mode, name):
    if mode == "nn":
        shape = (a.shape[0], b.shape[1])
    else:
        shape = (a.shape[1], b.shape[1])

    def body(a_ref, b_ref, o_ref):
        o_ref[...] = _dot(a_ref[...], b_ref[...], mode)

    return pl.pallas_call(body, name=name, out_shape=jax.ShapeDtypeStruct(shape, F32),
                          compiler_params=_params())(a, b)


def _adamw(parts, w, m, v, name):
    npart, rows, cols = parts.shape
    tr = rows if rows <= 128 else _tile8(rows, 128)

    def body(p_ref, w_ref, m_ref, v_ref, g_ref, d_ref, nm_ref, nv_ref):
        g = p_ref[0].astype(F32)
        for j in range(1, npart):
            g = g + p_ref[j].astype(F32)
        mm = ADAM_B1 * m_ref[...] + (1.0 - ADAM_B1) * g
        vv = ADAM_B2 * v_ref[...] + (1.0 - ADAM_B2) * (g * g)
        m_hat = mm / (1.0 - ADAM_B1 ** ADAM_STEP)
        v_hat = vv / (1.0 - ADAM_B2 ** ADAM_STEP)
        g_ref[...] = g
        d_ref[...] = -ADAM_LR * (m_hat / (jnp.sqrt(v_hat) + ADAM_EPS) + ADAM_WD * w_ref[...])
        nm_ref[...] = mm
        nv_ref[...] = vv

    blk = pl.BlockSpec((tr, cols), lambda i: (i, 0))
    out = jax.ShapeDtypeStruct((rows, cols), F32)
    return pl.pallas_call(
        body, name=name, out_shape=[out] * 4, grid=(rows // tr,),
        in_specs=[pl.BlockSpec((npart, tr, cols), lambda i: (0, i, 0)), blk, blk, blk], out_specs=[blk] * 4,
        compiler_params=_params(("parallel",)))(parts, w, m, v)


def _tile8(n, target):
    best = 8
    for d in range(8, target + 1, 8):
        if n % d == 0:
            best = d
    return best


def _pad_rows8(a):
    return jnp.concatenate([a, jnp.zeros((8 - a.shape[0], a.shape[1]), a.dtype)], axis=0)


def _pad_lanes(a, n):
    return jnp.concatenate([a, jnp.zeros(a.shape[:-1] + (n - a.shape[-1],), a.dtype)], axis=-1)


def _pack(vecs, mult):
    flat = jnp.concatenate([v.reshape(-1) for v in vecs])
    pad = (-flat.shape[0]) % mult
    flat = jnp.concatenate([flat, jnp.zeros((pad,), flat.dtype)])
    return flat.reshape(-1, LANE)


def _unpack(flat, shapes):
    out, o = [], 0
    for s in shapes:
        n = int(np.prod(s))
        out.append(flat[o:o + n].reshape(s))
        o += n
    return out


def kernel(x, c, w_ada, b_ada, ln1, ln2, w_in, conv_w, ssm_conv_w, ssm_conv_b, dt_bias, a_log, d_skip, ssm_norm_w, w_conv_out, w_ssm_out, w_o, w_up, w_down, final_norm, loss_target, m_w_ada, m_b_ada, m_ln1, m_ln2, m_w_in, m_conv_w, m_ssm_conv_w, m_ssm_conv_b, m_dt_bias, m_a_log, m_d_skip, m_ssm_norm_w, m_w_conv_out, m_w_ssm_out, m_w_o, m_w_up, m_w_down, m_final_norm, v_w_ada, v_b_ada, v_ln1, v_ln2, v_w_in, v_conv_w, v_ssm_conv_w, v_ssm_conv_b, v_dt_bias, v_a_log, v_d_skip, v_ssm_norm_w, v_w_conv_out, v_w_ssm_out, v_w_o, v_w_up, v_w_down, v_final_norm):
    names = ["w_ada", "b_ada", "ln1", "ln2", "w_in", "conv_w", "ssm_conv_w", "ssm_conv_b", "dt_bias", "a_log",
             "d_skip", "ssm_norm_w", "w_conv_out", "w_ssm_out", "w_o", "w_up", "w_down", "final_norm"]
    w_of = dict(zip(names, [w_ada, b_ada, ln1, ln2, w_in, conv_w, ssm_conv_w, ssm_conv_b, dt_bias, a_log, d_skip,
                            ssm_norm_w, w_conv_out, w_ssm_out, w_o, w_up, w_down, final_norm]))
    m_of = dict(zip(names, [m_w_ada, m_b_ada, m_ln1, m_ln2, m_w_in, m_conv_w, m_ssm_conv_w, m_ssm_conv_b, m_dt_bias,
                            m_a_log, m_d_skip, m_ssm_norm_w, m_w_conv_out, m_w_ssm_out, m_w_o, m_w_up, m_w_down,
                            m_final_norm]))
    v_of = dict(zip(names, [v_w_ada, v_b_ada, v_ln1, v_ln2, v_w_in, v_conv_w, v_ssm_conv_w, v_ssm_conv_b, v_dt_bias,
                            v_a_log, v_d_skip, v_ssm_norm_w, v_w_conv_out, v_w_ssm_out, v_w_o, v_w_up, v_w_down,
                            v_final_norm]))

    _, t, d = x.shape
    nl = w_ada.shape[0]
    ada_w = w_ada.shape[2]
    ds = ssm_norm_w.shape[1]
    nh = dt_bias.shape[1]
    ng = nh // HEADS_PER_GROUP
    gn = ng * N_STATE
    xbc_w = ds + 2 * gn
    z_off, xbc_off = 5 * d, 5 * d + ds
    dt_off = xbc_off + xbc_w
    proj_w = dt_off + nh
    pw = dt_off + 2 * LANE
    me = 4 * lax.axis_index("x") + 2 * lax.axis_index("y") + lax.axis_index("c")

    x2, tgt = x[0], loss_target[0]

    c_act = c * jax.nn.sigmoid(c)
    sizes1 = [(1, d), conv_w.shape, ssm_conv_w.shape]
    (g1,) = _all_gather([_pack([c_act, conv_w, ssm_conv_w], LANE)], "ag_small_in")
    parts1 = [_unpack(g1[j].reshape(-1), sizes1) for j in range(N_DEV)]
    c_act_all = jnp.concatenate([p[0] for p in parts1], axis=0)
    conv_w_full = jnp.concatenate([p[1] for p in parts1], axis=-1)
    sconv_w_full = jnp.concatenate([p[2] for p in parts1], axis=-1)

    mod_part = jnp.stack([_small_matmul(c_act_all, w_ada[l], "nn", f"ada_fwd{l}") for l in range(nl)])
    (gmod,) = _all_gather([mod_part], "ag_mod")
    mod = lax.dynamic_index_in_dim(gmod, me, axis=2, keepdims=False)
    mod = jnp.moveaxis(mod, 0, 1).reshape(nl, N_DEV * ada_w) + b_ada
    mod = mod.reshape(nl, 6, d)
    modp = [jnp.concatenate([mod[l], ln1[l][None], ln2[l][None]], axis=0) for l in range(nl)]

    big = ["w_in", "w_up", "w_conv_out", "w_ssm_out", "w_o", "w_down"]
    rest = big[1:]

    def blocks_of(l, keys):
        return [w_of[k][l].astype(BF16) for k in keys]

    def w_in_halves(l):
        w = w_of["w_in"][l].astype(BF16)
        return [w[:d // 2]], [w[d // 2:]]

    def full_cols(g):
        return jnp.moveaxis(g, 0, 1).reshape(g.shape[1], N_DEV * g.shape[2])

    def full_rows(g):
        return g.reshape(N_DEV * g.shape[1], g.shape[2])

    def pad_w_in(pieces):
        w = jnp.concatenate([full_cols(g) for g in pieces], axis=0)
        return jnp.concatenate([w, jnp.zeros((d, pw - proj_w), BF16)], axis=-1)

    wi = [None] * nl
    wi[0] = pad_w_in(_all_gather(blocks_of(0, ["w_in"]), "ag_w_in0"))

    cw8 = [_pad_rows8(conv_w_full[l]) for l in range(nl)]
    sw8 = [_pad_rows8(sconv_w_full[l]) for l in range(nl)]
    sb8 = [_pad_rows8(ssm_conv_b[l][None]) for l in range(nl)]
    dtp8 = [_pad_rows8(_pad_lanes(jnp.stack([dt_bias[l], a_log[l]]), LANE)) for l in range(nl)]
    sp8 = [_pad_rows8(jnp.stack([jnp.repeat(d_skip[l], HEAD_DIM), ssm_norm_w[l]])) for l in range(nl)]
    fn8 = _pad_rows8(final_norm[None])
    xbc_col, ncol, dt_col, z_col = xbc_off // d, xbc_w // d, dt_off // (2 * LANE), z_off // GROUP_W

    saved = []
    x_cur, br_prev = x2, None
    for l in range(nl):
        if l == 0:
            x_in, u1 = x_cur, _norm_fwd(x_cur, modp[0], 0, "norm_first")
        else:
            x_in, u1 = _resid_norm_fwd(x_cur, br_prev, modp[l - 1], 5, modp[l], 0, f"resid_norm_a{l}")
        proj, got = _matmul(u1, wi[l], "nn", BF16, f"mm_in{l}", comm=_GatherComm(blocks_of(l, rest)))
        got = dict(zip(rest, got))
        wup, wdown = full_cols(got["w_up"]), full_rows(got["w_down"])
        wco, wso, wo = full_rows(got["w_conv_out"]), full_rows(got["w_ssm_out"]), full_rows(got["w_o"])
        y_conv = _gconv_fwd(proj, cw8[l], d, f"gconv_fwd{l}")
        xc = _sconv_fwd(proj, sw8[l], sb8[l], d, xbc_col, ncol, f"sconv_fwd{l}")
        dt, ac = _dt_fwd(proj, dtp8[l], nh, dt_col, f"dt_fwd{l}")
        yn, states = _ssd_fwd(xc, proj, dt, ac, sp8[l], ng, z_col, f"ssd_fwd{l}")
        p_conv = _matmul(y_conv, wco, "nn", F32, f"mm_conv_out{l}")
        p_ssm = _matmul(yn, wso, "nn", F32, f"mm_ssm_out{l}")
        merged = _merge_fwd(proj, p_conv, p_ssm, f"merge_fwd{l}")
        mix = _matmul(merged, wo, "nn", F32, f"mm_o{l}")
        x_mid, u2 = _resid_norm_fwd(x_in, mix, modp[l], 2, modp[l], 1, f"resid_norm_b{l}")
        if l + 1 < nl:
            top, bot = w_in_halves(l + 1)
            (h, hid), g_top = _matmul(u2, wup, "nn", BF16, f"mm_up{l}", epi="relu2", comm=_GatherComm(top))
            mlp, g_bot = _matmul(hid, wdown, "nn", F32, f"mm_down{l}", comm=_GatherComm(bot))
            wi[l + 1] = pad_w_in(g_top + g_bot)
        else:
            h, hid = _matmul(u2, wup, "nn", BF16, f"mm_up{l}", epi="relu2")
            mlp = _matmul(hid, wdown, "nn", F32, f"mm_down{l}")
        saved.append(dict(x_in=x_in, u1=u1, proj=proj, y_conv=y_conv, xc=xc, dt=dt, ac=ac, yn=yn, states=states,
                          p_conv=p_conv, p_ssm=p_ssm, merged=merged, mix=mix, x_mid=x_mid, u2=u2, h=h, hid=hid,
                          mlp=mlp, wup=wup, wdown=wdown, wco=wco, wso=wso, wo=wo))
        x_cur, br_prev = x_mid, mlp

    dx, dbr, acc = _final_fwd_bwd(x_cur, br_prev, modp[nl - 1], fn8, tgt, "final")
    loss = lax.psum(acc[0, 0], ("x", "y", "c"))
    g_final_norm = acc[1]
    dgate2 = acc[2]

    gw = {k: [None] * nl for k in big}
    small = {k: [None] * nl for k in ["mod", "ln1", "ln2", "conv_w", "ssm_conv_w", "ssm_conv_b", "dt_bias",
                                      "a_log", "d_skip", "ssm_norm_w"]}

    def grad_cols(a):
        return jnp.moveaxis(a.reshape(a.shape[0], N_DEV, a.shape[1] // N_DEV), 1, 0)

    def grad_rows(a):
        return a.reshape(N_DEV, a.shape[0] // N_DEV, a.shape[1])

    def recv_buf(name):
        return jnp.zeros((N_DEV, nl) + w_of[name].shape[1:], BF16)

    bufs_a, bufs_b = [recv_buf("w_in")], [recv_buf(k) for k in big[1:]]
    pending = None
    for l in reversed(range(nl)):
        s = saved[l]
        dh = _matmul(dbr, s["wdown"], "nt", BF16, f"mm_down_dx{l}", epi="relu2_bwd", extra=s["h"])
        gw["w_down"][l] = _matmul(s["hid"], dbr, "tn", BF16, f"mm_down_dw{l}")
        du2 = _matmul(dh, s["wup"], "nt", F32, f"mm_up_dx{l}")
        gw["w_up"][l] = _matmul(s["u2"], dh, "tn", BF16, f"mm_up_dw{l}")
        dx_mid, dmix, acc2 = _resid_norm_bwd(dx, du2, s["x_mid"], modp[l], 1, f"resid_norm_b_bwd{l}",
                                             br=s["mix"], mp_gate=modp[l], gate_row=2)
        dmerged = _matmul(dmix, s["wo"], "nt", F32, f"mm_o_dx{l}")
        gw["w_o"][l] = _matmul(s["merged"], dmix, "tn", BF16, f"mm_o_dw{l}")
        dpc, dps, dproj = _merge_bwd(dmerged, s["proj"], s["p_conv"], s["p_ssm"], f"merge_bwd{l}")
        dyc = _matmul(dpc, s["wco"], "nt", F32, f"mm_conv_out_dx{l}")
        gw["w_conv_out"][l] = _matmul(s["y_conv"], dpc, "tn", BF16, f"mm_conv_out_dw{l}")
        dyn = _matmul(dps, s["wso"], "nt", F32, f"mm_ssm_out_dx{l}")
        gw["w_ssm_out"][l] = _matmul(s["yn"], dps, "tn", BF16, f"mm_ssm_out_dw{l}")
        dproj, dxs, dbm, dcm, ddt_g, dac_g, dsp = _ssd_bwd(dyn, s["xc"], s["proj"], s["dt"], s["ac"], sp8[l],
                                                           s["states"], dproj, ng, z_col, f"ssd_bwd{l}")
        dproj, dcw = _gconv_bwd(dyc, s["proj"], cw8[l], dproj, d, f"gconv_bwd{l}")
        dsw = []
        for seg, (dseg, c0) in enumerate([(dxs, 0), (dbm, ds // d), (dcm, ds // d + gn // d)]):
            cols = slice(c0 * d, c0 * d + dseg.shape[1])
            dproj, part = _sconv_bwd(dseg, s["proj"], sw8[l][:, cols], sb8[l][:, cols], dproj, d, xbc_col + c0,
                                     dseg.shape[1] // d, f"sconv_bwd{l}_{seg}")
            dsw.append(part)
        dsw = jnp.concatenate(dsw, axis=1)
        dproj, ddtp = _dt_bwd(ddt_g, dac_g, s["dt"], s["proj"], dtp8[l], dproj, nh, dt_col, f"dt_bwd{l}")
        rest_grads = [grad_cols(gw["w_up"][l])] + [grad_rows(gw[k][l]) for k in big[2:]]
        du1, bufs_b = _matmul(dproj, wi[l], "nt", F32, f"mm_in_dx{l}", comm=_ExchangeComm(rest_grads, bufs_b, l))

        def mm_in_dw(u1, name, comm):
            if comm is None:
                return _matmul(u1, dproj, "tn", BF16, name), None
            return _matmul(u1, dproj, "tn", BF16, name, comm=comm)

        def w_in_blocks(g):
            return grad_cols(g[:, :proj_w])

        prev = None if pending is None else _ExchangeComm(pending, bufs_a, l + 1)
        if l > 0:
            g_in, got_bufs = mm_in_dw(s["u1"], f"mm_in_dw{l}", prev)
            bufs_a = bufs_a if got_bufs is None else got_bufs
            pending = [w_in_blocks(g_in)]
        else:
            g_top, got_bufs = mm_in_dw(s["u1"][:, :d // 2], "mm_in_dw0_top", prev)
            bufs_a = bufs_a if got_bufs is None else got_bufs
            g_bot, bufs_a = mm_in_dw(s["u1"][:, d // 2:], "mm_in_dw0_bot",
                                     _ExchangeComm([w_in_blocks(g_top)], bufs_a, 0, row0=0))
            pending = [w_in_blocks(g_bot)]
        if l > 0:
            dx, dbr, acc1 = _resid_norm_bwd(dx_mid, du1, s["x_in"], modp[l], 0, f"resid_norm_a_bwd{l}",
                                            br=saved[l - 1]["mlp"], mp_gate=modp[l - 1], gate_row=5)
        else:
            dx, acc1 = _resid_norm_bwd(dx_mid, du1, s["x_in"], modp[0], 0, "norm_first_bwd")
        sc1, sc2 = mod[l, 1], mod[l, 4]
        small["mod"][l] = jnp.stack([acc1[0], acc1[1] * ln1[l], acc2[2], acc2[0], acc2[1] * ln2[l], dgate2])
        small["ln1"][l] = acc1[1] * (1.0 + sc1)
        small["ln2"][l] = acc2[1] * (1.0 + sc2)
        small["conv_w"][l] = dcw[:3]
        small["ssm_conv_w"][l] = dsw[:4]
        small["ssm_conv_b"][l] = dsw[4]
        small["dt_bias"][l] = ddtp[0, :nh]
        small["a_log"][l] = ddtp[1, :nh]
        small["d_skip"][l] = dsp[0].reshape(nh, HEAD_DIM).sum(axis=-1)
        small["ssm_norm_w"][l] = dsp[1]
        if l > 0:
            dgate2 = acc1[2]
    grad_x = dx[None]

    sm = {k: jnp.stack(v) for k, v in small.items()}
    rep_names = ["b_ada", "ln1", "ln2", "ssm_conv_b", "dt_bias", "a_log", "d_skip", "ssm_norm_w", "final_norm"]
    rep_grads = [sm["mod"].reshape(nl, 6 * d), sm["ln1"], sm["ln2"], sm["ssm_conv_b"], sm["dt_bias"], sm["a_log"],
                 sm["d_skip"], sm["ssm_norm_w"], g_final_norm]
    rep_pack = _pack(rep_grads, 8 * LANE)
    conv_pack = _pack([sm["conv_w"], sm["ssm_conv_w"]], 8 * LANE)
    g_rep, g_conv = _all_gather([rep_pack, conv_pack], "ag_small_grads")

    outs = {}

    def run_adamw(name, parts, tag):
        w, m, v = w_of[name], m_of[name], v_of[name]
        shp = w.shape
        r2 = (int(np.prod(shp[:-1])), shp[-1])
        res = _adamw(parts.reshape((parts.shape[0],) + r2), w.reshape(r2), m.reshape(r2), v.reshape(r2), tag)
        outs[name] = [a.reshape(shp) for a in res]

    rep_shapes = [w_of[k].shape for k in rep_names]
    res = _adamw(g_rep, _pack([w_of[k] for k in rep_names], 8 * LANE), _pack([m_of[k] for k in rep_names], 8 * LANE),
                 _pack([v_of[k] for k in rep_names], 8 * LANE), "adamw_replicated")
    for k, vals in zip(rep_names, zip(*[_unpack(a.reshape(-1), rep_shapes) for a in res])):
        outs[k] = list(vals)

    conv_parts = [_unpack(g_conv[j].reshape(-1), [sm["conv_w"].shape, sm["ssm_conv_w"].shape]) for j in range(N_DEV)]
    for idx, name in enumerate(["conv_w", "ssm_conv_w"]):
        wsh = w_of[name].shape[-1]
        full = jnp.stack([p[idx] for p in conv_parts])
        run_adamw(name, lax.dynamic_slice_in_dim(full, me * wsh, wsh, axis=3), "adamw_" + name)

    dmod_all = g_rep.reshape(N_DEV, -1)[:, :nl * 6 * d].reshape(N_DEV, nl, 6 * d)
    dmod_mine = lax.dynamic_slice_in_dim(dmod_all, me * ada_w, ada_w, axis=2)
    g_ada = jnp.stack([_small_matmul(c_act_all, dmod_mine[:, l], "tn", f"ada_bwd{l}") for l in range(nl)])
    run_adamw("w_ada", g_ada[None], "adamw_w_ada")

    bufs_a = _exchange(pending, bufs_a, 0, "a2a_w_in0_bot", row0=d // 2)
    for name, parts in zip(big, list(bufs_a) + list(bufs_b)):
        run_adamw(name, parts, "adamw_" + name)

    result = [loss, grad_x]
    for i in range(4):
        result += [outs[k][i] for k in names]
    return tuple(result)
```

```python
import functools

import numpy as np
import jax
import jax.numpy as jnp
from jax import lax
from jax.experimental import pallas as pl
from jax.experimental.pallas import tpu as pltpu

F32 = jnp.float32
BF16 = jnp.bfloat16
EPS = 1e-6
N_STATE = 128
HEAD_DIM = 64
HEADS_PER_GROUP = 4
GROUP_W = HEAD_DIM * HEADS_PER_GROUP
SSD_GROUPS = 2
SSD_Q = 256
N_DEV = 8
ROW_TILE = 256
LANE = 128
VMEM_LIMIT = 56 * 1024 * 1024
MATMUL_VMEM_BUDGET = 40 * 1024 * 1024
MATMUL_TILE_CAP = 2048
MXU_FLOPS = 9.0e14
HBM_BYTES_PER_S = 3.0e12
STEP_OVERHEAD_S = 0.35e-6

ADAM_LR, ADAM_B1, ADAM_B2, ADAM_EPS, ADAM_WD, ADAM_STEP = 0.001, 0.9, 0.999, 1e-08, 0.01, 10

MESH = pl.DeviceIdType.MESH


def _params(sem=None):
    return pltpu.CompilerParams(dimension_semantics=sem, vmem_limit_bytes=VMEM_LIMIT)


def _divisors(n, cap):
    if n <= cap:
        return [n]
    return [d for d in range(cap - cap % LANE, 0, -LANE) if n % d == 0]


def _matmul_tiles(m, n, k, out_bytes, n_out, n_extra):
    best = None
    for tm in _divisors(m, MATMUL_TILE_CAP):
        for tn in _divisors(n, MATMUL_TILE_CAP):
            for tk in _divisors(k, MATMUL_TILE_CAP):
                out_tile = tm * tn * (out_bytes * n_out + 2 * n_extra)
                vmem = 2 * 2 * (tm * tk + tk * tn) + 2 * out_tile + (4 * tm * tn if tk < k else 0)
                if vmem > MATMUL_VMEM_BUDGET:
                    continue
                steps = (m // tm) * (n // tn) * (k // tk)
                a_reads = 1 if tk == k else n // tn
                traffic = 2 * (a_reads * m * k + (m // tm) * k * n) + m * n * (out_bytes * n_out + 2 * n_extra)
                est = (max(2.0 * m * n * k / MXU_FLOPS, traffic / HBM_BYTES_PER_S) + steps * STEP_OVERHEAD_S
                       + (2 * (tm * tk + tk * tn) + out_tile) / HBM_BYTES_PER_S)
                cand = (est, tm, tn, tk)
                if best is None or cand < best:
                    best = cand
    assert best is not None, (m, n, k)
    return best[1:]


def _tile(n, target):
    if n <= target:
        return n
    best = None
    for d in range(LANE, target + 1, LANE):
        if n % d == 0:
            best = d
    assert best is not None, (n, target)
    return best


def _slot(p):
    return 4 * p[0] + 2 * p[1] + p[2]


def _all_gather(xs, name):
    comm = _GatherComm(xs)

    def body(*refs):
        begin, end = comm.ops(refs[:comm.n_in], refs[comm.n_in:comm.n_in + comm.n_out], refs[-3:])
        begin()
        end()

    any_spec = pl.BlockSpec(memory_space=pl.ANY)
    return pl.pallas_call(
        body, name=name, out_shape=comm.out_shape,
        in_specs=[any_spec] * comm.n_in, out_specs=[any_spec] * comm.n_out, scratch_shapes=comm.sems,
    )(*comm.operands)


class _GatherComm:
    def __init__(self, xs):
        n = len(xs)
        self.operands = list(xs)
        self.n_in = self.n_out = n
        self.aliases = {}
        self.out_shape = [jax.ShapeDtypeStruct((N_DEV,) + x.shape, x.dtype) for x in xs]
        self.sems = [pltpu.SemaphoreType.DMA((7 * n,)), pltpu.SemaphoreType.DMA((7 * n,)),
                     pltpu.SemaphoreType.DMA((n,))]

    def ops(self, x_refs, o_refs, sem_refs):
        n = self.n_in
        send, recv, loc = sem_refs
        x, y, c = lax.axis_index("x"), lax.axis_index("y"), lax.axis_index("c")
        me, sib = (x, y, c), (x, y, 1 - c)
        chips = [(1 - x, y), (x, 1 - y), (1 - x, 1 - y)]

        def cp(a, k, block, to, src=None):
            dst = o_refs[a].at[_slot(block)]
            return pltpu.make_async_remote_copy(
                src_ref=dst if src is None else src, dst_ref=dst,
                send_sem=send.at[7 * a + k], recv_sem=recv.at[7 * a + k],
                device_id=to, device_id_type=MESH)

        mine = [pltpu.make_async_copy(x_refs[a], o_refs[a].at[_slot(me)], loc.at[a]) for a in range(n)]
        first = []
        for a in range(n):
            first.append(cp(a, 0, me, sib, src=x_refs[a]))
            first += [cp(a, 1 + j, me, (*chip, c), src=x_refs[a]) for j, chip in enumerate(chips)]

        def begin():
            for m in mine:
                m.start()
            for f in first:
                f.start()

        def end():
            passed = []
            for j, chip in enumerate(chips):
                for a in range(n):
                    cp(a, 1 + j, (*chip, c), me).wait_recv()
                    p = cp(a, 4 + j, (*chip, c), sib)
                    p.start()
                    passed.append(p)
            for a in range(n):
                cp(a, 0, sib, me).wait_recv()
                for j, chip in enumerate(chips):
                    cp(a, 4 + j, (*chip, 1 - c), me).wait_recv()
            for f in first + passed:
                f.wait_send()
            for m in mine:
                m.wait()

        return begin, end


class _ExchangeComm:
    def __init__(self, xs, bufs, layer, row0=0):
        n = len(xs)
        self.layer = layer
        self.row0 = row0
        self.nrows = [x.shape[1] for x in xs]
        self.operands = list(xs) + list(bufs)
        self.n_in, self.n_out = 2 * n, n
        self.aliases = {n + a: a for a in range(n)}
        self.out_shape = [jax.ShapeDtypeStruct(b.shape, b.dtype) for b in bufs]
        self.sems = [pltpu.SemaphoreType.DMA((7 * n,)), pltpu.SemaphoreType.DMA((7 * n,)),
                     pltpu.SemaphoreType.DMA((n,))]

    def ops(self, in_refs, o_refs, sem_refs):
        n = self.n_out
        x_refs = in_refs[:n]
        send, recv, loc = sem_refs
        layer = self.layer
        x, y, c = lax.axis_index("x"), lax.axis_index("y"), lax.axis_index("c")
        me = (x, y, c)
        peers = []
        for k in range(1, 8):
            kx, ky, kc = (k >> 2) & 1, (k >> 1) & 1, k & 1
            peers.append((x + kx - 2 * x * kx, y + ky - 2 * y * ky, c + kc - 2 * c * kc))

        def land(a, slot):
            return o_refs[a].at[slot, layer, pl.ds(self.row0, self.nrows[a])]

        def cp(a, k, src_slot, dst_slot, to):
            return pltpu.make_async_remote_copy(
                src_ref=x_refs[a].at[src_slot], dst_ref=land(a, dst_slot),
                send_sem=send.at[7 * a + k], recv_sem=recv.at[7 * a + k],
                device_id=to, device_id_type=MESH)

        mine = [pltpu.make_async_copy(x_refs[a].at[_slot(me)], land(a, _slot(me)), loc.at[a])
                for a in range(n)]
        sends = [cp(a, k, _slot(p), _slot(me), p) for a in range(n) for k, p in enumerate(peers)]

        def begin():
            for m in mine:
                m.start()
            for s in sends:
                s.start()

        def end():
            for a in range(n):
                for k, p in enumerate(peers):
                    cp(a, k, _slot(me), _slot(p), me).wait_recv()
            for s in sends:
                s.wait_send()
            for m in mine:
                m.wait()

        return begin, end


def _exchange(xs, bufs, layer, name, row0=0):
    comm = _ExchangeComm(xs, bufs, layer, row0)

    def body(*refs):
        begin, end = comm.ops(refs[:comm.n_in], refs[comm.n_in:comm.n_in + comm.n_out], refs[-3:])
        begin()
        end()

    any_spec = pl.BlockSpec(memory_space=pl.ANY)
    return pl.pallas_call(
        body, name=name, out_shape=comm.out_shape,
        in_specs=[any_spec] * comm.n_in, out_specs=[any_spec] * comm.n_out, scratch_shapes=comm.sems,
        input_output_aliases=dict(comm.aliases),
    )(*comm.operands)


_DIMS = {"nn": (((1,), (0,)), ((), ())), "nt": (((1,), (1,)), ((), ())), "tn": (((0,), (0,)), ((), ()))}


def _matmul(a, b, mode, out_dtype, name, comm=None, epi=None, extra=None):
    if mode == "nn":
        (m, k), (_, n) = a.shape, b.shape
    elif mode == "nt":
        (m, k), (n, _) = a.shape, b.shape
    else:
        (k, m), (_, n) = a.shape, b.shape
    n_ext = 1 if epi == "relu2_bwd" else 0
    n_main = 2 if epi == "relu2" else 1
    tm, tn, tk = _matmul_tiles(m, n, k, jnp.dtype(out_dtype).itemsize, n_main, n_ext)
    nk = k // tk
    grid = (m // tm, n // tn, nk)
    dn = _DIMS[mode]
    n_cin = comm.n_in if comm else 0
    n_cout = comm.n_out if comm else 0
    n_acc = 0 if nk == 1 else 1
    first_in, first_out = 2 + n_ext, 2 + n_ext + n_cin

    def finish(acc, refs):
        outs = refs[first_out:first_out + n_main]
        if epi == "relu2":
            outs[0][...] = acc.astype(outs[0].dtype)
            r = jnp.maximum(acc, 0.0)
            outs[1][...] = (r * r).astype(outs[1].dtype)
        elif epi == "relu2_bwd":
            outs[0][...] = (acc * (2.0 * jnp.maximum(refs[2][...].astype(F32), 0.0))).astype(outs[0].dtype)
        else:
            outs[0][...] = acc.astype(outs[0].dtype)

    def body(*refs):
        a_ref, b_ref = refs[:2]
        scratch = refs[first_out + n_main + n_cout:]
        pid = [pl.program_id(ax) for ax in range(3)]
        if comm:
            begin, end = comm.ops(refs[first_in:first_in + n_cin],
                                  refs[first_out + n_main:first_out + n_main + n_cout], scratch[n_acc:])
            pl.when((pid[0] == 0) & (pid[1] == 0) & (pid[2] == 0))(begin)
        part = lax.dot_general(a_ref[...], b_ref[...], dn, preferred_element_type=F32)
        if nk == 1:
            finish(part, refs)
        else:
            acc_ref = scratch[0]

            @pl.when(pid[2] == 0)
            def _():
                acc_ref[...] = part

            @pl.when(pid[2] > 0)
            def _():
                acc_ref[...] += part

            @pl.when(pid[2] == nk - 1)
            def _():
                finish(acc_ref[...], refs)
        if comm:
            pl.when((pid[0] == grid[0] - 1) & (pid[1] == grid[1] - 1) & (pid[2] == grid[2] - 1))(end)

    a_spec = pl.BlockSpec((tk, tm), lambda i, j, kk: (kk, i)) if mode == "tn" else \
        pl.BlockSpec((tm, tk), lambda i, j, kk: (i, kk))
    b_spec = pl.BlockSpec((tn, tk), lambda i, j, kk: (j, kk)) if mode == "nt" else \
        pl.BlockSpec((tk, tn), lambda i, j, kk: (kk, j))
    o_spec = pl.BlockSpec((tm, tn), lambda i, j, kk: (i, j))
    any_spec = pl.BlockSpec(memory_space=pl.ANY)
    res = pl.pallas_call(
        body, name=name,
        out_shape=[jax.ShapeDtypeStruct((m, n), out_dtype)] * n_main + (comm.out_shape if comm else []),
        grid=grid,
        in_specs=[a_spec, b_spec] + [o_spec] * n_ext + [any_spec] * n_cin,
        out_specs=[o_spec] * n_main + [any_spec] * n_cout,
        scratch_shapes=([] if nk == 1 else [pltpu.VMEM((tm, tn), F32)]) + (comm.sems if comm else []),
        input_output_aliases={first_in + i: n_main + o for i, o in comm.aliases.items()} if comm else {},
        compiler_params=_params(("arbitrary",) * 3 if comm else ("parallel", "parallel", "arbitrary")),
    )(a, b, *([extra] if n_ext else []), *(comm.operands if comm else []))
    main = res[0] if n_main == 1 else tuple(res[:n_main])
    return (main, list(res[n_main:])) if comm else main


def _dot(a, b, mode):
    return lax.dot_general(a.astype(BF16), b.astype(BF16), _DIMS[mode], preferred_element_type=F32)


@functools.partial(jax.custom_vjp, nondiff_argnums=(2,))
def _bdot(a, b, mode):
    return _dot(a, b, mode)


def _bdot_fwd(a, b, mode):
    return _dot(a, b, mode), (a, b)


def _bdot_bwd(mode, res, g):
    a, b = res
    if mode == "nn":
        return _dot(g, b, "nt"), _dot(a, g, "tn")
    if mode == "nt":
        return _dot(g, b, "nn"), _dot(g, a, "tn")
    return _dot(b, g, "nt"), _dot(a, g, "nn")


_bdot.defvjp(_bdot_fwd, _bdot_bwd)


def _split3(x):
    hi = x.astype(BF16)
    r1 = x - hi.astype(F32)
    mid = r1.astype(BF16)
    return hi, mid, (r1 - mid.astype(F32)).astype(BF16)


def _xdot01(m01, x):
    return sum(lax.dot_general(m01, p, _DIMS["nn"], preferred_element_type=F32) for p in _split3(x))


@jax.custom_vjp
def _spread(v, m01):
    return sum(lax.dot_general(p, m01, _DIMS["nn"], preferred_element_type=F32) for p in _split3(v))


def _spread_fwd(v, m01):
    return _spread(v, m01), m01


def _spread_bwd(m01, g):
    dv = sum(lax.dot_general(p, m01, _DIMS["nt"], preferred_element_type=F32) for p in _split3(g)[:2])
    return dv, jnp.zeros_like(m01)


_spread.defvjp(_spread_fwd, _spread_bwd)


def _chunk_tri(n, lower):
    r = lax.broadcasted_iota(jnp.int32, (n, n), 0)
    c = lax.broadcasted_iota(jnp.int32, (n, n), 1)
    tri = (r >= c) if lower else (r <= c)
    return jnp.where(tri & (r // SSD_Q == c // SSD_Q), 1.0, 0.0).astype(BF16)


def _shift_down(x, prev8, k):
    if k == 0:
        return x
    n = x.shape[0]
    r = pltpu.roll(x, k, 0)
    rp = pltpu.roll(prev8, k, 0)
    rows = lax.broadcasted_iota(jnp.int32, (8, x.shape[1]), 0)
    head = jnp.where(rows < k, rp, r[:8])
    return head if n == 8 else jnp.concatenate([head, r[8:]], axis=0)


def _shift_up(x, next8, k):
    if k == 0:
        return x
    n = x.shape[0]
    r = pltpu.roll(x, n - k, 0)
    rn = pltpu.roll(next8, 8 - k, 0)
    rows = lax.broadcasted_iota(jnp.int32, (8, x.shape[1]), 0)
    tail = jnp.where(rows >= 8 - k, rn, r[n - 8:])
    return tail if n == 8 else jnp.concatenate([r[:n - 8], tail], axis=0)


def _sigmoid(x):
    return 0.5 * jnp.tanh(0.5 * x) + 0.5


def _rows(ref, i):
    return ref[i:i + 1, :]


def _norm_parts(x, mp_ref, which):
    ln, sh, sc = _rows(mp_ref, 6 + which), _rows(mp_ref, 3 * which), _rows(mp_ref, 3 * which + 1)
    r = lax.rsqrt(jnp.mean(x * x, axis=1, keepdims=True) + EPS)
    return r, ln, sh, sc


def _norm_fwd(x, modp, which, name):
    t, d = x.shape

    def body(x_ref, mp_ref, u_ref):
        xv = x_ref[...]
        r, ln, sh, sc = _norm_parts(xv, mp_ref, which)
        u_ref[...] = (((xv * r) * ln) * (1.0 + sc) + sh).astype(BF16)

    row = pl.BlockSpec((ROW_TILE, d), lambda i: (i, 0))
    return pl.pallas_call(
        body, name=name, out_shape=jax.ShapeDtypeStruct((t, d), BF16), grid=(t // ROW_TILE,),
        in_specs=[row, pl.BlockSpec((8, d), lambda i: (0, 0))], out_specs=row,
        compiler_params=_params(("parallel",)))(x, modp)


def _resid_norm_fwd(x, br, mp_gate, gate_row, mp_norm, which, name):
    t, d = x.shape

    def body(x_ref, br_ref, mg_ref, mn_ref, xn_ref, u_ref):
        xv = x_ref[...] + _rows(mg_ref, gate_row) * br_ref[...]
        xn_ref[...] = xv
        r, ln, sh, sc = _norm_parts(xv, mn_ref, which)
        u_ref[...] = (((xv * r) * ln) * (1.0 + sc) + sh).astype(BF16)

    row = pl.BlockSpec((ROW_TILE, d), lambda i: (i, 0))
    mp = pl.BlockSpec((8, d), lambda i: (0, 0))
    return pl.pallas_call(
        body, name=name,
        out_shape=[jax.ShapeDtypeStruct((t, d), F32), jax.ShapeDtypeStruct((t, d), BF16)],
        grid=(t // ROW_TILE,), in_specs=[row, row, mp, mp], out_specs=[row, row],
        compiler_params=_params(("parallel",)))(x, br, mp_gate, mp_norm)


def _final_fwd_bwd(x, br, mp_gate, fnorm8, target, name):
    t, d = x.shape

    def body(x_ref, br_ref, mg_ref, fn_ref, tg_ref, dx_ref, dbr_ref, acc_ref):
        gate = _rows(mg_ref, 5)
        brv = br_ref[...]
        xv = x_ref[...] + gate * brv
        fn = _rows(fn_ref, 0)
        r = lax.rsqrt(jnp.mean(xv * xv, axis=1, keepdims=True) + EPS)
        nrm = xv * r
        err = nrm * fn - tg_ref[...]
        loss = 0.5 * jnp.sum(jnp.mean(err * err, axis=1, keepdims=True), axis=0, keepdims=True)
        dy = err * (1.0 / d)
        dn = dy * fn
        dx = r * (dn - nrm * jnp.mean(dn * nrm, axis=1, keepdims=True))
        dx_ref[...] = dx
        dbr_ref[...] = (dx * gate).astype(BF16)
        upd = jnp.concatenate([
            jnp.broadcast_to(loss, (1, d)),
            jnp.sum(dy * nrm, axis=0, keepdims=True),
            jnp.sum(dx * brv, axis=0, keepdims=True),
            jnp.zeros((5, d), F32)], axis=0)

        @pl.when(pl.program_id(0) == 0)
        def _():
            acc_ref[...] = upd

        @pl.when(pl.program_id(0) > 0)
        def _():
            acc_ref[...] += upd

    row = pl.BlockSpec((ROW_TILE, d), lambda i: (i, 0))
    mp = pl.BlockSpec((8, d), lambda i: (0, 0))
    return pl.pallas_call(
        body, name=name,
        out_shape=[jax.ShapeDtypeStruct((t, d), F32), jax.ShapeDtypeStruct((t, d), BF16),
                   jax.ShapeDtypeStruct((8, d), F32)],
        grid=(t // ROW_TILE,), in_specs=[row, row, mp, mp, row], out_specs=[row, row, mp],
        compiler_params=_params(("arbitrary",)))(x, br, mp_gate, fnorm8, target)


def _resid_norm_bwd(dx, du, x, mp_norm, which, name, br=None, mp_gate=None, gate_row=None):
    t, d = x.shape
    has_gate = br is not None

    def body(*refs):
        if has_gate:
            dx_ref, du_ref, x_ref, mn_ref, br_ref, mg_ref, dxn_ref, dbr_ref, acc_ref = refs
        else:
            dx_ref, du_ref, x_ref, mn_ref, dxn_ref, acc_ref = refs
        xv, duv = x_ref[...], du_ref[...]
        r, ln, _, sc = _norm_parts(xv, mn_ref, which)
        nrm = xv * r
        dn = duv * (ln * (1.0 + sc))
        dxn = dx_ref[...] + r * (dn - nrm * jnp.mean(dn * nrm, axis=1, keepdims=True))
        dxn_ref[...] = dxn
        rows = [jnp.sum(duv, axis=0, keepdims=True), jnp.sum(duv * nrm, axis=0, keepdims=True)]
        if has_gate:
            dbr_ref[...] = (dxn * _rows(mg_ref, gate_row)).astype(BF16)
            rows.append(jnp.sum(dxn * br_ref[...], axis=0, keepdims=True))
        upd = jnp.concatenate(rows + [jnp.zeros((8 - len(rows), d), F32)], axis=0)

        @pl.when(pl.program_id(0) == 0)
        def _():
            acc_ref[...] = upd

        @pl.when(pl.program_id(0) > 0)
        def _():
            acc_ref[...] += upd

    row = pl.BlockSpec((ROW_TILE, d), lambda i: (i, 0))
    mp = pl.BlockSpec((8, d), lambda i: (0, 0))
    ins, in_specs = [dx, du, x, mp_norm], [row, row, row, mp]
    outs = [jax.ShapeDtypeStruct((t, d), F32)]
    out_specs = [row]
    if has_gate:
        ins += [br, mp_gate]
        in_specs += [row, mp]
        outs.append(jax.ShapeDtypeStruct((t, d), BF16))
        out_specs.append(row)
    outs.append(jax.ShapeDtypeStruct((8, d), F32))
    out_specs.append(mp)
    return pl.pallas_call(
        body, name=name, out_shape=outs, grid=(t // ROW_TILE,), in_specs=in_specs, out_specs=out_specs,
        compiler_params=_params(("arbitrary",)))(*ins)


def _merge_fwd(proj, p_conv, p_ssm, name):
    t, d = p_conv.shape

    def body(gl_ref, pc_ref, ps_ref, o_ref):
        g = _sigmoid(gl_ref[...].astype(F32))
        o_ref[...] = (g[:, :d] * pc_ref[...] + g[:, d:] * ps_ref[...]).astype(BF16)

    row = pl.BlockSpec((ROW_TILE, d), lambda i: (i, 0))
    return pl.pallas_call(
        body, name=name, out_shape=jax.ShapeDtypeStruct((t, d), BF16), grid=(t // ROW_TILE,),
        in_specs=[pl.BlockSpec((ROW_TILE, 2 * d), lambda i: (i, 0)), row, row], out_specs=row,
        compiler_params=_params(("parallel",)))(proj, p_conv, p_ssm)


def _merge_bwd(dmerged, proj, p_conv, p_ssm, name):
    t, d = p_conv.shape
    pw = proj.shape[1]

    def body(dm_ref, gl_ref, pc_ref, ps_ref, dpc_ref, dps_ref, dgl_ref):
        g = _sigmoid(gl_ref[...].astype(F32))
        gc, gs = g[:, :d], g[:, d:]
        dm = dm_ref[...]
        dpc_ref[...] = (dm * gc).astype(BF16)
        dps_ref[...] = (dm * gs).astype(BF16)
        dgl_ref[...] = jnp.concatenate(
            [dm * pc_ref[...] * gc * (1.0 - gc), dm * ps_ref[...] * gs * (1.0 - gs)], axis=1).astype(BF16)

    row = pl.BlockSpec((ROW_TILE, d), lambda i: (i, 0))
    wide = pl.BlockSpec((ROW_TILE, 2 * d), lambda i: (i, 0))
    return pl.pallas_call(
        body, name=name,
        out_shape=[jax.ShapeDtypeStruct((t, d), BF16), jax.ShapeDtypeStruct((t, d), BF16),
                   jax.ShapeDtypeStruct((t, pw), BF16)],
        grid=(t // ROW_TILE,), in_specs=[row, wide, row, row], out_specs=[row, row, wide],
        compiler_params=_params(("parallel",)))(dmerged, proj, p_conv, p_ssm)


def _halo_specs(t, width, col):
    nb = t // 8
    step = ROW_TILE // 8
    prev = pl.BlockSpec((8, width), lambda i: (jnp.maximum(i * step - 1, 0), col))
    nxt = pl.BlockSpec((8, width), lambda i: (jnp.minimum((i + 1) * step, nb - 1), col))
    return prev, nxt


def _gconv_fwd(proj, conv_w8, d, name):
    t = proj.shape[0]

    def body(cb_ref, cc_ref, cx_ref, ccp_ref, cxp_ref, w_ref, o_ref):
        first = pl.program_id(0) == 0
        v = cc_ref[...].astype(F32) * cx_ref[...].astype(F32)
        vp = jnp.where(first, 0.0, ccp_ref[...].astype(F32) * cxp_ref[...].astype(F32))
        cv = sum(_rows(w_ref, k) * _shift_down(v, vp, 2 - k) for k in range(3))
        o_ref[...] = (cb_ref[...].astype(F32) * cv).astype(BF16)

    def win(col):
        return pl.BlockSpec((ROW_TILE, d), lambda i: (i, col))

    return pl.pallas_call(
        body, name=name, out_shape=jax.ShapeDtypeStruct((t, d), BF16), grid=(t // ROW_TILE,),
        in_specs=[win(2), win(3), win(4), _halo_specs(t, d, 3)[0], _halo_specs(t, d, 4)[0],
                  pl.BlockSpec((8, d), lambda i: (0, 0))],
        out_specs=pl.BlockSpec((ROW_TILE, d), lambda i: (i, 0)),
        compiler_params=_params(("parallel",)))(proj, proj, proj, proj, proj, conv_w8)


def _gconv_bwd(dy, proj, conv_w8, dproj, d, name):
    t = proj.shape[0]
    nt = t // ROW_TILE

    def body(dy_ref, dyn_ref, cb_ref, cc_ref, cx_ref, ccp_ref, cxp_ref, cbn_ref, w_ref, _, dp_ref, dw_ref, st_ref):
        i, j = pl.program_id(0), pl.program_id(1)

        @pl.when(j == 0)
        def _():
            cb, cc, cx = cb_ref[...].astype(F32), cc_ref[...].astype(F32), cx_ref[...].astype(F32)
            v = cc * cx
            vp = jnp.where(i == 0, 0.0, ccp_ref[...].astype(F32) * cxp_ref[...].astype(F32))
            sh = [_shift_down(v, vp, 2 - k) for k in range(3)]
            cv = sum(_rows(w_ref, k) * sh[k] for k in range(3))
            dyv = dy_ref[...]
            dcv = dyv * cb
            dcvn = jnp.where(i == nt - 1, 0.0, dyn_ref[...] * cbn_ref[...].astype(F32))
            dv = sum(_rows(w_ref, k) * _shift_up(dcv, dcvn, 2 - k) for k in range(3))
            st_ref[0] = (dyv * cv).astype(BF16)
            st_ref[1] = (dv * cx).astype(BF16)
            st_ref[2] = (dv * cc).astype(BF16)
            upd = jnp.concatenate([jnp.sum(dcv * sh[k], axis=0, keepdims=True) for k in range(3)]
                                  + [jnp.zeros((5, d), F32)], axis=0)

            @pl.when(i == 0)
            def _():
                dw_ref[...] = upd

            @pl.when(i > 0)
            def _():
                dw_ref[...] += upd

        dp_ref[...] = st_ref[j]

    def win(col):
        return pl.BlockSpec((ROW_TILE, d), lambda i, j: (i, col))

    def halo(col, which):
        nb, step = t // 8, ROW_TILE // 8
        if which == 0:
            return pl.BlockSpec((8, d), lambda i, j: (jnp.maximum(i * step - 1, 0), col))
        return pl.BlockSpec((8, d), lambda i, j: (jnp.minimum((i + 1) * step, nb - 1), col))

    row = pl.BlockSpec((ROW_TILE, d), lambda i, j: (i, 0))
    w8 = pl.BlockSpec((8, d), lambda i, j: (0, 0))
    return pl.pallas_call(
        body, name=name,
        out_shape=[jax.ShapeDtypeStruct(dproj.shape, BF16), jax.ShapeDtypeStruct((8, d), F32)],
        grid=(nt, 3),
        in_specs=[row, halo(0, 1), win(2), win(3), win(4), halo(3, 0), halo(4, 0), halo(2, 1), w8,
                  pl.BlockSpec(memory_space=pl.ANY)],
        out_specs=[pl.BlockSpec((ROW_TILE, d), lambda i, j: (i, 2 + j)), w8],
        scratch_shapes=[pltpu.VMEM((3, ROW_TILE, d), BF16)],
        input_output_aliases={9: 0},
        compiler_params=_params(("arbitrary", "arbitrary")))(
            dy, dy, proj, proj, proj, proj, proj, proj, conv_w8, dproj)


def _dsilu(p):
    s = _sigmoid(p)
    return s * (1.0 + p * (1.0 - s))


def _sconv_specs(t, d, col0, nt):
    nb, step = t // 8, ROW_TILE // 8
    win = pl.BlockSpec((ROW_TILE, d), lambda j, i: (i, col0 + j))
    prev = pl.BlockSpec((8, d), lambda j, i: (jnp.maximum(i * step - 1, 0), col0 + j))
    nxt = pl.BlockSpec((8, d), lambda j, i: (jnp.minimum((i + 1) * step, nb - 1), col0 + j))
    return win, prev, nxt


def _sconv_fwd(proj, w8, b8, d, col0, ncol, name):
    t = proj.shape[0]
    nt = t // ROW_TILE

    def body(x_ref, xp_ref, w_ref, b_ref, o_ref):
        xv = x_ref[...].astype(F32)
        xp = jnp.where(pl.program_id(1) == 0, 0.0, xp_ref[...].astype(F32))
        pre = _rows(b_ref, 0) + sum(_rows(w_ref, k) * _shift_down(xv, xp, 3 - k) for k in range(4))
        o_ref[...] = (pre * _sigmoid(pre)).astype(BF16)

    win, prev, _ = _sconv_specs(t, d, col0, nt)
    par = pl.BlockSpec((8, d), lambda j, i: (0, j))
    return pl.pallas_call(
        body, name=name, out_shape=jax.ShapeDtypeStruct((t, ncol * d), BF16), grid=(ncol, nt),
        in_specs=[win, prev, par, par], out_specs=pl.BlockSpec((ROW_TILE, d), lambda j, i: (i, j)),
        compiler_params=_params(("parallel", "parallel")))(proj, proj, w8, b8)


def _sconv_bwd(dxc, proj, w8, b8, dproj, d, col0, ncol, name):
    t = proj.shape[0]
    nt = t // ROW_TILE

    def body(d_ref, dn_ref, x_ref, xp_ref, xn_ref, w_ref, b_ref, _, dp_ref, dw_ref):
        i = pl.program_id(1)
        xv = x_ref[...].astype(F32)
        xp = jnp.where(i == 0, 0.0, xp_ref[...].astype(F32))
        bias = _rows(b_ref, 0)
        sh = [_shift_down(xv, xp, 3 - k) for k in range(4)]
        pre = bias + sum(_rows(w_ref, k) * sh[k] for k in range(4))
        dpre = d_ref[...] * _dsilu(pre)
        xn = xn_ref[...].astype(F32)
        pre_n = bias + sum(_rows(w_ref, k) * _shift_down(xn, xv[ROW_TILE - 8:], 3 - k) for k in range(4))
        dpre_n = jnp.where(i == nt - 1, 0.0, dn_ref[...] * _dsilu(pre_n))
        dx = sum(_rows(w_ref, k) * _shift_up(dpre, dpre_n, 3 - k) for k in range(4))
        dp_ref[...] = dx.astype(BF16)
        upd = jnp.concatenate([jnp.sum(dpre * sh[k], axis=0, keepdims=True) for k in range(4)]
                              + [jnp.sum(dpre, axis=0, keepdims=True), jnp.zeros((3, d), F32)], axis=0)

        @pl.when(i == 0)
        def _():
            dw_ref[...] = upd

        @pl.when(i > 0)
        def _():
            dw_ref[...] += upd

    win, prev, nxt = _sconv_specs(t, d, col0, nt)
    nb, step = t // 8, ROW_TILE // 8
    dwin = pl.BlockSpec((ROW_TILE, d), lambda j, i: (i, j))
    dnxt = pl.BlockSpec((8, d), lambda j, i: (jnp.minimum((i + 1) * step, nb - 1), j))
    par = pl.BlockSpec((8, d), lambda j, i: (0, j))
    return pl.pallas_call(
        body, name=name,
        out_shape=[jax.ShapeDtypeStruct(dproj.shape, BF16), jax.ShapeDtypeStruct((8, ncol * d), F32)],
        grid=(ncol, nt),
        in_specs=[dwin, dnxt, win, prev, nxt, par, par, pl.BlockSpec(memory_space=pl.ANY)],
        out_specs=[win, par], input_output_aliases={7: 0},
        compiler_params=_params(("arbitrary", "arbitrary")))(dxc, dxc, proj, proj, proj, w8, b8, dproj)


def _softplus(x):
    return jnp.maximum(x, 0.0) + jnp.log(1.0 + jnp.exp(-jnp.abs(x)))


def _dt_fwd(proj, par8, nheads, dt_col, name):
    t = proj.shape[0]

    def body(r_ref, p_ref, dt_ref, ac_ref):
        lane = lax.broadcasted_iota(jnp.int32, (1, LANE), 1)
        valid = lane < nheads
        raw = r_ref[...][:, :LANE].astype(F32)
        dt = jnp.where(valid, _softplus(raw + _rows(p_ref, 0)), 0.0)
        a = jnp.where(valid, -jnp.exp(_rows(p_ref, 1)), 0.0)
        dt_ref[...] = dt
        ac_ref[...] = _xdot01(_chunk_tri(ROW_TILE, True), dt * a)

    out = pl.BlockSpec((ROW_TILE, LANE), lambda i: (i, 0))
    return pl.pallas_call(
        body, name=name,
        out_shape=[jax.ShapeDtypeStruct((t, LANE), F32), jax.ShapeDtypeStruct((t, LANE), F32)],
        grid=(t // ROW_TILE,),
        in_specs=[pl.BlockSpec((ROW_TILE, 2 * LANE), lambda i: (i, dt_col)), pl.BlockSpec((8, LANE), lambda i: (0, 0))],
        out_specs=[out, out], compiler_params=_params(("parallel",)))(proj, par8)


def _dt_bwd(ddt_g, dac_g, dt, proj, par8, dproj, nheads, dt_col, name):
    t = proj.shape[0]
    ng = ddt_g.shape[0]

    def body(dd_ref, da_ref, dt_ref, r_ref, p_ref, _, dp_ref, acc_ref):
        lane = lax.broadcasted_iota(jnp.int32, (1, LANE), 1)
        valid = lane < nheads
        ddt = sum(dd_ref[g] for g in range(ng))
        dac = sum(da_ref[g] for g in range(ng))
        a = jnp.where(valid, -jnp.exp(_rows(p_ref, 1)), 0.0)
        d_a = _xdot01(_chunk_tri(ROW_TILE, False), dac)
        ddt = ddt + d_a * a
        raw = r_ref[...][:, :LANE].astype(F32)
        draw = jnp.where(valid, ddt * _sigmoid(raw + _rows(p_ref, 0)), 0.0)
        dp_ref[...] = jnp.concatenate([draw, jnp.zeros_like(draw)], axis=1).astype(BF16)
        upd = jnp.concatenate([jnp.sum(draw, axis=0, keepdims=True),
                               jnp.sum(d_a * dt_ref[...], axis=0, keepdims=True) * a,
                               jnp.zeros((6, LANE), F32)], axis=0)

        @pl.when(pl.program_id(0) == 0)
        def _():
            acc_ref[...] = upd

        @pl.when(pl.program_id(0) > 0)
        def _():
            acc_ref[...] += upd

    row = pl.BlockSpec((ROW_TILE, LANE), lambda i: (i, 0))
    grp = pl.BlockSpec((ng, ROW_TILE, LANE), lambda i: (0, i, 0))
    win = pl.BlockSpec((ROW_TILE, 2 * LANE), lambda i: (i, dt_col))
    par = pl.BlockSpec((8, LANE), lambda i: (0, 0))
    return pl.pallas_call(
        body, name=name,
        out_shape=[jax.ShapeDtypeStruct(dproj.shape, BF16), jax.ShapeDtypeStruct((8, LANE), F32)],
        grid=(t // ROW_TILE,),
        in_specs=[grp, grp, row, win, par, pl.BlockSpec(memory_space=pl.ANY)],
        out_specs=[win, par], input_output_aliases={5: 0},
        compiler_params=_params(("arbitrary",)))(ddt_g, dac_g, dt, proj, par8, dproj)


def _ssd_chunk(xs, bm, cm, z, dt, ac, st, dsk, nw, g):
    q = xs.shape[0]
    lane_h = lax.broadcasted_iota(jnp.int32, (1, LANE), 1)
    sub_h = lax.broadcasted_iota(jnp.int32, (LANE, 1), 0)
    lane_c = lax.broadcasted_iota(jnp.int32, (1, GROUP_W), 1) // HEAD_DIM
    rr = lax.broadcasted_iota(jnp.int32, (q, q), 0)
    cc = lax.broadcasted_iota(jnp.int32, (q, q), 1)
    tril = rr >= cc
    last_row = jnp.where(lax.broadcasted_iota(jnp.int32, (q, 1), 0) == q - 1, 1.0, 0.0)
    act = ac.T
    col, row = [], []
    for r in range(HEADS_PER_GROUP):
        h = g * HEADS_PER_GROUP + r
        col.append(jnp.sum(ac * jnp.where(lane_h == h, 1.0, 0.0), axis=1, keepdims=True))
        row.append(jnp.sum(act * jnp.where(sub_h == h, 1.0, 0.0), axis=0, keepdims=True))

    def per_head(vals):
        out = jnp.broadcast_to(vals[0], (q, GROUP_W))
        for r in range(1, HEADS_PER_GROUP):
            out = jnp.where(lane_c == r, vals[r], out)
        return out

    spread = jnp.where(lax.broadcasted_iota(jnp.int32, (LANE, GROUP_W), 0)
                       == g * HEADS_PER_GROUP + lax.broadcasted_iota(jnp.int32, (LANE, GROUP_W), 1) // HEAD_DIM,
                       1.0, 0.0).astype(BF16)
    acx = _spread(ac, spread)
    x = xs * _spread(dt, spread)
    s = _bdot(cm, bm, "nt")
    y = per_head([_bdot(s * jnp.exp(jnp.where(tril, col[r] - row[r], -jnp.inf)), x, "nn")
                  for r in range(HEADS_PER_GROUP)])
    y = y + jnp.exp(acx) * _bdot(cm, st, "nn")
    last = jnp.sum(acx * last_row, axis=0, keepdims=True)
    new_st = st * jnp.exp(last) + _bdot(bm, x * jnp.exp(last - acx), "tn")
    y = y + dsk * xs
    y = y * (z * _sigmoid(z))
    yn = y * lax.rsqrt(jnp.mean(y * y, axis=1, keepdims=True) + EPS) * nw
    return yn, new_st


def _ssd_specs(nc, z_col, ngroups, rev):
    assert ngroups % SSD_GROUPS == 0 and z_col % SSD_GROUPS == 0
    z_col //= SSD_GROUPS
    b_col, c_col = 2 * ngroups // SSD_GROUPS, 3 * ngroups // SSD_GROUPS

    def ci(c):
        return nc - 1 - c if rev else c

    nx, nb = SSD_GROUPS * GROUP_W, SSD_GROUPS * N_STATE
    xs = pl.BlockSpec((SSD_Q, nx), lambda g, c: (ci(c), g))
    bm = pl.BlockSpec((SSD_Q, nb), lambda g, c: (ci(c), b_col + g))
    cm = pl.BlockSpec((SSD_Q, nb), lambda g, c: (ci(c), c_col + g))
    z = pl.BlockSpec((SSD_Q, nx), lambda g, c: (ci(c), z_col + g))
    hd = pl.BlockSpec((SSD_Q, LANE), lambda g, c: (ci(c), 0))
    par = pl.BlockSpec((8, nx), lambda g, c: (0, g))
    stb = pl.BlockSpec((None, N_STATE, nx), lambda g, c: (ci(c), 0, g))
    db = pl.BlockSpec((SSD_Q, nb), lambda g, c: (ci(c), g))
    return xs, bm, cm, z, hd, par, stb, db


def _ssd_fwd(xc, proj, dt, ac, par8, ngroups, z_col, name):
    t = xc.shape[0]
    nc = t // SSD_Q
    steps = ngroups // SSD_GROUPS

    def body(x_ref, b_ref, c_ref, z_ref, dt_ref, ac_ref, p_ref, y_ref, so_ref, st_ref):
        @pl.when(pl.program_id(1) == 0)
        def _():
            st_ref[...] = jnp.zeros_like(st_ref)

        so_ref[...] = st_ref[...]
        dtv, acv = dt_ref[...], ac_ref[...]
        for i in range(SSD_GROUPS):
            cx, cb = slice(i * GROUP_W, (i + 1) * GROUP_W), slice(i * N_STATE, (i + 1) * N_STATE)
            yn, new_st = _ssd_chunk(x_ref[:, cx].astype(F32), b_ref[:, cb].astype(F32), c_ref[:, cb].astype(F32),
                                    z_ref[:, cx].astype(F32), dtv, acv, st_ref[:, cx],
                                    p_ref[0:1, cx], p_ref[1:2, cx], pl.program_id(0) * SSD_GROUPS + i)
            y_ref[:, cx] = yn.astype(BF16)
            st_ref[:, cx] = new_st

    xs, bm, cm, zs, hd, par, stb, _ = _ssd_specs(nc, z_col, ngroups, False)
    return pl.pallas_call(
        body, name=name,
        out_shape=[jax.ShapeDtypeStruct((t, ngroups * GROUP_W), BF16),
                   jax.ShapeDtypeStruct((nc, N_STATE, ngroups * GROUP_W), F32)],
        grid=(steps, nc), in_specs=[xs, bm, cm, zs, hd, hd, par], out_specs=[xs, stb],
        scratch_shapes=[pltpu.VMEM((N_STATE, SSD_GROUPS * GROUP_W), F32)],
        compiler_params=_params(("parallel", "arbitrary")))(xc, xc, xc, proj, dt, ac, par8)


def _ssd_bwd(dyn, xc, proj, dt, ac, par8, states, dproj, ngroups, z_col, name):
    t = xc.shape[0]
    nc = t // SSD_Q
    steps = ngroups // SSD_GROUPS

    def body(dy_ref, x_ref, b_ref, c_ref, z_ref, dt_ref, ac_ref, p_ref, s_ref, _,
             dz_ref, dx_ref, db_ref, dc_ref, ddt_ref, dac_ref, dp_ref, ds_ref):
        c = pl.program_id(1)

        @pl.when(c == 0)
        def _():
            ds_ref[...] = jnp.zeros_like(ds_ref)

        dtv, acv = dt_ref[...], ac_ref[...]
        ddt_sum, dac_sum, upds = None, None, []
        for i in range(SSD_GROUPS):
            cx, cb = slice(i * GROUP_W, (i + 1) * GROUP_W), slice(i * N_STATE, (i + 1) * N_STATE)
            g = pl.program_id(0) * SSD_GROUPS + i
            args = (x_ref[:, cx].astype(F32), b_ref[:, cb].astype(F32), c_ref[:, cb].astype(F32),
                    z_ref[:, cx].astype(F32), dtv, acv, s_ref[:, cx], p_ref[0:1, cx], p_ref[1:2, cx])
            _, vjp = jax.vjp(lambda *a, g=g: _ssd_chunk(*a, g), *args)
            dxs, dbm, dcm, dz, ddt, dac, dst, ddsk, dnw = vjp((dy_ref[:, cx], ds_ref[:, cx]))
            ds_ref[:, cx] = dst
            dz_ref[:, cx] = dz.astype(BF16)
            dx_ref[:, cx] = dxs
            db_ref[:, cb] = dbm
            dc_ref[:, cb] = dcm
            ddt_sum = ddt if ddt_sum is None else ddt_sum + ddt
            dac_sum = dac if dac_sum is None else dac_sum + dac
            upds.append(jnp.concatenate([ddsk, dnw, jnp.zeros((6, GROUP_W), F32)], axis=0))
        ddt_ref[...] = ddt_sum
        dac_ref[...] = dac_sum
        upd = jnp.concatenate(upds, axis=1)

        @pl.when(c == 0)
        def _():
            dp_ref[...] = upd

        @pl.when(c > 0)
        def _():
            dp_ref[...] += upd

    xs, bm, cm, zs, hd, par, stb, db = _ssd_specs(nc, z_col, ngroups, True)
    hg = pl.BlockSpec((None, SSD_Q, LANE), lambda g, c: (g, nc - 1 - c, 0))
    return pl.pallas_call(
        body, name=name,
        out_shape=[jax.ShapeDtypeStruct(dproj.shape, BF16),
                   jax.ShapeDtypeStruct((t, ngroups * GROUP_W), F32),
                   jax.ShapeDtypeStruct((t, ngroups * N_STATE), F32), jax.ShapeDtypeStruct((t, ngroups * N_STATE), F32),
                   jax.ShapeDtypeStruct((steps, t, LANE), F32), jax.ShapeDtypeStruct((steps, t, LANE), F32),
                   jax.ShapeDtypeStruct((8, ngroups * GROUP_W), F32)],
        grid=(steps, nc),
        in_specs=[xs, xs, bm, cm, zs, hd, hd, par, stb, pl.BlockSpec(memory_space=pl.ANY)],
        out_specs=[zs, xs, db, db, hg, hg, par], input_output_aliases={9: 0},
        scratch_shapes=[pltpu.VMEM((N_STATE, SSD_GROUPS * GROUP_W), F32)],
        compiler_params=_params(("arbitrary", "arbitrary")))(dyn, xc, xc, xc, proj, dt, ac, par8, states, dproj)


def _small_matmul(a, b, mode, name):
    if mode == "nn":
        shape = (a.shape[0], b.shape[1])
    else:
        shape = (a.shape[1], b.shape[1])

    def body(a_ref, b_ref, o_ref):
        o_ref[...] = _dot(a_ref[...], b_ref[...], mode)

    return pl.pallas_call(body, name=name, out_shape=jax.ShapeDtypeStruct(shape, F32),
                          compiler_params=_params())(a, b)


def _adamw(parts, w, m, v, name):
    npart, rows, cols = parts.shape
    tr = rows if rows <= 128 else _tile8(rows, 128)

    def body(p_ref, w_ref, m_ref, v_ref, g_ref, d_ref, nm_ref, nv_ref):
        g = p_ref[0].astype(F32)
        for j in range(1, npart):
            g = g + p_ref[j].astype(F32)
        mm = ADAM_B1 * m_ref[...] + (1.0 - ADAM_B1) * g
        vv = ADAM_B2 * v_ref[...] + (1.0 - ADAM_B2) * (g * g)
        m_hat = mm / (1.0 - ADAM_B1 ** ADAM_STEP)
        v_hat = vv / (1.0 - ADAM_B2 ** ADAM_STEP)
        g_ref[...] = g
        d_ref[...] = -ADAM_LR * (m_hat / (jnp.sqrt(v_hat) + ADAM_EPS) + ADAM_WD * w_ref[...])
        nm_ref[...] = mm
        nv_ref[...] = vv

    blk = pl.BlockSpec((tr, cols), lambda i: (i, 0))
    out = jax.ShapeDtypeStruct((rows, cols), F32)
    return pl.pallas_call(
        body, name=name, out_shape=[out] * 4, grid=(rows // tr,),
        in_specs=[pl.BlockSpec((npart, tr, cols), lambda i: (0, i, 0)), blk, blk, blk], out_specs=[blk] * 4,
        compiler_params=_params(("parallel",)))(parts, w, m, v)


def _tile8(n, target):
    best = 8
    for d in range(8, target + 1, 8):
        if n % d == 0:
            best = d
    return best


def _pad_rows8(a):
    return jnp.concatenate([a, jnp.zeros((8 - a.shape[0], a.shape[1]), a.dtype)], axis=0)


def _pad_lanes(a, n):
    return jnp.concatenate([a, jnp.zeros(a.shape[:-1] + (n - a.shape[-1],), a.dtype)], axis=-1)


def _pack(vecs, mult):
    flat = jnp.concatenate([v.reshape(-1) for v in vecs])
    pad = (-flat.shape[0]) % mult
    flat = jnp.concatenate([flat, jnp.zeros((pad,), flat.dtype)])
    return flat.reshape(-1, LANE)


def _unpack(flat, shapes):
    out, o = [], 0
    for s in shapes:
        n = int(np.prod(s))
        out.append(flat[o:o + n].reshape(s))
        o += n
    return out


def kernel(x, c, w_ada, b_ada, ln1, ln2, w_in, conv_w, ssm_conv_w, ssm_conv_b, dt_bias, a_log, d_skip, ssm_norm_w, w_conv_out, w_ssm_out, w_o, w_up, w_down, final_norm, loss_target, m_w_ada, m_b_ada, m_ln1, m_ln2, m_w_in, m_conv_w, m_ssm_conv_w, m_ssm_conv_b, m_dt_bias, m_a_log, m_d_skip, m_ssm_norm_w, m_w_conv_out, m_w_ssm_out, m_w_o, m_w_up, m_w_down, m_final_norm, v_w_ada, v_b_ada, v_ln1, v_ln2, v_w_in, v_conv_w, v_ssm_conv_w, v_ssm_conv_b, v_dt_bias, v_a_log, v_d_skip, v_ssm_norm_w, v_w_conv_out, v_w_ssm_out, v_w_o, v_w_up, v_w_down, v_final_norm):
    names = ["w_ada", "b_ada", "ln1", "ln2", "w_in", "conv_w", "ssm_conv_w", "ssm_conv_b", "dt_bias", "a_log",
             "d_skip", "ssm_norm_w", "w_conv_out", "w_ssm_out", "w_o", "w_up", "w_down", "final_norm"]
    w_of = dict(zip(names, [w_ada, b_ada, ln1, ln2, w_in, conv_w, ssm_conv_w, ssm_conv_b, dt_bias, a_log, d_skip,
                            ssm_norm_w, w_conv_out, w_ssm_out, w_o, w_up, w_down, final_norm]))
    m_of = dict(zip(names, [m_w_ada, m_b_ada, m_ln1, m_ln2, m_w_in, m_conv_w, m_ssm_conv_w, m_ssm_conv_b, m_dt_bias,
                            m_a_log, m_d_skip, m_ssm_norm_w, m_w_conv_out, m_w_ssm_out, m_w_o, m_w_up, m_w_down,
                            m_final_norm]))
    v_of = dict(zip(names, [v_w_ada, v_b_ada, v_ln1, v_ln2, v_w_in, v_conv_w, v_ssm_conv_w, v_ssm_conv_b, v_dt_bias,
                            v_a_log, v_d_skip, v_ssm_norm_w, v_w_conv_out, v_w_ssm_out, v_w_o, v_w_up, v_w_down,
                            v_final_norm]))

    _, t, d = x.shape
    nl = w_ada.shape[0]
    ada_w = w_ada.shape[2]
    ds = ssm_norm_w.shape[1]
    nh = dt_bias.shape[1]
    ng = nh // HEADS_PER_GROUP
    gn = ng * N_STATE
    xbc_w = ds + 2 * gn
    z_off, xbc_off = 5 * d, 5 * d + ds
    dt_off = xbc_off + xbc_w
    proj_w = dt_off + nh
    pw = dt_off + 2 * LANE
    me = 4 * lax.axis_index("x") + 2 * lax.axis_index("y") + lax.axis_index("c")

    x2, tgt = x[0], loss_target[0]

    c_act = c * jax.nn.sigmoid(c)
    sizes1 = [(1, d), conv_w.shape, ssm_conv_w.shape]
    (g1,) = _all_gather([_pack([c_act, conv_w, ssm_conv_w], LANE)], "ag_small_in")
    parts1 = [_unpack(g1[j].reshape(-1), sizes1) for j in range(N_DEV)]
    c_act_all = jnp.concatenate([p[0] for p in parts1], axis=0)
    conv_w_full = jnp.concatenate([p[1] for p in parts1], axis=-1)
    sconv_w_full = jnp.concatenate([p[2] for p in parts1], axis=-1)

    mod_part = jnp.stack([_small_matmul(c_act_all, w_ada[l], "nn", f"ada_fwd{l}") for l in range(nl)])
    (gmod,) = _all_gather([mod_part], "ag_mod")
    mod = lax.dynamic_index_in_dim(gmod, me, axis=2, keepdims=False)
    mod = jnp.moveaxis(mod, 0, 1).reshape(nl, N_DEV * ada_w) + b_ada
    mod = mod.reshape(nl, 6, d)
    modp = [jnp.concatenate([mod[l], ln1[l][None], ln2[l][None]], axis=0) for l in range(nl)]

    big = ["w_in", "w_up", "w_conv_out", "w_ssm_out", "w_o", "w_down"]
    rest = big[1:]

    def blocks_of(l, keys):
        return [w_of[k][l].astype(BF16) for k in keys]

    def w_in_halves(l):
        w = w_of["w_in"][l].astype(BF16)
        return [w[:d // 2]], [w[d // 2:]]

    def full_cols(g):
        return jnp.moveaxis(g, 0, 1).reshape(g.shape[1], N_DEV * g.shape[2])

    def full_rows(g):
        return g.reshape(N_DEV * g.shape[1], g.shape[2])

    def pad_w_in(pieces):
        w = jnp.concatenate([full_cols(g) for g in pieces], axis=0)
        return jnp.concatenate([w, jnp.zeros((d, pw - proj_w), BF16)], axis=-1)

    wi = [None] * nl
    wi[0] = pad_w_in(_all_gather(blocks_of(0, ["w_in"]), "ag_w_in0"))

    cw8 = [_pad_rows8(conv_w_full[l]) for l in range(nl)]
    sw8 = [_pad_rows8(sconv_w_full[l]) for l in range(nl)]
    sb8 = [_pad_rows8(ssm_conv_b[l][None]) for l in range(nl)]
    dtp8 = [_pad_rows8(_pad_lanes(jnp.stack([dt_bias[l], a_log[l]]), LANE)) for l in range(nl)]
    sp8 = [_pad_rows8(jnp.stack([jnp.repeat(d_skip[l], HEAD_DIM), ssm_norm_w[l]])) for l in range(nl)]
    fn8 = _pad_rows8(final_norm[None])
    xbc_col, ncol, dt_col, z_col = xbc_off // d, xbc_w // d, dt_off // (2 * LANE), z_off // GROUP_W

    saved = []
    x_cur, br_prev = x2, None
    for l in range(nl):
        if l == 0:
            x_in, u1 = x_cur, _norm_fwd(x_cur, modp[0], 0, "norm_first")
        else:
            x_in, u1 = _resid_norm_fwd(x_cur, br_prev, modp[l - 1], 5, modp[l], 0, f"resid_norm_a{l}")
        proj, got = _matmul(u1, wi[l], "nn", BF16, f"mm_in{l}", comm=_GatherComm(blocks_of(l, rest)))
        got = dict(zip(rest, got))
        wup, wdown = full_cols(got["w_up"]), full_rows(got["w_down"])
        wco, wso, wo = full_rows(got["w_conv_out"]), full_rows(got["w_ssm_out"]), full_rows(got["w_o"])
        y_conv = _gconv_fwd(proj, cw8[l], d, f"gconv_fwd{l}")
        xc = _sconv_fwd(proj, sw8[l], sb8[l], d, xbc_col, ncol, f"sconv_fwd{l}")
        dt, ac = _dt_fwd(proj, dtp8[l], nh, dt_col, f"dt_fwd{l}")
        yn, states = _ssd_fwd(xc, proj, dt, ac, sp8[l], ng, z_col, f"ssd_fwd{l}")
        p_conv = _matmul(y_conv, wco, "nn", F32, f"mm_conv_out{l}")
        p_ssm = _matmul(yn, wso, "nn", F32, f"mm_ssm_out{l}")
        merged = _merge_fwd(proj, p_conv, p_ssm, f"merge_fwd{l}")
        mix = _matmul(merged, wo, "nn", F32, f"mm_o{l}")
        x_mid, u2 = _resid_norm_fwd(x_in, mix, modp[l], 2, modp[l], 1, f"resid_norm_b{l}")
        if l + 1 < nl:
            top, bot = w_in_halves(l + 1)
            (h, hid), g_top = _matmul(u2, wup, "nn", BF16, f"mm_up{l}", epi="relu2", comm=_GatherComm(top))
            mlp, g_bot = _matmul(hid, wdown, "nn", F32, f"mm_down{l}", comm=_GatherComm(bot))
            wi[l + 1] = pad_w_in(g_top + g_bot)
        else:
            h, hid = _matmul(u2, wup, "nn", BF16, f"mm_up{l}", epi="relu2")
            mlp = _matmul(hid, wdown, "nn", F32, f"mm_down{l}")
        saved.append(dict(x_in=x_in, u1=u1, proj=proj, y_conv=y_conv, xc=xc, dt=dt, ac=ac, yn=yn, states=states,
                          p_conv=p_conv, p_ssm=p_ssm, merged=merged, mix=mix, x_mid=x_mid, u2=u2, h=h, hid=hid,
                          mlp=mlp, wup=wup, wdown=wdown, wco=wco, wso=wso, wo=wo))
        x_cur, br_prev = x_mid, mlp

    dx, dbr, acc = _final_fwd_bwd(x_cur, br_prev, modp[nl - 1], fn8, tgt, "final")
    loss = lax.psum(acc[0, 0], ("x", "y", "c"))
    g_final_norm = acc[1]
    dgate2 = acc[2]

    gw = {k: [None] * nl for k in big}
    small = {k: [None] * nl for k in ["mod", "ln1", "ln2", "conv_w", "ssm_conv_w", "ssm_conv_b", "dt_bias",
                                      "a_log", "d_skip", "ssm_norm_w"]}

    def grad_cols(a):
        return jnp.moveaxis(a.reshape(a.shape[0], N_DEV, a.shape[1] // N_DEV), 1, 0)

    def grad_rows(a):
        return a.reshape(N_DEV, a.shape[0] // N_DEV, a.shape[1])

    def recv_buf(name):
        return jnp.zeros((N_DEV, nl) + w_of[name].shape[1:], BF16)

    bufs_a, bufs_b = [recv_buf("w_in")], [recv_buf(k) for k in big[1:]]
    pending = None
    for l in reversed(range(nl)):
        s = saved[l]
        dh = _matmul(dbr, s["wdown"], "nt", BF16, f"mm_down_dx{l}", epi="relu2_bwd", extra=s["h"])
        gw["w_down"][l] = _matmul(s["hid"], dbr, "tn", BF16, f"mm_down_dw{l}")
        du2 = _matmul(dh, s["wup"], "nt", F32, f"mm_up_dx{l}")
        gw["w_up"][l] = _matmul(s["u2"], dh, "tn", BF16, f"mm_up_dw{l}")
        dx_mid, dmix, acc2 = _resid_norm_bwd(dx, du2, s["x_mid"], modp[l], 1, f"resid_norm_b_bwd{l}",
                                             br=s["mix"], mp_gate=modp[l], gate_row=2)
        dmerged = _matmul(dmix, s["wo"], "nt", F32, f"mm_o_dx{l}")
        gw["w_o"][l] = _matmul(s["merged"], dmix, "tn", BF16, f"mm_o_dw{l}")
        dpc, dps, dproj = _merge_bwd(dmerged, s["proj"], s["p_conv"], s["p_ssm"], f"merge_bwd{l}")
        dyc = _matmul(dpc, s["wco"], "nt", F32, f"mm_conv_out_dx{l}")
        gw["w_conv_out"][l] = _matmul(s["y_conv"], dpc, "tn", BF16, f"mm_conv_out_dw{l}")
        dyn = _matmul(dps, s["wso"], "nt", F32, f"mm_ssm_out_dx{l}")
        gw["w_ssm_out"][l] = _matmul(s["yn"], dps, "tn", BF16, f"mm_ssm_out_dw{l}")
        dproj, dxs, dbm, dcm, ddt_g, dac_g, dsp = _ssd_bwd(dyn, s["xc"], s["proj"], s["dt"], s["ac"], sp8[l],
                                                           s["states"], dproj, ng, z_col, f"ssd_bwd{l}")
        dproj, dcw = _gconv_bwd(dyc, s["proj"], cw8[l], dproj, d, f"gconv_bwd{l}")
        dsw = []
        for seg, (dseg, c0) in enumerate([(dxs, 0), (dbm, ds // d), (dcm, ds // d + gn // d)]):
            cols = slice(c0 * d, c0 * d + dseg.shape[1])
            dproj, part = _sconv_bwd(dseg, s["proj"], sw8[l][:, cols], sb8[l][:, cols], dproj, d, xbc_col + c0,
                                     dseg.shape[1] // d, f"sconv_bwd{l}_{seg}")
            dsw.append(part)
        dsw = jnp.concatenate(dsw, axis=1)
        dproj, ddtp = _dt_bwd(ddt_g, dac_g, s["dt"], s["proj"], dtp8[l], dproj, nh, dt_col, f"dt_bwd{l}")
        rest_grads = [grad_cols(gw["w_up"][l])] + [grad_rows(gw[k][l]) for k in big[2:]]
        du1, bufs_b = _matmul(dproj, wi[l], "nt", F32, f"mm_in_dx{l}", comm=_ExchangeComm(rest_grads, bufs_b, l))

        def mm_in_dw(u1, name, comm):
            if comm is None:
                return _matmul(u1, dproj, "tn", BF16, name), None
            return _matmul(u1, dproj, "tn", BF16, name, comm=comm)

        def w_in_blocks(g):
            return grad_cols(g[:, :proj_w])

        prev = None if pending is None else _ExchangeComm(pending, bufs_a, l + 1)
        if l > 0:
            g_in, got_bufs = mm_in_dw(s["u1"], f"mm_in_dw{l}", prev)
            bufs_a = bufs_a if got_bufs is None else got_bufs
            pending = [w_in_blocks(g_in)]
        else:
            g_top, got_bufs = mm_in_dw(s["u1"][:, :d // 2], "mm_in_dw0_top", prev)
            bufs_a = bufs_a if got_bufs is None else got_bufs
            g_bot, bufs_a = mm_in_dw(s["u1"][:, d // 2:], "mm_in_dw0_bot",
                                     _ExchangeComm([w_in_blocks(g_top)], bufs_a, 0, row0=0))
            pending = [w_in_blocks(g_bot)]
        if l > 0:
            dx, dbr, acc1 = _resid_norm_bwd(dx_mid, du1, s["x_in"], modp[l], 0, f"resid_norm_a_bwd{l}",
                                            br=saved[l - 1]["mlp"], mp_gate=modp[l - 1], gate_row=5)
        else:
            dx, acc1 = _resid_norm_bwd(dx_mid, du1, s["x_in"], modp[0], 0, "norm_first_bwd")
        sc1, sc2 = mod[l, 1], mod[l, 4]
        small["mod"][l] = jnp.stack([acc1[0], acc1[1] * ln1[l], acc2[2], acc2[0], acc2[1] * ln2[l], dgate2])
        small["ln1"][l] = acc1[1] * (1.0 + sc1)
        small["ln2"][l] = acc2[1] * (1.0 + sc2)
        small["conv_w"][l] = dcw[:3]
        small["ssm_conv_w"][l] = dsw[:4]
        small["ssm_conv_b"][l] = dsw[4]
        small["dt_bias"][l] = ddtp[0, :nh]
        small["a_log"][l] = ddtp[1, :nh]
        small["d_skip"][l] = dsp[0].reshape(nh, HEAD_DIM).sum(axis=-1)
        small["ssm_norm_w"][l] = dsp[1]
        if l > 0:
            dgate2 = acc1[2]
    grad_x = dx[None]

    sm = {k: jnp.stack(v) for k, v in small.items()}
    rep_names = ["b_ada", "ln1", "ln2", "ssm_conv_b", "dt_bias", "a_log", "d_skip", "ssm_norm_w", "final_norm"]
    rep_grads = [sm["mod"].reshape(nl, 6 * d), sm["ln1"], sm["ln2"], sm["ssm_conv_b"], sm["dt_bias"], sm["a_log"],
                 sm["d_skip"], sm["ssm_norm_w"], g_final_norm]
    rep_pack = _pack(rep_grads, 8 * LANE)
    conv_pack = _pack([sm["conv_w"], sm["ssm_conv_w"]], 8 * LANE)
    g_rep, g_conv = _all_gather([rep_pack, conv_pack], "ag_small_grads")

    outs = {}

    def run_adamw(name, parts, tag):
        w, m, v = w_of[name], m_of[name], v_of[name]
        shp = w.shape
        r2 = (int(np.prod(shp[:-1])), shp[-1])
        res = _adamw(parts.reshape((parts.shape[0],) + r2), w.reshape(r2), m.reshape(r2), v.reshape(r2), tag)
        outs[name] = [a.reshape(shp) for a in res]

    rep_shapes = [w_of[k].shape for k in rep_names]
    res = _adamw(g_rep, _pack([w_of[k] for k in rep_names], 8 * LANE), _pack([m_of[k] for k in rep_names], 8 * LANE),
                 _pack([v_of[k] for k in rep_names], 8 * LANE), "adamw_replicated")
    for k, vals in zip(rep_names, zip(*[_unpack(a.reshape(-1), rep_shapes) for a in res])):
        outs[k] = list(vals)

    conv_parts = [_unpack(g_conv[j].reshape(-1), [sm["conv_w"].shape, sm["ssm_conv_w"].shape]) for j in range(N_DEV)]
    for idx, name in enumerate(["conv_w", "ssm_conv_w"]):
        wsh = w_of[name].shape[-1]
        full = jnp.stack([p[idx] for p in conv_parts])
        run_adamw(name, lax.dynamic_slice_in_dim(full, me * wsh, wsh, axis=3), "adamw_" + name)

    dmod_all = g_rep.reshape(N_DEV, -1)[:, :nl * 6 * d].reshape(N_DEV, nl, 6 * d)
    dmod_mine = lax.dynamic_slice_in_dim(dmod_all, me * ada_w, ada_w, axis=2)
    g_ada = jnp.stack([_small_matmul(c_act_all, dmod_mine[:, l], "tn", f"ada_bwd{l}") for l in range(nl)])
    run_adamw("w_ada", g_ada[None], "adamw_w_ada")

    bufs_a = _exchange(pending, bufs_a, 0, "a2a_w_in0_bot", row0=d // 2)
    for name, parts in zip(big, list(bufs_a) + list(bufs_b)):
        run_adamw(name, parts, "adamw_" + name)

    result = [loss, grad_x]
    for i in range(4):
        result += [outs[k][i] for k in names]
    return tuple(result)
```

```python
import functools

import numpy as np
import jax
import jax.numpy as jnp
from jax import lax
from jax.experimental import pallas as pl
from jax.experimental.pallas import tpu as pltpu

F32 = jnp.float32
BF16 = jnp.bfloat16
EPS = 1e-6
N_STATE = 128
HEAD_DIM = 64
HEADS_PER_GROUP = 4
GROUP_W = HEAD_DIM * HEADS_PER_GROUP
SSD_GROUPS = 2
SSD_Q = 256
N_DEV = 8
ROW_TILE = 256
LANE = 128
VMEM_LIMIT = 56 * 1024 * 1024
MATMUL_VMEM_BUDGET = 40 * 1024 * 1024
MATMUL_TILE_CAP = 2048
MXU_FLOPS = 9.0e14
HBM_BYTES_PER_S = 3.0e12
STEP_OVERHEAD_S = 0.35e-6

ADAM_LR, ADAM_B1, ADAM_B2, ADAM_EPS, ADAM_WD, ADAM_STEP = 0.001, 0.9, 0.999, 1e-08, 0.01, 10

MESH = pl.DeviceIdType.MESH


def _params(sem=None):
    return pltpu.CompilerParams(dimension_semantics=sem, vmem_limit_bytes=VMEM_LIMIT)


def _divisors(n, cap):
    if n <= cap:
        return [n]
    return [d for d in range(cap - cap % LANE, 0, -LANE) if n % d == 0]


def _matmul_tiles(m, n, k, out_bytes, n_out, n_extra):
    best = None
    for tm in _divisors(m, MATMUL_TILE_CAP):
        for tn in _divisors(n, MATMUL_TILE_CAP):
            for tk in _divisors(k, MATMUL_TILE_CAP):
                out_tile = tm * tn * (out_bytes * n_out + 2 * n_extra)
                vmem = 2 * 2 * (tm * tk + tk * tn) + 2 * out_tile + (4 * tm * tn if tk < k else 0)
                if vmem > MATMUL_VMEM_BUDGET:
                    continue
                steps = (m // tm) * (n // tn) * (k // tk)
                a_reads = 1 if tk == k else n // tn
                traffic = 2 * (a_reads * m * k + (m // tm) * k * n) + m * n * (out_bytes * n_out + 2 * n_extra)
                est = (max(2.0 * m * n * k / MXU_FLOPS, traffic / HBM_BYTES_PER_S) + steps * STEP_OVERHEAD_S
                       + (2 * (tm * tk + tk * tn) + out_tile) / HBM_BYTES_PER_S)
                cand = (est, tm, tn, tk)
                if best is None or cand < best:
                    best = cand
    assert best is not None, (m, n, k)
    return best[1:]


def _tile(n, target):
    if n <= target:
        return n
    best = None
    for d in range(LANE, target + 1, LANE):
        if n % d == 0:
            best = d
    assert best is not None, (n, target)
    return best


def _slot(p):
    return 4 * p[0] + 2 * p[1] + p[2]


def _all_gather(xs, name):
    comm = _GatherComm(xs)

    def body(*refs):
        begin, end = comm.ops(refs[:comm.n_in], refs[comm.n_in:comm.n_in + comm.n_out], refs[-3:])
        begin()
        end()

    any_spec = pl.BlockSpec(memory_space=pl.ANY)
    return pl.pallas_call(
        body, name=name, out_shape=comm.out_shape,
        in_specs=[any_spec] * comm.n_in, out_specs=[any_spec] * comm.n_out, scratch_shapes=comm.sems,
    )(*comm.operands)


class _GatherComm:
    def __init__(self, xs):
        n = len(xs)
        self.operands = list(xs)
        self.n_in = self.n_out = n
        self.aliases = {}
        self.out_shape = [jax.ShapeDtypeStruct((N_DEV,) + x.shape, x.dtype) for x in xs]
        self.sems = [pltpu.SemaphoreType.DMA((7 * n,)), pltpu.SemaphoreType.DMA((7 * n,)),
                     pltpu.SemaphoreType.DMA((n,))]

    def ops(self, x_refs, o_refs, sem_refs):
        n = self.n_in
        send, recv, loc = sem_refs
        x, y, c = lax.axis_index("x"), lax.axis_index("y"), lax.axis_index("c")
        me, sib = (x, y, c), (x, y, 1 - c)
        chips = [(1 - x, y), (x, 1 - y), (1 - x, 1 - y)]

        def cp(a, k, block, to, src=None):
            dst = o_refs[a].at[_slot(block)]
            return pltpu.make_async_remote_copy(
                src_ref=dst if src is None else src, dst_ref=dst,
                send_sem=send.at[7 * a + k], recv_sem=recv.at[7 * a + k],
                device_id=to, device_id_type=MESH)

        mine = [pltpu.make_async_copy(x_refs[a], o_refs[a].at[_slot(me)], loc.at[a]) for a in range(n)]
        first = []
        for a in range(n):
            first.append(cp(a, 0, me, sib, src=x_refs[a]))
            first += [cp(a, 1 + j, me, (*chip, c), src=x_refs[a]) for j, chip in enumerate(chips)]

        def begin():
            for m in mine:
                m.start()
            for f in first:
                f.start()

        def end():
            passed = []
            for j, chip in enumerate(chips):
                for a in range(n):
                    cp(a, 1 + j, (*chip, c), me).wait_recv()
                    p = cp(a, 4 + j, (*chip, c), sib)
                    p.start()
                    passed.append(p)
            for a in range(n):
                cp(a, 0, sib, me).wait_recv()
                for j, chip in enumerate(chips):
                    cp(a, 4 + j, (*chip, 1 - c), me).wait_recv()
            for f in first + passed:
                f.wait_send()
            for m in mine:
                m.wait()

        return begin, end


class _ExchangeComm:
    def __init__(self, xs, bufs, layer, row0=0):
        n = len(xs)
        self.layers = list(layer) if isinstance(layer, (list, tuple)) else [layer] * n
        self.row0 = row0
        self.nrows = [x.shape[1] for x in xs]
        self.operands = list(xs) + list(bufs)
        self.n_in, self.n_out = 2 * n, n
        self.aliases = {n + a: a for a in range(n)}
        self.out_shape = [jax.ShapeDtypeStruct(b.shape, b.dtype) for b in bufs]
        self.sems = [pltpu.SemaphoreType.DMA((7 * n,)), pltpu.SemaphoreType.DMA((7 * n,)),
                     pltpu.SemaphoreType.DMA((n,))]

    def ops(self, in_refs, o_refs, sem_refs):
        n = self.n_out
        x_refs = in_refs[:n]
        send, recv, loc = sem_refs
        x, y, c = lax.axis_index("x"), lax.axis_index("y"), lax.axis_index("c")
        me = (x, y, c)
        peers = []
        for k in range(1, 8):
            kx, ky, kc = (k >> 2) & 1, (k >> 1) & 1, k & 1
            peers.append((x + kx - 2 * x * kx, y + ky - 2 * y * ky, c + kc - 2 * c * kc))

        def land(a, slot):
            return o_refs[a].at[slot, self.layers[a], pl.ds(self.row0, self.nrows[a])]

        def cp(a, k, src_slot, dst_slot, to):
            return pltpu.make_async_remote_copy(
                src_ref=x_refs[a].at[src_slot], dst_ref=land(a, dst_slot),
                send_sem=send.at[7 * a + k], recv_sem=recv.at[7 * a + k],
                device_id=to, device_id_type=MESH)

        mine = [pltpu.make_async_copy(x_refs[a].at[_slot(me)], land(a, _slot(me)), loc.at[a])
                for a in range(n)]
        sends = [cp(a, k, _slot(p), _slot(me), p) for a in range(n) for k, p in enumerate(peers)]

        def begin():
            for m in mine:
                m.start()
            for s in sends:
                s.start()

        def end():
            for a in range(n):
                for k, p in enumerate(peers):
                    cp(a, k, _slot(me), _slot(p), me).wait_recv()
            for s in sends:
                s.wait_send()
            for m in mine:
                m.wait()

        return begin, end


def _exchange(xs, bufs, layer, name, row0=0):
    comm = _ExchangeComm(xs, bufs, layer, row0)

    def body(*refs):
        begin, end = comm.ops(refs[:comm.n_in], refs[comm.n_in:comm.n_in + comm.n_out], refs[-3:])
        begin()
        end()

    any_spec = pl.BlockSpec(memory_space=pl.ANY)
    return pl.pallas_call(
        body, name=name, out_shape=comm.out_shape,
        in_specs=[any_spec] * comm.n_in, out_specs=[any_spec] * comm.n_out, scratch_shapes=comm.sems,
        input_output_aliases=dict(comm.aliases),
    )(*comm.operands)


_DIMS = {"nn": (((1,), (0,)), ((), ())), "nt": (((1,), (1,)), ((), ())), "tn": (((0,), (0,)), ((), ()))}


def _matmul(a, b, mode, out_dtype, name, comm=None, epi=None, extra=None):
    if mode == "nn":
        (m, k), (_, n) = a.shape, b.shape
    elif mode == "nt":
        (m, k), (n, _) = a.shape, b.shape
    else:
        (k, m), (_, n) = a.shape, b.shape
    n_ext = 1 if epi == "relu2_bwd" else 0
    n_main = 2 if epi == "relu2" else 1
    tm, tn, tk = _matmul_tiles(m, n, k, jnp.dtype(out_dtype).itemsize, n_main, n_ext)
    nk = k // tk
    grid = (m // tm, n // tn, nk)
    dn = _DIMS[mode]
    n_cin = comm.n_in if comm else 0
    n_cout = comm.n_out if comm else 0
    n_acc = 0 if nk == 1 else 1
    first_in, first_out = 2 + n_ext, 2 + n_ext + n_cin

    def finish(acc, refs):
        outs = refs[first_out:first_out + n_main]
        if epi == "relu2":
            outs[0][...] = acc.astype(outs[0].dtype)
            r = jnp.maximum(acc, 0.0)
            outs[1][...] = (r * r).astype(outs[1].dtype)
        elif epi == "relu2_bwd":
            outs[0][...] = (acc * (2.0 * jnp.maximum(refs[2][...].astype(F32), 0.0))).astype(outs[0].dtype)
        else:
            outs[0][...] = acc.astype(outs[0].dtype)

    def body(*refs):
        a_ref, b_ref = refs[:2]
        scratch = refs[first_out + n_main + n_cout:]
        pid = [pl.program_id(ax) for ax in range(3)]
        if comm:
            begin, end = comm.ops(refs[first_in:first_in + n_cin],
                                  refs[first_out + n_main:first_out + n_main + n_cout], scratch[n_acc:])
            pl.when((pid[0] == 0) & (pid[1] == 0) & (pid[2] == 0))(begin)
        part = lax.dot_general(a_ref[...], b_ref[...], dn, preferred_element_type=F32)
        if nk == 1:
            finish(part, refs)
        else:
            acc_ref = scratch[0]

            @pl.when(pid[2] == 0)
            def _():
                acc_ref[...] = part

            @pl.when(pid[2] > 0)
            def _():
                acc_ref[...] += part

            @pl.when(pid[2] == nk - 1)
            def _():
                finish(acc_ref[...], refs)
        if comm:
            pl.when((pid[0] == grid[0] - 1) & (pid[1] == grid[1] - 1) & (pid[2] == grid[2] - 1))(end)

    a_spec = pl.BlockSpec((tk, tm), lambda i, j, kk: (kk, i)) if mode == "tn" else \
        pl.BlockSpec((tm, tk), lambda i, j, kk: (i, kk))
    b_spec = pl.BlockSpec((tn, tk), lambda i, j, kk: (j, kk)) if mode == "nt" else \
        pl.BlockSpec((tk, tn), lambda i, j, kk: (kk, j))
    o_spec = pl.BlockSpec((tm, tn), lambda i, j, kk: (i, j))
    any_spec = pl.BlockSpec(memory_space=pl.ANY)
    res = pl.pallas_call(
        body, name=name,
        out_shape=[jax.ShapeDtypeStruct((m, n), out_dtype)] * n_main + (comm.out_shape if comm else []),
        grid=grid,
        in_specs=[a_spec, b_spec] + [o_spec] * n_ext + [any_spec] * n_cin,
        out_specs=[o_spec] * n_main + [any_spec] * n_cout,
        scratch_shapes=([] if nk == 1 else [pltpu.VMEM((tm, tn), F32)]) + (comm.sems if comm else []),
        input_output_aliases={first_in + i: n_main + o for i, o in comm.aliases.items()} if comm else {},
        compiler_params=_params(("arbitrary",) * 3 if comm else ("parallel", "parallel", "arbitrary")),
    )(a, b, *([extra] if n_ext else []), *(comm.operands if comm else []))
    main = res[0] if n_main == 1 else tuple(res[:n_main])
    return (main, list(res[n_main:])) if comm else main


def _dot(a, b, mode):
    return lax.dot_general(a.astype(BF16), b.astype(BF16), _DIMS[mode], preferred_element_type=F32)


@functools.partial(jax.custom_vjp, nondiff_argnums=(2,))
def _bdot(a, b, mode):
    return _dot(a, b, mode)


def _bdot_fwd(a, b, mode):
    return _dot(a, b, mode), (a, b)


def _bdot_bwd(mode, res, g):
    a, b = res
    if mode == "nn":
        return _dot(g, b, "nt"), _dot(a, g, "tn")
    if mode == "nt":
        return _dot(g, b, "nn"), _dot(g, a, "tn")
    return _dot(b, g, "nt"), _dot(a, g, "nn")


_bdot.defvjp(_bdot_fwd, _bdot_bwd)


def _split3(x):
    hi = x.astype(BF16)
    r1 = x - hi.astype(F32)
    mid = r1.astype(BF16)
    return hi, mid, (r1 - mid.astype(F32)).astype(BF16)


def _xdot01(m01, x):
    return sum(lax.dot_general(m01, p, _DIMS["nn"], preferred_element_type=F32) for p in _split3(x))


@jax.custom_vjp
def _spread(v, m01):
    return sum(lax.dot_general(p, m01, _DIMS["nn"], preferred_element_type=F32) for p in _split3(v))


def _spread_fwd(v, m01):
    return _spread(v, m01), m01


def _spread_bwd(m01, g):
    dv = sum(lax.dot_general(p, m01, _DIMS["nt"], preferred_element_type=F32) for p in _split3(g)[:2])
    return dv, jnp.zeros_like(m01)


_spread.defvjp(_spread_fwd, _spread_bwd)


def _chunk_tri(n, lower):
    r = lax.broadcasted_iota(jnp.int32, (n, n), 0)
    c = lax.broadcasted_iota(jnp.int32, (n, n), 1)
    tri = (r >= c) if lower else (r <= c)
    return jnp.where(tri & (r // SSD_Q == c // SSD_Q), 1.0, 0.0).astype(BF16)


def _shift_down(x, prev8, k):
    if k == 0:
        return x
    n = x.shape[0]
    r = pltpu.roll(x, k, 0)
    rp = pltpu.roll(prev8, k, 0)
    rows = lax.broadcasted_iota(jnp.int32, (8, x.shape[1]), 0)
    head = jnp.where(rows < k, rp, r[:8])
    return head if n == 8 else jnp.concatenate([head, r[8:]], axis=0)


def _shift_up(x, next8, k):
    if k == 0:
        return x
    n = x.shape[0]
    r = pltpu.roll(x, n - k, 0)
    rn = pltpu.roll(next8, 8 - k, 0)
    rows = lax.broadcasted_iota(jnp.int32, (8, x.shape[1]), 0)
    tail = jnp.where(rows >= 8 - k, rn, r[n - 8:])
    return tail if n == 8 else jnp.concatenate([r[:n - 8], tail], axis=0)


def _sigmoid(x):
    return 0.5 * jnp.tanh(0.5 * x) + 0.5


def _rows(ref, i):
    return ref[i:i + 1, :]


def _norm_parts(x, mp_ref, which):
    ln, sh, sc = _rows(mp_ref, 6 + which), _rows(mp_ref, 3 * which), _rows(mp_ref, 3 * which + 1)
    r = lax.rsqrt(jnp.mean(x * x, axis=1, keepdims=True) + EPS)
    return r, ln, sh, sc


def _norm_fwd(x, modp, which, name):
    t, d = x.shape

    def body(x_ref, mp_ref, u_ref):
        xv = x_ref[...]
        r, ln, sh, sc = _norm_parts(xv, mp_ref, which)
        u_ref[...] = (((xv * r) * ln) * (1.0 + sc) + sh).astype(BF16)

    row = pl.BlockSpec((ROW_TILE, d), lambda i: (i, 0))
    return pl.pallas_call(
        body, name=name, out_shape=jax.ShapeDtypeStruct((t, d), BF16), grid=(t // ROW_TILE,),
        in_specs=[row, pl.BlockSpec((8, d), lambda i: (0, 0))], out_specs=row,
        compiler_params=_params(("parallel",)))(x, modp)


def _resid_norm_fwd(x, br, mp_gate, gate_row, mp_norm, which, name):
    t, d = x.shape

    def body(x_ref, br_ref, mg_ref, mn_ref, xn_ref, u_ref):
        xv = x_ref[...] + _rows(mg_ref, gate_row) * br_ref[...]
        xn_ref[...] = xv
        r, ln, sh, sc = _norm_parts(xv, mn_ref, which)
        u_ref[...] = (((xv * r) * ln) * (1.0 + sc) + sh).astype(BF16)

    row = pl.BlockSpec((ROW_TILE, d), lambda i: (i, 0))
    mp = pl.BlockSpec((8, d), lambda i: (0, 0))
    return pl.pallas_call(
        body, name=name,
        out_shape=[jax.ShapeDtypeStruct((t, d), F32), jax.ShapeDtypeStruct((t, d), BF16)],
        grid=(t // ROW_TILE,), in_specs=[row, row, mp, mp], out_specs=[row, row],
        compiler_params=_params(("parallel",)))(x, br, mp_gate, mp_norm)


def _final_fwd_bwd(x, br, mp_gate, fnorm8, target, name):
    t, d = x.shape

    def body(x_ref, br_ref, mg_ref, fn_ref, tg_ref, dx_ref, dbr_ref, acc_ref):
        gate = _rows(mg_ref, 5)
        brv = br_ref[...]
        xv = x_ref[...] + gate * brv
        fn = _rows(fn_ref, 0)
        r = lax.rsqrt(jnp.mean(xv * xv, axis=1, keepdims=True) + EPS)
        nrm = xv * r
        err = nrm * fn - tg_ref[...]
        loss = 0.5 * jnp.sum(jnp.mean(err * err, axis=1, keepdims=True), axis=0, keepdims=True)
        dy = err * (1.0 / d)
        dn = dy * fn
        dx = r * (dn - nrm * jnp.mean(dn * nrm, axis=1, keepdims=True))
        dx_ref[...] = dx
        dbr_ref[...] = (dx * gate).astype(BF16)
        upd = jnp.concatenate([
            jnp.broadcast_to(loss, (1, d)),
            jnp.sum(dy * nrm, axis=0, keepdims=True),
            jnp.sum(dx * brv, axis=0, keepdims=True),
            jnp.zeros((5, d), F32)], axis=0)

        @pl.when(pl.program_id(0) == 0)
        def _():
            acc_ref[...] = upd

        @pl.when(pl.program_id(0) > 0)
        def _():
            acc_ref[...] += upd

    row = pl.BlockSpec((ROW_TILE, d), lambda i: (i, 0))
    mp = pl.BlockSpec((8, d), lambda i: (0, 0))
    return pl.pallas_call(
        body, name=name,
        out_shape=[jax.ShapeDtypeStruct((t, d), F32), jax.ShapeDtypeStruct((t, d), BF16),
                   jax.ShapeDtypeStruct((8, d), F32)],
        grid=(t // ROW_TILE,), in_specs=[row, row, mp, mp, row], out_specs=[row, row, mp],
        compiler_params=_params(("arbitrary",)))(x, br, mp_gate, fnorm8, target)


def _resid_norm_bwd(dx, du, x, mp_norm, which, name, br=None, mp_gate=None, gate_row=None):
    t, d = x.shape
    has_gate = br is not None

    def body(*refs):
        if has_gate:
            dx_ref, du_ref, x_ref, mn_ref, br_ref, mg_ref, dxn_ref, dbr_ref, acc_ref = refs
        else:
            dx_ref, du_ref, x_ref, mn_ref, dxn_ref, acc_ref = refs
        xv, duv = x_ref[...], du_ref[...]
        r, ln, _, sc = _norm_parts(xv, mn_ref, which)
        nrm = xv * r
        dn = duv * (ln * (1.0 + sc))
        dxn = dx_ref[...] + r * (dn - nrm * jnp.mean(dn * nrm, axis=1, keepdims=True))
        dxn_ref[...] = dxn
        rows = [jnp.sum(duv, axis=0, keepdims=True), jnp.sum(duv * nrm, axis=0, keepdims=True)]
        if has_gate:
            dbr_ref[...] = (dxn * _rows(mg_ref, gate_row)).astype(BF16)
            rows.append(jnp.sum(dxn * br_ref[...], axis=0, keepdims=True))
        upd = jnp.concatenate(rows + [jnp.zeros((8 - len(rows), d), F32)], axis=0)

        @pl.when(pl.program_id(0) == 0)
        def _():
            acc_ref[...] = upd

        @pl.when(pl.program_id(0) > 0)
        def _():
            acc_ref[...] += upd

    row = pl.BlockSpec((ROW_TILE, d), lambda i: (i, 0))
    mp = pl.BlockSpec((8, d), lambda i: (0, 0))
    ins, in_specs = [dx, du, x, mp_norm], [row, row, row, mp]
    outs = [jax.ShapeDtypeStruct((t, d), F32)]
    out_specs = [row]
    if has_gate:
        ins += [br, mp_gate]
        in_specs += [row, mp]
        outs.append(jax.ShapeDtypeStruct((t, d), BF16))
        out_specs.append(row)
    outs.append(jax.ShapeDtypeStruct((8, d), F32))
    out_specs.append(mp)
    return pl.pallas_call(
        body, name=name, out_shape=outs, grid=(t // ROW_TILE,), in_specs=in_specs, out_specs=out_specs,
        compiler_params=_params(("arbitrary",)))(*ins)


def _merge_fwd(proj, p_conv, p_ssm, name):
    t, d = p_conv.shape

    def body(gl_ref, pc_ref, ps_ref, o_ref):
        g = _sigmoid(gl_ref[...].astype(F32))
        o_ref[...] = (g[:, :d] * pc_ref[...] + g[:, d:] * ps_ref[...]).astype(BF16)

    row = pl.BlockSpec((ROW_TILE, d), lambda i: (i, 0))
    return pl.pallas_call(
        body, name=name, out_shape=jax.ShapeDtypeStruct((t, d), BF16), grid=(t // ROW_TILE,),
        in_specs=[pl.BlockSpec((ROW_TILE, 2 * d), lambda i: (i, 0)), row, row], out_specs=row,
        compiler_params=_params(("parallel",)))(proj, p_conv, p_ssm)


def _merge_bwd(dmerged, proj, p_conv, p_ssm, name):
    t, d = p_conv.shape
    pw = proj.shape[1]

    def body(dm_ref, gl_ref, pc_ref, ps_ref, dpc_ref, dps_ref, dgl_ref):
        g = _sigmoid(gl_ref[...].astype(F32))
        gc, gs = g[:, :d], g[:, d:]
        dm = dm_ref[...]
        dpc_ref[...] = (dm * gc).astype(BF16)
        dps_ref[...] = (dm * gs).astype(BF16)
        dgl_ref[...] = jnp.concatenate(
            [dm * pc_ref[...] * gc * (1.0 - gc), dm * ps_ref[...] * gs * (1.0 - gs)], axis=1).astype(BF16)

    row = pl.BlockSpec((ROW_TILE, d), lambda i: (i, 0))
    wide = pl.BlockSpec((ROW_TILE, 2 * d), lambda i: (i, 0))
    return pl.pallas_call(
        body, name=name,
        out_shape=[jax.ShapeDtypeStruct((t, d), BF16), jax.ShapeDtypeStruct((t, d), BF16),
                   jax.ShapeDtypeStruct((t, pw), BF16)],
        grid=(t // ROW_TILE,), in_specs=[row, wide, row, row], out_specs=[row, row, wide],
        compiler_params=_params(("parallel",)))(dmerged, proj, p_conv, p_ssm)


def _halo_specs(t, width, col):
    nb = t // 8
    step = ROW_TILE // 8
    prev = pl.BlockSpec((8, width), lambda i: (jnp.maximum(i * step - 1, 0), col))
    nxt = pl.BlockSpec((8, width), lambda i: (jnp.minimum((i + 1) * step, nb - 1), col))
    return prev, nxt


def _gconv_fwd(proj, conv_w8, d, name):
    t = proj.shape[0]

    def body(cb_ref, cc_ref, cx_ref, ccp_ref, cxp_ref, w_ref, o_ref):
        first = pl.program_id(0) == 0
        v = cc_ref[...].astype(F32) * cx_ref[...].astype(F32)
        vp = jnp.where(first, 0.0, ccp_ref[...].astype(F32) * cxp_ref[...].astype(F32))
        cv = sum(_rows(w_ref, k) * _shift_down(v, vp, 2 - k) for k in range(3))
        o_ref[...] = (cb_ref[...].astype(F32) * cv).astype(BF16)

    def win(col):
        return pl.BlockSpec((ROW_TILE, d), lambda i: (i, col))

    return pl.pallas_call(
        body, name=name, out_shape=jax.ShapeDtypeStruct((t, d), BF16), grid=(t // ROW_TILE,),
        in_specs=[win(2), win(3), win(4), _halo_specs(t, d, 3)[0], _halo_specs(t, d, 4)[0],
                  pl.BlockSpec((8, d), lambda i: (0, 0))],
        out_specs=pl.BlockSpec((ROW_TILE, d), lambda i: (i, 0)),
        compiler_params=_params(("parallel",)))(proj, proj, proj, proj, proj, conv_w8)


def _gconv_bwd(dy, proj, conv_w8, dproj, d, name):
    t = proj.shape[0]
    nt = t // ROW_TILE

    def body(dy_ref, dyn_ref, cb_ref, cc_ref, cx_ref, ccp_ref, cxp_ref, cbn_ref, w_ref, _, dp_ref, dw_ref, st_ref):
        i, j = pl.program_id(0), pl.program_id(1)

        @pl.when(j == 0)
        def _():
            cb, cc, cx = cb_ref[...].astype(F32), cc_ref[...].astype(F32), cx_ref[...].astype(F32)
            v = cc * cx
            vp = jnp.where(i == 0, 0.0, ccp_ref[...].astype(F32) * cxp_ref[...].astype(F32))
            sh = [_shift_down(v, vp, 2 - k) for k in range(3)]
            cv = sum(_rows(w_ref, k) * sh[k] for k in range(3))
            dyv = dy_ref[...]
            dcv = dyv * cb
            dcvn = jnp.where(i == nt - 1, 0.0, dyn_ref[...] * cbn_ref[...].astype(F32))
            dv = sum(_rows(w_ref, k) * _shift_up(dcv, dcvn, 2 - k) for k in range(3))
            st_ref[0] = (dyv * cv).astype(BF16)
            st_ref[1] = (dv * cx).astype(BF16)
            st_ref[2] = (dv * cc).astype(BF16)
            upd = jnp.concatenate([jnp.sum(dcv * sh[k], axis=0, keepdims=True) for k in range(3)]
                                  + [jnp.zeros((5, d), F32)], axis=0)

            @pl.when(i == 0)
            def _():
                dw_ref[...] = upd

            @pl.when(i > 0)
            def _():
                dw_ref[...] += upd

        dp_ref[...] = st_ref[j]

    def win(col):
        return pl.BlockSpec((ROW_TILE, d), lambda i, j: (i, col))

    def halo(col, which):
        nb, step = t // 8, ROW_TILE // 8
        if which == 0:
            return pl.BlockSpec((8, d), lambda i, j: (jnp.maximum(i * step - 1, 0), col))
        return pl.BlockSpec((8, d), lambda i, j: (jnp.minimum((i + 1) * step, nb - 1), col))

    row = pl.BlockSpec((ROW_TILE, d), lambda i, j: (i, 0))
    w8 = pl.BlockSpec((8, d), lambda i, j: (0, 0))
    return pl.pallas_call(
        body, name=name,
        out_shape=[jax.ShapeDtypeStruct(dproj.shape, BF16), jax.ShapeDtypeStruct((8, d), F32)],
        grid=(nt, 3),
        in_specs=[row, halo(0, 1), win(2), win(3), win(4), halo(3, 0), halo(4, 0), halo(2, 1), w8,
                  pl.BlockSpec(memory_space=pl.ANY)],
        out_specs=[pl.BlockSpec((ROW_TILE, d), lambda i, j: (i, 2 + j)), w8],
        scratch_shapes=[pltpu.VMEM((3, ROW_TILE, d), BF16)],
        input_output_aliases={9: 0},
        compiler_params=_params(("arbitrary", "arbitrary")))(
            dy, dy, proj, proj, proj, proj, proj, proj, conv_w8, dproj)


def _dsilu(p):
    s = _sigmoid(p)
    return s * (1.0 + p * (1.0 - s))


def _sconv_specs(t, d, col0, nt):
    nb, step = t // 8, ROW_TILE // 8
    win = pl.BlockSpec((ROW_TILE, d), lambda j, i: (i, col0 + j))
    prev = pl.BlockSpec((8, d), lambda j, i: (jnp.maximum(i * step - 1, 0), col0 + j))
    nxt = pl.BlockSpec((8, d), lambda j, i: (jnp.minimum((i + 1) * step, nb - 1), col0 + j))
    return win, prev, nxt


def _sconv_fwd(proj, w8, b8, d, col0, ncol, name):
    t = proj.shape[0]
    nt = t // ROW_TILE

    def body(x_ref, xp_ref, w_ref, b_ref, o_ref):
        xv = x_ref[...].astype(F32)
        xp = jnp.where(pl.program_id(1) == 0, 0.0, xp_ref[...].astype(F32))
        pre = _rows(b_ref, 0) + sum(_rows(w_ref, k) * _shift_down(xv, xp, 3 - k) for k in range(4))
        o_ref[...] = (pre * _sigmoid(pre)).astype(BF16)

    win, prev, _ = _sconv_specs(t, d, col0, nt)
    par = pl.BlockSpec((8, d), lambda j, i: (0, j))
    return pl.pallas_call(
        body, name=name, out_shape=jax.ShapeDtypeStruct((t, ncol * d), BF16), grid=(ncol, nt),
        in_specs=[win, prev, par, par], out_specs=pl.BlockSpec((ROW_TILE, d), lambda j, i: (i, j)),
        compiler_params=_params(("parallel", "parallel")))(proj, proj, w8, b8)


def _sconv_bwd(dxc, proj, w8, b8, dproj, d, col0, ncol, name):
    t = proj.shape[0]
    nt = t // ROW_TILE

    def body(d_ref, dn_ref, x_ref, xp_ref, xn_ref, w_ref, b_ref, _, dp_ref, dw_ref):
        i = pl.program_id(1)
        xv = x_ref[...].astype(F32)
        xp = jnp.where(i == 0, 0.0, xp_ref[...].astype(F32))
        bias = _rows(b_ref, 0)
        sh = [_shift_down(xv, xp, 3 - k) for k in range(4)]
        pre = bias + sum(_rows(w_ref, k) * sh[k] for k in range(4))
        dpre = d_ref[...] * _dsilu(pre)
        xn = xn_ref[...].astype(F32)
        pre_n = bias + sum(_rows(w_ref, k) * _shift_down(xn, xv[ROW_TILE - 8:], 3 - k) for k in range(4))
        dpre_n = jnp.where(i == nt - 1, 0.0, dn_ref[...] * _dsilu(pre_n))
        dx = sum(_rows(w_ref, k) * _shift_up(dpre, dpre_n, 3 - k) for k in range(4))
        dp_ref[...] = dx.astype(BF16)
        upd = jnp.concatenate([jnp.sum(dpre * sh[k], axis=0, keepdims=True) for k in range(4)]
                              + [jnp.sum(dpre, axis=0, keepdims=True), jnp.zeros((3, d), F32)], axis=0)

        @pl.when(i == 0)
        def _():
            dw_ref[...] = upd

        @pl.when(i > 0)
        def _():
            dw_ref[...] += upd

    win, prev, nxt = _sconv_specs(t, d, col0, nt)
    nb, step = t // 8, ROW_TILE // 8
    dwin = pl.BlockSpec((ROW_TILE, d), lambda j, i: (i, j))
    dnxt = pl.BlockSpec((8, d), lambda j, i: (jnp.minimum((i + 1) * step, nb - 1), j))
    par = pl.BlockSpec((8, d), lambda j, i: (0, j))
    return pl.pallas_call(
        body, name=name,
        out_shape=[jax.ShapeDtypeStruct(dproj.shape, BF16), jax.ShapeDtypeStruct((8, ncol * d), F32)],
        grid=(ncol, nt),
        in_specs=[dwin, dnxt, win, prev, nxt, par, par, pl.BlockSpec(memory_space=pl.ANY)],
        out_specs=[win, par], input_output_aliases={7: 0},
        compiler_params=_params(("arbitrary", "arbitrary")))(dxc, dxc, proj, proj, proj, w8, b8, dproj)


def _softplus(x):
    return jnp.maximum(x, 0.0) + jnp.log(1.0 + jnp.exp(-jnp.abs(x)))


def _dt_fwd(proj, par8, nheads, dt_col, name):
    t = proj.shape[0]

    def body(r_ref, p_ref, dt_ref, ac_ref):
        lane = lax.broadcasted_iota(jnp.int32, (1, LANE), 1)
        valid = lane < nheads
        raw = r_ref[...][:, :LANE].astype(F32)
        dt = jnp.where(valid, _softplus(raw + _rows(p_ref, 0)), 0.0)
        a = jnp.where(valid, -jnp.exp(_rows(p_ref, 1)), 0.0)
        dt_ref[...] = dt
        ac_ref[...] = _xdot01(_chunk_tri(ROW_TILE, True), dt * a)

    out = pl.BlockSpec((ROW_TILE, LANE), lambda i: (i, 0))
    return pl.pallas_call(
        body, name=name,
        out_shape=[jax.ShapeDtypeStruct((t, LANE), F32), jax.ShapeDtypeStruct((t, LANE), F32)],
        grid=(t // ROW_TILE,),
        in_specs=[pl.BlockSpec((ROW_TILE, 2 * LANE), lambda i: (i, dt_col)), pl.BlockSpec((8, LANE), lambda i: (0, 0))],
        out_specs=[out, out], compiler_params=_params(("parallel",)))(proj, par8)


def _dt_bwd(ddt_g, dac_g, dt, proj, par8, dproj, nheads, dt_col, name):
    t = proj.shape[0]
    ng = ddt_g.shape[0]

    def body(dd_ref, da_ref, dt_ref, r_ref, p_ref, _, dp_ref, acc_ref):
        lane = lax.broadcasted_iota(jnp.int32, (1, LANE), 1)
        valid = lane < nheads
        ddt = sum(dd_ref[g] for g in range(ng))
        dac = sum(da_ref[g] for g in range(ng))
        a = jnp.where(valid, -jnp.exp(_rows(p_ref, 1)), 0.0)
        d_a = _xdot01(_chunk_tri(ROW_TILE, False), dac)
        ddt = ddt + d_a * a
        raw = r_ref[...][:, :LANE].astype(F32)
        draw = jnp.where(valid, ddt * _sigmoid(raw + _rows(p_ref, 0)), 0.0)
        dp_ref[...] = jnp.concatenate([draw, jnp.zeros_like(draw)], axis=1).astype(BF16)
        upd = jnp.concatenate([jnp.sum(draw, axis=0, keepdims=True),
                               jnp.sum(d_a * dt_ref[...], axis=0, keepdims=True) * a,
                               jnp.zeros((6, LANE), F32)], axis=0)

        @pl.when(pl.program_id(0) == 0)
        def _():
            acc_ref[...] = upd

        @pl.when(pl.program_id(0) > 0)
        def _():
            acc_ref[...] += upd

    row = pl.BlockSpec((ROW_TILE, LANE), lambda i: (i, 0))
    grp = pl.BlockSpec((ng, ROW_TILE, LANE), lambda i: (0, i, 0))
    win = pl.BlockSpec((ROW_TILE, 2 * LANE), lambda i: (i, dt_col))
    par = pl.BlockSpec((8, LANE), lambda i: (0, 0))
    return pl.pallas_call(
        body, name=name,
        out_shape=[jax.ShapeDtypeStruct(dproj.shape, BF16), jax.ShapeDtypeStruct((8, LANE), F32)],
        grid=(t // ROW_TILE,),
        in_specs=[grp, grp, row, win, par, pl.BlockSpec(memory_space=pl.ANY)],
        out_specs=[win, par], input_output_aliases={5: 0},
        compiler_params=_params(("arbitrary",)))(ddt_g, dac_g, dt, proj, par8, dproj)


def _ssd_chunk(xs, bm, cm, z, dt, ac, st, dsk, nw, g):
    q = xs.shape[0]
    lane_h = lax.broadcasted_iota(jnp.int32, (1, LANE), 1)
    sub_h = lax.broadcasted_iota(jnp.int32, (LANE, 1), 0)
    lane_c = lax.broadcasted_iota(jnp.int32, (1, GROUP_W), 1) // HEAD_DIM
    rr = lax.broadcasted_iota(jnp.int32, (q, q), 0)
    cc = lax.broadcasted_iota(jnp.int32, (q, q), 1)
    tril = rr >= cc
    last_row = jnp.where(lax.broadcasted_iota(jnp.int32, (q, 1), 0) == q - 1, 1.0, 0.0)
    act = ac.T
    col, row = [], []
    for r in range(HEADS_PER_GROUP):
        h = g * HEADS_PER_GROUP + r
        col.append(jnp.sum(ac * jnp.where(lane_h == h, 1.0, 0.0), axis=1, keepdims=True))
        row.append(jnp.sum(act * jnp.where(sub_h == h, 1.0, 0.0), axis=0, keepdims=True))

    def per_head(vals):
        out = jnp.broadcast_to(vals[0], (q, GROUP_W))
        for r in range(1, HEADS_PER_GROUP):
            out = jnp.where(lane_c == r, vals[r], out)
        return out

    spread = jnp.where(lax.broadcasted_iota(jnp.int32, (LANE, GROUP_W), 0)
                       == g * HEADS_PER_GROUP + lax.broadcasted_iota(jnp.int32, (LANE, GROUP_W), 1) // HEAD_DIM,
                       1.0, 0.0).astype(BF16)
    acx = _spread(ac, spread)
    x = xs * _spread(dt, spread)
    s = _bdot(cm, bm, "nt")
    y = per_head([_bdot(s * jnp.exp(jnp.where(tril, col[r] - row[r], -jnp.inf)), x, "nn")
                  for r in range(HEADS_PER_GROUP)])
    y = y + jnp.exp(acx) * _bdot(cm, st, "nn")
    last = jnp.sum(acx * last_row, axis=0, keepdims=True)
    new_st = st * jnp.exp(last) + _bdot(bm, x * jnp.exp(last - acx), "tn")
    y = y + dsk * xs
    y = y * (z * _sigmoid(z))
    yn = y * lax.rsqrt(jnp.mean(y * y, axis=1, keepdims=True) + EPS) * nw
    return yn, new_st


def _ssd_specs(nc, z_col, ngroups, rev):
    assert ngroups % SSD_GROUPS == 0 and z_col % SSD_GROUPS == 0
    z_col //= SSD_GROUPS
    b_col, c_col = 2 * ngroups // SSD_GROUPS, 3 * ngroups // SSD_GROUPS

    def ci(c):
        return nc - 1 - c if rev else c

    nx, nb = SSD_GROUPS * GROUP_W, SSD_GROUPS * N_STATE
    xs = pl.BlockSpec((SSD_Q, nx), lambda g, c: (ci(c), g))
    bm = pl.BlockSpec((SSD_Q, nb), lambda g, c: (ci(c), b_col + g))
    cm = pl.BlockSpec((SSD_Q, nb), lambda g, c: (ci(c), c_col + g))
    z = pl.BlockSpec((SSD_Q, nx), lambda g, c: (ci(c), z_col + g))
    hd = pl.BlockSpec((SSD_Q, LANE), lambda g, c: (ci(c), 0))
    par = pl.BlockSpec((8, nx), lambda g, c: (0, g))
    stb = pl.BlockSpec((None, N_STATE, nx), lambda g, c: (ci(c), 0, g))
    db = pl.BlockSpec((SSD_Q, nb), lambda g, c: (ci(c), g))
    return xs, bm, cm, z, hd, par, stb, db


def _ssd_fwd(xc, proj, dt, ac, par8, ngroups, z_col, name):
    t = xc.shape[0]
    nc = t // SSD_Q
    steps = ngroups // SSD_GROUPS

    def body(x_ref, b_ref, c_ref, z_ref, dt_ref, ac_ref, p_ref, y_ref, so_ref, st_ref):
        @pl.when(pl.program_id(1) == 0)
        def _():
            st_ref[...] = jnp.zeros_like(st_ref)

        so_ref[...] = st_ref[...]
        dtv, acv = dt_ref[...], ac_ref[...]
        for i in range(SSD_GROUPS):
            cx, cb = slice(i * GROUP_W, (i + 1) * GROUP_W), slice(i * N_STATE, (i + 1) * N_STATE)
            yn, new_st = _ssd_chunk(x_ref[:, cx].astype(F32), b_ref[:, cb].astype(F32), c_ref[:, cb].astype(F32),
                                    z_ref[:, cx].astype(F32), dtv, acv, st_ref[:, cx],
                                    p_ref[0:1, cx], p_ref[1:2, cx], pl.program_id(0) * SSD_GROUPS + i)
            y_ref[:, cx] = yn.astype(BF16)
            st_ref[:, cx] = new_st

    xs, bm, cm, zs, hd, par, stb, _ = _ssd_specs(nc, z_col, ngroups, False)
    return pl.pallas_call(
        body, name=name,
        out_shape=[jax.ShapeDtypeStruct((t, ngroups * GROUP_W), BF16),
                   jax.ShapeDtypeStruct((nc, N_STATE, ngroups * GROUP_W), F32)],
        grid=(steps, nc), in_specs=[xs, bm, cm, zs, hd, hd, par], out_specs=[xs, stb],
        scratch_shapes=[pltpu.VMEM((N_STATE, SSD_GROUPS * GROUP_W), F32)],
        compiler_params=_params(("parallel", "arbitrary")))(xc, xc, xc, proj, dt, ac, par8)


def _ssd_bwd(dyn, xc, proj, dt, ac, par8, states, dproj, ngroups, z_col, name, comm=None):
    t = xc.shape[0]
    nc = t // SSD_Q
    steps = ngroups // SSD_GROUPS

    n_cin = comm.n_in if comm else 0
    n_cout = comm.n_out if comm else 0

    def body(*refs):
        dy_ref, x_ref, b_ref, c_ref, z_ref, dt_ref, ac_ref, p_ref, s_ref = refs[:9]
        dz_ref, dx_ref, db_ref, dc_ref, ddt_ref, dac_ref, dp_ref = refs[10 + n_cin:17 + n_cin]
        ds_ref = refs[17 + n_cin + n_cout]
        c = pl.program_id(1)
        if comm:
            begin, end = comm.ops(refs[10:10 + n_cin], refs[17 + n_cin:17 + n_cin + n_cout], refs[-3:])
            pl.when((pl.program_id(0) == 0) & (c == 0))(begin)

        @pl.when(c == 0)
        def _():
            ds_ref[...] = jnp.zeros_like(ds_ref)

        dtv, acv = dt_ref[...], ac_ref[...]
        ddt_sum, dac_sum, upds = None, None, []
        for i in range(SSD_GROUPS):
            cx, cb = slice(i * GROUP_W, (i + 1) * GROUP_W), slice(i * N_STATE, (i + 1) * N_STATE)
            g = pl.program_id(0) * SSD_GROUPS + i
            args = (x_ref[:, cx].astype(F32), b_ref[:, cb].astype(F32), c_ref[:, cb].astype(F32),
                    z_ref[:, cx].astype(F32), dtv, acv, s_ref[:, cx], p_ref[0:1, cx], p_ref[1:2, cx])
            _, vjp = jax.vjp(lambda *a, g=g: _ssd_chunk(*a, g), *args)
            dxs, dbm, dcm, dz, ddt, dac, dst, ddsk, dnw = vjp((dy_ref[:, cx], ds_ref[:, cx]))
            ds_ref[:, cx] = dst
            dz_ref[:, cx] = dz.astype(BF16)
            dx_ref[:, cx] = dxs
            db_ref[:, cb] = dbm
            dc_ref[:, cb] = dcm
            ddt_sum = ddt if ddt_sum is None else ddt_sum + ddt
            dac_sum = dac if dac_sum is None else dac_sum + dac
            upds.append(jnp.concatenate([ddsk, dnw, jnp.zeros((6, GROUP_W), F32)], axis=0))
        ddt_ref[...] = ddt_sum
        dac_ref[...] = dac_sum
        upd = jnp.concatenate(upds, axis=1)

        @pl.when(c == 0)
        def _():
            dp_ref[...] = upd

        @pl.when(c > 0)
        def _():
            dp_ref[...] += upd

        if comm:
            pl.when((pl.program_id(0) == steps - 1) & (c == nc - 1))(end)

    xs, bm, cm, zs, hd, par, stb, db = _ssd_specs(nc, z_col, ngroups, True)
    hg = pl.BlockSpec((None, SSD_Q, LANE), lambda g, c: (g, nc - 1 - c, 0))
    any_spec = pl.BlockSpec(memory_space=pl.ANY)
    aliases = {9: 0}
    if comm:
        aliases.update({10 + i: 7 + o for i, o in comm.aliases.items()})
    res = pl.pallas_call(
        body, name=name,
        out_shape=[jax.ShapeDtypeStruct(dproj.shape, BF16),
                   jax.ShapeDtypeStruct((t, ngroups * GROUP_W), F32),
                   jax.ShapeDtypeStruct((t, ngroups * N_STATE), F32), jax.ShapeDtypeStruct((t, ngroups * N_STATE), F32),
                   jax.ShapeDtypeStruct((steps, t, LANE), F32), jax.ShapeDtypeStruct((steps, t, LANE), F32),
                   jax.ShapeDtypeStruct((8, ngroups * GROUP_W), F32)] + (comm.out_shape if comm else []),
        grid=(steps, nc),
        in_specs=[xs, xs, bm, cm, zs, hd, hd, par, stb, any_spec] + [any_spec] * n_cin,
        out_specs=[zs, xs, db, db, hg, hg, par] + [any_spec] * n_cout, input_output_aliases=aliases,
        scratch_shapes=[pltpu.VMEM((N_STATE, SSD_GROUPS * GROUP_W), F32)] + (comm.sems if comm else []),
        compiler_params=_params(("arbitrary", "arbitrary")))(
            dyn, xc, xc, xc, proj, dt, ac, par8, states, dproj, *(comm.operands if comm else []))
    return tuple(res[:7]), list(res[7:])


def _small_matmul(a, b, mode, name):
    if mode == "nn":
        shape = (a.shape[0], b.shape[1])
    else:
        shape = (a.shape[1], b.shape[1])

    def body(a_ref, b_ref, o_ref):
        o_ref[...] = _dot(a_ref[...], b_ref[...], mode)

    return pl.pallas_call(body, name=name, out_shape=jax.ShapeDtypeStruct(shape, F32),
                          compiler_params=_params())(a, b)


def _adamw(parts, w, m, v, name):
    npart, rows, cols = parts.shape
    tr = rows if rows <= 128 else _tile8(rows, 128)

    def body(p_ref, w_ref, m_ref, v_ref, g_ref, d_ref, nm_ref, nv_ref):
        g = p_ref[0].astype(F32)
        for j in range(1, npart):
            g = g + p_ref[j].astype(F32)
        mm = ADAM_B1 * m_ref[...] + (1.0 - ADAM_B1) * g
        vv = ADAM_B2 * v_ref[...] + (1.0 - ADAM_B2) * (g * g)
        m_hat = mm / (1.0 - ADAM_B1 ** ADAM_STEP)
        v_hat = vv / (1.0 - ADAM_B2 ** ADAM_STEP)
        g_ref[...] = g
        d_ref[...] = -ADAM_LR * (m_hat / (jnp.sqrt(v_hat) + ADAM_EPS) + ADAM_WD * w_ref[...])
        nm_ref[...] = mm
        nv_ref[...] = vv

    blk = pl.BlockSpec((tr, cols), lambda i: (i, 0))
    out = jax.ShapeDtypeStruct((rows, cols), F32)
    return pl.pallas_call(
        body, name=name, out_shape=[out] * 4, grid=(rows // tr,),
        in_specs=[pl.BlockSpec((npart, tr, cols), lambda i: (0, i, 0)), blk, blk, blk], out_specs=[blk] * 4,
        compiler_params=_params(("parallel",)))(parts, w, m, v)


def _tile8(n, target):
    best = 8
    for d in range(8, target + 1, 8):
        if n % d == 0:
            best = d
    return best


def _pad_rows8(a):
    return jnp.concatenate([a, jnp.zeros((8 - a.shape[0], a.shape[1]), a.dtype)], axis=0)


def _pad_lanes(a, n):
    return jnp.concatenate([a, jnp.zeros(a.shape[:-1] + (n - a.shape[-1],), a.dtype)], axis=-1)


def _pack(vecs, mult):
    flat = jnp.concatenate([v.reshape(-1) for v in vecs])
    pad = (-flat.shape[0]) % mult
    flat = jnp.concatenate([flat, jnp.zeros((pad,), flat.dtype)])
    return flat.reshape(-1, LANE)


def _unpack(flat, shapes):
    out, o = [], 0
    for s in shapes:
        n = int(np.prod(s))
        out.append(flat[o:o + n].reshape(s))
        o += n
    return out


def kernel(x, c, w_ada, b_ada, ln1, ln2, w_in, conv_w, ssm_conv_w, ssm_conv_b, dt_bias, a_log, d_skip, ssm_norm_w, w_conv_out, w_ssm_out, w_o, w_up, w_down, final_norm, loss_target, m_w_ada, m_b_ada, m_ln1, m_ln2, m_w_in, m_conv_w, m_ssm_conv_w, m_ssm_conv_b, m_dt_bias, m_a_log, m_d_skip, m_ssm_norm_w, m_w_conv_out, m_w_ssm_out, m_w_o, m_w_up, m_w_down, m_final_norm, v_w_ada, v_b_ada, v_ln1, v_ln2, v_w_in, v_conv_w, v_ssm_conv_w, v_ssm_conv_b, v_dt_bias, v_a_log, v_d_skip, v_ssm_norm_w, v_w_conv_out, v_w_ssm_out, v_w_o, v_w_up, v_w_down, v_final_norm):
    names = ["w_ada", "b_ada", "ln1", "ln2", "w_in", "conv_w", "ssm_conv_w", "ssm_conv_b", "dt_bias", "a_log",
             "d_skip", "ssm_norm_w", "w_conv_out", "w_ssm_out", "w_o", "w_up", "w_down", "final_norm"]
    w_of = dict(zip(names, [w_ada, b_ada, ln1, ln2, w_in, conv_w, ssm_conv_w, ssm_conv_b, dt_bias, a_log, d_skip,
                            ssm_norm_w, w_conv_out, w_ssm_out, w_o, w_up, w_down, final_norm]))
    m_of = dict(zip(names, [m_w_ada, m_b_ada, m_ln1, m_ln2, m_w_in, m_conv_w, m_ssm_conv_w, m_ssm_conv_b, m_dt_bias,
                            m_a_log, m_d_skip, m_ssm_norm_w, m_w_conv_out, m_w_ssm_out, m_w_o, m_w_up, m_w_down,
                            m_final_norm]))
    v_of = dict(zip(names, [v_w_ada, v_b_ada, v_ln1, v_ln2, v_w_in, v_conv_w, v_ssm_conv_w, v_ssm_conv_b, v_dt_bias,
                            v_a_log, v_d_skip, v_ssm_norm_w, v_w_conv_out, v_w_ssm_out, v_w_o, v_w_up, v_w_down,
                            v_final_norm]))

    _, t, d = x.shape
    nl = w_ada.shape[0]
    ada_w = w_ada.shape[2]
    ds = ssm_norm_w.shape[1]
    nh = dt_bias.shape[1]
    ng = nh // HEADS_PER_GROUP
    gn = ng * N_STATE
    xbc_w = ds + 2 * gn
    z_off, xbc_off = 5 * d, 5 * d + ds
    dt_off = xbc_off + xbc_w
    proj_w = dt_off + nh
    pw = dt_off + 2 * LANE
    me = 4 * lax.axis_index("x") + 2 * lax.axis_index("y") + lax.axis_index("c")

    x2, tgt = x[0], loss_target[0]

    c_act = c * jax.nn.sigmoid(c)
    sizes1 = [(1, d), conv_w.shape, ssm_conv_w.shape]
    (g1,) = _all_gather([_pack([c_act, conv_w, ssm_conv_w], LANE)], "ag_small_in")
    parts1 = [_unpack(g1[j].reshape(-1), sizes1) for j in range(N_DEV)]
    c_act_all = jnp.concatenate([p[0] for p in parts1], axis=0)
    conv_w_full = jnp.concatenate([p[1] for p in parts1], axis=-1)
    sconv_w_full = jnp.concatenate([p[2] for p in parts1], axis=-1)

    mod_part = jnp.stack([_small_matmul(c_act_all, w_ada[l], "nn", f"ada_fwd{l}") for l in range(nl)])
    (gmod,) = _all_gather([mod_part], "ag_mod")
    mod = lax.dynamic_index_in_dim(gmod, me, axis=2, keepdims=False)
    mod = jnp.moveaxis(mod, 0, 1).reshape(nl, N_DEV * ada_w) + b_ada
    mod = mod.reshape(nl, 6, d)
    modp = [jnp.concatenate([mod[l], ln1[l][None], ln2[l][None]], axis=0) for l in range(nl)]

    big = ["w_in", "w_up", "w_conv_out", "w_ssm_out", "w_o", "w_down"]
    rest = big[1:]

    def blocks_of(l, keys):
        return [w_of[k][l].astype(BF16) for k in keys]

    def w_in_halves(l):
        w = w_of["w_in"][l].astype(BF16)
        return [w[:d // 2]], [w[d // 2:]]

    def full_cols(g):
        return jnp.moveaxis(g, 0, 1).reshape(g.shape[1], N_DEV * g.shape[2])

    def full_rows(g):
        return g.reshape(N_DEV * g.shape[1], g.shape[2])

    def pad_w_in(pieces):
        w = jnp.concatenate([full_cols(g) for g in pieces], axis=0)
        return jnp.concatenate([w, jnp.zeros((d, pw - proj_w), BF16)], axis=-1)

    wi = [None] * nl
    wi[0] = pad_w_in(_all_gather(blocks_of(0, ["w_in"]), "ag_w_in0"))

    cw8 = [_pad_rows8(conv_w_full[l]) for l in range(nl)]
    sw8 = [_pad_rows8(sconv_w_full[l]) for l in range(nl)]
    sb8 = [_pad_rows8(ssm_conv_b[l][None]) for l in range(nl)]
    dtp8 = [_pad_rows8(_pad_lanes(jnp.stack([dt_bias[l], a_log[l]]), LANE)) for l in range(nl)]
    sp8 = [_pad_rows8(jnp.stack([jnp.repeat(d_skip[l], HEAD_DIM), ssm_norm_w[l]])) for l in range(nl)]
    fn8 = _pad_rows8(final_norm[None])
    xbc_col, ncol, dt_col, z_col = xbc_off // d, xbc_w // d, dt_off // (2 * LANE), z_off // GROUP_W

    saved = []
    x_cur, br_prev = x2, None
    for l in range(nl):
        if l == 0:
            x_in, u1 = x_cur, _norm_fwd(x_cur, modp[0], 0, "norm_first")
        else:
            x_in, u1 = _resid_norm_fwd(x_cur, br_prev, modp[l - 1], 5, modp[l], 0, f"resid_norm_a{l}")
        proj, got = _matmul(u1, wi[l], "nn", BF16, f"mm_in{l}", comm=_GatherComm(blocks_of(l, rest)))
        got = dict(zip(rest, got))
        wup, wdown = full_cols(got["w_up"]), full_rows(got["w_down"])
        wco, wso, wo = full_rows(got["w_conv_out"]), full_rows(got["w_ssm_out"]), full_rows(got["w_o"])
        y_conv = _gconv_fwd(proj, cw8[l], d, f"gconv_fwd{l}")
        xc = _sconv_fwd(proj, sw8[l], sb8[l], d, xbc_col, ncol, f"sconv_fwd{l}")
        dt, ac = _dt_fwd(proj, dtp8[l], nh, dt_col, f"dt_fwd{l}")
        yn, states = _ssd_fwd(xc, proj, dt, ac, sp8[l], ng, z_col, f"ssd_fwd{l}")
        p_conv = _matmul(y_conv, wco, "nn", F32, f"mm_conv_out{l}")
        p_ssm = _matmul(yn, wso, "nn", F32, f"mm_ssm_out{l}")
        merged = _merge_fwd(proj, p_conv, p_ssm, f"merge_fwd{l}")
        mix = _matmul(merged, wo, "nn", F32, f"mm_o{l}")
        x_mid, u2 = _resid_norm_fwd(x_in, mix, modp[l], 2, modp[l], 1, f"resid_norm_b{l}")
        if l + 1 < nl:
            top, bot = w_in_halves(l + 1)
            (h, hid), g_top = _matmul(u2, wup, "nn", BF16, f"mm_up{l}", epi="relu2", comm=_GatherComm(top))
            mlp, g_bot = _matmul(hid, wdown, "nn", F32, f"mm_down{l}", comm=_GatherComm(bot))
            wi[l + 1] = pad_w_in(g_top + g_bot)
        else:
            h, hid = _matmul(u2, wup, "nn", BF16, f"mm_up{l}", epi="relu2")
            mlp = _matmul(hid, wdown, "nn", F32, f"mm_down{l}")
        saved.append(dict(x_in=x_in, u1=u1, proj=proj, y_conv=y_conv, xc=xc, dt=dt, ac=ac, yn=yn, states=states,
                          p_conv=p_conv, p_ssm=p_ssm, merged=merged, mix=mix, x_mid=x_mid, u2=u2, h=h, hid=hid,
                          mlp=mlp, wup=wup, wdown=wdown, wco=wco, wso=wso, wo=wo))
        x_cur, br_prev = x_mid, mlp

    dx, dbr, acc = _final_fwd_bwd(x_cur, br_prev, modp[nl - 1], fn8, tgt, "final")
    loss = lax.psum(acc[0, 0], ("x", "y", "c"))
    g_final_norm = acc[1]
    dgate2 = acc[2]

    gw = {k: [None] * nl for k in big}
    small = {k: [None] * nl for k in ["mod", "ln1", "ln2", "conv_w", "ssm_conv_w", "ssm_conv_b", "dt_bias",
                                      "a_log", "d_skip", "ssm_norm_w"]}

    def grad_cols(a):
        return jnp.moveaxis(a.reshape(a.shape[0], N_DEV, a.shape[1] // N_DEV), 1, 0)

    def grad_rows(a):
        return a.reshape(N_DEV, a.shape[0] // N_DEV, a.shape[1])

    def recv_buf(name):
        return jnp.zeros((N_DEV, nl) + w_of[name].shape[1:], BF16)

    bufs_a, bufs_b = [recv_buf("w_in")], [recv_buf(k) for k in big[1:]]
    pending = None
    for l in reversed(range(nl)):
        s = saved[l]
        dh = _matmul(dbr, s["wdown"], "nt", BF16, f"mm_down_dx{l}", epi="relu2_bwd", extra=s["h"])
        gw["w_down"][l] = _matmul(s["hid"], dbr, "tn", BF16, f"mm_down_dw{l}")
        du2 = _matmul(dh, s["wup"], "nt", F32, f"mm_up_dx{l}")
        gw["w_up"][l] = _matmul(s["u2"], dh, "tn", BF16, f"mm_up_dw{l}")
        dx_mid, dmix, acc2 = _resid_norm_bwd(dx, du2, s["x_mid"], modp[l], 1, f"resid_norm_b_bwd{l}",
                                             br=s["mix"], mp_gate=modp[l], gate_row=2)
        dmerged = _matmul(dmix, s["wo"], "nt", F32, f"mm_o_dx{l}")
        gw["w_o"][l] = _matmul(s["merged"], dmix, "tn", BF16, f"mm_o_dw{l}")
        dpc, dps, dproj = _merge_bwd(dmerged, s["proj"], s["p_conv"], s["p_ssm"], f"merge_bwd{l}")
        dyc = _matmul(dpc, s["wco"], "nt", F32, f"mm_conv_out_dx{l}")
        gw["w_conv_out"][l] = _matmul(s["y_conv"], dpc, "tn", BF16, f"mm_conv_out_dw{l}")
        dyn = _matmul(dps, s["wso"], "nt", F32, f"mm_ssm_out_dx{l}")
        gw["w_ssm_out"][l] = _matmul(s["yn"], dps, "tn", BF16, f"mm_ssm_out_dw{l}")
        rest_grads = [grad_cols(gw["w_up"][l])] + [grad_rows(gw[k][l]) for k in big[2:]]
        sends, bufs, layers = rest_grads, list(bufs_b), [l] * len(rest_grads)
        if pending is not None:
            sends, bufs, layers = sends + pending, bufs + list(bufs_a), layers + [l + 1]
        (dproj, dxs, dbm, dcm, ddt_g, dac_g, dsp), bufs = _ssd_bwd(
            dyn, s["xc"], s["proj"], s["dt"], s["ac"], sp8[l], s["states"], dproj, ng, z_col, f"ssd_bwd{l}",
            comm=_ExchangeComm(sends, bufs, layers))
        bufs_b = bufs[:len(rest_grads)]
        if pending is not None:
            bufs_a = bufs[len(rest_grads):]
        dproj, dcw = _gconv_bwd(dyc, s["proj"], cw8[l], dproj, d, f"gconv_bwd{l}")
        dsw = []
        for seg, (dseg, c0) in enumerate([(dxs, 0), (dbm, ds // d), (dcm, ds // d + gn // d)]):
            cols = slice(c0 * d, c0 * d + dseg.shape[1])
            dproj, part = _sconv_bwd(dseg, s["proj"], sw8[l][:, cols], sb8[l][:, cols], dproj, d, xbc_col + c0,
                                     dseg.shape[1] // d, f"sconv_bwd{l}_{seg}")
            dsw.append(part)
        dsw = jnp.concatenate(dsw, axis=1)
        dproj, ddtp = _dt_bwd(ddt_g, dac_g, s["dt"], s["proj"], dtp8[l], dproj, nh, dt_col, f"dt_bwd{l}")
        du1 = _matmul(dproj, wi[l], "nt", F32, f"mm_in_dx{l}")

        def w_in_blocks(g):
            return grad_cols(g[:, :proj_w])

        if l > 0:
            pending = [w_in_blocks(_matmul(s["u1"], dproj, "tn", BF16, f"mm_in_dw{l}"))]
        else:
            g_top = _matmul(s["u1"][:, :d // 2], dproj, "tn", BF16, "mm_in_dw0_top")
            g_bot, bufs_a = _matmul(s["u1"][:, d // 2:], dproj, "tn", BF16, "mm_in_dw0_bot",
                                    comm=_ExchangeComm([w_in_blocks(g_top)], bufs_a, 0, row0=0))
            pending = [w_in_blocks(g_bot)]
        if l > 0:
            dx, dbr, acc1 = _resid_norm_bwd(dx_mid, du1, s["x_in"], modp[l], 0, f"resid_norm_a_bwd{l}",
                                            br=saved[l - 1]["mlp"], mp_gate=modp[l - 1], gate_row=5)
        else:
            dx, acc1 = _resid_norm_bwd(dx_mid, du1, s["x_in"], modp[0], 0, "norm_first_bwd")
        sc1, sc2 = mod[l, 1], mod[l, 4]
        small["mod"][l] = jnp.stack([acc1[0], acc1[1] * ln1[l], acc2[2], acc2[0], acc2[1] * ln2[l], dgate2])
        small["ln1"][l] = acc1[1] * (1.0 + sc1)
        small["ln2"][l] = acc2[1] * (1.0 + sc2)
        small["conv_w"][l] = dcw[:3]
        small["ssm_conv_w"][l] = dsw[:4]
        small["ssm_conv_b"][l] = dsw[4]
        small["dt_bias"][l] = ddtp[0, :nh]
        small["a_log"][l] = ddtp[1, :nh]
        small["d_skip"][l] = dsp[0].reshape(nh, HEAD_DIM).sum(axis=-1)
        small["ssm_norm_w"][l] = dsp[1]
        if l > 0:
            dgate2 = acc1[2]
    grad_x = dx[None]

    sm = {k: jnp.stack(v) for k, v in small.items()}
    rep_names = ["b_ada", "ln1", "ln2", "ssm_conv_b", "dt_bias", "a_log", "d_skip", "ssm_norm_w", "final_norm"]
    rep_grads = [sm["mod"].reshape(nl, 6 * d), sm["ln1"], sm["ln2"], sm["ssm_conv_b"], sm["dt_bias"], sm["a_log"],
                 sm["d_skip"], sm["ssm_norm_w"], g_final_norm]
    rep_pack = _pack(rep_grads, 8 * LANE)
    conv_pack = _pack([sm["conv_w"], sm["ssm_conv_w"]], 8 * LANE)
    g_rep, g_conv = _all_gather([rep_pack, conv_pack], "ag_small_grads")

    outs = {}

    def run_adamw(name, parts, tag):
        w, m, v = w_of[name], m_of[name], v_of[name]
        shp = w.shape
        r2 = (int(np.prod(shp[:-1])), shp[-1])
        res = _adamw(parts.reshape((parts.shape[0],) + r2), w.reshape(r2), m.reshape(r2), v.reshape(r2), tag)
        outs[name] = [a.reshape(shp) for a in res]

    rep_shapes = [w_of[k].shape for k in rep_names]
    res = _adamw(g_rep, _pack([w_of[k] for k in rep_names], 8 * LANE), _pack([m_of[k] for k in rep_names], 8 * LANE),
                 _pack([v_of[k] for k in rep_names], 8 * LANE), "adamw_replicated")
    for k, vals in zip(rep_names, zip(*[_unpack(a.reshape(-1), rep_shapes) for a in res])):
        outs[k] = list(vals)

    conv_parts = [_unpack(g_conv[j].reshape(-1), [sm["conv_w"].shape, sm["ssm_conv_w"].shape]) for j in range(N_DEV)]
    for idx, name in enumerate(["conv_w", "ssm_conv_w"]):
        wsh = w_of[name].shape[-1]
        full = jnp.stack([p[idx] for p in conv_parts])
        run_adamw(name, lax.dynamic_slice_in_dim(full, me * wsh, wsh, axis=3), "adamw_" + name)

    dmod_all = g_rep.reshape(N_DEV, -1)[:, :nl * 6 * d].reshape(N_DEV, nl, 6 * d)
    dmod_mine = lax.dynamic_slice_in_dim(dmod_all, me * ada_w, ada_w, axis=2)
    g_ada = jnp.stack([_small_matmul(c_act_all, dmod_mine[:, l], "tn", f"ada_bwd{l}") for l in range(nl)])
    run_adamw("w_ada", g_ada[None], "adamw_w_ada")

    bufs_a = _exchange(pending, bufs_a, 0, "a2a_w_in0_bot", row0=d // 2)
    for name, parts in zip(big, list(bufs_a) + list(bufs_b)):
        run_adamw(name, parts, "adamw_" + name)

    result = [loss, grad_x]
    for i in range(4):
        result += [outs[k][i] for k in names]
    return tuple(result)
```

```python
import functools

import numpy as np
import jax
import jax.numpy as jnp
from jax import lax
from jax.experimental import pallas as pl
from jax.experimental.pallas import tpu as pltpu

F32 = jnp.float32
BF16 = jnp.bfloat16
EPS = 1e-6
N_STATE = 128
HEAD_DIM = 64
HEADS_PER_GROUP = 4
GROUP_W = HEAD_DIM * HEADS_PER_GROUP
SSD_GROUPS = 2
SSD_Q = 256
N_DEV = 8
ROW_TILE = 256
LANE = 128
VMEM_LIMIT = 56 * 1024 * 1024
MATMUL_VMEM_BUDGET = 40 * 1024 * 1024
MATMUL_TILE_CAP = 2048
MXU_FLOPS = 9.0e14
HBM_BYTES_PER_S = 3.0e12
STEP_OVERHEAD_S = 0.35e-6

ADAM_LR, ADAM_B1, ADAM_B2, ADAM_EPS, ADAM_WD, ADAM_STEP = 0.001, 0.9, 0.999, 1e-08, 0.01, 10

MESH = pl.DeviceIdType.MESH


def _params(sem=None):
    return pltpu.CompilerParams(dimension_semantics=sem, vmem_limit_bytes=VMEM_LIMIT)


def _divisors(n, cap):
    if n <= cap:
        return [n]
    return [d for d in range(cap - cap % LANE, 0, -LANE) if n % d == 0]


def _matmul_tiles(m, n, k, out_bytes, n_out, n_extra):
    best = None
    for tm in _divisors(m, MATMUL_TILE_CAP):
        for tn in _divisors(n, MATMUL_TILE_CAP):
            for tk in _divisors(k, MATMUL_TILE_CAP):
                out_tile = tm * tn * (out_bytes * n_out + 2 * n_extra)
                vmem = 2 * 2 * (tm * tk + tk * tn) + 2 * out_tile + (4 * tm * tn if tk < k else 0)
                if vmem > MATMUL_VMEM_BUDGET:
                    continue
                steps = (m // tm) * (n // tn) * (k // tk)
                a_reads = 1 if tk == k else n // tn
                traffic = 2 * (a_reads * m * k + (m // tm) * k * n) + m * n * (out_bytes * n_out + 2 * n_extra)
                est = (max(2.0 * m * n * k / MXU_FLOPS, traffic / HBM_BYTES_PER_S) + steps * STEP_OVERHEAD_S
                       + (2 * (tm * tk + tk * tn) + out_tile) / HBM_BYTES_PER_S)
                cand = (est, tm, tn, tk)
                if best is None or cand < best:
                    best = cand
    assert best is not None, (m, n, k)
    return best[1:]


def _tile(n, target):
    if n <= target:
        return n
    best = None
    for d in range(LANE, target + 1, LANE):
        if n % d == 0:
            best = d
    assert best is not None, (n, target)
    return best


def _slot(p):
    return 4 * p[0] + 2 * p[1] + p[2]


def _all_gather(xs, name):
    comm = _GatherComm(xs)

    def body(*refs):
        begin, end = comm.ops(refs[:comm.n_in], refs[comm.n_in:comm.n_in + comm.n_out], refs[-3:])
        begin()
        end()

    any_spec = pl.BlockSpec(memory_space=pl.ANY)
    return pl.pallas_call(
        body, name=name, out_shape=comm.out_shape,
        in_specs=[any_spec] * comm.n_in, out_specs=[any_spec] * comm.n_out, scratch_shapes=comm.sems,
    )(*comm.operands)


class _GatherComm:
    def __init__(self, xs):
        n = len(xs)
        self.operands = list(xs)
        self.n_in = self.n_out = n
        self.aliases = {}
        self.out_shape = [jax.ShapeDtypeStruct((N_DEV,) + x.shape, x.dtype) for x in xs]
        self.sems = [pltpu.SemaphoreType.DMA((7 * n,)), pltpu.SemaphoreType.DMA((7 * n,)),
                     pltpu.SemaphoreType.DMA((n,))]

    def ops(self, x_refs, o_refs, sem_refs):
        n = self.n_in
        send, recv, loc = sem_refs
        x, y, c = lax.axis_index("x"), lax.axis_index("y"), lax.axis_index("c")
        me, sib = (x, y, c), (x, y, 1 - c)
        chips = [(1 - x, y), (x, 1 - y), (1 - x, 1 - y)]

        def cp(a, k, block, to, src=None):
            dst = o_refs[a].at[_slot(block)]
            return pltpu.make_async_remote_copy(
                src_ref=dst if src is None else src, dst_ref=dst,
                send_sem=send.at[7 * a + k], recv_sem=recv.at[7 * a + k],
                device_id=to, device_id_type=MESH)

        mine = [pltpu.make_async_copy(x_refs[a], o_refs[a].at[_slot(me)], loc.at[a]) for a in range(n)]
        first = []
        for a in range(n):
            first.append(cp(a, 0, me, sib, src=x_refs[a]))
            first += [cp(a, 1 + j, me, (*chip, c), src=x_refs[a]) for j, chip in enumerate(chips)]

        def begin():
            for m in mine:
                m.start()
            for f in first:
                f.start()

        def end():
            passed = []
            for j, chip in enumerate(chips):
                for a in range(n):
                    cp(a, 1 + j, (*chip, c), me).wait_recv()
                    p = cp(a, 4 + j, (*chip, c), sib)
                    p.start()
                    passed.append(p)
            for a in range(n):
                cp(a, 0, sib, me).wait_recv()
                for j, chip in enumerate(chips):
                    cp(a, 4 + j, (*chip, 1 - c), me).wait_recv()
            for f in first + passed:
                f.wait_send()
            for m in mine:
                m.wait()

        return begin, end


class _ExchangeComm:
    def __init__(self, xs, bufs, layer, row0=0):
        n = len(xs)
        self.layers = list(layer) if isinstance(layer, (list, tuple)) else [layer] * n
        self.row0 = row0
        self.nrows = [x.shape[1] for x in xs]
        self.operands = list(xs) + list(bufs)
        self.n_in, self.n_out = 2 * n, n
        self.aliases = {n + a: a for a in range(n)}
        self.out_shape = [jax.ShapeDtypeStruct(b.shape, b.dtype) for b in bufs]
        self.sems = [pltpu.SemaphoreType.DMA((7 * n,)), pltpu.SemaphoreType.DMA((7 * n,)),
                     pltpu.SemaphoreType.DMA((n,))]

    def ops(self, in_refs, o_refs, sem_refs):
        n = self.n_out
        x_refs = in_refs[:n]
        send, recv, loc = sem_refs
        x, y, c = lax.axis_index("x"), lax.axis_index("y"), lax.axis_index("c")
        me = (x, y, c)
        peers = []
        for k in range(1, 8):
            kx, ky, kc = (k >> 2) & 1, (k >> 1) & 1, k & 1
            peers.append((x + kx - 2 * x * kx, y + ky - 2 * y * ky, c + kc - 2 * c * kc))

        def land(a, slot):
            return o_refs[a].at[slot, self.layers[a], pl.ds(self.row0, self.nrows[a])]

        def cp(a, k, src_slot, dst_slot, to):
            return pltpu.make_async_remote_copy(
                src_ref=x_refs[a].at[src_slot], dst_ref=land(a, dst_slot),
                send_sem=send.at[7 * a + k], recv_sem=recv.at[7 * a + k],
                device_id=to, device_id_type=MESH)

        mine = [pltpu.make_async_copy(x_refs[a].at[_slot(me)], land(a, _slot(me)), loc.at[a])
                for a in range(n)]
        sends = [cp(a, k, _slot(p), _slot(me), p) for a in range(n) for k, p in enumerate(peers)]

        def begin():
            for m in mine:
                m.start()
            for s in sends:
                s.start()

        def end():
            for a in range(n):
                for k, p in enumerate(peers):
                    cp(a, k, _slot(me), _slot(p), me).wait_recv()
            for s in sends:
                s.wait_send()
            for m in mine:
                m.wait()

        return begin, end


def _exchange(xs, bufs, layer, name, row0=0):
    comm = _ExchangeComm(xs, bufs, layer, row0)

    def body(*refs):
        begin, end = comm.ops(refs[:comm.n_in], refs[comm.n_in:comm.n_in + comm.n_out], refs[-3:])
        begin()
        end()

    any_spec = pl.BlockSpec(memory_space=pl.ANY)
    return pl.pallas_call(
        body, name=name, out_shape=comm.out_shape,
        in_specs=[any_spec] * comm.n_in, out_specs=[any_spec] * comm.n_out, scratch_shapes=comm.sems,
        input_output_aliases=dict(comm.aliases),
    )(*comm.operands)


_DIMS = {"nn": (((1,), (0,)), ((), ())), "nt": (((1,), (1,)), ((), ())), "tn": (((0,), (0,)), ((), ()))}


def _matmul(a, b, mode, out_dtype, name, comm=None, epi=None, extra=None):
    if mode == "nn":
        (m, k), (_, n) = a.shape, b.shape
    elif mode == "nt":
        (m, k), (n, _) = a.shape, b.shape
    else:
        (k, m), (_, n) = a.shape, b.shape
    n_ext = 1 if epi == "relu2_bwd" else 0
    n_main = 2 if epi == "relu2" else 1
    tm, tn, tk = _matmul_tiles(m, n, k, jnp.dtype(out_dtype).itemsize, n_main, n_ext)
    nk = k // tk
    grid = (m // tm, n // tn, nk)
    dn = _DIMS[mode]
    n_cin = comm.n_in if comm else 0
    n_cout = comm.n_out if comm else 0
    n_acc = 0 if nk == 1 else 1
    first_in, first_out = 2 + n_ext, 2 + n_ext + n_cin

    def finish(acc, refs):
        outs = refs[first_out:first_out + n_main]
        if epi == "relu2":
            outs[0][...] = acc.astype(outs[0].dtype)
            r = jnp.maximum(acc, 0.0)
            outs[1][...] = (r * r).astype(outs[1].dtype)
        elif epi == "relu2_bwd":
            outs[0][...] = (acc * (2.0 * jnp.maximum(refs[2][...].astype(F32), 0.0))).astype(outs[0].dtype)
        else:
            outs[0][...] = acc.astype(outs[0].dtype)

    def body(*refs):
        a_ref, b_ref = refs[:2]
        scratch = refs[first_out + n_main + n_cout:]
        pid = [pl.program_id(ax) for ax in range(3)]
        if comm:
            begin, end = comm.ops(refs[first_in:first_in + n_cin],
                                  refs[first_out + n_main:first_out + n_main + n_cout], scratch[n_acc:])
            pl.when((pid[0] == 0) & (pid[1] == 0) & (pid[2] == 0))(begin)
        part = lax.dot_general(a_ref[...], b_ref[...], dn, preferred_element_type=F32)
        if nk == 1:
            finish(part, refs)
        else:
            acc_ref = scratch[0]

            @pl.when(pid[2] == 0)
            def _():
                acc_ref[...] = part

            @pl.when(pid[2] > 0)
            def _():
                acc_ref[...] += part

            @pl.when(pid[2] == nk - 1)
            def _():
                finish(acc_ref[...], refs)
        if comm:
            pl.when((pid[0] == grid[0] - 1) & (pid[1] == grid[1] - 1) & (pid[2] == grid[2] - 1))(end)

    a_spec = pl.BlockSpec((tk, tm), lambda i, j, kk: (kk, i)) if mode == "tn" else \
        pl.BlockSpec((tm, tk), lambda i, j, kk: (i, kk))
    b_spec = pl.BlockSpec((tn, tk), lambda i, j, kk: (j, kk)) if mode == "nt" else \
        pl.BlockSpec((tk, tn), lambda i, j, kk: (kk, j))
    o_spec = pl.BlockSpec((tm, tn), lambda i, j, kk: (i, j))
    any_spec = pl.BlockSpec(memory_space=pl.ANY)
    res = pl.pallas_call(
        body, name=name,
        out_shape=[jax.ShapeDtypeStruct((m, n), out_dtype)] * n_main + (comm.out_shape if comm else []),
        grid=grid,
        in_specs=[a_spec, b_spec] + [o_spec] * n_ext + [any_spec] * n_cin,
        out_specs=[o_spec] * n_main + [any_spec] * n_cout,
        scratch_shapes=([] if nk == 1 else [pltpu.VMEM((tm, tn), F32)]) + (comm.sems if comm else []),
        input_output_aliases={first_in + i: n_main + o for i, o in comm.aliases.items()} if comm else {},
        compiler_params=_params(("arbitrary",) * 3 if comm else ("parallel", "parallel", "arbitrary")),
    )(a, b, *([extra] if n_ext else []), *(comm.operands if comm else []))
    main = res[0] if n_main == 1 else tuple(res[:n_main])
    return (main, list(res[n_main:])) if comm else main


def _dot(a, b, mode):
    return lax.dot_general(a.astype(BF16), b.astype(BF16), _DIMS[mode], preferred_element_type=F32)


@functools.partial(jax.custom_vjp, nondiff_argnums=(2,))
def _bdot(a, b, mode):
    return _dot(a, b, mode)


def _bdot_fwd(a, b, mode):
    return _dot(a, b, mode), (a, b)


def _bdot_bwd(mode, res, g):
    a, b = res
    if mode == "nn":
        return _dot(g, b, "nt"), _dot(a, g, "tn")
    if mode == "nt":
        return _dot(g, b, "nn"), _dot(g, a, "tn")
    return _dot(b, g, "nt"), _dot(a, g, "nn")


_bdot.defvjp(_bdot_fwd, _bdot_bwd)


def _split3(x):
    hi = x.astype(BF16)
    r1 = x - hi.astype(F32)
    mid = r1.astype(BF16)
    return hi, mid, (r1 - mid.astype(F32)).astype(BF16)


def _xdot01(m01, x):
    return sum(lax.dot_general(m01, p, _DIMS["nn"], preferred_element_type=F32) for p in _split3(x))


@jax.custom_vjp
def _spread(v, m01):
    return sum(lax.dot_general(p, m01, _DIMS["nn"], preferred_element_type=F32) for p in _split3(v))


def _spread_fwd(v, m01):
    return _spread(v, m01), m01


def _spread_bwd(m01, g):
    dv = sum(lax.dot_general(p, m01, _DIMS["nt"], preferred_element_type=F32) for p in _split3(g)[:2])
    return dv, jnp.zeros_like(m01)


_spread.defvjp(_spread_fwd, _spread_bwd)


def _chunk_tri(n, lower):
    r = lax.broadcasted_iota(jnp.int32, (n, n), 0)
    c = lax.broadcasted_iota(jnp.int32, (n, n), 1)
    tri = (r >= c) if lower else (r <= c)
    return jnp.where(tri & (r // SSD_Q == c // SSD_Q), 1.0, 0.0).astype(BF16)


def _shift_down(x, prev8, k):
    if k == 0:
        return x
    n = x.shape[0]
    r = pltpu.roll(x, k, 0)
    rp = pltpu.roll(prev8, k, 0)
    rows = lax.broadcasted_iota(jnp.int32, (8, x.shape[1]), 0)
    head = jnp.where(rows < k, rp, r[:8])
    return head if n == 8 else jnp.concatenate([head, r[8:]], axis=0)


def _shift_up(x, next8, k):
    if k == 0:
        return x
    n = x.shape[0]
    r = pltpu.roll(x, n - k, 0)
    rn = pltpu.roll(next8, 8 - k, 0)
    rows = lax.broadcasted_iota(jnp.int32, (8, x.shape[1]), 0)
    tail = jnp.where(rows >= 8 - k, rn, r[n - 8:])
    return tail if n == 8 else jnp.concatenate([r[:n - 8], tail], axis=0)


def _sigmoid(x):
    return 0.5 * jnp.tanh(0.5 * x) + 0.5


def _rows(ref, i):
    return ref[i:i + 1, :]


def _norm_parts(x, mp_ref, which):
    ln, sh, sc = _rows(mp_ref, 6 + which), _rows(mp_ref, 3 * which), _rows(mp_ref, 3 * which + 1)
    r = lax.rsqrt(jnp.mean(x * x, axis=1, keepdims=True) + EPS)
    return r, ln, sh, sc


def _norm_fwd(x, modp, which, name):
    t, d = x.shape

    def body(x_ref, mp_ref, u_ref):
        xv = x_ref[...]
        r, ln, sh, sc = _norm_parts(xv, mp_ref, which)
        u_ref[...] = (((xv * r) * ln) * (1.0 + sc) + sh).astype(BF16)

    row = pl.BlockSpec((ROW_TILE, d), lambda i: (i, 0))
    return pl.pallas_call(
        body, name=name, out_shape=jax.ShapeDtypeStruct((t, d), BF16), grid=(t // ROW_TILE,),
        in_specs=[row, pl.BlockSpec((8, d), lambda i: (0, 0))], out_specs=row,
        compiler_params=_params(("parallel",)))(x, modp)


def _resid_norm_fwd(x, br, mp_gate, gate_row, mp_norm, which, name):
    t, d = x.shape

    def body(x_ref, br_ref, mg_ref, mn_ref, xn_ref, u_ref):
        xv = x_ref[...] + _rows(mg_ref, gate_row) * br_ref[...]
        xn_ref[...] = xv
        r, ln, sh, sc = _norm_parts(xv, mn_ref, which)
        u_ref[...] = (((xv * r) * ln) * (1.0 + sc) + sh).astype(BF16)

    row = pl.BlockSpec((ROW_TILE, d), lambda i: (i, 0))
    mp = pl.BlockSpec((8, d), lambda i: (0, 0))
    return pl.pallas_call(
        body, name=name,
        out_shape=[jax.ShapeDtypeStruct((t, d), F32), jax.ShapeDtypeStruct((t, d), BF16)],
        grid=(t // ROW_TILE,), in_specs=[row, row, mp, mp], out_specs=[row, row],
        compiler_params=_params(("parallel",)))(x, br, mp_gate, mp_norm)


def _final_fwd_bwd(x, br, mp_gate, fnorm8, target, name):
    t, d = x.shape

    def body(x_ref, br_ref, mg_ref, fn_ref, tg_ref, dx_ref, dbr_ref, acc_ref):
        gate = _rows(mg_ref, 5)
        brv = br_ref[...]
        xv = x_ref[...] + gate * brv
        fn = _rows(fn_ref, 0)
        r = lax.rsqrt(jnp.mean(xv * xv, axis=1, keepdims=True) + EPS)
        nrm = xv * r
        err = nrm * fn - tg_ref[...]
        loss = 0.5 * jnp.sum(jnp.mean(err * err, axis=1, keepdims=True), axis=0, keepdims=True)
        dy = err * (1.0 / d)
        dn = dy * fn
        dx = r * (dn - nrm * jnp.mean(dn * nrm, axis=1, keepdims=True))
        dx_ref[...] = dx
        dbr_ref[...] = (dx * gate).astype(BF16)
        upd = jnp.concatenate([
            jnp.broadcast_to(loss, (1, d)),
            jnp.sum(dy * nrm, axis=0, keepdims=True),
            jnp.sum(dx * brv, axis=0, keepdims=True),
            jnp.zeros((5, d), F32)], axis=0)

        @pl.when(pl.program_id(0) == 0)
        def _():
            acc_ref[...] = upd

        @pl.when(pl.program_id(0) > 0)
        def _():
            acc_ref[...] += upd

    row = pl.BlockSpec((ROW_TILE, d), lambda i: (i, 0))
    mp = pl.BlockSpec((8, d), lambda i: (0, 0))
    return pl.pallas_call(
        body, name=name,
        out_shape=[jax.ShapeDtypeStruct((t, d), F32), jax.ShapeDtypeStruct((t, d), BF16),
                   jax.ShapeDtypeStruct((8, d), F32)],
        grid=(t // ROW_TILE,), in_specs=[row, row, mp, mp, row], out_specs=[row, row, mp],
        compiler_params=_params(("arbitrary",)))(x, br, mp_gate, fnorm8, target)


def _resid_norm_bwd(dx, du, x, mp_norm, which, name, br=None, mp_gate=None, gate_row=None):
    t, d = x.shape
    has_gate = br is not None

    def body(*refs):
        if has_gate:
            dx_ref, du_ref, x_ref, mn_ref, br_ref, mg_ref, dxn_ref, dbr_ref, acc_ref = refs
        else:
            dx_ref, du_ref, x_ref, mn_ref, dxn_ref, acc_ref = refs
        xv, duv = x_ref[...], du_ref[...]
        r, ln, _, sc = _norm_parts(xv, mn_ref, which)
        nrm = xv * r
        dn = duv * (ln * (1.0 + sc))
        dxn = dx_ref[...] + r * (dn - nrm * jnp.mean(dn * nrm, axis=1, keepdims=True))
        dxn_ref[...] = dxn
        rows = [jnp.sum(duv, axis=0, keepdims=True), jnp.sum(duv * nrm, axis=0, keepdims=True)]
        if has_gate:
            dbr_ref[...] = (dxn * _rows(mg_ref, gate_row)).astype(BF16)
            rows.append(jnp.sum(dxn * br_ref[...], axis=0, keepdims=True))
        upd = jnp.concatenate(rows + [jnp.zeros((8 - len(rows), d), F32)], axis=0)

        @pl.when(pl.program_id(0) == 0)
        def _():
            acc_ref[...] = upd

        @pl.when(pl.program_id(0) > 0)
        def _():
            acc_ref[...] += upd

    row = pl.BlockSpec((ROW_TILE, d), lambda i: (i, 0))
    mp = pl.BlockSpec((8, d), lambda i: (0, 0))
    ins, in_specs = [dx, du, x, mp_norm], [row, row, row, mp]
    outs = [jax.ShapeDtypeStruct((t, d), F32)]
    out_specs = [row]
    if has_gate:
        ins += [br, mp_gate]
        in_specs += [row, mp]
        outs.append(jax.ShapeDtypeStruct((t, d), BF16))
        out_specs.append(row)
    outs.append(jax.ShapeDtypeStruct((8, d), F32))
    out_specs.append(mp)
    return pl.pallas_call(
        body, name=name, out_shape=outs, grid=(t // ROW_TILE,), in_specs=in_specs, out_specs=out_specs,
        compiler_params=_params(("arbitrary",)))(*ins)


def _merge_fwd(proj, p_conv, p_ssm, name):
    t, d = p_conv.shape

    def body(gl_ref, pc_ref, ps_ref, o_ref):
        g = _sigmoid(gl_ref[...].astype(F32))
        o_ref[...] = (g[:, :d] * pc_ref[...] + g[:, d:] * ps_ref[...]).astype(BF16)

    row = pl.BlockSpec((ROW_TILE, d), lambda i: (i, 0))
    return pl.pallas_call(
        body, name=name, out_shape=jax.ShapeDtypeStruct((t, d), BF16), grid=(t // ROW_TILE,),
        in_specs=[pl.BlockSpec((ROW_TILE, 2 * d), lambda i: (i, 0)), row, row], out_specs=row,
        compiler_params=_params(("parallel",)))(proj, p_conv, p_ssm)


def _merge_bwd(dmerged, proj, p_conv, p_ssm, name):
    t, d = p_conv.shape

    def body(dm_ref, gl_ref, pc_ref, ps_ref, dpc_ref, dps_ref, dgl_ref):
        g = _sigmoid(gl_ref[...].astype(F32))
        gc, gs = g[:, :d], g[:, d:]
        dm = dm_ref[...]
        dpc_ref[...] = (dm * gc).astype(BF16)
        dps_ref[...] = (dm * gs).astype(BF16)
        dgl_ref[...] = jnp.concatenate(
            [dm * pc_ref[...] * gc * (1.0 - gc), dm * ps_ref[...] * gs * (1.0 - gs)], axis=1).astype(BF16)

    row = pl.BlockSpec((ROW_TILE, d), lambda i: (i, 0))
    wide = pl.BlockSpec((ROW_TILE, 2 * d), lambda i: (i, 0))
    return pl.pallas_call(
        body, name=name,
        out_shape=[jax.ShapeDtypeStruct((t, d), BF16), jax.ShapeDtypeStruct((t, d), BF16),
                   jax.ShapeDtypeStruct((t, 2 * d), BF16)],
        grid=(t // ROW_TILE,), in_specs=[row, wide, row, row], out_specs=[row, row, wide],
        compiler_params=_params(("parallel",)))(dmerged, proj, p_conv, p_ssm)


def _halo_specs(t, width, col):
    nb = t // 8
    step = ROW_TILE // 8
    prev = pl.BlockSpec((8, width), lambda i: (jnp.maximum(i * step - 1, 0), col))
    nxt = pl.BlockSpec((8, width), lambda i: (jnp.minimum((i + 1) * step, nb - 1), col))
    return prev, nxt


def _gconv_fwd(proj, conv_w8, d, name):
    t = proj.shape[0]

    def body(cb_ref, cc_ref, cx_ref, ccp_ref, cxp_ref, w_ref, o_ref):
        first = pl.program_id(0) == 0
        v = cc_ref[...].astype(F32) * cx_ref[...].astype(F32)
        vp = jnp.where(first, 0.0, ccp_ref[...].astype(F32) * cxp_ref[...].astype(F32))
        cv = sum(_rows(w_ref, k) * _shift_down(v, vp, 2 - k) for k in range(3))
        o_ref[...] = (cb_ref[...].astype(F32) * cv).astype(BF16)

    def win(col):
        return pl.BlockSpec((ROW_TILE, d), lambda i: (i, col))

    return pl.pallas_call(
        body, name=name, out_shape=jax.ShapeDtypeStruct((t, d), BF16), grid=(t // ROW_TILE,),
        in_specs=[win(2), win(3), win(4), _halo_specs(t, d, 3)[0], _halo_specs(t, d, 4)[0],
                  pl.BlockSpec((8, d), lambda i: (0, 0))],
        out_specs=pl.BlockSpec((ROW_TILE, d), lambda i: (i, 0)),
        compiler_params=_params(("parallel",)))(proj, proj, proj, proj, proj, conv_w8)


def _gconv_bwd(dy, proj, conv_w8, dgl, d, name):
    t = proj.shape[0]
    nt = t // ROW_TILE

    def body(dy_ref, dyn_ref, cb_ref, cc_ref, cx_ref, ccp_ref, cxp_ref, cbn_ref, w_ref, dgl_ref, dp_ref, dw_ref):
        i = pl.program_id(0)
        cb, cc, cx = cb_ref[...].astype(F32), cc_ref[...].astype(F32), cx_ref[...].astype(F32)
        v = cc * cx
        vp = jnp.where(i == 0, 0.0, ccp_ref[...].astype(F32) * cxp_ref[...].astype(F32))
        sh = [_shift_down(v, vp, 2 - k) for k in range(3)]
        cv = sum(_rows(w_ref, k) * sh[k] for k in range(3))
        dyv = dy_ref[...]
        dcv = dyv * cb
        dcvn = jnp.where(i == nt - 1, 0.0, dyn_ref[...] * cbn_ref[...].astype(F32))
        dv = sum(_rows(w_ref, k) * _shift_up(dcv, dcvn, 2 - k) for k in range(3))
        dp_ref[:, :2 * d] = dgl_ref[...]
        dp_ref[:, 2 * d:3 * d] = (dyv * cv).astype(BF16)
        dp_ref[:, 3 * d:4 * d] = (dv * cx).astype(BF16)
        dp_ref[:, 4 * d:] = (dv * cc).astype(BF16)
        upd = jnp.concatenate([jnp.sum(dcv * sh[k], axis=0, keepdims=True) for k in range(3)]
                              + [jnp.zeros((5, d), F32)], axis=0)

        @pl.when(i == 0)
        def _():
            dw_ref[...] = upd

        @pl.when(i > 0)
        def _():
            dw_ref[...] += upd

    def win(col):
        return pl.BlockSpec((ROW_TILE, d), lambda i: (i, col))

    row = pl.BlockSpec((ROW_TILE, d), lambda i: (i, 0))
    w8 = pl.BlockSpec((8, d), lambda i: (0, 0))
    return pl.pallas_call(
        body, name=name,
        out_shape=[jax.ShapeDtypeStruct(proj.shape, BF16), jax.ShapeDtypeStruct((8, d), F32)],
        grid=(nt,),
        in_specs=[row, _halo_specs(t, d, 0)[1], win(2), win(3), win(4), _halo_specs(t, d, 3)[0],
                  _halo_specs(t, d, 4)[0], _halo_specs(t, d, 2)[1], w8,
                  pl.BlockSpec((ROW_TILE, 2 * d), lambda i: (i, 0))],
        out_specs=[pl.BlockSpec((ROW_TILE, 5 * d), lambda i: (i, 0)), w8],
        compiler_params=_params(("arbitrary",)))(
            dy, dy, proj, proj, proj, proj, proj, proj, conv_w8, dgl)


def _dsilu(p):
    s = _sigmoid(p)
    return s * (1.0 + p * (1.0 - s))


def _sconv_specs(t, d, col0, nt):
    nb, step = t // 8, ROW_TILE // 8
    win = pl.BlockSpec((ROW_TILE, d), lambda j, i: (i, col0 + j))
    prev = pl.BlockSpec((8, d), lambda j, i: (jnp.maximum(i * step - 1, 0), col0 + j))
    nxt = pl.BlockSpec((8, d), lambda j, i: (jnp.minimum((i + 1) * step, nb - 1), col0 + j))
    return win, prev, nxt


def _sconv_fwd(proj, w8, b8, d, col0, ncol, name):
    t = proj.shape[0]
    nt = t // ROW_TILE

    def body(x_ref, xp_ref, w_ref, b_ref, o_ref, pre_ref):
        xv = x_ref[...].astype(F32)
        xp = jnp.where(pl.program_id(1) == 0, 0.0, xp_ref[...].astype(F32))
        pre = _rows(b_ref, 0) + sum(_rows(w_ref, k) * _shift_down(xv, xp, 3 - k) for k in range(4))
        o_ref[...] = (pre * _sigmoid(pre)).astype(BF16)
        pre_ref[...] = pre.astype(BF16)

    win, prev, _ = _sconv_specs(t, d, col0, nt)
    par = pl.BlockSpec((8, d), lambda j, i: (0, j))
    out = pl.BlockSpec((ROW_TILE, d), lambda j, i: (i, j))
    return pl.pallas_call(
        body, name=name, out_shape=[jax.ShapeDtypeStruct((t, ncol * d), BF16)] * 2, grid=(ncol, nt),
        in_specs=[win, prev, par, par], out_specs=[out, out],
        compiler_params=_params(("parallel", "parallel")))(proj, proj, w8, b8)


def _sconv_bwd(dxc, pre, pre_col0, proj, w8, dproj, d, col0, ncol, name):
    t = proj.shape[0]
    nt = t // ROW_TILE

    def body(d_ref, dn_ref, p_ref, pn_ref, x_ref, w_ref, _, dp_ref, dw_ref):
        i = pl.program_id(1)
        dpre = d_ref[...] * _dsilu(p_ref[...].astype(F32))
        dpre_n = jnp.where(i == nt - 1, 0.0, dn_ref[...] * _dsilu(pn_ref[...].astype(F32)))
        up = [_shift_up(dpre, dpre_n, 3 - k) for k in range(4)]
        dp_ref[...] = sum(_rows(w_ref, k) * up[k] for k in range(4)).astype(BF16)
        xv = x_ref[...].astype(F32)
        upd = jnp.concatenate([jnp.sum(xv * up[k], axis=0, keepdims=True) for k in range(4)]
                              + [jnp.sum(dpre, axis=0, keepdims=True), jnp.zeros((3, d), F32)], axis=0)

        @pl.when(i == 0)
        def _():
            dw_ref[...] = upd

        @pl.when(i > 0)
        def _():
            dw_ref[...] += upd

    win, _, _ = _sconv_specs(t, d, col0, nt)
    nb, step = t // 8, ROW_TILE // 8
    dwin = pl.BlockSpec((ROW_TILE, d), lambda j, i: (i, j))
    dnxt = pl.BlockSpec((8, d), lambda j, i: (jnp.minimum((i + 1) * step, nb - 1), j))
    par = pl.BlockSpec((8, d), lambda j, i: (0, j))
    return pl.pallas_call(
        body, name=name,
        out_shape=[jax.ShapeDtypeStruct(dproj.shape, BF16), jax.ShapeDtypeStruct((8, ncol * d), F32)],
        grid=(ncol, nt),
        in_specs=[dwin, dnxt,
                  pl.BlockSpec((ROW_TILE, d), lambda j, i: (i, pre_col0 + j)),
                  pl.BlockSpec((8, d), lambda j, i: (jnp.minimum((i + 1) * step, nb - 1), pre_col0 + j)),
                  win, par, pl.BlockSpec(memory_space=pl.ANY)],
        out_specs=[win, par], input_output_aliases={6: 0},
        compiler_params=_params(("arbitrary", "arbitrary")))(dxc, dxc, pre, pre, proj, w8, dproj)


def _softplus(x):
    return jnp.maximum(x, 0.0) + jnp.log(1.0 + jnp.exp(-jnp.abs(x)))


def _dt_fwd(proj, par8, nheads, dt_col, name):
    t = proj.shape[0]

    def body(r_ref, p_ref, dt_ref, ac_ref):
        lane = lax.broadcasted_iota(jnp.int32, (1, LANE), 1)
        valid = lane < nheads
        raw = r_ref[...][:, :LANE].astype(F32)
        dt = jnp.where(valid, _softplus(raw + _rows(p_ref, 0)), 0.0)
        a = jnp.where(valid, -jnp.exp(_rows(p_ref, 1)), 0.0)
        dt_ref[...] = dt
        ac_ref[...] = _xdot01(_chunk_tri(ROW_TILE, True), dt * a)

    out = pl.BlockSpec((ROW_TILE, LANE), lambda i: (i, 0))
    return pl.pallas_call(
        body, name=name,
        out_shape=[jax.ShapeDtypeStruct((t, LANE), F32), jax.ShapeDtypeStruct((t, LANE), F32)],
        grid=(t // ROW_TILE,),
        in_specs=[pl.BlockSpec((ROW_TILE, 2 * LANE), lambda i: (i, dt_col)), pl.BlockSpec((8, LANE), lambda i: (0, 0))],
        out_specs=[out, out], compiler_params=_params(("parallel",)))(proj, par8)


def _dt_bwd(ddt_g, dac_g, dt, proj, par8, dproj, nheads, dt_col, name):
    t = proj.shape[0]
    ng = ddt_g.shape[0]

    def body(dd_ref, da_ref, dt_ref, r_ref, p_ref, _, dp_ref, acc_ref):
        lane = lax.broadcasted_iota(jnp.int32, (1, LANE), 1)
        valid = lane < nheads
        ddt = sum(dd_ref[g] for g in range(ng))
        dac = sum(da_ref[g] for g in range(ng))
        a = jnp.where(valid, -jnp.exp(_rows(p_ref, 1)), 0.0)
        d_a = _xdot01(_chunk_tri(ROW_TILE, False), dac)
        ddt = ddt + d_a * a
        raw = r_ref[...][:, :LANE].astype(F32)
        draw = jnp.where(valid, ddt * _sigmoid(raw + _rows(p_ref, 0)), 0.0)
        dp_ref[...] = jnp.concatenate([draw, jnp.zeros_like(draw)], axis=1).astype(BF16)
        upd = jnp.concatenate([jnp.sum(draw, axis=0, keepdims=True),
                               jnp.sum(d_a * dt_ref[...], axis=0, keepdims=True) * a,
                               jnp.zeros((6, LANE), F32)], axis=0)

        @pl.when(pl.program_id(0) == 0)
        def _():
            acc_ref[...] = upd

        @pl.when(pl.program_id(0) > 0)
        def _():
            acc_ref[...] += upd

    row = pl.BlockSpec((ROW_TILE, LANE), lambda i: (i, 0))
    grp = pl.BlockSpec((ng, ROW_TILE, LANE), lambda i: (0, i, 0))
    win = pl.BlockSpec((ROW_TILE, 2 * LANE), lambda i: (i, dt_col))
    par = pl.BlockSpec((8, LANE), lambda i: (0, 0))
    return pl.pallas_call(
        body, name=name,
        out_shape=[jax.ShapeDtypeStruct(dproj.shape, BF16), jax.ShapeDtypeStruct((8, LANE), F32)],
        grid=(t // ROW_TILE,),
        in_specs=[grp, grp, row, win, par, pl.BlockSpec(memory_space=pl.ANY)],
        out_specs=[win, par], input_output_aliases={5: 0},
        compiler_params=_params(("arbitrary",)))(ddt_g, dac_g, dt, proj, par8, dproj)


def _ssd_chunk(xs, bm, cm, z, dt, ac, st, dsk, nw, g):
    q = xs.shape[0]
    lane_h = lax.broadcasted_iota(jnp.int32, (1, LANE), 1)
    sub_h = lax.broadcasted_iota(jnp.int32, (LANE, 1), 0)
    lane_c = lax.broadcasted_iota(jnp.int32, (1, GROUP_W), 1) // HEAD_DIM
    rr = lax.broadcasted_iota(jnp.int32, (q, q), 0)
    cc = lax.broadcasted_iota(jnp.int32, (q, q), 1)
    tril = rr >= cc
    last_row = jnp.where(lax.broadcasted_iota(jnp.int32, (q, 1), 0) == q - 1, 1.0, 0.0)
    act = ac.T
    col, row = [], []
    for r in range(HEADS_PER_GROUP):
        h = g * HEADS_PER_GROUP + r
        col.append(jnp.sum(ac * jnp.where(lane_h == h, 1.0, 0.0), axis=1, keepdims=True))
        row.append(jnp.sum(act * jnp.where(sub_h == h, 1.0, 0.0), axis=0, keepdims=True))

    def per_head(vals):
        out = jnp.broadcast_to(vals[0], (q, GROUP_W))
        for r in range(1, HEADS_PER_GROUP):
            out = jnp.where(lane_c == r, vals[r], out)
        return out

    spread = jnp.where(lax.broadcasted_iota(jnp.int32, (LANE, GROUP_W), 0)
                       == g * HEADS_PER_GROUP + lax.broadcasted_iota(jnp.int32, (LANE, GROUP_W), 1) // HEAD_DIM,
                       1.0, 0.0).astype(BF16)
    acx = _spread(ac, spread)
    x = xs * _spread(dt, spread)
    s = _bdot(cm, bm, "nt")
    y = per_head([_bdot(s * jnp.exp(jnp.where(tril, col[r] - row[r], -jnp.inf)), x, "nn")
                  for r in range(HEADS_PER_GROUP)])
    y = y + jnp.exp(acx) * _bdot(cm, st, "nn")
    last = jnp.sum(acx * last_row, axis=0, keepdims=True)
    new_st = st * jnp.exp(last) + _bdot(bm, x * jnp.exp(last - acx), "tn")
    y = y + dsk * xs
    y = y * (z * _sigmoid(z))
    yn = y * lax.rsqrt(jnp.mean(y * y, axis=1, keepdims=True) + EPS) * nw
    return yn, new_st


def _ssd_specs(nc, z_col, ngroups, rev):
    assert ngroups % SSD_GROUPS == 0 and z_col % SSD_GROUPS == 0
    z_col //= SSD_GROUPS
    b_col, c_col = 2 * ngroups // SSD_GROUPS, 3 * ngroups // SSD_GROUPS

    def ci(c):
        return nc - 1 - c if rev else c

    nx, nb = SSD_GROUPS * GROUP_W, SSD_GROUPS * N_STATE
    xs = pl.BlockSpec((SSD_Q, nx), lambda g, c: (ci(c), g))
    bm = pl.BlockSpec((SSD_Q, nb), lambda g, c: (ci(c), b_col + g))
    cm = pl.BlockSpec((SSD_Q, nb), lambda g, c: (ci(c), c_col + g))
    z = pl.BlockSpec((SSD_Q, nx), lambda g, c: (ci(c), z_col + g))
    hd = pl.BlockSpec((SSD_Q, LANE), lambda g, c: (ci(c), 0))
    par = pl.BlockSpec((8, nx), lambda g, c: (0, g))
    stb = pl.BlockSpec((None, N_STATE, nx), lambda g, c: (ci(c), 0, g))
    db = pl.BlockSpec((SSD_Q, nb), lambda g, c: (ci(c), g))
    return xs, bm, cm, z, hd, par, stb, db


def _ssd_fwd(xc, proj, dt, ac, par8, ngroups, z_col, name):
    t = xc.shape[0]
    nc = t // SSD_Q
    steps = ngroups // SSD_GROUPS

    def body(x_ref, b_ref, c_ref, z_ref, dt_ref, ac_ref, p_ref, y_ref, so_ref, st_ref):
        @pl.when(pl.program_id(1) == 0)
        def _():
            st_ref[...] = jnp.zeros_like(st_ref)

        so_ref[...] = st_ref[...]
        dtv, acv = dt_ref[...], ac_ref[...]
        for i in range(SSD_GROUPS):
            cx, cb = slice(i * GROUP_W, (i + 1) * GROUP_W), slice(i * N_STATE, (i + 1) * N_STATE)
            yn, new_st = _ssd_chunk(x_ref[:, cx].astype(F32), b_ref[:, cb].astype(F32), c_ref[:, cb].astype(F32),
                                    z_ref[:, cx].astype(F32), dtv, acv, st_ref[:, cx],
                                    p_ref[0:1, cx], p_ref[1:2, cx], pl.program_id(0) * SSD_GROUPS + i)
            y_ref[:, cx] = yn.astype(BF16)
            st_ref[:, cx] = new_st

    xs, bm, cm, zs, hd, par, stb, _ = _ssd_specs(nc, z_col, ngroups, False)
    return pl.pallas_call(
        body, name=name,
        out_shape=[jax.ShapeDtypeStruct((t, ngroups * GROUP_W), BF16),
                   jax.ShapeDtypeStruct((nc, N_STATE, ngroups * GROUP_W), F32)],
        grid=(steps, nc), in_specs=[xs, bm, cm, zs, hd, hd, par], out_specs=[xs, stb],
        scratch_shapes=[pltpu.VMEM((N_STATE, SSD_GROUPS * GROUP_W), F32)],
        compiler_params=_params(("parallel", "arbitrary")))(xc, xc, xc, proj, dt, ac, par8)


def _ssd_bwd(dyn, xc, proj, dt, ac, par8, states, dproj, ngroups, z_col, name, comm=None):
    t = xc.shape[0]
    nc = t // SSD_Q
    steps = ngroups // SSD_GROUPS

    n_cin = comm.n_in if comm else 0
    n_cout = comm.n_out if comm else 0

    def body(*refs):
        dy_ref, x_ref, b_ref, c_ref, z_ref, dt_ref, ac_ref, p_ref, s_ref = refs[:9]
        dz_ref, dx_ref, db_ref, dc_ref, ddt_ref, dac_ref, dp_ref = refs[10 + n_cin:17 + n_cin]
        ds_ref = refs[17 + n_cin + n_cout]
        c = pl.program_id(1)
        if comm:
            begin, end = comm.ops(refs[10:10 + n_cin], refs[17 + n_cin:17 + n_cin + n_cout], refs[-3:])
            pl.when((pl.program_id(0) == 0) & (c == 0))(begin)

        @pl.when(c == 0)
        def _():
            ds_ref[...] = jnp.zeros_like(ds_ref)

        dtv, acv = dt_ref[...], ac_ref[...]
        ddt_sum, dac_sum, upds = None, None, []
        for i in range(SSD_GROUPS):
            cx, cb = slice(i * GROUP_W, (i + 1) * GROUP_W), slice(i * N_STATE, (i + 1) * N_STATE)
            g = pl.program_id(0) * SSD_GROUPS + i
            args = (x_ref[:, cx].astype(F32), b_ref[:, cb].astype(F32), c_ref[:, cb].astype(F32),
                    z_ref[:, cx].astype(F32), dtv, acv, s_ref[:, cx], p_ref[0:1, cx], p_ref[1:2, cx])
            _, vjp = jax.vjp(lambda *a, g=g: _ssd_chunk(*a, g), *args)
            dxs, dbm, dcm, dz, ddt, dac, dst, ddsk, dnw = vjp((dy_ref[:, cx], ds_ref[:, cx]))
            ds_ref[:, cx] = dst
            dz_ref[:, cx] = dz.astype(BF16)
            dx_ref[:, cx] = dxs
            db_ref[:, cb] = dbm
            dc_ref[:, cb] = dcm
            ddt_sum = ddt if ddt_sum is None else ddt_sum + ddt
            dac_sum = dac if dac_sum is None else dac_sum + dac
            upds.append(jnp.concatenate([ddsk, dnw, jnp.zeros((6, GROUP_W), F32)], axis=0))
        ddt_ref[...] = ddt_sum
        dac_ref[...] = dac_sum
        upd = jnp.concatenate(upds, axis=1)

        @pl.when(c == 0)
        def _():
            dp_ref[...] = upd

        @pl.when(c > 0)
        def _():
            dp_ref[...] += upd

        if comm:
            pl.when((pl.program_id(0) == steps - 1) & (c == nc - 1))(end)

    xs, bm, cm, zs, hd, par, stb, db = _ssd_specs(nc, z_col, ngroups, True)
    hg = pl.BlockSpec((None, SSD_Q, LANE), lambda g, c: (g, nc - 1 - c, 0))
    any_spec = pl.BlockSpec(memory_space=pl.ANY)
    aliases = {9: 0}
    if comm:
        aliases.update({10 + i: 7 + o for i, o in comm.aliases.items()})
    res = pl.pallas_call(
        body, name=name,
        out_shape=[jax.ShapeDtypeStruct(dproj.shape, BF16),
                   jax.ShapeDtypeStruct((t, ngroups * GROUP_W), F32),
                   jax.ShapeDtypeStruct((t, ngroups * N_STATE), F32), jax.ShapeDtypeStruct((t, ngroups * N_STATE), F32),
                   jax.ShapeDtypeStruct((steps, t, LANE), F32), jax.ShapeDtypeStruct((steps, t, LANE), F32),
                   jax.ShapeDtypeStruct((8, ngroups * GROUP_W), F32)] + (comm.out_shape if comm else []),
        grid=(steps, nc),
        in_specs=[xs, xs, bm, cm, zs, hd, hd, par, stb, any_spec] + [any_spec] * n_cin,
        out_specs=[zs, xs, db, db, hg, hg, par] + [any_spec] * n_cout, input_output_aliases=aliases,
        scratch_shapes=[pltpu.VMEM((N_STATE, SSD_GROUPS * GROUP_W), F32)] + (comm.sems if comm else []),
        compiler_params=_params(("arbitrary", "arbitrary")))(
            dyn, xc, xc, xc, proj, dt, ac, par8, states, dproj, *(comm.operands if comm else []))
    return tuple(res[:7]), list(res[7:])


def _small_matmul(a, b, mode, name):
    if mode == "nn":
        shape = (a.shape[0], b.shape[1])
    else:
        shape = (a.shape[1], b.shape[1])

    def body(a_ref, b_ref, o_ref):
        o_ref[...] = _dot(a_ref[...], b_ref[...], mode)

    return pl.pallas_call(body, name=name, out_shape=jax.ShapeDtypeStruct(shape, F32),
                          compiler_params=_params())(a, b)


def _adamw(parts, w, m, v, name):
    npart, rows, cols = parts.shape
    tr = rows if rows <= 128 else _tile8(rows, 128)

    def body(p_ref, w_ref, m_ref, v_ref, g_ref, d_ref, nm_ref, nv_ref):
        g = p_ref[0].astype(F32)
        for j in range(1, npart):
            g = g + p_ref[j].astype(F32)
        mm = ADAM_B1 * m_ref[...] + (1.0 - ADAM_B1) * g
        vv = ADAM_B2 * v_ref[...] + (1.0 - ADAM_B2) * (g * g)
        m_hat = mm / (1.0 - ADAM_B1 ** ADAM_STEP)
        v_hat = vv / (1.0 - ADAM_B2 ** ADAM_STEP)
        g_ref[...] = g
        d_ref[...] = -ADAM_LR * (m_hat / (jnp.sqrt(v_hat) + ADAM_EPS) + ADAM_WD * w_ref[...])
        nm_ref[...] = mm
        nv_ref[...] = vv

    blk = pl.BlockSpec((tr, cols), lambda i: (i, 0))
    out = jax.ShapeDtypeStruct((rows, cols), F32)
    return pl.pallas_call(
        body, name=name, out_shape=[out] * 4, grid=(rows // tr,),
        in_specs=[pl.BlockSpec((npart, tr, cols), lambda i: (0, i, 0)), blk, blk, blk], out_specs=[blk] * 4,
        compiler_params=_params(("parallel",)))(parts, w, m, v)


def _tile8(n, target):
    best = 8
    for d in range(8, target + 1, 8):
        if n % d == 0:
            best = d
    return best


def _pad_rows8(a):
    return jnp.concatenate([a, jnp.zeros((8 - a.shape[0], a.shape[1]), a.dtype)], axis=0)


def _pad_lanes(a, n):
    return jnp.concatenate([a, jnp.zeros(a.shape[:-1] + (n - a.shape[-1],), a.dtype)], axis=-1)


def _pack(vecs, mult):
    flat = jnp.concatenate([v.reshape(-1) for v in vecs])
    pad = (-flat.shape[0]) % mult
    flat = jnp.concatenate([flat, jnp.zeros((pad,), flat.dtype)])
    return flat.reshape(-1, LANE)


def _unpack(flat, shapes):
    out, o = [], 0
    for s in shapes:
        n = int(np.prod(s))
        out.append(flat[o:o + n].reshape(s))
        o += n
    return out


def kernel(x, c, w_ada, b_ada, ln1, ln2, w_in, conv_w, ssm_conv_w, ssm_conv_b, dt_bias, a_log, d_skip, ssm_norm_w, w_conv_out, w_ssm_out, w_o, w_up, w_down, final_norm, loss_target, m_w_ada, m_b_ada, m_ln1, m_ln2, m_w_in, m_conv_w, m_ssm_conv_w, m_ssm_conv_b, m_dt_bias, m_a_log, m_d_skip, m_ssm_norm_w, m_w_conv_out, m_w_ssm_out, m_w_o, m_w_up, m_w_down, m_final_norm, v_w_ada, v_b_ada, v_ln1, v_ln2, v_w_in, v_conv_w, v_ssm_conv_w, v_ssm_conv_b, v_dt_bias, v_a_log, v_d_skip, v_ssm_norm_w, v_w_conv_out, v_w_ssm_out, v_w_o, v_w_up, v_w_down, v_final_norm):
    names = ["w_ada", "b_ada", "ln1", "ln2", "w_in", "conv_w", "ssm_conv_w", "ssm_conv_b", "dt_bias", "a_log",
             "d_skip", "ssm_norm_w", "w_conv_out", "w_ssm_out", "w_o", "w_up", "w_down", "final_norm"]
    w_of = dict(zip(names, [w_ada, b_ada, ln1, ln2, w_in, conv_w, ssm_conv_w, ssm_conv_b, dt_bias, a_log, d_skip,
                            ssm_norm_w, w_conv_out, w_ssm_out, w_o, w_up, w_down, final_norm]))
    m_of = dict(zip(names, [m_w_ada, m_b_ada, m_ln1, m_ln2, m_w_in, m_conv_w, m_ssm_conv_w, m_ssm_conv_b, m_dt_bias,
                            m_a_log, m_d_skip, m_ssm_norm_w, m_w_conv_out, m_w_ssm_out, m_w_o, m_w_up, m_w_down,
                            m_final_norm]))
    v_of = dict(zip(names, [v_w_ada, v_b_ada, v_ln1, v_ln2, v_w_in, v_conv_w, v_ssm_conv_w, v_ssm_conv_b, v_dt_bias,
                            v_a_log, v_d_skip, v_ssm_norm_w, v_w_conv_out, v_w_ssm_out, v_w_o, v_w_up, v_w_down,
                            v_final_norm]))

    _, t, d = x.shape
    nl = w_ada.shape[0]
    ada_w = w_ada.shape[2]
    ds = ssm_norm_w.shape[1]
    nh = dt_bias.shape[1]
    ng = nh // HEADS_PER_GROUP
    gn = ng * N_STATE
    xbc_w = ds + 2 * gn
    z_off, xbc_off = 5 * d, 5 * d + ds
    dt_off = xbc_off + xbc_w
    proj_w = dt_off + nh
    pw = dt_off + 2 * LANE
    me = 4 * lax.axis_index("x") + 2 * lax.axis_index("y") + lax.axis_index("c")

    x2, tgt = x[0], loss_target[0]

    c_act = c * jax.nn.sigmoid(c)
    sizes1 = [(1, d), conv_w.shape, ssm_conv_w.shape]
    (g1,) = _all_gather([_pack([c_act, conv_w, ssm_conv_w], LANE)], "ag_small_in")
    parts1 = [_unpack(g1[j].reshape(-1), sizes1) for j in range(N_DEV)]
    c_act_all = jnp.concatenate([p[0] for p in parts1], axis=0)
    conv_w_full = jnp.concatenate([p[1] for p in parts1], axis=-1)
    sconv_w_full = jnp.concatenate([p[2] for p in parts1], axis=-1)

    mod_part = jnp.stack([_small_matmul(c_act_all, w_ada[l], "nn", f"ada_fwd{l}") for l in range(nl)])
    (gmod,) = _all_gather([mod_part], "ag_mod")
    mod = lax.dynamic_index_in_dim(gmod, me, axis=2, keepdims=False)
    mod = jnp.moveaxis(mod, 0, 1).reshape(nl, N_DEV * ada_w) + b_ada
    mod = mod.reshape(nl, 6, d)
    modp = [jnp.concatenate([mod[l], ln1[l][None], ln2[l][None]], axis=0) for l in range(nl)]

    big = ["w_in", "w_up", "w_conv_out", "w_ssm_out", "w_o", "w_down"]
    rest = big[1:]

    def blocks_of(l, keys):
        return [w_of[k][l].astype(BF16) for k in keys]

    def w_in_halves(l):
        w = w_of["w_in"][l].astype(BF16)
        return [w[:d // 2]], [w[d // 2:]]

    def full_cols(g):
        return jnp.moveaxis(g, 0, 1).reshape(g.shape[1], N_DEV * g.shape[2])

    def full_rows(g):
        return g.reshape(N_DEV * g.shape[1], g.shape[2])

    def pad_w_in(pieces):
        w = jnp.concatenate([full_cols(g) for g in pieces], axis=0)
        return jnp.concatenate([w, jnp.zeros((d, pw - proj_w), BF16)], axis=-1)

    wi = [None] * nl
    wi[0] = pad_w_in(_all_gather(blocks_of(0, ["w_in"]), "ag_w_in0"))

    cw8 = [_pad_rows8(conv_w_full[l]) for l in range(nl)]
    sw8 = [_pad_rows8(sconv_w_full[l]) for l in range(nl)]
    sb8 = [_pad_rows8(ssm_conv_b[l][None]) for l in range(nl)]
    dtp8 = [_pad_rows8(_pad_lanes(jnp.stack([dt_bias[l], a_log[l]]), LANE)) for l in range(nl)]
    sp8 = [_pad_rows8(jnp.stack([jnp.repeat(d_skip[l], HEAD_DIM), ssm_norm_w[l]])) for l in range(nl)]
    fn8 = _pad_rows8(final_norm[None])
    xbc_col, ncol, dt_col, z_col = xbc_off // d, xbc_w // d, dt_off // (2 * LANE), z_off // GROUP_W

    saved = []
    x_cur, br_prev = x2, None
    for l in range(nl):
        if l == 0:
            x_in, u1 = x_cur, _norm_fwd(x_cur, modp[0], 0, "norm_first")
        else:
            x_in, u1 = _resid_norm_fwd(x_cur, br_prev, modp[l - 1], 5, modp[l], 0, f"resid_norm_a{l}")
        proj, got = _matmul(u1, wi[l], "nn", BF16, f"mm_in{l}", comm=_GatherComm(blocks_of(l, rest)))
        got = dict(zip(rest, got))
        wup, wdown = full_cols(got["w_up"]), full_rows(got["w_down"])
        wco, wso, wo = full_rows(got["w_conv_out"]), full_rows(got["w_ssm_out"]), full_rows(got["w_o"])
        y_conv = _gconv_fwd(proj, cw8[l], d, f"gconv_fwd{l}")
        xc, pre = _sconv_fwd(proj, sw8[l], sb8[l], d, xbc_col, ncol, f"sconv_fwd{l}")
        dt, ac = _dt_fwd(proj, dtp8[l], nh, dt_col, f"dt_fwd{l}")
        yn, states = _ssd_fwd(xc, proj, dt, ac, sp8[l], ng, z_col, f"ssd_fwd{l}")
        p_conv = _matmul(y_conv, wco, "nn", F32, f"mm_conv_out{l}")
        p_ssm = _matmul(yn, wso, "nn", F32, f"mm_ssm_out{l}")
        merged = _merge_fwd(proj, p_conv, p_ssm, f"merge_fwd{l}")
        mix = _matmul(merged, wo, "nn", F32, f"mm_o{l}")
        x_mid, u2 = _resid_norm_fwd(x_in, mix, modp[l], 2, modp[l], 1, f"resid_norm_b{l}")
        if l + 1 < nl:
            top, bot = w_in_halves(l + 1)
            (h, hid), g_top = _matmul(u2, wup, "nn", BF16, f"mm_up{l}", epi="relu2", comm=_GatherComm(top))
            mlp, g_bot = _matmul(hid, wdown, "nn", F32, f"mm_down{l}", comm=_GatherComm(bot))
            wi[l + 1] = pad_w_in(g_top + g_bot)
        else:
            h, hid = _matmul(u2, wup, "nn", BF16, f"mm_up{l}", epi="relu2")
            mlp = _matmul(hid, wdown, "nn", F32, f"mm_down{l}")
        saved.append(dict(x_in=x_in, u1=u1, proj=proj, y_conv=y_conv, xc=xc, pre=pre, dt=dt, ac=ac, yn=yn,
                          states=states,
                          p_conv=p_conv, p_ssm=p_ssm, merged=merged, mix=mix, x_mid=x_mid, u2=u2, h=h, hid=hid,
                          mlp=mlp, wup=wup, wdown=wdown, wco=wco, wso=wso, wo=wo))
        x_cur, br_prev = x_mid, mlp

    dx, dbr, acc = _final_fwd_bwd(x_cur, br_prev, modp[nl - 1], fn8, tgt, "final")
    loss = lax.psum(acc[0, 0], ("x", "y", "c"))
    g_final_norm = acc[1]
    dgate2 = acc[2]

    gw = {k: [None] * nl for k in big}
    small = {k: [None] * nl for k in ["mod", "ln1", "ln2", "conv_w", "ssm_conv_w", "ssm_conv_b", "dt_bias",
                                      "a_log", "d_skip", "ssm_norm_w"]}

    def grad_cols(a):
        return jnp.moveaxis(a.reshape(a.shape[0], N_DEV, a.shape[1] // N_DEV), 1, 0)

    def grad_rows(a):
        return a.reshape(N_DEV, a.shape[0] // N_DEV, a.shape[1])

    def recv_buf(name):
        return jnp.zeros((N_DEV, nl) + w_of[name].shape[1:], BF16)

    bufs_a, bufs_b = [recv_buf("w_in")], [recv_buf(k) for k in big[1:]]
    pending = None
    for l in reversed(range(nl)):
        s = saved[l]
        dh = _matmul(dbr, s["wdown"], "nt", BF16, f"mm_down_dx{l}", epi="relu2_bwd", extra=s["h"])
        gw["w_down"][l] = _matmul(s["hid"], dbr, "tn", BF16, f"mm_down_dw{l}")
        du2 = _matmul(dh, s["wup"], "nt", F32, f"mm_up_dx{l}")
        gw["w_up"][l] = _matmul(s["u2"], dh, "tn", BF16, f"mm_up_dw{l}")
        dx_mid, dmix, acc2 = _resid_norm_bwd(dx, du2, s["x_mid"], modp[l], 1, f"resid_norm_b_bwd{l}",
                                             br=s["mix"], mp_gate=modp[l], gate_row=2)
        dmerged = _matmul(dmix, s["wo"], "nt", F32, f"mm_o_dx{l}")
        gw["w_o"][l] = _matmul(s["merged"], dmix, "tn", BF16, f"mm_o_dw{l}")
        dpc, dps, dgl = _merge_bwd(dmerged, s["proj"], s["p_conv"], s["p_ssm"], f"merge_bwd{l}")
        dyc = _matmul(dpc, s["wco"], "nt", F32, f"mm_conv_out_dx{l}")
        dproj, dcw = _gconv_bwd(dyc, s["proj"], cw8[l], dgl, d, f"gconv_bwd{l}")
        gw["w_conv_out"][l] = _matmul(s["y_conv"], dpc, "tn", BF16, f"mm_conv_out_dw{l}")
        dyn = _matmul(dps, s["wso"], "nt", F32, f"mm_ssm_out_dx{l}")
        gw["w_ssm_out"][l] = _matmul(s["yn"], dps, "tn", BF16, f"mm_ssm_out_dw{l}")
        rest_grads = [grad_cols(gw["w_up"][l])] + [grad_rows(gw[k][l]) for k in big[2:]]
        sends, bufs, layers = rest_grads, list(bufs_b), [l] * len(rest_grads)
        if pending is not None:
            sends, bufs, layers = sends + pending, bufs + list(bufs_a), layers + [l + 1]
        (dproj, dxs, dbm, dcm, ddt_g, dac_g, dsp), bufs = _ssd_bwd(
            dyn, s["xc"], s["proj"], s["dt"], s["ac"], sp8[l], s["states"], dproj, ng, z_col, f"ssd_bwd{l}",
            comm=_ExchangeComm(sends, bufs, layers))
        bufs_b = bufs[:len(rest_grads)]
        if pending is not None:
            bufs_a = bufs[len(rest_grads):]
        dsw = []
        for seg, (dseg, c0) in enumerate([(dxs, 0), (dbm, ds // d), (dcm, ds // d + gn // d)]):
            cols = slice(c0 * d, c0 * d + dseg.shape[1])
            dproj, part = _sconv_bwd(dseg, s["pre"], c0, s["proj"], sw8[l][:, cols], dproj, d, xbc_col + c0,
                                     dseg.shape[1] // d, f"sconv_bwd{l}_{seg}")
            dsw.append(part)
        dsw = jnp.concatenate(dsw, axis=1)
        dproj, ddtp = _dt_bwd(ddt_g, dac_g, s["dt"], s["proj"], dtp8[l], dproj, nh, dt_col, f"dt_bwd{l}")
        du1 = _matmul(dproj, wi[l], "nt", F32, f"mm_in_dx{l}")

        def w_in_blocks(g):
            return grad_cols(g[:, :proj_w])

        if l > 0:
            pending = [w_in_blocks(_matmul(s["u1"], dproj, "tn", BF16, f"mm_in_dw{l}"))]
        else:
            g_top = _matmul(s["u1"][:, :d // 2], dproj, "tn", BF16, "mm_in_dw0_top")
            g_bot, bufs_a = _matmul(s["u1"][:, d // 2:], dproj, "tn", BF16, "mm_in_dw0_bot",
                                    comm=_ExchangeComm([w_in_blocks(g_top)], bufs_a, 0, row0=0))
            pending = [w_in_blocks(g_bot)]
        if l > 0:
            dx, dbr, acc1 = _resid_norm_bwd(dx_mid, du1, s["x_in"], modp[l], 0, f"resid_norm_a_bwd{l}",
                                            br=saved[l - 1]["mlp"], mp_gate=modp[l - 1], gate_row=5)
        else:
            dx, acc1 = _resid_norm_bwd(dx_mid, du1, s["x_in"], modp[0], 0, "norm_first_bwd")
        sc1, sc2 = mod[l, 1], mod[l, 4]
        small["mod"][l] = jnp.stack([acc1[0], acc1[1] * ln1[l], acc2[2], acc2[0], acc2[1] * ln2[l], dgate2])
        small["ln1"][l] = acc1[1] * (1.0 + sc1)
        small["ln2"][l] = acc2[1] * (1.0 + sc2)
        small["conv_w"][l] = dcw[:3]
        small["ssm_conv_w"][l] = dsw[:4]
        small["ssm_conv_b"][l] = dsw[4]
        small["dt_bias"][l] = ddtp[0, :nh]
        small["a_log"][l] = ddtp[1, :nh]
        small["d_skip"][l] = dsp[0].reshape(nh, HEAD_DIM).sum(axis=-1)
        small["ssm_norm_w"][l] = dsp[1]
        if l > 0:
            dgate2 = acc1[2]
    grad_x = dx[None]

    sm = {k: jnp.stack(v) for k, v in small.items()}
    rep_names = ["b_ada", "ln1", "ln2", "ssm_conv_b", "dt_bias", "a_log", "d_skip", "ssm_norm_w", "final_norm"]
    rep_grads = [sm["mod"].reshape(nl, 6 * d), sm["ln1"], sm["ln2"], sm["ssm_conv_b"], sm["dt_bias"], sm["a_log"],
                 sm["d_skip"], sm["ssm_norm_w"], g_final_norm]
    rep_pack = _pack(rep_grads, 8 * LANE)
    conv_pack = _pack([sm["conv_w"], sm["ssm_conv_w"]], 8 * LANE)
    g_rep, g_conv = _all_gather([rep_pack, conv_pack], "ag_small_grads")

    outs = {}

    def run_adamw(name, parts, tag):
        w, m, v = w_of[name], m_of[name], v_of[name]
        shp = w.shape
        r2 = (int(np.prod(shp[:-1])), shp[-1])
        res = _adamw(parts.reshape((parts.shape[0],) + r2), w.reshape(r2), m.reshape(r2), v.reshape(r2), tag)
        outs[name] = [a.reshape(shp) for a in res]

    rep_shapes = [w_of[k].shape for k in rep_names]
    res = _adamw(g_rep, _pack([w_of[k] for k in rep_names], 8 * LANE), _pack([m_of[k] for k in rep_names], 8 * LANE),
                 _pack([v_of[k] for k in rep_names], 8 * LANE), "adamw_replicated")
    for k, vals in zip(rep_names, zip(*[_unpack(a.reshape(-1), rep_shapes) for a in res])):
        outs[k] = list(vals)

    conv_parts = [_unpack(g_conv[j].reshape(-1), [sm["conv_w"].shape, sm["ssm_conv_w"].shape]) for j in range(N_DEV)]
    for idx, name in enumerate(["conv_w", "ssm_conv_w"]):
        wsh = w_of[name].shape[-1]
        full = jnp.stack([p[idx] for p in conv_parts])
        run_adamw(name, lax.dynamic_slice_in_dim(full, me * wsh, wsh, axis=3), "adamw_" + name)

    dmod_all = g_rep.reshape(N_DEV, -1)[:, :nl * 6 * d].reshape(N_DEV, nl, 6 * d)
    dmod_mine = lax.dynamic_slice_in_dim(dmod_all, me * ada_w, ada_w, axis=2)
    g_ada = jnp.stack([_small_matmul(c_act_all, dmod_mine[:, l], "tn", f"ada_bwd{l}") for l in range(nl)])
    run_adamw("w_ada", g_ada[None], "adamw_w_ada")

    bufs_a = _exchange(pending, bufs_a, 0, "a2a_w_in0_bot", row0=d // 2)
    for name, parts in zip(big, list(bufs_a) + list(bufs_b)):
        run_adamw(name, parts, "adamw_" + name)

    result = [loss, grad_x]
    for i in range(4):
        result += [outs[k][i] for k in names]
    return tuple(result)
```

```python
import functools

import numpy as np
import jax
import jax.numpy as jnp
from jax import lax
from jax.experimental import pallas as pl
from jax.experimental.pallas import tpu as pltpu

F32 = jnp.float32
BF16 = jnp.bfloat16
EPS = 1e-6
N_STATE = 128
HEAD_DIM = 64
HEADS_PER_GROUP = 4
GROUP_W = HEAD_DIM * HEADS_PER_GROUP
SSD_GROUPS = 2
SSD_Q = 256
N_DEV = 8
ROW_TILE = 256
LANE = 128
VMEM_LIMIT = 56 * 1024 * 1024
MATMUL_VMEM_BUDGET = 40 * 1024 * 1024
MATMUL_TILE_CAP = 2048
MXU_FLOPS = 9.0e14
HBM_BYTES_PER_S = 3.0e12
STEP_OVERHEAD_S = 0.35e-6

ADAM_LR, ADAM_B1, ADAM_B2, ADAM_EPS, ADAM_WD, ADAM_STEP = 0.001, 0.9, 0.999, 1e-08, 0.01, 10

MESH = pl.DeviceIdType.MESH


def _params(sem=None):
    return pltpu.CompilerParams(dimension_semantics=sem, vmem_limit_bytes=VMEM_LIMIT)


def _divisors(n, cap):
    if n <= cap:
        return [n]
    return [d for d in range(cap - cap % LANE, 0, -LANE) if n % d == 0]


def _matmul_tiles(m, n, k, out_bytes, n_out, n_extra):
    best = None
    for tm in _divisors(m, MATMUL_TILE_CAP):
        for tn in _divisors(n, MATMUL_TILE_CAP):
            for tk in _divisors(k, MATMUL_TILE_CAP):
                out_tile = tm * tn * (out_bytes * n_out + 2 * n_extra)
                vmem = 2 * 2 * (tm * tk + tk * tn) + 2 * out_tile + (4 * tm * tn if tk < k else 0)
                if vmem > MATMUL_VMEM_BUDGET:
                    continue
                steps = (m // tm) * (n // tn) * (k // tk)
                a_reads = 1 if tk == k else n // tn
                traffic = 2 * (a_reads * m * k + (m // tm) * k * n) + m * n * (out_bytes * n_out + 2 * n_extra)
                est = (max(2.0 * m * n * k / MXU_FLOPS, traffic / HBM_BYTES_PER_S) + steps * STEP_OVERHEAD_S
                       + (2 * (tm * tk + tk * tn) + out_tile) / HBM_BYTES_PER_S)
                cand = (est, tm, tn, tk)
                if best is None or cand < best:
                    best = cand
    assert best is not None, (m, n, k)
    return best[1:]


def _tile(n, target):
    if n <= target:
        return n
    best = None
    for d in range(LANE, target + 1, LANE):
        if n % d == 0:
            best = d
    assert best is not None, (n, target)
    return best


def _slot(p):
    return 4 * p[0] + 2 * p[1] + p[2]


def _all_gather(xs, name):
    comm = _GatherComm(xs)

    def body(*refs):
        begin, end = comm.ops(refs[:comm.n_in], refs[comm.n_in:comm.n_in + comm.n_out], refs[-3:])
        begin()
        end()

    any_spec = pl.BlockSpec(memory_space=pl.ANY)
    return pl.pallas_call(
        body, name=name, out_shape=comm.out_shape,
        in_specs=[any_spec] * comm.n_in, out_specs=[any_spec] * comm.n_out, scratch_shapes=comm.sems,
    )(*comm.operands)


class _GatherComm:
    def __init__(self, xs):
        n = len(xs)
        self.operands = list(xs)
        self.n_in = self.n_out = n
        self.aliases = {}
        self.out_shape = [jax.ShapeDtypeStruct((N_DEV,) + x.shape, x.dtype) for x in xs]
        self.sems = [pltpu.SemaphoreType.DMA((7 * n,)), pltpu.SemaphoreType.DMA((7 * n,)),
                     pltpu.SemaphoreType.DMA((n,))]

    def ops(self, x_refs, o_refs, sem_refs):
        n = self.n_in
        send, recv, loc = sem_refs
        x, y, c = lax.axis_index("x"), lax.axis_index("y"), lax.axis_index("c")
        me, sib = (x, y, c), (x, y, 1 - c)
        chips = [(1 - x, y), (x, 1 - y), (1 - x, 1 - y)]

        def cp(a, k, block, to, src=None):
            dst = o_refs[a].at[_slot(block)]
            return pltpu.make_async_remote_copy(
                src_ref=dst if src is None else src, dst_ref=dst,
                send_sem=send.at[7 * a + k], recv_sem=recv.at[7 * a + k],
                device_id=to, device_id_type=MESH)

        mine = [pltpu.make_async_copy(x_refs[a], o_refs[a].at[_slot(me)], loc.at[a]) for a in range(n)]
        first = []
        for a in range(n):
            first.append(cp(a, 0, me, sib, src=x_refs[a]))
            first += [cp(a, 1 + j, me, (*chip, c), src=x_refs[a]) for j, chip in enumerate(chips)]

        def begin():
            for m in mine:
                m.start()
            for f in first:
                f.start()

        def end():
            passed = []
            for j, chip in enumerate(chips):
                for a in range(n):
                    cp(a, 1 + j, (*chip, c), me).wait_recv()
                    p = cp(a, 4 + j, (*chip, c), sib)
                    p.start()
                    passed.append(p)
            for a in range(n):
                cp(a, 0, sib, me).wait_recv()
                for j, chip in enumerate(chips):
                    cp(a, 4 + j, (*chip, 1 - c), me).wait_recv()
            for f in first + passed:
                f.wait_send()
            for m in mine:
                m.wait()

        return begin, end


class _ExchangeComm:
    def __init__(self, xs, bufs, layer, row0=0):
        n = len(xs)
        self.layers = list(layer) if isinstance(layer, (list, tuple)) else [layer] * n
        self.row0 = row0
        self.nrows = [x.shape[1] for x in xs]
        self.operands = list(xs) + list(bufs)
        self.n_in, self.n_out = 2 * n, n
        self.aliases = {n + a: a for a in range(n)}
        self.out_shape = [jax.ShapeDtypeStruct(b.shape, b.dtype) for b in bufs]
        self.sems = [pltpu.SemaphoreType.DMA((7 * n,)), pltpu.SemaphoreType.DMA((7 * n,)),
                     pltpu.SemaphoreType.DMA((n,))]

    def ops(self, in_refs, o_refs, sem_refs):
        n = self.n_out
        x_refs = in_refs[:n]
        send, recv, loc = sem_refs
        x, y, c = lax.axis_index("x"), lax.axis_index("y"), lax.axis_index("c")
        me = (x, y, c)
        peers = []
        for k in range(1, 8):
            kx, ky, kc = (k >> 2) & 1, (k >> 1) & 1, k & 1
            peers.append((x + kx - 2 * x * kx, y + ky - 2 * y * ky, c + kc - 2 * c * kc))

        def land(a, slot):
            return o_refs[a].at[slot, self.layers[a], pl.ds(self.row0, self.nrows[a])]

        def cp(a, k, src_slot, dst_slot, to):
            return pltpu.make_async_remote_copy(
                src_ref=x_refs[a].at[src_slot], dst_ref=land(a, dst_slot),
                send_sem=send.at[7 * a + k], recv_sem=recv.at[7 * a + k],
                device_id=to, device_id_type=MESH)

        mine = [pltpu.make_async_copy(x_refs[a].at[_slot(me)], land(a, _slot(me)), loc.at[a])
                for a in range(n)]
        sends = [cp(a, k, _slot(p), _slot(me), p) for a in range(n) for k, p in enumerate(peers)]

        def begin():
            for m in mine:
                m.start()
            for s in sends:
                s.start()

        def end():
            for a in range(n):
                for k, p in enumerate(peers):
                    cp(a, k, _slot(me), _slot(p), me).wait_recv()
            for s in sends:
                s.wait_send()
            for m in mine:
                m.wait()

        return begin, end


def _exchange(xs, bufs, layer, name, row0=0):
    comm = _ExchangeComm(xs, bufs, layer, row0)

    def body(*refs):
        begin, end = comm.ops(refs[:comm.n_in], refs[comm.n_in:comm.n_in + comm.n_out], refs[-3:])
        begin()
        end()

    any_spec = pl.BlockSpec(memory_space=pl.ANY)
    return pl.pallas_call(
        body, name=name, out_shape=comm.out_shape,
        in_specs=[any_spec] * comm.n_in, out_specs=[any_spec] * comm.n_out, scratch_shapes=comm.sems,
        input_output_aliases=dict(comm.aliases),
    )(*comm.operands)


_DIMS = {"nn": (((1,), (0,)), ((), ())), "nt": (((1,), (1,)), ((), ())), "tn": (((0,), (0,)), ((), ()))}


def _matmul(a, b, mode, out_dtype, name, comm=None, epi=None, extra=None):
    if mode == "nn":
        (m, k), (_, n) = a.shape, b.shape
    elif mode == "nt":
        (m, k), (n, _) = a.shape, b.shape
    else:
        (k, m), (_, n) = a.shape, b.shape
    n_ext = 1 if epi == "relu2_bwd" else 0
    n_main = 2 if epi == "relu2" else 1
    tm, tn, tk = _matmul_tiles(m, n, k, jnp.dtype(out_dtype).itemsize, n_main, n_ext)
    nk = k // tk
    grid = (m // tm, n // tn, nk)
    dn = _DIMS[mode]
    n_cin = comm.n_in if comm else 0
    n_cout = comm.n_out if comm else 0
    n_acc = 0 if nk == 1 else 1
    first_in, first_out = 2 + n_ext, 2 + n_ext + n_cin

    def finish(acc, refs):
        outs = refs[first_out:first_out + n_main]
        if epi == "relu2":
            outs[0][...] = acc.astype(outs[0].dtype)
            r = jnp.maximum(acc, 0.0)
            outs[1][...] = (r * r).astype(outs[1].dtype)
        elif epi == "relu2_bwd":
            outs[0][...] = (acc * (2.0 * jnp.maximum(refs[2][...].astype(F32), 0.0))).astype(outs[0].dtype)
        else:
            outs[0][...] = acc.astype(outs[0].dtype)

    def body(*refs):
        a_ref, b_ref = refs[:2]
        scratch = refs[first_out + n_main + n_cout:]
        pid = [pl.program_id(ax) for ax in range(3)]
        if comm:
            begin, end = comm.ops(refs[first_in:first_in + n_cin],
                                  refs[first_out + n_main:first_out + n_main + n_cout], scratch[n_acc:])
            pl.when((pid[0] == 0) & (pid[1] == 0) & (pid[2] == 0))(begin)
        part = lax.dot_general(a_ref[...], b_ref[...], dn, preferred_element_type=F32)
        if nk == 1:
            finish(part, refs)
        else:
            acc_ref = scratch[0]

            @pl.when(pid[2] == 0)
            def _():
                acc_ref[...] = part

            @pl.when(pid[2] > 0)
            def _():
                acc_ref[...] += part

            @pl.when(pid[2] == nk - 1)
            def _():
                finish(acc_ref[...], refs)
        if comm:
            pl.when((pid[0] == grid[0] - 1) & (pid[1] == grid[1] - 1) & (pid[2] == grid[2] - 1))(end)

    a_spec = pl.BlockSpec((tk, tm), lambda i, j, kk: (kk, i)) if mode == "tn" else \
        pl.BlockSpec((tm, tk), lambda i, j, kk: (i, kk))
    b_spec = pl.BlockSpec((tn, tk), lambda i, j, kk: (j, kk)) if mode == "nt" else \
        pl.BlockSpec((tk, tn), lambda i, j, kk: (kk, j))
    o_spec = pl.BlockSpec((tm, tn), lambda i, j, kk: (i, j))
    any_spec = pl.BlockSpec(memory_space=pl.ANY)
    res = pl.pallas_call(
        body, name=name,
        out_shape=[jax.ShapeDtypeStruct((m, n), out_dtype)] * n_main + (comm.out_shape if comm else []),
        grid=grid,
        in_specs=[a_spec, b_spec] + [o_spec] * n_ext + [any_spec] * n_cin,
        out_specs=[o_spec] * n_main + [any_spec] * n_cout,
        scratch_shapes=([] if nk == 1 else [pltpu.VMEM((tm, tn), F32)]) + (comm.sems if comm else []),
        input_output_aliases={first_in + i: n_main + o for i, o in comm.aliases.items()} if comm else {},
        compiler_params=_params(("arbitrary",) * 3 if comm else ("parallel", "parallel", "arbitrary")),
    )(a, b, *([extra] if n_ext else []), *(comm.operands if comm else []))
    main = res[0] if n_main == 1 else tuple(res[:n_main])
    return (main, list(res[n_main:])) if comm else main


def _dot(a, b, mode):
    return lax.dot_general(a.astype(BF16), b.astype(BF16), _DIMS[mode], preferred_element_type=F32)


@functools.partial(jax.custom_vjp, nondiff_argnums=(2,))
def _bdot(a, b, mode):
    return _dot(a, b, mode)


def _bdot_fwd(a, b, mode):
    return _dot(a, b, mode), (a, b)


def _bdot_bwd(mode, res, g):
    a, b = res
    if mode == "nn":
        return _dot(g, b, "nt"), _dot(a, g, "tn")
    if mode == "nt":
        return _dot(g, b, "nn"), _dot(g, a, "tn")
    return _dot(b, g, "nt"), _dot(a, g, "nn")


_bdot.defvjp(_bdot_fwd, _bdot_bwd)


def _split3(x):
    hi = x.astype(BF16)
    r1 = x - hi.astype(F32)
    mid = r1.astype(BF16)
    return hi, mid, (r1 - mid.astype(F32)).astype(BF16)


def _xdot01(m01, x):
    return sum(lax.dot_general(m01, p, _DIMS["nn"], preferred_element_type=F32) for p in _split3(x))


@jax.custom_vjp
def _spread(v, m01):
    return sum(lax.dot_general(p, m01, _DIMS["nn"], preferred_element_type=F32) for p in _split3(v))


def _spread_fwd(v, m01):
    return _spread(v, m01), m01


def _spread_bwd(m01, g):
    dv = sum(lax.dot_general(p, m01, _DIMS["nt"], preferred_element_type=F32) for p in _split3(g)[:2])
    return dv, jnp.zeros_like(m01)


_spread.defvjp(_spread_fwd, _spread_bwd)


def _chunk_tri(n, lower):
    r = lax.broadcasted_iota(jnp.int32, (n, n), 0)
    c = lax.broadcasted_iota(jnp.int32, (n, n), 1)
    tri = (r >= c) if lower else (r <= c)
    return jnp.where(tri & (r // SSD_Q == c // SSD_Q), 1.0, 0.0).astype(BF16)


def _shift_down(x, prev8, k):
    if k == 0:
        return x
    n = x.shape[0]
    r = pltpu.roll(x, k, 0)
    rp = pltpu.roll(prev8, k, 0)
    rows = lax.broadcasted_iota(jnp.int32, (8, x.shape[1]), 0)
    head = jnp.where(rows < k, rp, r[:8])
    return head if n == 8 else jnp.concatenate([head, r[8:]], axis=0)


def _shift_up(x, next8, k):
    if k == 0:
        return x
    n = x.shape[0]
    r = pltpu.roll(x, n - k, 0)
    rn = pltpu.roll(next8, 8 - k, 0)
    rows = lax.broadcasted_iota(jnp.int32, (8, x.shape[1]), 0)
    tail = jnp.where(rows >= 8 - k, rn, r[n - 8:])
    return tail if n == 8 else jnp.concatenate([r[:n - 8], tail], axis=0)


def _sigmoid(x):
    return 0.5 * jnp.tanh(0.5 * x) + 0.5


def _rows(ref, i):
    return ref[i:i + 1, :]


def _norm_parts(x, mp_ref, which):
    ln, sh, sc = _rows(mp_ref, 6 + which), _rows(mp_ref, 3 * which), _rows(mp_ref, 3 * which + 1)
    r = lax.rsqrt(jnp.mean(x * x, axis=1, keepdims=True) + EPS)
    return r, ln, sh, sc


def _norm_fwd(x, modp, which, name):
    t, d = x.shape

    def body(x_ref, mp_ref, u_ref):
        xv = x_ref[...]
        r, ln, sh, sc = _norm_parts(xv, mp_ref, which)
        u_ref[...] = (((xv * r) * ln) * (1.0 + sc) + sh).astype(BF16)

    row = pl.BlockSpec((ROW_TILE, d), lambda i: (i, 0))
    return pl.pallas_call(
        body, name=name, out_shape=jax.ShapeDtypeStruct((t, d), BF16), grid=(t // ROW_TILE,),
        in_specs=[row, pl.BlockSpec((8, d), lambda i: (0, 0))], out_specs=row,
        compiler_params=_params(("parallel",)))(x, modp)


def _resid_norm_fwd(x, br, mp_gate, gate_row, mp_norm, which, name):
    t, d = x.shape

    def body(x_ref, br_ref, mg_ref, mn_ref, xn_ref, u_ref):
        xv = x_ref[...] + _rows(mg_ref, gate_row) * br_ref[...].astype(F32)
        xn_ref[...] = xv
        r, ln, sh, sc = _norm_parts(xv, mn_ref, which)
        u_ref[...] = (((xv * r) * ln) * (1.0 + sc) + sh).astype(BF16)

    row = pl.BlockSpec((ROW_TILE, d), lambda i: (i, 0))
    mp = pl.BlockSpec((8, d), lambda i: (0, 0))
    return pl.pallas_call(
        body, name=name,
        out_shape=[jax.ShapeDtypeStruct((t, d), F32), jax.ShapeDtypeStruct((t, d), BF16)],
        grid=(t // ROW_TILE,), in_specs=[row, row, mp, mp], out_specs=[row, row],
        compiler_params=_params(("parallel",)))(x, br, mp_gate, mp_norm)


def _final_fwd_bwd(x, br, mp_gate, fnorm8, target, name):
    t, d = x.shape

    def body(x_ref, br_ref, mg_ref, fn_ref, tg_ref, dx_ref, dbr_ref, acc_ref):
        gate = _rows(mg_ref, 5)
        brv = br_ref[...].astype(F32)
        xv = x_ref[...] + gate * brv
        fn = _rows(fn_ref, 0)
        r = lax.rsqrt(jnp.mean(xv * xv, axis=1, keepdims=True) + EPS)
        nrm = xv * r
        err = nrm * fn - tg_ref[...]
        loss = 0.5 * jnp.sum(jnp.mean(err * err, axis=1, keepdims=True), axis=0, keepdims=True)
        dy = err * (1.0 / d)
        dn = dy * fn
        dx = r * (dn - nrm * jnp.mean(dn * nrm, axis=1, keepdims=True))
        dx_ref[...] = dx
        dbr_ref[...] = (dx * gate).astype(BF16)
        upd = jnp.concatenate([
            jnp.broadcast_to(loss, (1, d)),
            jnp.sum(dy * nrm, axis=0, keepdims=True),
            jnp.sum(dx * brv, axis=0, keepdims=True),
            jnp.zeros((5, d), F32)], axis=0)

        @pl.when(pl.program_id(0) == 0)
        def _():
            acc_ref[...] = upd

        @pl.when(pl.program_id(0) > 0)
        def _():
            acc_ref[...] += upd

    row = pl.BlockSpec((ROW_TILE, d), lambda i: (i, 0))
    mp = pl.BlockSpec((8, d), lambda i: (0, 0))
    return pl.pallas_call(
        body, name=name,
        out_shape=[jax.ShapeDtypeStruct((t, d), F32), jax.ShapeDtypeStruct((t, d), BF16),
                   jax.ShapeDtypeStruct((8, d), F32)],
        grid=(t // ROW_TILE,), in_specs=[row, row, mp, mp, row], out_specs=[row, row, mp],
        compiler_params=_params(("arbitrary",)))(x, br, mp_gate, fnorm8, target)


def _resid_norm_bwd(dx, du, x, mp_norm, which, name, br=None, mp_gate=None, gate_row=None):
    t, d = x.shape
    has_gate = br is not None

    def body(*refs):
        if has_gate:
            dx_ref, du_ref, x_ref, mn_ref, br_ref, mg_ref, dxn_ref, dbr_ref, acc_ref = refs
        else:
            dx_ref, du_ref, x_ref, mn_ref, dxn_ref, acc_ref = refs
        xv, duv = x_ref[...], du_ref[...].astype(F32)
        r, ln, _, sc = _norm_parts(xv, mn_ref, which)
        nrm = xv * r
        dn = duv * (ln * (1.0 + sc))
        dxn = dx_ref[...] + r * (dn - nrm * jnp.mean(dn * nrm, axis=1, keepdims=True))
        dxn_ref[...] = dxn
        rows = [jnp.sum(duv, axis=0, keepdims=True), jnp.sum(duv * nrm, axis=0, keepdims=True)]
        if has_gate:
            dbr_ref[...] = (dxn * _rows(mg_ref, gate_row)).astype(BF16)
            rows.append(jnp.sum(dxn * br_ref[...].astype(F32), axis=0, keepdims=True))
        upd = jnp.concatenate(rows + [jnp.zeros((8 - len(rows), d), F32)], axis=0)

        @pl.when(pl.program_id(0) == 0)
        def _():
            acc_ref[...] = upd

        @pl.when(pl.program_id(0) > 0)
        def _():
            acc_ref[...] += upd

    row = pl.BlockSpec((ROW_TILE, d), lambda i: (i, 0))
    mp = pl.BlockSpec((8, d), lambda i: (0, 0))
    ins, in_specs = [dx, du, x, mp_norm], [row, row, row, mp]
    outs = [jax.ShapeDtypeStruct((t, d), F32)]
    out_specs = [row]
    if has_gate:
        ins += [br, mp_gate]
        in_specs += [row, mp]
        outs.append(jax.ShapeDtypeStruct((t, d), BF16))
        out_specs.append(row)
    outs.append(jax.ShapeDtypeStruct((8, d), F32))
    out_specs.append(mp)
    return pl.pallas_call(
        body, name=name, out_shape=outs, grid=(t // ROW_TILE,), in_specs=in_specs, out_specs=out_specs,
        compiler_params=_params(("arbitrary",)))(*ins)


def _merge_fwd(proj, p_conv, p_ssm, name):
    t, d = p_conv.shape

    def body(gl_ref, pc_ref, ps_ref, o_ref):
        g = _sigmoid(gl_ref[...].astype(F32))
        o_ref[...] = (g[:, :d] * pc_ref[...].astype(F32) + g[:, d:] * ps_ref[...].astype(F32)).astype(BF16)

    row = pl.BlockSpec((ROW_TILE, d), lambda i: (i, 0))
    return pl.pallas_call(
        body, name=name, out_shape=jax.ShapeDtypeStruct((t, d), BF16), grid=(t // ROW_TILE,),
        in_specs=[pl.BlockSpec((ROW_TILE, 2 * d), lambda i: (i, 0)), row, row], out_specs=row,
        compiler_params=_params(("parallel",)))(proj, p_conv, p_ssm)


def _merge_bwd(dmerged, proj, p_conv, p_ssm, name):
    t, d = p_conv.shape

    def body(dm_ref, gl_ref, pc_ref, ps_ref, dpc_ref, dps_ref, dgl_ref):
        g = _sigmoid(gl_ref[...].astype(F32))
        gc, gs = g[:, :d], g[:, d:]
        dm = dm_ref[...].astype(F32)
        dpc_ref[...] = (dm * gc).astype(BF16)
        dps_ref[...] = (dm * gs).astype(BF16)
        dgl_ref[...] = jnp.concatenate(
            [dm * pc_ref[...].astype(F32) * gc * (1.0 - gc), dm * ps_ref[...].astype(F32) * gs * (1.0 - gs)], axis=1).astype(BF16)

    row = pl.BlockSpec((ROW_TILE, d), lambda i: (i, 0))
    wide = pl.BlockSpec((ROW_TILE, 2 * d), lambda i: (i, 0))
    return pl.pallas_call(
        body, name=name,
        out_shape=[jax.ShapeDtypeStruct((t, d), BF16), jax.ShapeDtypeStruct((t, d), BF16),
                   jax.ShapeDtypeStruct((t, 2 * d), BF16)],
        grid=(t // ROW_TILE,), in_specs=[row, wide, row, row], out_specs=[row, row, wide],
        compiler_params=_params(("parallel",)))(dmerged, proj, p_conv, p_ssm)


def _halo_specs(t, width, col):
    nb = t // 8
    step = ROW_TILE // 8
    prev = pl.BlockSpec((8, width), lambda i: (jnp.maximum(i * step - 1, 0), col))
    nxt = pl.BlockSpec((8, width), lambda i: (jnp.minimum((i + 1) * step, nb - 1), col))
    return prev, nxt


def _gconv_fwd(proj, conv_w8, d, name):
    t = proj.shape[0]

    def body(cb_ref, cc_ref, cx_ref, ccp_ref, cxp_ref, w_ref, o_ref):
        first = pl.program_id(0) == 0
        v = cc_ref[...].astype(F32) * cx_ref[...].astype(F32)
        vp = jnp.where(first, 0.0, ccp_ref[...].astype(F32) * cxp_ref[...].astype(F32))
        cv = sum(_rows(w_ref, k) * _shift_down(v, vp, 2 - k) for k in range(3))
        o_ref[...] = (cb_ref[...].astype(F32) * cv).astype(BF16)

    def win(col):
        return pl.BlockSpec((ROW_TILE, d), lambda i: (i, col))

    return pl.pallas_call(
        body, name=name, out_shape=jax.ShapeDtypeStruct((t, d), BF16), grid=(t // ROW_TILE,),
        in_specs=[win(2), win(3), win(4), _halo_specs(t, d, 3)[0], _halo_specs(t, d, 4)[0],
                  pl.BlockSpec((8, d), lambda i: (0, 0))],
        out_specs=pl.BlockSpec((ROW_TILE, d), lambda i: (i, 0)),
        compiler_params=_params(("parallel",)))(proj, proj, proj, proj, proj, conv_w8)


def _gconv_bwd(dy, proj, conv_w8, dgl, d, name):
    t = proj.shape[0]
    nt = t // ROW_TILE

    def body(dy_ref, dyn_ref, cb_ref, cc_ref, cx_ref, ccp_ref, cxp_ref, cbn_ref, w_ref, dgl_ref, dp_ref, dw_ref):
        i = pl.program_id(0)
        cb, cc, cx = cb_ref[...].astype(F32), cc_ref[...].astype(F32), cx_ref[...].astype(F32)
        v = cc * cx
        vp = jnp.where(i == 0, 0.0, ccp_ref[...].astype(F32) * cxp_ref[...].astype(F32))
        sh = [_shift_down(v, vp, 2 - k) for k in range(3)]
        cv = sum(_rows(w_ref, k) * sh[k] for k in range(3))
        dyv = dy_ref[...]
        dcv = dyv * cb
        dcvn = jnp.where(i == nt - 1, 0.0, dyn_ref[...] * cbn_ref[...].astype(F32))
        dv = sum(_rows(w_ref, k) * _shift_up(dcv, dcvn, 2 - k) for k in range(3))
        dp_ref[:, :2 * d] = dgl_ref[...]
        dp_ref[:, 2 * d:3 * d] = (dyv * cv).astype(BF16)
        dp_ref[:, 3 * d:4 * d] = (dv * cx).astype(BF16)
        dp_ref[:, 4 * d:] = (dv * cc).astype(BF16)
        upd = jnp.concatenate([jnp.sum(dcv * sh[k], axis=0, keepdims=True) for k in range(3)]
                              + [jnp.zeros((5, d), F32)], axis=0)

        @pl.when(i == 0)
        def _():
            dw_ref[...] = upd

        @pl.when(i > 0)
        def _():
            dw_ref[...] += upd

    def win(col):
        return pl.BlockSpec((ROW_TILE, d), lambda i: (i, col))

    row = pl.BlockSpec((ROW_TILE, d), lambda i: (i, 0))
    w8 = pl.BlockSpec((8, d), lambda i: (0, 0))
    return pl.pallas_call(
        body, name=name,
        out_shape=[jax.ShapeDtypeStruct(proj.shape, BF16), jax.ShapeDtypeStruct((8, d), F32)],
        grid=(nt,),
        in_specs=[row, _halo_specs(t, d, 0)[1], win(2), win(3), win(4), _halo_specs(t, d, 3)[0],
                  _halo_specs(t, d, 4)[0], _halo_specs(t, d, 2)[1], w8,
                  pl.BlockSpec((ROW_TILE, 2 * d), lambda i: (i, 0))],
        out_specs=[pl.BlockSpec((ROW_TILE, 5 * d), lambda i: (i, 0)), w8],
        compiler_params=_params(("arbitrary",)))(
            dy, dy, proj, proj, proj, proj, proj, proj, conv_w8, dgl)


def _dsilu(p):
    s = _sigmoid(p)
    return s * (1.0 + p * (1.0 - s))


def _sconv_specs(t, d, col0, nt):
    nb, step = t // 8, ROW_TILE // 8
    win = pl.BlockSpec((ROW_TILE, d), lambda j, i: (i, col0 + j))
    prev = pl.BlockSpec((8, d), lambda j, i: (jnp.maximum(i * step - 1, 0), col0 + j))
    nxt = pl.BlockSpec((8, d), lambda j, i: (jnp.minimum((i + 1) * step, nb - 1), col0 + j))
    return win, prev, nxt


def _sconv_fwd(proj, w8, b8, d, col0, ncol, name):
    t = proj.shape[0]
    nt = t // ROW_TILE

    def body(x_ref, xp_ref, w_ref, b_ref, o_ref, pre_ref):
        xv = x_ref[...].astype(F32)
        xp = jnp.where(pl.program_id(1) == 0, 0.0, xp_ref[...].astype(F32))
        pre = _rows(b_ref, 0) + sum(_rows(w_ref, k) * _shift_down(xv, xp, 3 - k) for k in range(4))
        o_ref[...] = (pre * _sigmoid(pre)).astype(BF16)
        pre_ref[...] = pre.astype(BF16)

    win, prev, _ = _sconv_specs(t, d, col0, nt)
    par = pl.BlockSpec((8, d), lambda j, i: (0, j))
    out = pl.BlockSpec((ROW_TILE, d), lambda j, i: (i, j))
    return pl.pallas_call(
        body, name=name, out_shape=[jax.ShapeDtypeStruct((t, ncol * d), BF16)] * 2, grid=(ncol, nt),
        in_specs=[win, prev, par, par], out_specs=[out, out],
        compiler_params=_params(("parallel", "parallel")))(proj, proj, w8, b8)


def _sconv_bwd(dxc, pre, pre_col0, proj, w8, dproj, d, col0, ncol, name):
    t = proj.shape[0]
    nt = t // ROW_TILE

    def body(d_ref, dn_ref, p_ref, pn_ref, x_ref, w_ref, _, dp_ref, dw_ref):
        i = pl.program_id(1)
        dpre = d_ref[...] * _dsilu(p_ref[...].astype(F32))
        dpre_n = jnp.where(i == nt - 1, 0.0, dn_ref[...] * _dsilu(pn_ref[...].astype(F32)))
        up = [_shift_up(dpre, dpre_n, 3 - k) for k in range(4)]
        dp_ref[...] = sum(_rows(w_ref, k) * up[k] for k in range(4)).astype(BF16)
        xv = x_ref[...].astype(F32)
        upd = jnp.concatenate([jnp.sum(xv * up[k], axis=0, keepdims=True) for k in range(4)]
                              + [jnp.sum(dpre, axis=0, keepdims=True), jnp.zeros((3, d), F32)], axis=0)

        @pl.when(i == 0)
        def _():
            dw_ref[...] = upd

        @pl.when(i > 0)
        def _():
            dw_ref[...] += upd

    win, _, _ = _sconv_specs(t, d, col0, nt)
    nb, step = t // 8, ROW_TILE // 8
    dwin = pl.BlockSpec((ROW_TILE, d), lambda j, i: (i, j))
    dnxt = pl.BlockSpec((8, d), lambda j, i: (jnp.minimum((i + 1) * step, nb - 1), j))
    par = pl.BlockSpec((8, d), lambda j, i: (0, j))
    return pl.pallas_call(
        body, name=name,
        out_shape=[jax.ShapeDtypeStruct(dproj.shape, BF16), jax.ShapeDtypeStruct((8, ncol * d), F32)],
        grid=(ncol, nt),
        in_specs=[dwin, dnxt,
                  pl.BlockSpec((ROW_TILE, d), lambda j, i: (i, pre_col0 + j)),
                  pl.BlockSpec((8, d), lambda j, i: (jnp.minimum((i + 1) * step, nb - 1), pre_col0 + j)),
                  win, par, pl.BlockSpec(memory_space=pl.ANY)],
        out_specs=[win, par], input_output_aliases={6: 0},
        compiler_params=_params(("arbitrary", "arbitrary")))(dxc, dxc, pre, pre, proj, w8, dproj)


def _softplus(x):
    return jnp.maximum(x, 0.0) + jnp.log(1.0 + jnp.exp(-jnp.abs(x)))


def _dt_fwd(proj, par8, nheads, dt_col, name):
    t = proj.shape[0]

    def body(r_ref, p_ref, dt_ref, ac_ref):
        lane = lax.broadcasted_iota(jnp.int32, (1, LANE), 1)
        valid = lane < nheads
        raw = r_ref[...][:, :LANE].astype(F32)
        dt = jnp.where(valid, _softplus(raw + _rows(p_ref, 0)), 0.0)
        a = jnp.where(valid, -jnp.exp(_rows(p_ref, 1)), 0.0)
        dt_ref[...] = dt
        ac_ref[...] = _xdot01(_chunk_tri(ROW_TILE, True), dt * a)

    out = pl.BlockSpec((ROW_TILE, LANE), lambda i: (i, 0))
    return pl.pallas_call(
        body, name=name,
        out_shape=[jax.ShapeDtypeStruct((t, LANE), F32), jax.ShapeDtypeStruct((t, LANE), F32)],
        grid=(t // ROW_TILE,),
        in_specs=[pl.BlockSpec((ROW_TILE, 2 * LANE), lambda i: (i, dt_col)), pl.BlockSpec((8, LANE), lambda i: (0, 0))],
        out_specs=[out, out], compiler_params=_params(("parallel",)))(proj, par8)


def _dt_bwd(ddt_g, dac_g, dt, proj, par8, dproj, nheads, dt_col, name):
    t = proj.shape[0]
    ng = ddt_g.shape[0]

    def body(dd_ref, da_ref, dt_ref, r_ref, p_ref, _, dp_ref, acc_ref):
        lane = lax.broadcasted_iota(jnp.int32, (1, LANE), 1)
        valid = lane < nheads
        ddt = sum(dd_ref[g] for g in range(ng))
        dac = sum(da_ref[g] for g in range(ng))
        a = jnp.where(valid, -jnp.exp(_rows(p_ref, 1)), 0.0)
        d_a = _xdot01(_chunk_tri(ROW_TILE, False), dac)
        ddt = ddt + d_a * a
        raw = r_ref[...][:, :LANE].astype(F32)
        draw = jnp.where(valid, ddt * _sigmoid(raw + _rows(p_ref, 0)), 0.0)
        dp_ref[...] = jnp.concatenate([draw, jnp.zeros_like(draw)], axis=1).astype(BF16)
        upd = jnp.concatenate([jnp.sum(draw, axis=0, keepdims=True),
                               jnp.sum(d_a * dt_ref[...], axis=0, keepdims=True) * a,
                               jnp.zeros((6, LANE), F32)], axis=0)

        @pl.when(pl.program_id(0) == 0)
        def _():
            acc_ref[...] = upd

        @pl.when(pl.program_id(0) > 0)
        def _():
            acc_ref[...] += upd

    row = pl.BlockSpec((ROW_TILE, LANE), lambda i: (i, 0))
    grp = pl.BlockSpec((ng, ROW_TILE, LANE), lambda i: (0, i, 0))
    win = pl.BlockSpec((ROW_TILE, 2 * LANE), lambda i: (i, dt_col))
    par = pl.BlockSpec((8, LANE), lambda i: (0, 0))
    return pl.pallas_call(
        body, name=name,
        out_shape=[jax.ShapeDtypeStruct(dproj.shape, BF16), jax.ShapeDtypeStruct((8, LANE), F32)],
        grid=(t // ROW_TILE,),
        in_specs=[grp, grp, row, win, par, pl.BlockSpec(memory_space=pl.ANY)],
        out_specs=[win, par], input_output_aliases={5: 0},
        compiler_params=_params(("arbitrary",)))(ddt_g, dac_g, dt, proj, par8, dproj)


def _ssd_chunk(xs, bm, cm, z, dt, ac, st, dsk, nw, g):
    q = xs.shape[0]
    lane_h = lax.broadcasted_iota(jnp.int32, (1, LANE), 1)
    sub_h = lax.broadcasted_iota(jnp.int32, (LANE, 1), 0)
    lane_c = lax.broadcasted_iota(jnp.int32, (1, GROUP_W), 1) // HEAD_DIM
    rr = lax.broadcasted_iota(jnp.int32, (q, q), 0)
    cc = lax.broadcasted_iota(jnp.int32, (q, q), 1)
    tril = rr >= cc
    last_row = jnp.where(lax.broadcasted_iota(jnp.int32, (q, 1), 0) == q - 1, 1.0, 0.0)
    act = ac.T
    col, row = [], []
    for r in range(HEADS_PER_GROUP):
        h = g * HEADS_PER_GROUP + r
        col.append(jnp.sum(ac * jnp.where(lane_h == h, 1.0, 0.0), axis=1, keepdims=True))
        row.append(jnp.sum(act * jnp.where(sub_h == h, 1.0, 0.0), axis=0, keepdims=True))

    def per_head(vals):
        out = jnp.broadcast_to(vals[0], (q, GROUP_W))
        for r in range(1, HEADS_PER_GROUP):
            out = jnp.where(lane_c == r, vals[r], out)
        return out

    spread = jnp.where(lax.broadcasted_iota(jnp.int32, (LANE, GROUP_W), 0)
                       == g * HEADS_PER_GROUP + lax.broadcasted_iota(jnp.int32, (LANE, GROUP_W), 1) // HEAD_DIM,
                       1.0, 0.0).astype(BF16)
    acx = _spread(ac, spread)
    x = xs * _spread(dt, spread)
    s = _bdot(cm, bm, "nt")
    y = per_head([_bdot(s * jnp.exp(jnp.where(tril, col[r] - row[r], -jnp.inf)), x, "nn")
                  for r in range(HEADS_PER_GROUP)])
    y = y + jnp.exp(acx) * _bdot(cm, st, "nn")
    last = jnp.sum(acx * last_row, axis=0, keepdims=True)
    new_st = st * jnp.exp(last) + _bdot(bm, x * jnp.exp(last - acx), "tn")
    y = y + dsk * xs
    y = y * (z * _sigmoid(z))
    yn = y * lax.rsqrt(jnp.mean(y * y, axis=1, keepdims=True) + EPS) * nw
    return yn, new_st


def _ssd_specs(nc, z_col, ngroups, rev):
    assert ngroups % SSD_GROUPS == 0 and z_col % SSD_GROUPS == 0
    z_col //= SSD_GROUPS
    b_col, c_col = 2 * ngroups // SSD_GROUPS, 3 * ngroups // SSD_GROUPS

    def ci(c):
        return nc - 1 - c if rev else c

    nx, nb = SSD_GROUPS * GROUP_W, SSD_GROUPS * N_STATE
    xs = pl.BlockSpec((SSD_Q, nx), lambda g, c: (ci(c), g))
    bm = pl.BlockSpec((SSD_Q, nb), lambda g, c: (ci(c), b_col + g))
    cm = pl.BlockSpec((SSD_Q, nb), lambda g, c: (ci(c), c_col + g))
    z = pl.BlockSpec((SSD_Q, nx), lambda g, c: (ci(c), z_col + g))
    hd = pl.BlockSpec((SSD_Q, LANE), lambda g, c: (ci(c), 0))
    par = pl.BlockSpec((8, nx), lambda g, c: (0, g))
    stb = pl.BlockSpec((None, N_STATE, nx), lambda g, c: (ci(c), 0, g))
    db = pl.BlockSpec((SSD_Q, nb), lambda g, c: (ci(c), g))
    return xs, bm, cm, z, hd, par, stb, db


def _ssd_fwd(xc, proj, dt, ac, par8, ngroups, z_col, name):
    t = xc.shape[0]
    nc = t // SSD_Q
    steps = ngroups // SSD_GROUPS

    def body(x_ref, b_ref, c_ref, z_ref, dt_ref, ac_ref, p_ref, y_ref, so_ref, st_ref):
        @pl.when(pl.program_id(1) == 0)
        def _():
            st_ref[...] = jnp.zeros_like(st_ref)

        so_ref[...] = st_ref[...]
        dtv, acv = dt_ref[...], ac_ref[...]
        for i in range(SSD_GROUPS):
            cx, cb = slice(i * GROUP_W, (i + 1) * GROUP_W), slice(i * N_STATE, (i + 1) * N_STATE)
            yn, new_st = _ssd_chunk(x_ref[:, cx].astype(F32), b_ref[:, cb].astype(F32), c_ref[:, cb].astype(F32),
                                    z_ref[:, cx].astype(F32), dtv, acv, st_ref[:, cx],
                                    p_ref[0:1, cx], p_ref[1:2, cx], pl.program_id(0) * SSD_GROUPS + i)
            y_ref[:, cx] = yn.astype(BF16)
            st_ref[:, cx] = new_st

    xs, bm, cm, zs, hd, par, stb, _ = _ssd_specs(nc, z_col, ngroups, False)
    return pl.pallas_call(
        body, name=name,
        out_shape=[jax.ShapeDtypeStruct((t, ngroups * GROUP_W), BF16),
                   jax.ShapeDtypeStruct((nc, N_STATE, ngroups * GROUP_W), F32)],
        grid=(steps, nc), in_specs=[xs, bm, cm, zs, hd, hd, par], out_specs=[xs, stb],
        scratch_shapes=[pltpu.VMEM((N_STATE, SSD_GROUPS * GROUP_W), F32)],
        compiler_params=_params(("parallel", "arbitrary")))(xc, xc, xc, proj, dt, ac, par8)


def _ssd_bwd(dyn, xc, proj, dt, ac, par8, states, dproj, ngroups, z_col, name, comm=None):
    t = xc.shape[0]
    nc = t // SSD_Q
    steps = ngroups // SSD_GROUPS

    n_cin = comm.n_in if comm else 0
    n_cout = comm.n_out if comm else 0

    def body(*refs):
        dy_ref, x_ref, b_ref, c_ref, z_ref, dt_ref, ac_ref, p_ref, s_ref = refs[:9]
        dz_ref, dx_ref, db_ref, dc_ref, ddt_ref, dac_ref, dp_ref = refs[10 + n_cin:17 + n_cin]
        ds_ref = refs[17 + n_cin + n_cout]
        c = pl.program_id(1)
        if comm:
            begin, end = comm.ops(refs[10:10 + n_cin], refs[17 + n_cin:17 + n_cin + n_cout], refs[-3:])
            pl.when((pl.program_id(0) == 0) & (c == 0))(begin)

        @pl.when(c == 0)
        def _():
            ds_ref[...] = jnp.zeros_like(ds_ref)

        dtv, acv = dt_ref[...], ac_ref[...]
        ddt_sum, dac_sum, upds = None, None, []
        for i in range(SSD_GROUPS):
            cx, cb = slice(i * GROUP_W, (i + 1) * GROUP_W), slice(i * N_STATE, (i + 1) * N_STATE)
            g = pl.program_id(0) * SSD_GROUPS + i
            args = (x_ref[:, cx].astype(F32), b_ref[:, cb].astype(F32), c_ref[:, cb].astype(F32),
                    z_ref[:, cx].astype(F32), dtv, acv, s_ref[:, cx], p_ref[0:1, cx], p_ref[1:2, cx])
            _, vjp = jax.vjp(lambda *a, g=g: _ssd_chunk(*a, g), *args)
            dxs, dbm, dcm, dz, ddt, dac, dst, ddsk, dnw = vjp((dy_ref[:, cx], ds_ref[:, cx]))
            ds_ref[:, cx] = dst
            dz_ref[:, cx] = dz.astype(BF16)
            dx_ref[:, cx] = dxs
            db_ref[:, cb] = dbm
            dc_ref[:, cb] = dcm
            ddt_sum = ddt if ddt_sum is None else ddt_sum + ddt
            dac_sum = dac if dac_sum is None else dac_sum + dac
            upds.append(jnp.concatenate([ddsk, dnw, jnp.zeros((6, GROUP_W), F32)], axis=0))
        ddt_ref[...] = ddt_sum
        dac_ref[...] = dac_sum
        upd = jnp.concatenate(upds, axis=1)

        @pl.when(c == 0)
        def _():
            dp_ref[...] = upd

        @pl.when(c > 0)
        def _():
            dp_ref[...] += upd

        if comm:
            pl.when((pl.program_id(0) == steps - 1) & (c == nc - 1))(end)

    xs, bm, cm, zs, hd, par, stb, db = _ssd_specs(nc, z_col, ngroups, True)
    hg = pl.BlockSpec((None, SSD_Q, LANE), lambda g, c: (g, nc - 1 - c, 0))
    any_spec = pl.BlockSpec(memory_space=pl.ANY)
    aliases = {9: 0}
    if comm:
        aliases.update({10 + i: 7 + o for i, o in comm.aliases.items()})
    res = pl.pallas_call(
        body, name=name,
        out_shape=[jax.ShapeDtypeStruct(dproj.shape, BF16),
                   jax.ShapeDtypeStruct((t, ngroups * GROUP_W), F32),
                   jax.ShapeDtypeStruct((t, ngroups * N_STATE), F32), jax.ShapeDtypeStruct((t, ngroups * N_STATE), F32),
                   jax.ShapeDtypeStruct((steps, t, LANE), F32), jax.ShapeDtypeStruct((steps, t, LANE), F32),
                   jax.ShapeDtypeStruct((8, ngroups * GROUP_W), F32)] + (comm.out_shape if comm else []),
        grid=(steps, nc),
        in_specs=[xs, xs, bm, cm, zs, hd, hd, par, stb, any_spec] + [any_spec] * n_cin,
        out_specs=[zs, xs, db, db, hg, hg, par] + [any_spec] * n_cout, input_output_aliases=aliases,
        scratch_shapes=[pltpu.VMEM((N_STATE, SSD_GROUPS * GROUP_W), F32)] + (comm.sems if comm else []),
        compiler_params=_params(("arbitrary", "arbitrary")))(
            dyn, xc, xc, xc, proj, dt, ac, par8, states, dproj, *(comm.operands if comm else []))
    return tuple(res[:7]), list(res[7:])


def _small_matmul(a, b, mode, name):
    if mode == "nn":
        shape = (a.shape[0], b.shape[1])
    else:
        shape = (a.shape[1], b.shape[1])

    def body(a_ref, b_ref, o_ref):
        o_ref[...] = _dot(a_ref[...], b_ref[...], mode)

    return pl.pallas_call(body, name=name, out_shape=jax.ShapeDtypeStruct(shape, F32),
                          compiler_params=_params())(a, b)


def _adamw(parts, w, m, v, name):
    npart, rows, cols = parts.shape
    tr = rows if rows <= 128 else _tile8(rows, 128)

    def body(p_ref, w_ref, m_ref, v_ref, g_ref, d_ref, nm_ref, nv_ref):
        g = p_ref[0].astype(F32)
        for j in range(1, npart):
            g = g + p_ref[j].astype(F32)
        mm = ADAM_B1 * m_ref[...] + (1.0 - ADAM_B1) * g
        vv = ADAM_B2 * v_ref[...] + (1.0 - ADAM_B2) * (g * g)
        m_hat = mm / (1.0 - ADAM_B1 ** ADAM_STEP)
        v_hat = vv / (1.0 - ADAM_B2 ** ADAM_STEP)
        g_ref[...] = g
        d_ref[...] = -ADAM_LR * (m_hat / (jnp.sqrt(v_hat) + ADAM_EPS) + ADAM_WD * w_ref[...])
        nm_ref[...] = mm
        nv_ref[...] = vv

    blk = pl.BlockSpec((tr, cols), lambda i: (i, 0))
    out = jax.ShapeDtypeStruct((rows, cols), F32)
    return pl.pallas_call(
        body, name=name, out_shape=[out] * 4, grid=(rows // tr,),
        in_specs=[pl.BlockSpec((npart, tr, cols), lambda i: (0, i, 0)), blk, blk, blk], out_specs=[blk] * 4,
        compiler_params=_params(("parallel",)))(parts, w, m, v)


def _tile8(n, target):
    best = 8
    for d in range(8, target + 1, 8):
        if n % d == 0:
            best = d
    return best


def _pad_rows8(a):
    return jnp.concatenate([a, jnp.zeros((8 - a.shape[0], a.shape[1]), a.dtype)], axis=0)


def _pad_lanes(a, n):
    return jnp.concatenate([a, jnp.zeros(a.shape[:-1] + (n - a.shape[-1],), a.dtype)], axis=-1)


def _pack(vecs, mult):
    flat = jnp.concatenate([v.reshape(-1) for v in vecs])
    pad = (-flat.shape[0]) % mult
    flat = jnp.concatenate([flat, jnp.zeros((pad,), flat.dtype)])
    return flat.reshape(-1, LANE)


def _unpack(flat, shapes):
    out, o = [], 0
    for s in shapes:
        n = int(np.prod(s))
        out.append(flat[o:o + n].reshape(s))
        o += n
    return out


def kernel(x, c, w_ada, b_ada, ln1, ln2, w_in, conv_w, ssm_conv_w, ssm_conv_b, dt_bias, a_log, d_skip, ssm_norm_w, w_conv_out, w_ssm_out, w_o, w_up, w_down, final_norm, loss_target, m_w_ada, m_b_ada, m_ln1, m_ln2, m_w_in, m_conv_w, m_ssm_conv_w, m_ssm_conv_b, m_dt_bias, m_a_log, m_d_skip, m_ssm_norm_w, m_w_conv_out, m_w_ssm_out, m_w_o, m_w_up, m_w_down, m_final_norm, v_w_ada, v_b_ada, v_ln1, v_ln2, v_w_in, v_conv_w, v_ssm_conv_w, v_ssm_conv_b, v_dt_bias, v_a_log, v_d_skip, v_ssm_norm_w, v_w_conv_out, v_w_ssm_out, v_w_o, v_w_up, v_w_down, v_final_norm):
    names = ["w_ada", "b_ada", "ln1", "ln2", "w_in", "conv_w", "ssm_conv_w", "ssm_conv_b", "dt_bias", "a_log",
             "d_skip", "ssm_norm_w", "w_conv_out", "w_ssm_out", "w_o", "w_up", "w_down", "final_norm"]
    w_of = dict(zip(names, [w_ada, b_ada, ln1, ln2, w_in, conv_w, ssm_conv_w, ssm_conv_b, dt_bias, a_log, d_skip,
                            ssm_norm_w, w_conv_out, w_ssm_out, w_o, w_up, w_down, final_norm]))
    m_of = dict(zip(names, [m_w_ada, m_b_ada, m_ln1, m_ln2, m_w_in, m_conv_w, m_ssm_conv_w, m_ssm_conv_b, m_dt_bias,
                            m_a_log, m_d_skip, m_ssm_norm_w, m_w_conv_out, m_w_ssm_out, m_w_o, m_w_up, m_w_down,
                            m_final_norm]))
    v_of = dict(zip(names, [v_w_ada, v_b_ada, v_ln1, v_ln2, v_w_in, v_conv_w, v_ssm_conv_w, v_ssm_conv_b, v_dt_bias,
                            v_a_log, v_d_skip, v_ssm_norm_w, v_w_conv_out, v_w_ssm_out, v_w_o, v_w_up, v_w_down,
                            v_final_norm]))

    _, t, d = x.shape
    nl = w_ada.shape[0]
    ada_w = w_ada.shape[2]
    ds = ssm_norm_w.shape[1]
    nh = dt_bias.shape[1]
    ng = nh // HEADS_PER_GROUP
    gn = ng * N_STATE
    xbc_w = ds + 2 * gn
    z_off, xbc_off = 5 * d, 5 * d + ds
    dt_off = xbc_off + xbc_w
    proj_w = dt_off + nh
    pw = dt_off + 2 * LANE
    me = 4 * lax.axis_index("x") + 2 * lax.axis_index("y") + lax.axis_index("c")

    x2, tgt = x[0], loss_target[0]

    c_act = c * jax.nn.sigmoid(c)
    sizes1 = [(1, d), conv_w.shape, ssm_conv_w.shape]
    (g1,) = _all_gather([_pack([c_act, conv_w, ssm_conv_w], LANE)], "ag_small_in")
    parts1 = [_unpack(g1[j].reshape(-1), sizes1) for j in range(N_DEV)]
    c_act_all = jnp.concatenate([p[0] for p in parts1], axis=0)
    conv_w_full = jnp.concatenate([p[1] for p in parts1], axis=-1)
    sconv_w_full = jnp.concatenate([p[2] for p in parts1], axis=-1)

    mod_part = jnp.stack([_small_matmul(c_act_all, w_ada[l], "nn", f"ada_fwd{l}") for l in range(nl)])
    (gmod,) = _all_gather([mod_part], "ag_mod")
    mod = lax.dynamic_index_in_dim(gmod, me, axis=2, keepdims=False)
    mod = jnp.moveaxis(mod, 0, 1).reshape(nl, N_DEV * ada_w) + b_ada
    mod = mod.reshape(nl, 6, d)
    modp = [jnp.concatenate([mod[l], ln1[l][None], ln2[l][None]], axis=0) for l in range(nl)]

    big = ["w_in", "w_up", "w_conv_out", "w_ssm_out", "w_o", "w_down"]
    rest = big[1:]

    def blocks_of(l, keys):
        return [w_of[k][l].astype(BF16) for k in keys]

    def w_in_halves(l):
        w = w_of["w_in"][l].astype(BF16)
        return [w[:d // 2]], [w[d // 2:]]

    def full_cols(g):
        return jnp.moveaxis(g, 0, 1).reshape(g.shape[1], N_DEV * g.shape[2])

    def full_rows(g):
        return g.reshape(N_DEV * g.shape[1], g.shape[2])

    def pad_w_in(pieces):
        w = jnp.concatenate([full_cols(g) for g in pieces], axis=0)
        return jnp.concatenate([w, jnp.zeros((d, pw - proj_w), BF16)], axis=-1)

    wi = [None] * nl
    wi[0] = pad_w_in(_all_gather(blocks_of(0, ["w_in"]), "ag_w_in0"))

    cw8 = [_pad_rows8(conv_w_full[l]) for l in range(nl)]
    sw8 = [_pad_rows8(sconv_w_full[l]) for l in range(nl)]
    sb8 = [_pad_rows8(ssm_conv_b[l][None]) for l in range(nl)]
    dtp8 = [_pad_rows8(_pad_lanes(jnp.stack([dt_bias[l], a_log[l]]), LANE)) for l in range(nl)]
    sp8 = [_pad_rows8(jnp.stack([jnp.repeat(d_skip[l], HEAD_DIM), ssm_norm_w[l]])) for l in range(nl)]
    fn8 = _pad_rows8(final_norm[None])
    xbc_col, ncol, dt_col, z_col = xbc_off // d, xbc_w // d, dt_off // (2 * LANE), z_off // GROUP_W

    saved = []
    x_cur, br_prev = x2, None
    for l in range(nl):
        if l == 0:
            x_in, u1 = x_cur, _norm_fwd(x_cur, modp[0], 0, "norm_first")
        else:
            x_in, u1 = _resid_norm_fwd(x_cur, br_prev, modp[l - 1], 5, modp[l], 0, f"resid_norm_a{l}")
        proj, got = _matmul(u1, wi[l], "nn", BF16, f"mm_in{l}", comm=_GatherComm(blocks_of(l, rest)))
        got = dict(zip(rest, got))
        wup, wdown = full_cols(got["w_up"]), full_rows(got["w_down"])
        wco, wso, wo = full_rows(got["w_conv_out"]), full_rows(got["w_ssm_out"]), full_rows(got["w_o"])
        y_conv = _gconv_fwd(proj, cw8[l], d, f"gconv_fwd{l}")
        xc, pre = _sconv_fwd(proj, sw8[l], sb8[l], d, xbc_col, ncol, f"sconv_fwd{l}")
        dt, ac = _dt_fwd(proj, dtp8[l], nh, dt_col, f"dt_fwd{l}")
        yn, states = _ssd_fwd(xc, proj, dt, ac, sp8[l], ng, z_col, f"ssd_fwd{l}")
        p_conv = _matmul(y_conv, wco, "nn", BF16, f"mm_conv_out{l}")
        p_ssm = _matmul(yn, wso, "nn", BF16, f"mm_ssm_out{l}")
        merged = _merge_fwd(proj, p_conv, p_ssm, f"merge_fwd{l}")
        mix = _matmul(merged, wo, "nn", BF16, f"mm_o{l}")
        x_mid, u2 = _resid_norm_fwd(x_in, mix, modp[l], 2, modp[l], 1, f"resid_norm_b{l}")
        if l + 1 < nl:
            top, bot = w_in_halves(l + 1)
            (h, hid), g_top = _matmul(u2, wup, "nn", BF16, f"mm_up{l}", epi="relu2", comm=_GatherComm(top))
            mlp, g_bot = _matmul(hid, wdown, "nn", BF16, f"mm_down{l}", comm=_GatherComm(bot))
            wi[l + 1] = pad_w_in(g_top + g_bot)
        else:
            h, hid = _matmul(u2, wup, "nn", BF16, f"mm_up{l}", epi="relu2")
            mlp = _matmul(hid, wdown, "nn", BF16, f"mm_down{l}")
        saved.append(dict(x_in=x_in, u1=u1, proj=proj, y_conv=y_conv, xc=xc, pre=pre, dt=dt, ac=ac, yn=yn,
                          states=states,
                          p_conv=p_conv, p_ssm=p_ssm, merged=merged, mix=mix, x_mid=x_mid, u2=u2, h=h, hid=hid,
                          mlp=mlp, wup=wup, wdown=wdown, wco=wco, wso=wso, wo=wo))
        x_cur, br_prev = x_mid, mlp

    dx, dbr, acc = _final_fwd_bwd(x_cur, br_prev, modp[nl - 1], fn8, tgt, "final")
    loss = lax.psum(acc[0, 0], ("x", "y", "c"))
    g_final_norm = acc[1]
    dgate2 = acc[2]

    gw = {k: [None] * nl for k in big}
    small = {k: [None] * nl for k in ["mod", "ln1", "ln2", "conv_w", "ssm_conv_w", "ssm_conv_b", "dt_bias",
                                      "a_log", "d_skip", "ssm_norm_w"]}

    def grad_cols(a):
        return jnp.moveaxis(a.reshape(a.shape[0], N_DEV, a.shape[1] // N_DEV), 1, 0)

    def grad_rows(a):
        return a.reshape(N_DEV, a.shape[0] // N_DEV, a.shape[1])

    def recv_buf(name):
        return jnp.zeros((N_DEV, nl) + w_of[name].shape[1:], BF16)

    bufs_a, bufs_b = [recv_buf("w_in")], [recv_buf(k) for k in big[1:]]
    pending = None
    for l in reversed(range(nl)):
        s = saved[l]
        dh = _matmul(dbr, s["wdown"], "nt", BF16, f"mm_down_dx{l}", epi="relu2_bwd", extra=s["h"])
        gw["w_down"][l] = _matmul(s["hid"], dbr, "tn", BF16, f"mm_down_dw{l}")
        du2 = _matmul(dh, s["wup"], "nt", BF16, f"mm_up_dx{l}")
        gw["w_up"][l] = _matmul(s["u2"], dh, "tn", BF16, f"mm_up_dw{l}")
        dx_mid, dmix, acc2 = _resid_norm_bwd(dx, du2, s["x_mid"], modp[l], 1, f"resid_norm_b_bwd{l}",
                                             br=s["mix"], mp_gate=modp[l], gate_row=2)
        dmerged = _matmul(dmix, s["wo"], "nt", BF16, f"mm_o_dx{l}")
        gw["w_o"][l] = _matmul(s["merged"], dmix, "tn", BF16, f"mm_o_dw{l}")
        dpc, dps, dgl = _merge_bwd(dmerged, s["proj"], s["p_conv"], s["p_ssm"], f"merge_bwd{l}")
        dyc = _matmul(dpc, s["wco"], "nt", F32, f"mm_conv_out_dx{l}")
        dproj, dcw = _gconv_bwd(dyc, s["proj"], cw8[l], dgl, d, f"gconv_bwd{l}")
        gw["w_conv_out"][l] = _matmul(s["y_conv"], dpc, "tn", BF16, f"mm_conv_out_dw{l}")
        dyn = _matmul(dps, s["wso"], "nt", F32, f"mm_ssm_out_dx{l}")
        gw["w_ssm_out"][l] = _matmul(s["yn"], dps, "tn", BF16, f"mm_ssm_out_dw{l}")
        rest_grads = [grad_cols(gw["w_up"][l])] + [grad_rows(gw[k][l]) for k in big[2:]]
        sends, bufs, layers = rest_grads, list(bufs_b), [l] * len(rest_grads)
        if pending is not None:
            sends, bufs, layers = sends + pending, bufs + list(bufs_a), layers + [l + 1]
        (dproj, dxs, dbm, dcm, ddt_g, dac_g, dsp), bufs = _ssd_bwd(
            dyn, s["xc"], s["proj"], s["dt"], s["ac"], sp8[l], s["states"], dproj, ng, z_col, f"ssd_bwd{l}",
            comm=_ExchangeComm(sends, bufs, layers))
        bufs_b = bufs[:len(rest_grads)]
        if pending is not None:
            bufs_a = bufs[len(rest_grads):]
        dsw = []
        for seg, (dseg, c0) in enumerate([(dxs, 0), (dbm, ds // d), (dcm, ds // d + gn // d)]):
            cols = slice(c0 * d, c0 * d + dseg.shape[1])
            dproj, part = _sconv_bwd(dseg, s["pre"], c0, s["proj"], sw8[l][:, cols], dproj, d, xbc_col + c0,
                                     dseg.shape[1] // d, f"sconv_bwd{l}_{seg}")
            dsw.append(part)
        dsw = jnp.concatenate(dsw, axis=1)
        dproj, ddtp = _dt_bwd(ddt_g, dac_g, s["dt"], s["proj"], dtp8[l], dproj, nh, dt_col, f"dt_bwd{l}")
        du1 = _matmul(dproj, wi[l], "nt", BF16, f"mm_in_dx{l}")

        def w_in_blocks(g):
            return grad_cols(g[:, :proj_w])

        if l > 0:
            pending = [w_in_blocks(_matmul(s["u1"], dproj, "tn", BF16, f"mm_in_dw{l}"))]
        else:
            g_top = _matmul(s["u1"][:, :d // 2], dproj, "tn", BF16, "mm_in_dw0_top")
            g_bot, bufs_a = _matmul(s["u1"][:, d // 2:], dproj, "tn", BF16, "mm_in_dw0_bot",
                                    comm=_ExchangeComm([w_in_blocks(g_top)], bufs_a, 0, row0=0))
            pending = [w_in_blocks(g_bot)]
        if l > 0:
            dx, dbr, acc1 = _resid_norm_bwd(dx_mid, du1, s["x_in"], modp[l], 0, f"resid_norm_a_bwd{l}",
                                            br=saved[l - 1]["mlp"], mp_gate=modp[l - 1], gate_row=5)
        else:
            dx, acc1 = _resid_norm_bwd(dx_mid, du1, s["x_in"], modp[0], 0, "norm_first_bwd")
        sc1, sc2 = mod[l, 1], mod[l, 4]
        small["mod"][l] = jnp.stack([acc1[0], acc1[1] * ln1[l], acc2[2], acc2[0], acc2[1] * ln2[l], dgate2])
        small["ln1"][l] = acc1[1] * (1.0 + sc1)
        small["ln2"][l] = acc2[1] * (1.0 + sc2)
        small["conv_w"][l] = dcw[:3]
        small["ssm_conv_w"][l] = dsw[:4]
        small["ssm_conv_b"][l] = dsw[4]
        small["dt_bias"][l] = ddtp[0, :nh]
        small["a_log"][l] = ddtp[1, :nh]
        small["d_skip"][l] = dsp[0].reshape(nh, HEAD_DIM).sum(axis=-1)
        small["ssm_norm_w"][l] = dsp[1]
        if l > 0:
            dgate2 = acc1[2]
    grad_x = dx[None]

    sm = {k: jnp.stack(v) for k, v in small.items()}
    rep_names = ["b_ada", "ln1", "ln2", "ssm_conv_b", "dt_bias", "a_log", "d_skip", "ssm_norm_w", "final_norm"]
    rep_grads = [sm["mod"].reshape(nl, 6 * d), sm["ln1"], sm["ln2"], sm["ssm_conv_b"], sm["dt_bias"], sm["a_log"],
                 sm["d_skip"], sm["ssm_norm_w"], g_final_norm]
    rep_pack = _pack(rep_grads, 8 * LANE)
    conv_pack = _pack([sm["conv_w"], sm["ssm_conv_w"]], 8 * LANE)
    g_rep, g_conv = _all_gather([rep_pack, conv_pack], "ag_small_grads")

    outs = {}

    def run_adamw(name, parts, tag):
        w, m, v = w_of[name], m_of[name], v_of[name]
        shp = w.shape
        r2 = (int(np.prod(shp[:-1])), shp[-1])
        res = _adamw(parts.reshape((parts.shape[0],) + r2), w.reshape(r2), m.reshape(r2), v.reshape(r2), tag)
        outs[name] = [a.reshape(shp) for a in res]

    rep_shapes = [w_of[k].shape for k in rep_names]
    res = _adamw(g_rep, _pack([w_of[k] for k in rep_names], 8 * LANE), _pack([m_of[k] for k in rep_names], 8 * LANE),
                 _pack([v_of[k] for k in rep_names], 8 * LANE), "adamw_replicated")
    for k, vals in zip(rep_names, zip(*[_unpack(a.reshape(-1), rep_shapes) for a in res])):
        outs[k] = list(vals)

    conv_parts = [_unpack(g_conv[j].reshape(-1), [sm["conv_w"].shape, sm["ssm_conv_w"].shape]) for j in range(N_DEV)]
    for idx, name in enumerate(["conv_w", "ssm_conv_w"]):
        wsh = w_of[name].shape[-1]
        full = jnp.stack([p[idx] for p in conv_parts])
        run_adamw(name, lax.dynamic_slice_in_dim(full, me * wsh, wsh, axis=3), "adamw_" + name)

    dmod_all = g_rep.reshape(N_DEV, -1)[:, :nl * 6 * d].reshape(N_DEV, nl, 6 * d)
    dmod_mine = lax.dynamic_slice_in_dim(dmod_all, me * ada_w, ada_w, axis=2)
    g_ada = jnp.stack([_small_matmul(c_act_all, dmod_mine[:, l], "tn", f"ada_bwd{l}") for l in range(nl)])
    run_adamw("w_ada", g_ada[None], "adamw_w_ada")

    bufs_a = _exchange(pending, bufs_a, 0, "a2a_w_in0_bot", row0=d // 2)
    for name, parts in zip(big, list(bufs_a) + list(bufs_b)):
        run_adamw(name, parts, "adamw_" + name)

    result = [loss, grad_x]
    for i in range(4):
        result += [outs[k][i] for k in names]
    return tuple(result)
```

```python
import functools

import numpy as np
import jax
import jax.numpy as jnp
from jax import lax
from jax.experimental import pallas as pl
from jax.experimental.pallas import tpu as pltpu

F32 = jnp.float32
BF16 = jnp.bfloat16
EPS = 1e-6
N_STATE = 128
HEAD_DIM = 64
HEADS_PER_GROUP = 4
GROUP_W = HEAD_DIM * HEADS_PER_GROUP
SSD_GROUPS = 2
SSD_Q = 256
N_DEV = 8
ROW_TILE = 256
LANE = 128
VMEM_LIMIT = 56 * 1024 * 1024
MATMUL_VMEM_BUDGET = 40 * 1024 * 1024
MATMUL_TILE_CAP = 2048
MXU_FLOPS = 9.0e14
HBM_BYTES_PER_S = 3.0e12
STEP_OVERHEAD_S = 0.35e-6

ADAM_LR, ADAM_B1, ADAM_B2, ADAM_EPS, ADAM_WD, ADAM_STEP = 0.001, 0.9, 0.999, 1e-08, 0.01, 10

MESH = pl.DeviceIdType.MESH


def _params(sem=None):
    return pltpu.CompilerParams(dimension_semantics=sem, vmem_limit_bytes=VMEM_LIMIT)


def _divisors(n, cap):
    if n <= cap:
        return [n]
    return [d for d in range(cap - cap % LANE, 0, -LANE) if n % d == 0]


def _matmul_tiles(m, n, k, out_bytes, n_out, n_extra):
    best = None
    for tm in _divisors(m, MATMUL_TILE_CAP):
        for tn in _divisors(n, MATMUL_TILE_CAP):
            for tk in _divisors(k, MATMUL_TILE_CAP):
                out_tile = tm * tn * (out_bytes * n_out + 2 * n_extra)
                vmem = 2 * 2 * (tm * tk + tk * tn) + 2 * out_tile + (4 * tm * tn if tk < k else 0)
                if vmem > MATMUL_VMEM_BUDGET:
                    continue
                steps = (m // tm) * (n // tn) * (k // tk)
                a_reads = 1 if tk == k else n // tn
                traffic = 2 * (a_reads * m * k + (m // tm) * k * n) + m * n * (out_bytes * n_out + 2 * n_extra)
                est = (max(2.0 * m * n * k / MXU_FLOPS, traffic / HBM_BYTES_PER_S) + steps * STEP_OVERHEAD_S
                       + (2 * (tm * tk + tk * tn) + out_tile) / HBM_BYTES_PER_S)
                cand = (est, tm, tn, tk)
                if best is None or cand < best:
                    best = cand
    assert best is not None, (m, n, k)
    return best[1:]


def _tile(n, target):
    if n <= target:
        return n
    best = None
    for d in range(LANE, target + 1, LANE):
        if n % d == 0:
            best = d
    assert best is not None, (n, target)
    return best


def _slot(p):
    return 4 * p[0] + 2 * p[1] + p[2]


def _all_gather(xs, name):
    comm = _GatherComm(xs)

    def body(*refs):
        begin, end = comm.ops(refs[:comm.n_in], refs[comm.n_in:comm.n_in + comm.n_out], refs[-3:])
        begin()
        end()

    any_spec = pl.BlockSpec(memory_space=pl.ANY)
    return pl.pallas_call(
        body, name=name, out_shape=comm.out_shape,
        in_specs=[any_spec] * comm.n_in, out_specs=[any_spec] * comm.n_out, scratch_shapes=comm.sems,
    )(*comm.operands)


class _GatherComm:
    def __init__(self, xs):
        n = len(xs)
        self.operands = list(xs)
        self.n_in = self.n_out = n
        self.aliases = {}
        self.out_shape = [jax.ShapeDtypeStruct((N_DEV,) + x.shape, x.dtype) for x in xs]
        self.sems = [pltpu.SemaphoreType.DMA((7 * n,)), pltpu.SemaphoreType.DMA((7 * n,)),
                     pltpu.SemaphoreType.DMA((n,))]

    def ops(self, x_refs, o_refs, sem_refs):
        n = self.n_in
        send, recv, loc = sem_refs
        x, y, c = lax.axis_index("x"), lax.axis_index("y"), lax.axis_index("c")
        me, sib = (x, y, c), (x, y, 1 - c)
        chips = [(1 - x, y), (x, 1 - y), (1 - x, 1 - y)]

        def cp(a, k, block, to, src=None):
            dst = o_refs[a].at[_slot(block)]
            return pltpu.make_async_remote_copy(
                src_ref=dst if src is None else src, dst_ref=dst,
                send_sem=send.at[7 * a + k], recv_sem=recv.at[7 * a + k],
                device_id=to, device_id_type=MESH)

        mine = [pltpu.make_async_copy(x_refs[a], o_refs[a].at[_slot(me)], loc.at[a]) for a in range(n)]
        first = []
        for a in range(n):
            first.append(cp(a, 0, me, sib, src=x_refs[a]))
            first += [cp(a, 1 + j, me, (*chip, c), src=x_refs[a]) for j, chip in enumerate(chips)]

        def begin():
            for m in mine:
                m.start()
            for f in first:
                f.start()

        def end():
            passed = []
            for j, chip in enumerate(chips):
                for a in range(n):
                    cp(a, 1 + j, (*chip, c), me).wait_recv()
                    p = cp(a, 4 + j, (*chip, c), sib)
                    p.start()
                    passed.append(p)
            for a in range(n):
                cp(a, 0, sib, me).wait_recv()
                for j, chip in enumerate(chips):
                    cp(a, 4 + j, (*chip, 1 - c), me).wait_recv()
            for f in first + passed:
                f.wait_send()
            for m in mine:
                m.wait()

        return begin, end


class _ExchangeComm:
    def __init__(self, xs, bufs, layer, row0=0, windows=None):
        n = len(xs)
        self.layers = list(layer) if isinstance(layer, (list, tuple)) else [layer] * n
        self.row0 = row0
        self.windows = list(windows) if windows else [None] * n
        self.nrows = [x.shape[1] if wdw is None else x.shape[0] for x, wdw in zip(xs, self.windows)]
        self.operands = list(xs) + list(bufs)
        self.n_in, self.n_out = 2 * n, n
        self.aliases = {n + a: a for a in range(n)}
        self.out_shape = [jax.ShapeDtypeStruct(b.shape, b.dtype) for b in bufs]
        self.sems = [pltpu.SemaphoreType.DMA((7 * n,)), pltpu.SemaphoreType.DMA((7 * n,)),
                     pltpu.SemaphoreType.DMA((n,))]

    def ops(self, in_refs, o_refs, sem_refs):
        n = self.n_out
        x_refs = in_refs[:n]
        send, recv, loc = sem_refs
        x, y, c = lax.axis_index("x"), lax.axis_index("y"), lax.axis_index("c")
        me = (x, y, c)
        peers = []
        for k in range(1, 8):
            kx, ky, kc = (k >> 2) & 1, (k >> 1) & 1, k & 1
            peers.append((x + kx - 2 * x * kx, y + ky - 2 * y * ky, c + kc - 2 * c * kc))

        def land(a, slot):
            return o_refs[a].at[slot, self.layers[a], pl.ds(self.row0, self.nrows[a])]

        def block(a, slot):
            if self.windows[a] is None:
                return x_refs[a].at[slot]
            width, stride = self.windows[a]
            start = pl.multiple_of((stride * slot) // LANE * LANE, LANE)
            return x_refs[a].at[:, pl.ds(start, width)]

        def cp(a, k, src_slot, dst_slot, to):
            return pltpu.make_async_remote_copy(
                src_ref=block(a, src_slot), dst_ref=land(a, dst_slot),
                send_sem=send.at[7 * a + k], recv_sem=recv.at[7 * a + k],
                device_id=to, device_id_type=MESH)

        mine = [pltpu.make_async_copy(block(a, _slot(me)), land(a, _slot(me)), loc.at[a])
                for a in range(n)]
        sends = [cp(a, k, _slot(p), _slot(me), p) for a in range(n) for k, p in enumerate(peers)]

        def begin():
            for m in mine:
                m.start()
            for s in sends:
                s.start()

        def end():
            for a in range(n):
                for k, p in enumerate(peers):
                    cp(a, k, _slot(me), _slot(p), me).wait_recv()
            for s in sends:
                s.wait_send()
            for m in mine:
                m.wait()

        return begin, end


def _exchange(xs, bufs, layer, name, row0=0, windows=None):
    comm = _ExchangeComm(xs, bufs, layer, row0, windows)

    def body(*refs):
        begin, end = comm.ops(refs[:comm.n_in], refs[comm.n_in:comm.n_in + comm.n_out], refs[-3:])
        begin()
        end()

    any_spec = pl.BlockSpec(memory_space=pl.ANY)
    return pl.pallas_call(
        body, name=name, out_shape=comm.out_shape,
        in_specs=[any_spec] * comm.n_in, out_specs=[any_spec] * comm.n_out, scratch_shapes=comm.sems,
        input_output_aliases=dict(comm.aliases),
    )(*comm.operands)


_DIMS = {"nn": (((1,), (0,)), ((), ())), "nt": (((1,), (1,)), ((), ())), "tn": (((0,), (0,)), ((), ()))}


def _matmul(a, b, mode, out_dtype, name, comm=None, epi=None, extra=None):
    if mode == "nn":
        (m, k), (_, n) = a.shape, b.shape
    elif mode == "nt":
        (m, k), (n, _) = a.shape, b.shape
    else:
        (k, m), (_, n) = a.shape, b.shape
    n_ext = 1 if epi == "relu2_bwd" else 0
    n_main = 2 if epi == "relu2" else 1
    tm, tn, tk = _matmul_tiles(m, n, k, jnp.dtype(out_dtype).itemsize, n_main, n_ext)
    nk = k // tk
    grid = (m // tm, n // tn, nk)
    dn = _DIMS[mode]
    n_cin = comm.n_in if comm else 0
    n_cout = comm.n_out if comm else 0
    n_acc = 0 if nk == 1 else 1
    first_in, first_out = 2 + n_ext, 2 + n_ext + n_cin

    def finish(acc, refs):
        outs = refs[first_out:first_out + n_main]
        if epi == "relu2":
            outs[0][...] = acc.astype(outs[0].dtype)
            r = jnp.maximum(acc, 0.0)
            outs[1][...] = (r * r).astype(outs[1].dtype)
        elif epi == "relu2_bwd":
            outs[0][...] = (acc * (2.0 * jnp.maximum(refs[2][...].astype(F32), 0.0))).astype(outs[0].dtype)
        else:
            outs[0][...] = acc.astype(outs[0].dtype)

    def body(*refs):
        a_ref, b_ref = refs[:2]
        scratch = refs[first_out + n_main + n_cout:]
        pid = [pl.program_id(ax) for ax in range(3)]
        if comm:
            begin, end = comm.ops(refs[first_in:first_in + n_cin],
                                  refs[first_out + n_main:first_out + n_main + n_cout], scratch[n_acc:])
            pl.when((pid[0] == 0) & (pid[1] == 0) & (pid[2] == 0))(begin)
        part = lax.dot_general(a_ref[...], b_ref[...], dn, preferred_element_type=F32)
        if nk == 1:
            finish(part, refs)
        else:
            acc_ref = scratch[0]

            @pl.when(pid[2] == 0)
            def _():
                acc_ref[...] = part

            @pl.when(pid[2] > 0)
            def _():
                acc_ref[...] += part

            @pl.when(pid[2] == nk - 1)
            def _():
                finish(acc_ref[...], refs)
        if comm:
            pl.when((pid[0] == grid[0] - 1) & (pid[1] == grid[1] - 1) & (pid[2] == grid[2] - 1))(end)

    a_spec = pl.BlockSpec((tk, tm), lambda i, j, kk: (kk, i)) if mode == "tn" else \
        pl.BlockSpec((tm, tk), lambda i, j, kk: (i, kk))
    b_spec = pl.BlockSpec((tn, tk), lambda i, j, kk: (j, kk)) if mode == "nt" else \
        pl.BlockSpec((tk, tn), lambda i, j, kk: (kk, j))
    o_spec = pl.BlockSpec((tm, tn), lambda i, j, kk: (i, j))
    any_spec = pl.BlockSpec(memory_space=pl.ANY)
    res = pl.pallas_call(
        body, name=name,
        out_shape=[jax.ShapeDtypeStruct((m, n), out_dtype)] * n_main + (comm.out_shape if comm else []),
        grid=grid,
        in_specs=[a_spec, b_spec] + [o_spec] * n_ext + [any_spec] * n_cin,
        out_specs=[o_spec] * n_main + [any_spec] * n_cout,
        scratch_shapes=([] if nk == 1 else [pltpu.VMEM((tm, tn), F32)]) + (comm.sems if comm else []),
        input_output_aliases={first_in + i: n_main + o for i, o in comm.aliases.items()} if comm else {},
        compiler_params=_params(("arbitrary",) * 3 if comm else ("parallel", "parallel", "arbitrary")),
    )(a, b, *([extra] if n_ext else []), *(comm.operands if comm else []))
    main = res[0] if n_main == 1 else tuple(res[:n_main])
    return (main, list(res[n_main:])) if comm else main


def _dot(a, b, mode):
    return lax.dot_general(a.astype(BF16), b.astype(BF16), _DIMS[mode], preferred_element_type=F32)


@functools.partial(jax.custom_vjp, nondiff_argnums=(2,))
def _bdot(a, b, mode):
    return _dot(a, b, mode)


def _bdot_fwd(a, b, mode):
    return _dot(a, b, mode), (a, b)


def _bdot_bwd(mode, res, g):
    a, b = res
    if mode == "nn":
        return _dot(g, b, "nt"), _dot(a, g, "tn")
    if mode == "nt":
        return _dot(g, b, "nn"), _dot(g, a, "tn")
    return _dot(b, g, "nt"), _dot(a, g, "nn")


_bdot.defvjp(_bdot_fwd, _bdot_bwd)


def _split3(x):
    hi = x.astype(BF16)
    r1 = x - hi.astype(F32)
    mid = r1.astype(BF16)
    return hi, mid, (r1 - mid.astype(F32)).astype(BF16)


def _xdot01(m01, x):
    return sum(lax.dot_general(m01, p, _DIMS["nn"], preferred_element_type=F32) for p in _split3(x))


@jax.custom_vjp
def _spread(v, m01):
    return sum(lax.dot_general(p, m01, _DIMS["nn"], preferred_element_type=F32) for p in _split3(v))


def _spread_fwd(v, m01):
    return _spread(v, m01), m01


def _spread_bwd(m01, g):
    dv = sum(lax.dot_general(p, m01, _DIMS["nt"], preferred_element_type=F32) for p in _split3(g)[:2])
    return dv, jnp.zeros_like(m01)


_spread.defvjp(_spread_fwd, _spread_bwd)


def _chunk_tri(n, lower):
    r = lax.broadcasted_iota(jnp.int32, (n, n), 0)
    c = lax.broadcasted_iota(jnp.int32, (n, n), 1)
    tri = (r >= c) if lower else (r <= c)
    return jnp.where(tri & (r // SSD_Q == c // SSD_Q), 1.0, 0.0).astype(BF16)


def _shift_down(x, prev8, k):
    if k == 0:
        return x
    n = x.shape[0]
    r = pltpu.roll(x, k, 0)
    rp = pltpu.roll(prev8, k, 0)
    rows = lax.broadcasted_iota(jnp.int32, (8, x.shape[1]), 0)
    head = jnp.where(rows < k, rp, r[:8])
    return head if n == 8 else jnp.concatenate([head, r[8:]], axis=0)


def _shift_up(x, next8, k):
    if k == 0:
        return x
    n = x.shape[0]
    r = pltpu.roll(x, n - k, 0)
    rn = pltpu.roll(next8, 8 - k, 0)
    rows = lax.broadcasted_iota(jnp.int32, (8, x.shape[1]), 0)
    tail = jnp.where(rows >= 8 - k, rn, r[n - 8:])
    return tail if n == 8 else jnp.concatenate([r[:n - 8], tail], axis=0)


def _sigmoid(x):
    return 0.5 * jnp.tanh(0.5 * x) + 0.5


def _rows(ref, i):
    return ref[i:i + 1, :]


def _norm_parts(x, mp_ref, which):
    ln, sh, sc = _rows(mp_ref, 6 + which), _rows(mp_ref, 3 * which), _rows(mp_ref, 3 * which + 1)
    r = lax.rsqrt(jnp.mean(x * x, axis=1, keepdims=True) + EPS)
    return r, ln, sh, sc


def _norm_fwd(x, modp, which, name):
    t, d = x.shape

    def body(x_ref, mp_ref, u_ref):
        xv = x_ref[...]
        r, ln, sh, sc = _norm_parts(xv, mp_ref, which)
        u_ref[...] = (((xv * r) * ln) * (1.0 + sc) + sh).astype(BF16)

    row = pl.BlockSpec((ROW_TILE, d), lambda i: (i, 0))
    return pl.pallas_call(
        body, name=name, out_shape=jax.ShapeDtypeStruct((t, d), BF16), grid=(t // ROW_TILE,),
        in_specs=[row, pl.BlockSpec((8, d), lambda i: (0, 0))], out_specs=row,
        compiler_params=_params(("parallel",)))(x, modp)


def _resid_norm_fwd(x, br, mp_gate, gate_row, mp_norm, which, name):
    t, d = x.shape

    def body(x_ref, br_ref, mg_ref, mn_ref, xn_ref, u_ref):
        xv = x_ref[...] + _rows(mg_ref, gate_row) * br_ref[...].astype(F32)
        xn_ref[...] = xv
        r, ln, sh, sc = _norm_parts(xv, mn_ref, which)
        u_ref[...] = (((xv * r) * ln) * (1.0 + sc) + sh).astype(BF16)

    row = pl.BlockSpec((ROW_TILE, d), lambda i: (i, 0))
    mp = pl.BlockSpec((8, d), lambda i: (0, 0))
    return pl.pallas_call(
        body, name=name,
        out_shape=[jax.ShapeDtypeStruct((t, d), F32), jax.ShapeDtypeStruct((t, d), BF16)],
        grid=(t // ROW_TILE,), in_specs=[row, row, mp, mp], out_specs=[row, row],
        compiler_params=_params(("parallel",)))(x, br, mp_gate, mp_norm)


def _final_fwd_bwd(x, br, mp_gate, fnorm8, target, name):
    t, d = x.shape

    def body(x_ref, br_ref, mg_ref, fn_ref, tg_ref, dx_ref, dbr_ref, acc_ref):
        gate = _rows(mg_ref, 5)
        brv = br_ref[...].astype(F32)
        xv = x_ref[...] + gate * brv
        fn = _rows(fn_ref, 0)
        r = lax.rsqrt(jnp.mean(xv * xv, axis=1, keepdims=True) + EPS)
        nrm = xv * r
        err = nrm * fn - tg_ref[...]
        loss = 0.5 * jnp.sum(jnp.mean(err * err, axis=1, keepdims=True), axis=0, keepdims=True)
        dy = err * (1.0 / d)
        dn = dy * fn
        dx = r * (dn - nrm * jnp.mean(dn * nrm, axis=1, keepdims=True))
        dx_ref[...] = dx
        dbr_ref[...] = (dx * gate).astype(BF16)
        upd = jnp.concatenate([
            jnp.broadcast_to(loss, (1, d)),
            jnp.sum(dy * nrm, axis=0, keepdims=True),
            jnp.sum(dx * brv, axis=0, keepdims=True),
            jnp.zeros((5, d), F32)], axis=0)

        @pl.when(pl.program_id(0) == 0)
        def _():
            acc_ref[...] = upd

        @pl.when(pl.program_id(0) > 0)
        def _():
            acc_ref[...] += upd

    row = pl.BlockSpec((ROW_TILE, d), lambda i: (i, 0))
    mp = pl.BlockSpec((8, d), lambda i: (0, 0))
    return pl.pallas_call(
        body, name=name,
        out_shape=[jax.ShapeDtypeStruct((t, d), F32), jax.ShapeDtypeStruct((t, d), BF16),
                   jax.ShapeDtypeStruct((8, d), F32)],
        grid=(t // ROW_TILE,), in_specs=[row, row, mp, mp, row], out_specs=[row, row, mp],
        compiler_params=_params(("arbitrary",)))(x, br, mp_gate, fnorm8, target)


def _resid_norm_bwd(dx, du, x, mp_norm, which, name, br=None, mp_gate=None, gate_row=None):
    t, d = x.shape
    has_gate = br is not None

    def body(*refs):
        if has_gate:
            dx_ref, du_ref, x_ref, mn_ref, br_ref, mg_ref, dxn_ref, dbr_ref, acc_ref = refs
        else:
            dx_ref, du_ref, x_ref, mn_ref, dxn_ref, acc_ref = refs
        xv, duv = x_ref[...], du_ref[...].astype(F32)
        r, ln, _, sc = _norm_parts(xv, mn_ref, which)
        nrm = xv * r
        dn = duv * (ln * (1.0 + sc))
        dxn = dx_ref[...] + r * (dn - nrm * jnp.mean(dn * nrm, axis=1, keepdims=True))
        dxn_ref[...] = dxn
        rows = [jnp.sum(duv, axis=0, keepdims=True), jnp.sum(duv * nrm, axis=0, keepdims=True)]
        if has_gate:
            dbr_ref[...] = (dxn * _rows(mg_ref, gate_row)).astype(BF16)
            rows.append(jnp.sum(dxn * br_ref[...].astype(F32), axis=0, keepdims=True))
        upd = jnp.concatenate(rows + [jnp.zeros((8 - len(rows), d), F32)], axis=0)

        @pl.when(pl.program_id(0) == 0)
        def _():
            acc_ref[...] = upd

        @pl.when(pl.program_id(0) > 0)
        def _():
            acc_ref[...] += upd

    row = pl.BlockSpec((ROW_TILE, d), lambda i: (i, 0))
    mp = pl.BlockSpec((8, d), lambda i: (0, 0))
    ins, in_specs = [dx, du, x, mp_norm], [row, row, row, mp]
    outs = [jax.ShapeDtypeStruct((t, d), F32)]
    out_specs = [row]
    if has_gate:
        ins += [br, mp_gate]
        in_specs += [row, mp]
        outs.append(jax.ShapeDtypeStruct((t, d), BF16))
        out_specs.append(row)
    outs.append(jax.ShapeDtypeStruct((8, d), F32))
    out_specs.append(mp)
    return pl.pallas_call(
        body, name=name, out_shape=outs, grid=(t // ROW_TILE,), in_specs=in_specs, out_specs=out_specs,
        compiler_params=_params(("arbitrary",)))(*ins)


def _merge_fwd(proj, p_conv, p_ssm, name):
    t, d = p_conv.shape

    def body(gl_ref, pc_ref, ps_ref, o_ref):
        g = _sigmoid(gl_ref[...].astype(F32))
        o_ref[...] = (g[:, :d] * pc_ref[...].astype(F32) + g[:, d:] * ps_ref[...].astype(F32)).astype(BF16)

    row = pl.BlockSpec((ROW_TILE, d), lambda i: (i, 0))
    return pl.pallas_call(
        body, name=name, out_shape=jax.ShapeDtypeStruct((t, d), BF16), grid=(t // ROW_TILE,),
        in_specs=[pl.BlockSpec((ROW_TILE, 2 * d), lambda i: (i, 0)), row, row], out_specs=row,
        compiler_params=_params(("parallel",)))(proj, p_conv, p_ssm)


def _merge_bwd(dmerged, proj, p_conv, p_ssm, name):
    t, d = p_conv.shape

    def body(dm_ref, gl_ref, pc_ref, ps_ref, dpc_ref, dps_ref, dgl_ref):
        g = _sigmoid(gl_ref[...].astype(F32))
        gc, gs = g[:, :d], g[:, d:]
        dm = dm_ref[...].astype(F32)
        dpc_ref[...] = (dm * gc).astype(BF16)
        dps_ref[...] = (dm * gs).astype(BF16)
        dgl_ref[...] = jnp.concatenate(
            [dm * pc_ref[...].astype(F32) * gc * (1.0 - gc), dm * ps_ref[...].astype(F32) * gs * (1.0 - gs)], axis=1).astype(BF16)

    row = pl.BlockSpec((ROW_TILE, d), lambda i: (i, 0))
    wide = pl.BlockSpec((ROW_TILE, 2 * d), lambda i: (i, 0))
    return pl.pallas_call(
        body, name=name,
        out_shape=[jax.ShapeDtypeStruct((t, d), BF16), jax.ShapeDtypeStruct((t, d), BF16),
                   jax.ShapeDtypeStruct((t, 2 * d), BF16)],
        grid=(t // ROW_TILE,), in_specs=[row, wide, row, row], out_specs=[row, row, wide],
        compiler_params=_params(("parallel",)))(dmerged, proj, p_conv, p_ssm)


def _halo_specs(t, width, col):
    nb = t // 8
    step = ROW_TILE // 8
    prev = pl.BlockSpec((8, width), lambda i: (jnp.maximum(i * step - 1, 0), col))
    nxt = pl.BlockSpec((8, width), lambda i: (jnp.minimum((i + 1) * step, nb - 1), col))
    return prev, nxt


def _gconv_fwd(proj, conv_w8, d, name):
    t = proj.shape[0]

    def body(cb_ref, cc_ref, cx_ref, ccp_ref, cxp_ref, w_ref, o_ref):
        first = pl.program_id(0) == 0
        v = cc_ref[...].astype(F32) * cx_ref[...].astype(F32)
        vp = jnp.where(first, 0.0, ccp_ref[...].astype(F32) * cxp_ref[...].astype(F32))
        cv = sum(_rows(w_ref, k) * _shift_down(v, vp, 2 - k) for k in range(3))
        o_ref[...] = (cb_ref[...].astype(F32) * cv).astype(BF16)

    def win(col):
        return pl.BlockSpec((ROW_TILE, d), lambda i: (i, col))

    return pl.pallas_call(
        body, name=name, out_shape=jax.ShapeDtypeStruct((t, d), BF16), grid=(t // ROW_TILE,),
        in_specs=[win(2), win(3), win(4), _halo_specs(t, d, 3)[0], _halo_specs(t, d, 4)[0],
                  pl.BlockSpec((8, d), lambda i: (0, 0))],
        out_specs=pl.BlockSpec((ROW_TILE, d), lambda i: (i, 0)),
        compiler_params=_params(("parallel",)))(proj, proj, proj, proj, proj, conv_w8)


def _gconv_bwd(dy, proj, conv_w8, dgl, d, name):
    t = proj.shape[0]
    nt = t // ROW_TILE

    def body(dy_ref, dyn_ref, cb_ref, cc_ref, cx_ref, ccp_ref, cxp_ref, cbn_ref, w_ref, dgl_ref, dp_ref, dw_ref):
        i = pl.program_id(0)
        cb, cc, cx = cb_ref[...].astype(F32), cc_ref[...].astype(F32), cx_ref[...].astype(F32)
        v = cc * cx
        vp = jnp.where(i == 0, 0.0, ccp_ref[...].astype(F32) * cxp_ref[...].astype(F32))
        sh = [_shift_down(v, vp, 2 - k) for k in range(3)]
        cv = sum(_rows(w_ref, k) * sh[k] for k in range(3))
        dyv = dy_ref[...]
        dcv = dyv * cb
        dcvn = jnp.where(i == nt - 1, 0.0, dyn_ref[...] * cbn_ref[...].astype(F32))
        dv = sum(_rows(w_ref, k) * _shift_up(dcv, dcvn, 2 - k) for k in range(3))
        dp_ref[:, :2 * d] = dgl_ref[...]
        dp_ref[:, 2 * d:3 * d] = (dyv * cv).astype(BF16)
        dp_ref[:, 3 * d:4 * d] = (dv * cx).astype(BF16)
        dp_ref[:, 4 * d:] = (dv * cc).astype(BF16)
        upd = jnp.concatenate([jnp.sum(dcv * sh[k], axis=0, keepdims=True) for k in range(3)]
                              + [jnp.zeros((5, d), F32)], axis=0)

        @pl.when(i == 0)
        def _():
            dw_ref[...] = upd

        @pl.when(i > 0)
        def _():
            dw_ref[...] += upd

    def win(col):
        return pl.BlockSpec((ROW_TILE, d), lambda i: (i, col))

    row = pl.BlockSpec((ROW_TILE, d), lambda i: (i, 0))
    w8 = pl.BlockSpec((8, d), lambda i: (0, 0))
    return pl.pallas_call(
        body, name=name,
        out_shape=[jax.ShapeDtypeStruct(proj.shape, BF16), jax.ShapeDtypeStruct((8, d), F32)],
        grid=(nt,),
        in_specs=[row, _halo_specs(t, d, 0)[1], win(2), win(3), win(4), _halo_specs(t, d, 3)[0],
                  _halo_specs(t, d, 4)[0], _halo_specs(t, d, 2)[1], w8,
                  pl.BlockSpec((ROW_TILE, 2 * d), lambda i: (i, 0))],
        out_specs=[pl.BlockSpec((ROW_TILE, 5 * d), lambda i: (i, 0)), w8],
        compiler_params=_params(("arbitrary",)))(
            dy, dy, proj, proj, proj, proj, proj, proj, conv_w8, dgl)


def _dsilu(p):
    s = _sigmoid(p)
    return s * (1.0 + p * (1.0 - s))


def _sconv_specs(t, d, col0, nt):
    nb, step = t // 8, ROW_TILE // 8
    win = pl.BlockSpec((ROW_TILE, d), lambda j, i: (i, col0 + j))
    prev = pl.BlockSpec((8, d), lambda j, i: (jnp.maximum(i * step - 1, 0), col0 + j))
    nxt = pl.BlockSpec((8, d), lambda j, i: (jnp.minimum((i + 1) * step, nb - 1), col0 + j))
    return win, prev, nxt


def _sconv_fwd(proj, w8, b8, d, col0, ncol, name):
    t = proj.shape[0]
    nt = t // ROW_TILE

    def body(x_ref, xp_ref, w_ref, b_ref, o_ref, pre_ref):
        xv = x_ref[...].astype(F32)
        xp = jnp.where(pl.program_id(1) == 0, 0.0, xp_ref[...].astype(F32))
        pre = _rows(b_ref, 0) + sum(_rows(w_ref, k) * _shift_down(xv, xp, 3 - k) for k in range(4))
        o_ref[...] = (pre * _sigmoid(pre)).astype(BF16)
        pre_ref[...] = pre.astype(BF16)

    win, prev, _ = _sconv_specs(t, d, col0, nt)
    par = pl.BlockSpec((8, d), lambda j, i: (0, j))
    out = pl.BlockSpec((ROW_TILE, d), lambda j, i: (i, j))
    return pl.pallas_call(
        body, name=name, out_shape=[jax.ShapeDtypeStruct((t, ncol * d), BF16)] * 2, grid=(ncol, nt),
        in_specs=[win, prev, par, par], out_specs=[out, out],
        compiler_params=_params(("parallel", "parallel")))(proj, proj, w8, b8)


def _sconv_bwd(dxc, pre, pre_col0, proj, w8, dproj, d, col0, ncol, name):
    t = proj.shape[0]
    nt = t // ROW_TILE

    def body(d_ref, dn_ref, p_ref, pn_ref, x_ref, w_ref, _, dp_ref, dw_ref):
        i = pl.program_id(1)
        dpre = d_ref[...] * _dsilu(p_ref[...].astype(F32))
        dpre_n = jnp.where(i == nt - 1, 0.0, dn_ref[...] * _dsilu(pn_ref[...].astype(F32)))
        up = [_shift_up(dpre, dpre_n, 3 - k) for k in range(4)]
        dp_ref[...] = sum(_rows(w_ref, k) * up[k] for k in range(4)).astype(BF16)
        xv = x_ref[...].astype(F32)
        upd = jnp.concatenate([jnp.sum(xv * up[k], axis=0, keepdims=True) for k in range(4)]
                              + [jnp.sum(dpre, axis=0, keepdims=True), jnp.zeros((3, d), F32)], axis=0)

        @pl.when(i == 0)
        def _():
            dw_ref[...] = upd

        @pl.when(i > 0)
        def _():
            dw_ref[...] += upd

    win, _, _ = _sconv_specs(t, d, col0, nt)
    nb, step = t // 8, ROW_TILE // 8
    dwin = pl.BlockSpec((ROW_TILE, d), lambda j, i: (i, j))
    dnxt = pl.BlockSpec((8, d), lambda j, i: (jnp.minimum((i + 1) * step, nb - 1), j))
    par = pl.BlockSpec((8, d), lambda j, i: (0, j))
    return pl.pallas_call(
        body, name=name,
        out_shape=[jax.ShapeDtypeStruct(dproj.shape, BF16), jax.ShapeDtypeStruct((8, ncol * d), F32)],
        grid=(ncol, nt),
        in_specs=[dwin, dnxt,
                  pl.BlockSpec((ROW_TILE, d), lambda j, i: (i, pre_col0 + j)),
                  pl.BlockSpec((8, d), lambda j, i: (jnp.minimum((i + 1) * step, nb - 1), pre_col0 + j)),
                  win, par, pl.BlockSpec(memory_space=pl.ANY)],
        out_specs=[win, par], input_output_aliases={6: 0},
        compiler_params=_params(("arbitrary", "arbitrary")))(dxc, dxc, pre, pre, proj, w8, dproj)


def _softplus(x):
    return jnp.maximum(x, 0.0) + jnp.log(1.0 + jnp.exp(-jnp.abs(x)))


def _dt_fwd(proj, par8, nheads, dt_col, name):
    t = proj.shape[0]

    def body(r_ref, p_ref, dt_ref, ac_ref):
        lane = lax.broadcasted_iota(jnp.int32, (1, LANE), 1)
        valid = lane < nheads
        raw = r_ref[...][:, :LANE].astype(F32)
        dt = jnp.where(valid, _softplus(raw + _rows(p_ref, 0)), 0.0)
        a = jnp.where(valid, -jnp.exp(_rows(p_ref, 1)), 0.0)
        dt_ref[...] = dt
        ac_ref[...] = _xdot01(_chunk_tri(ROW_TILE, True), dt * a)

    out = pl.BlockSpec((ROW_TILE, LANE), lambda i: (i, 0))
    return pl.pallas_call(
        body, name=name,
        out_shape=[jax.ShapeDtypeStruct((t, LANE), F32), jax.ShapeDtypeStruct((t, LANE), F32)],
        grid=(t // ROW_TILE,),
        in_specs=[pl.BlockSpec((ROW_TILE, 2 * LANE), lambda i: (i, dt_col)), pl.BlockSpec((8, LANE), lambda i: (0, 0))],
        out_specs=[out, out], compiler_params=_params(("parallel",)))(proj, par8)


def _dt_bwd(ddt_g, dac_g, dt, proj, par8, dproj, nheads, dt_col, name):
    t = proj.shape[0]
    ng = ddt_g.shape[0]

    def body(dd_ref, da_ref, dt_ref, r_ref, p_ref, _, dp_ref, acc_ref):
        lane = lax.broadcasted_iota(jnp.int32, (1, LANE), 1)
        valid = lane < nheads
        ddt = sum(dd_ref[g] for g in range(ng))
        dac = sum(da_ref[g] for g in range(ng))
        a = jnp.where(valid, -jnp.exp(_rows(p_ref, 1)), 0.0)
        d_a = _xdot01(_chunk_tri(ROW_TILE, False), dac)
        ddt = ddt + d_a * a
        raw = r_ref[...][:, :LANE].astype(F32)
        draw = jnp.where(valid, ddt * _sigmoid(raw + _rows(p_ref, 0)), 0.0)
        dp_ref[...] = jnp.concatenate([draw, jnp.zeros_like(draw)], axis=1).astype(BF16)
        upd = jnp.concatenate([jnp.sum(draw, axis=0, keepdims=True),
                               jnp.sum(d_a * dt_ref[...], axis=0, keepdims=True) * a,
                               jnp.zeros((6, LANE), F32)], axis=0)

        @pl.when(pl.program_id(0) == 0)
        def _():
            acc_ref[...] = upd

        @pl.when(pl.program_id(0) > 0)
        def _():
            acc_ref[...] += upd

    row = pl.BlockSpec((ROW_TILE, LANE), lambda i: (i, 0))
    grp = pl.BlockSpec((ng, ROW_TILE, LANE), lambda i: (0, i, 0))
    win = pl.BlockSpec((ROW_TILE, 2 * LANE), lambda i: (i, dt_col))
    par = pl.BlockSpec((8, LANE), lambda i: (0, 0))
    return pl.pallas_call(
        body, name=name,
        out_shape=[jax.ShapeDtypeStruct(dproj.shape, BF16), jax.ShapeDtypeStruct((8, LANE), F32)],
        grid=(t // ROW_TILE,),
        in_specs=[grp, grp, row, win, par, pl.BlockSpec(memory_space=pl.ANY)],
        out_specs=[win, par], input_output_aliases={5: 0},
        compiler_params=_params(("arbitrary",)))(ddt_g, dac_g, dt, proj, par8, dproj)


def _ssd_chunk(xs, bm, cm, z, dt, ac, st, dsk, nw, g):
    q = xs.shape[0]
    lane_h = lax.broadcasted_iota(jnp.int32, (1, LANE), 1)
    sub_h = lax.broadcasted_iota(jnp.int32, (LANE, 1), 0)
    lane_c = lax.broadcasted_iota(jnp.int32, (1, GROUP_W), 1) // HEAD_DIM
    rr = lax.broadcasted_iota(jnp.int32, (q, q), 0)
    cc = lax.broadcasted_iota(jnp.int32, (q, q), 1)
    tril = rr >= cc
    last_row = jnp.where(lax.broadcasted_iota(jnp.int32, (q, 1), 0) == q - 1, 1.0, 0.0)
    act = ac.T
    col, row = [], []
    for r in range(HEADS_PER_GROUP):
        h = g * HEADS_PER_GROUP + r
        col.append(jnp.sum(ac * jnp.where(lane_h == h, 1.0, 0.0), axis=1, keepdims=True))
        row.append(jnp.sum(act * jnp.where(sub_h == h, 1.0, 0.0), axis=0, keepdims=True))

    def per_head(vals):
        out = jnp.broadcast_to(vals[0], (q, GROUP_W))
        for r in range(1, HEADS_PER_GROUP):
            out = jnp.where(lane_c == r, vals[r], out)
        return out

    spread = jnp.where(lax.broadcasted_iota(jnp.int32, (LANE, GROUP_W), 0)
                       == g * HEADS_PER_GROUP + lax.broadcasted_iota(jnp.int32, (LANE, GROUP_W), 1) // HEAD_DIM,
                       1.0, 0.0).astype(BF16)
    acx = _spread(ac, spread)
    x = xs * _spread(dt, spread)
    s = _bdot(cm, bm, "nt")
    y = per_head([_bdot(s * jnp.exp(jnp.where(tril, col[r] - row[r], -jnp.inf)), x, "nn")
                  for r in range(HEADS_PER_GROUP)])
    y = y + jnp.exp(acx) * _bdot(cm, st, "nn")
    last = jnp.sum(acx * last_row, axis=0, keepdims=True)
    new_st = st * jnp.exp(last) + _bdot(bm, x * jnp.exp(last - acx), "tn")
    y = y + dsk * xs
    y = y * (z * _sigmoid(z))
    yn = y * lax.rsqrt(jnp.mean(y * y, axis=1, keepdims=True) + EPS) * nw
    return yn, new_st


def _ssd_specs(nc, z_col, ngroups, rev):
    assert ngroups % SSD_GROUPS == 0 and z_col % SSD_GROUPS == 0
    z_col //= SSD_GROUPS
    b_col, c_col = 2 * ngroups // SSD_GROUPS, 3 * ngroups // SSD_GROUPS

    def ci(c):
        return nc - 1 - c if rev else c

    nx, nb = SSD_GROUPS * GROUP_W, SSD_GROUPS * N_STATE
    xs = pl.BlockSpec((SSD_Q, nx), lambda g, c: (ci(c), g))
    bm = pl.BlockSpec((SSD_Q, nb), lambda g, c: (ci(c), b_col + g))
    cm = pl.BlockSpec((SSD_Q, nb), lambda g, c: (ci(c), c_col + g))
    z = pl.BlockSpec((SSD_Q, nx), lambda g, c: (ci(c), z_col + g))
    hd = pl.BlockSpec((SSD_Q, LANE), lambda g, c: (ci(c), 0))
    par = pl.BlockSpec((8, nx), lambda g, c: (0, g))
    stb = pl.BlockSpec((None, N_STATE, nx), lambda g, c: (ci(c), 0, g))
    db = pl.BlockSpec((SSD_Q, nb), lambda g, c: (ci(c), g))
    return xs, bm, cm, z, hd, par, stb, db


def _ssd_fwd(xc, proj, dt, ac, par8, ngroups, z_col, name):
    t = xc.shape[0]
    nc = t // SSD_Q
    steps = ngroups // SSD_GROUPS

    def body(x_ref, b_ref, c_ref, z_ref, dt_ref, ac_ref, p_ref, y_ref, so_ref, st_ref):
        @pl.when(pl.program_id(1) == 0)
        def _():
            st_ref[...] = jnp.zeros_like(st_ref)

        so_ref[...] = st_ref[...]
        dtv, acv = dt_ref[...], ac_ref[...]
        for i in range(SSD_GROUPS):
            cx, cb = slice(i * GROUP_W, (i + 1) * GROUP_W), slice(i * N_STATE, (i + 1) * N_STATE)
            yn, new_st = _ssd_chunk(x_ref[:, cx].astype(F32), b_ref[:, cb].astype(F32), c_ref[:, cb].astype(F32),
                                    z_ref[:, cx].astype(F32), dtv, acv, st_ref[:, cx],
                                    p_ref[0:1, cx], p_ref[1:2, cx], pl.program_id(0) * SSD_GROUPS + i)
            y_ref[:, cx] = yn.astype(BF16)
            st_ref[:, cx] = new_st

    xs, bm, cm, zs, hd, par, stb, _ = _ssd_specs(nc, z_col, ngroups, False)
    return pl.pallas_call(
        body, name=name,
        out_shape=[jax.ShapeDtypeStruct((t, ngroups * GROUP_W), BF16),
                   jax.ShapeDtypeStruct((nc, N_STATE, ngroups * GROUP_W), F32)],
        grid=(steps, nc), in_specs=[xs, bm, cm, zs, hd, hd, par], out_specs=[xs, stb],
        scratch_shapes=[pltpu.VMEM((N_STATE, SSD_GROUPS * GROUP_W), F32)],
        compiler_params=_params(("parallel", "arbitrary")))(xc, xc, xc, proj, dt, ac, par8)


def _ssd_bwd(dyn, xc, proj, dt, ac, par8, states, dproj, ngroups, z_col, name, comm=None):
    t = xc.shape[0]
    nc = t // SSD_Q
    steps = ngroups // SSD_GROUPS

    n_cin = comm.n_in if comm else 0
    n_cout = comm.n_out if comm else 0

    def body(*refs):
        dy_ref, x_ref, b_ref, c_ref, z_ref, dt_ref, ac_ref, p_ref, s_ref = refs[:9]
        dz_ref, dx_ref, db_ref, dc_ref, ddt_ref, dac_ref, dp_ref = refs[10 + n_cin:17 + n_cin]
        ds_ref = refs[17 + n_cin + n_cout]
        c = pl.program_id(1)
        if comm:
            begin, end = comm.ops(refs[10:10 + n_cin], refs[17 + n_cin:17 + n_cin + n_cout], refs[-3:])
            pl.when((pl.program_id(0) == 0) & (c == 0))(begin)

        @pl.when(c == 0)
        def _():
            ds_ref[...] = jnp.zeros_like(ds_ref)

        dtv, acv = dt_ref[...], ac_ref[...]
        ddt_sum, dac_sum, upds = None, None, []
        for i in range(SSD_GROUPS):
            cx, cb = slice(i * GROUP_W, (i + 1) * GROUP_W), slice(i * N_STATE, (i + 1) * N_STATE)
            g = pl.program_id(0) * SSD_GROUPS + i
            args = (x_ref[:, cx].astype(F32), b_ref[:, cb].astype(F32), c_ref[:, cb].astype(F32),
                    z_ref[:, cx].astype(F32), dtv, acv, s_ref[:, cx], p_ref[0:1, cx], p_ref[1:2, cx])
            _, vjp = jax.vjp(lambda *a, g=g: _ssd_chunk(*a, g), *args)
            dxs, dbm, dcm, dz, ddt, dac, dst, ddsk, dnw = vjp((dy_ref[:, cx], ds_ref[:, cx]))
            ds_ref[:, cx] = dst
            dz_ref[:, cx] = dz.astype(BF16)
            dx_ref[:, cx] = dxs
            db_ref[:, cb] = dbm
            dc_ref[:, cb] = dcm
            ddt_sum = ddt if ddt_sum is None else ddt_sum + ddt
            dac_sum = dac if dac_sum is None else dac_sum + dac
            upds.append(jnp.concatenate([ddsk, dnw, jnp.zeros((6, GROUP_W), F32)], axis=0))
        ddt_ref[...] = ddt_sum
        dac_ref[...] = dac_sum
        upd = jnp.concatenate(upds, axis=1)

        @pl.when(c == 0)
        def _():
            dp_ref[...] = upd

        @pl.when(c > 0)
        def _():
            dp_ref[...] += upd

        if comm:
            pl.when((pl.program_id(0) == steps - 1) & (c == nc - 1))(end)

    xs, bm, cm, zs, hd, par, stb, db = _ssd_specs(nc, z_col, ngroups, True)
    hg = pl.BlockSpec((None, SSD_Q, LANE), lambda g, c: (g, nc - 1 - c, 0))
    any_spec = pl.BlockSpec(memory_space=pl.ANY)
    aliases = {9: 0}
    if comm:
        aliases.update({10 + i: 7 + o for i, o in comm.aliases.items()})
    res = pl.pallas_call(
        body, name=name,
        out_shape=[jax.ShapeDtypeStruct(dproj.shape, BF16),
                   jax.ShapeDtypeStruct((t, ngroups * GROUP_W), F32),
                   jax.ShapeDtypeStruct((t, ngroups * N_STATE), F32), jax.ShapeDtypeStruct((t, ngroups * N_STATE), F32),
                   jax.ShapeDtypeStruct((steps, t, LANE), F32), jax.ShapeDtypeStruct((steps, t, LANE), F32),
                   jax.ShapeDtypeStruct((8, ngroups * GROUP_W), F32)] + (comm.out_shape if comm else []),
        grid=(steps, nc),
        in_specs=[xs, xs, bm, cm, zs, hd, hd, par, stb, any_spec] + [any_spec] * n_cin,
        out_specs=[zs, xs, db, db, hg, hg, par] + [any_spec] * n_cout, input_output_aliases=aliases,
        scratch_shapes=[pltpu.VMEM((N_STATE, SSD_GROUPS * GROUP_W), F32)] + (comm.sems if comm else []),
        compiler_params=_params(("arbitrary", "arbitrary")))(
            dyn, xc, xc, xc, proj, dt, ac, par8, states, dproj, *(comm.operands if comm else []))
    return tuple(res[:7]), list(res[7:])


def _small_matmul(a, b, mode, name):
    if mode == "nn":
        shape = (a.shape[0], b.shape[1])
    else:
        shape = (a.shape[1], b.shape[1])

    def body(a_ref, b_ref, o_ref):
        o_ref[...] = _dot(a_ref[...], b_ref[...], mode)

    return pl.pallas_call(body, name=name, out_shape=jax.ShapeDtypeStruct(shape, F32),
                          compiler_params=_params())(a, b)


def _adamw(parts, w, m, v, name, shard_stride=None):
    npart, rows, pcols = parts.shape
    cols = w.shape[1]
    tr = rows if rows <= 128 else _tile8(rows, 128)

    def body(p_ref, w_ref, m_ref, v_ref, g_ref, d_ref, nm_ref, nv_ref):
        g = p_ref[0].astype(F32)
        for j in range(1, npart):
            g = g + p_ref[j].astype(F32)
        if shard_stride is not None:
            me = 4 * lax.axis_index("x") + 2 * lax.axis_index("y") + lax.axis_index("c")
            off = (shard_stride * me) % LANE
            g = pltpu.roll(g, (pcols - off) % pcols, 1)[:, :cols]
        mm = ADAM_B1 * m_ref[...] + (1.0 - ADAM_B1) * g
        vv = ADAM_B2 * v_ref[...] + (1.0 - ADAM_B2) * (g * g)
        m_hat = mm / (1.0 - ADAM_B1 ** ADAM_STEP)
        v_hat = vv / (1.0 - ADAM_B2 ** ADAM_STEP)
        g_ref[...] = g
        d_ref[...] = -ADAM_LR * (m_hat / (jnp.sqrt(v_hat) + ADAM_EPS) + ADAM_WD * w_ref[...])
        nm_ref[...] = mm
        nv_ref[...] = vv

    blk = pl.BlockSpec((tr, cols), lambda i: (i, 0))
    out = jax.ShapeDtypeStruct((rows, cols), F32)
    return pl.pallas_call(
        body, name=name, out_shape=[out] * 4, grid=(rows // tr,),
        in_specs=[pl.BlockSpec((npart, tr, pcols), lambda i: (0, i, 0)), blk, blk, blk], out_specs=[blk] * 4,
        compiler_params=_params(("parallel",)))(parts, w, m, v)


def _tile8(n, target):
    best = 8
    for d in range(8, target + 1, 8):
        if n % d == 0:
            best = d
    return best


def _pad_rows8(a):
    return jnp.concatenate([a, jnp.zeros((8 - a.shape[0], a.shape[1]), a.dtype)], axis=0)


def _pad_lanes(a, n):
    return jnp.concatenate([a, jnp.zeros(a.shape[:-1] + (n - a.shape[-1],), a.dtype)], axis=-1)


def _pack(vecs, mult):
    flat = jnp.concatenate([v.reshape(-1) for v in vecs])
    pad = (-flat.shape[0]) % mult
    flat = jnp.concatenate([flat, jnp.zeros((pad,), flat.dtype)])
    return flat.reshape(-1, LANE)


def _unpack(flat, shapes):
    out, o = [], 0
    for s in shapes:
        n = int(np.prod(s))
        out.append(flat[o:o + n].reshape(s))
        o += n
    return out


def kernel(x, c, w_ada, b_ada, ln1, ln2, w_in, conv_w, ssm_conv_w, ssm_conv_b, dt_bias, a_log, d_skip, ssm_norm_w, w_conv_out, w_ssm_out, w_o, w_up, w_down, final_norm, loss_target, m_w_ada, m_b_ada, m_ln1, m_ln2, m_w_in, m_conv_w, m_ssm_conv_w, m_ssm_conv_b, m_dt_bias, m_a_log, m_d_skip, m_ssm_norm_w, m_w_conv_out, m_w_ssm_out, m_w_o, m_w_up, m_w_down, m_final_norm, v_w_ada, v_b_ada, v_ln1, v_ln2, v_w_in, v_conv_w, v_ssm_conv_w, v_ssm_conv_b, v_dt_bias, v_a_log, v_d_skip, v_ssm_norm_w, v_w_conv_out, v_w_ssm_out, v_w_o, v_w_up, v_w_down, v_final_norm):
    names = ["w_ada", "b_ada", "ln1", "ln2", "w_in", "conv_w", "ssm_conv_w", "ssm_conv_b", "dt_bias", "a_log",
             "d_skip", "ssm_norm_w", "w_conv_out", "w_ssm_out", "w_o", "w_up", "w_down", "final_norm"]
    w_of = dict(zip(names, [w_ada, b_ada, ln1, ln2, w_in, conv_w, ssm_conv_w, ssm_conv_b, dt_bias, a_log, d_skip,
                            ssm_norm_w, w_conv_out, w_ssm_out, w_o, w_up, w_down, final_norm]))
    m_of = dict(zip(names, [m_w_ada, m_b_ada, m_ln1, m_ln2, m_w_in, m_conv_w, m_ssm_conv_w, m_ssm_conv_b, m_dt_bias,
                            m_a_log, m_d_skip, m_ssm_norm_w, m_w_conv_out, m_w_ssm_out, m_w_o, m_w_up, m_w_down,
                            m_final_norm]))
    v_of = dict(zip(names, [v_w_ada, v_b_ada, v_ln1, v_ln2, v_w_in, v_conv_w, v_ssm_conv_w, v_ssm_conv_b, v_dt_bias,
                            v_a_log, v_d_skip, v_ssm_norm_w, v_w_conv_out, v_w_ssm_out, v_w_o, v_w_up, v_w_down,
                            v_final_norm]))

    _, t, d = x.shape
    nl = w_ada.shape[0]
    ada_w = w_ada.shape[2]
    ds = ssm_norm_w.shape[1]
    nh = dt_bias.shape[1]
    ng = nh // HEADS_PER_GROUP
    gn = ng * N_STATE
    xbc_w = ds + 2 * gn
    z_off, xbc_off = 5 * d, 5 * d + ds
    dt_off = xbc_off + xbc_w
    proj_w = dt_off + nh
    pw = dt_off + 2 * LANE
    me = 4 * lax.axis_index("x") + 2 * lax.axis_index("y") + lax.axis_index("c")

    x2, tgt = x[0], loss_target[0]

    c_act = c * jax.nn.sigmoid(c)
    sizes1 = [(1, d), conv_w.shape, ssm_conv_w.shape]
    (g1,) = _all_gather([_pack([c_act, conv_w, ssm_conv_w], LANE)], "ag_small_in")
    parts1 = [_unpack(g1[j].reshape(-1), sizes1) for j in range(N_DEV)]
    c_act_all = jnp.concatenate([p[0] for p in parts1], axis=0)
    conv_w_full = jnp.concatenate([p[1] for p in parts1], axis=-1)
    sconv_w_full = jnp.concatenate([p[2] for p in parts1], axis=-1)

    mod_part = jnp.stack([_small_matmul(c_act_all, w_ada[l], "nn", f"ada_fwd{l}") for l in range(nl)])
    (gmod,) = _all_gather([mod_part], "ag_mod")
    mod = lax.dynamic_index_in_dim(gmod, me, axis=2, keepdims=False)
    mod = jnp.moveaxis(mod, 0, 1).reshape(nl, N_DEV * ada_w) + b_ada
    mod = mod.reshape(nl, 6, d)
    modp = [jnp.concatenate([mod[l], ln1[l][None], ln2[l][None]], axis=0) for l in range(nl)]

    big = ["w_in", "w_up", "w_conv_out", "w_ssm_out", "w_o", "w_down"]
    rest = big[1:]

    def blocks_of(l, keys):
        return [w_of[k][l].astype(BF16) for k in keys]

    def w_in_halves(l):
        w = w_of["w_in"][l].astype(BF16)
        return [w[:d // 2]], [w[d // 2:]]

    def full_cols(g):
        return jnp.moveaxis(g, 0, 1).reshape(g.shape[1], N_DEV * g.shape[2])

    def full_rows(g):
        return g.reshape(N_DEV * g.shape[1], g.shape[2])

    def pad_w_in(pieces):
        w = jnp.concatenate([full_cols(g) for g in pieces], axis=0)
        return jnp.concatenate([w, jnp.zeros((d, pw - proj_w), BF16)], axis=-1)

    wi = [None] * nl
    wi[0] = pad_w_in(_all_gather(blocks_of(0, ["w_in"]), "ag_w_in0"))

    cw8 = [_pad_rows8(conv_w_full[l]) for l in range(nl)]
    sw8 = [_pad_rows8(sconv_w_full[l]) for l in range(nl)]
    sb8 = [_pad_rows8(ssm_conv_b[l][None]) for l in range(nl)]
    dtp8 = [_pad_rows8(_pad_lanes(jnp.stack([dt_bias[l], a_log[l]]), LANE)) for l in range(nl)]
    sp8 = [_pad_rows8(jnp.stack([jnp.repeat(d_skip[l], HEAD_DIM), ssm_norm_w[l]])) for l in range(nl)]
    fn8 = _pad_rows8(final_norm[None])
    xbc_col, ncol, dt_col, z_col = xbc_off // d, xbc_w // d, dt_off // (2 * LANE), z_off // GROUP_W

    saved = []
    x_cur, br_prev = x2, None
    for l in range(nl):
        if l == 0:
            x_in, u1 = x_cur, _norm_fwd(x_cur, modp[0], 0, "norm_first")
        else:
            x_in, u1 = _resid_norm_fwd(x_cur, br_prev, modp[l - 1], 5, modp[l], 0, f"resid_norm_a{l}")
        proj, got = _matmul(u1, wi[l], "nn", BF16, f"mm_in{l}", comm=_GatherComm(blocks_of(l, rest)))
        got = dict(zip(rest, got))
        wup, wdown = full_cols(got["w_up"]), full_rows(got["w_down"])
        wco, wso, wo = full_rows(got["w_conv_out"]), full_rows(got["w_ssm_out"]), full_rows(got["w_o"])
        y_conv = _gconv_fwd(proj, cw8[l], d, f"gconv_fwd{l}")
        xc, pre = _sconv_fwd(proj, sw8[l], sb8[l], d, xbc_col, ncol, f"sconv_fwd{l}")
        dt, ac = _dt_fwd(proj, dtp8[l], nh, dt_col, f"dt_fwd{l}")
        yn, states = _ssd_fwd(xc, proj, dt, ac, sp8[l], ng, z_col, f"ssd_fwd{l}")
        p_conv = _matmul(y_conv, wco, "nn", BF16, f"mm_conv_out{l}")
        p_ssm = _matmul(yn, wso, "nn", BF16, f"mm_ssm_out{l}")
        merged = _merge_fwd(proj, p_conv, p_ssm, f"merge_fwd{l}")
        mix = _matmul(merged, wo, "nn", BF16, f"mm_o{l}")
        x_mid, u2 = _resid_norm_fwd(x_in, mix, modp[l], 2, modp[l], 1, f"resid_norm_b{l}")
        if l + 1 < nl:
            top, bot = w_in_halves(l + 1)
            (h, hid), g_top = _matmul(u2, wup, "nn", BF16, f"mm_up{l}", epi="relu2", comm=_GatherComm(top))
            mlp, g_bot = _matmul(hid, wdown, "nn", BF16, f"mm_down{l}", comm=_GatherComm(bot))
            wi[l + 1] = pad_w_in(g_top + g_bot)
        else:
            h, hid = _matmul(u2, wup, "nn", BF16, f"mm_up{l}", epi="relu2")
            mlp = _matmul(hid, wdown, "nn", BF16, f"mm_down{l}")
        saved.append(dict(x_in=x_in, u1=u1, proj=proj, y_conv=y_conv, xc=xc, pre=pre, dt=dt, ac=ac, yn=yn,
                          states=states,
                          p_conv=p_conv, p_ssm=p_ssm, merged=merged, mix=mix, x_mid=x_mid, u2=u2, h=h, hid=hid,
                          mlp=mlp, wup=wup, wdown=wdown, wco=wco, wso=wso, wo=wo))
        x_cur, br_prev = x_mid, mlp

    dx, dbr, acc = _final_fwd_bwd(x_cur, br_prev, modp[nl - 1], fn8, tgt, "final")
    loss = lax.psum(acc[0, 0], ("x", "y", "c"))
    g_final_norm = acc[1]
    dgate2 = acc[2]

    gw = {k: [None] * nl for k in big}
    small = {k: [None] * nl for k in ["mod", "ln1", "ln2", "conv_w", "ssm_conv_w", "ssm_conv_b", "dt_bias",
                                      "a_log", "d_skip", "ssm_norm_w"]}

    def grad_rows(a):
        return a.reshape(N_DEV, a.shape[0] // N_DEV, a.shape[1])

    def recv_buf(name):
        return jnp.zeros((N_DEV, nl) + w_of[name].shape[1:], BF16)

    in_shard, up_shard = w_in.shape[2], w_up.shape[2]
    in_win = -(-(in_shard + max((in_shard * j) % LANE for j in range(N_DEV))) // LANE) * LANE
    assert up_shard % LANE == 0 and (in_shard * (N_DEV - 1)) // LANE * LANE + in_win <= pw
    win_a, win_b = [(in_win, in_shard)], [(up_shard, up_shard)] + [None] * 4
    bufs_a = [jnp.zeros((N_DEV, nl, d, in_win), BF16)]
    bufs_b = [recv_buf(k) for k in big[1:]]
    pending = None
    for l in reversed(range(nl)):
        s = saved[l]
        dh = _matmul(dbr, s["wdown"], "nt", BF16, f"mm_down_dx{l}", epi="relu2_bwd", extra=s["h"])
        gw["w_down"][l] = _matmul(s["hid"], dbr, "tn", BF16, f"mm_down_dw{l}")
        du2 = _matmul(dh, s["wup"], "nt", BF16, f"mm_up_dx{l}")
        gw["w_up"][l] = _matmul(s["u2"], dh, "tn", BF16, f"mm_up_dw{l}")
        dx_mid, dmix, acc2 = _resid_norm_bwd(dx, du2, s["x_mid"], modp[l], 1, f"resid_norm_b_bwd{l}",
                                             br=s["mix"], mp_gate=modp[l], gate_row=2)
        dmerged = _matmul(dmix, s["wo"], "nt", BF16, f"mm_o_dx{l}")
        gw["w_o"][l] = _matmul(s["merged"], dmix, "tn", BF16, f"mm_o_dw{l}")
        dpc, dps, dgl = _merge_bwd(dmerged, s["proj"], s["p_conv"], s["p_ssm"], f"merge_bwd{l}")
        dyc = _matmul(dpc, s["wco"], "nt", F32, f"mm_conv_out_dx{l}")
        dproj, dcw = _gconv_bwd(dyc, s["proj"], cw8[l], dgl, d, f"gconv_bwd{l}")
        gw["w_conv_out"][l] = _matmul(s["y_conv"], dpc, "tn", BF16, f"mm_conv_out_dw{l}")
        dyn = _matmul(dps, s["wso"], "nt", F32, f"mm_ssm_out_dx{l}")
        gw["w_ssm_out"][l] = _matmul(s["yn"], dps, "tn", BF16, f"mm_ssm_out_dw{l}")
        rest_grads = [gw["w_up"][l]] + [grad_rows(gw[k][l]) for k in big[2:]]
        sends, bufs, layers, wins = rest_grads, list(bufs_b), [l] * len(rest_grads), win_b
        if pending is not None:
            sends, bufs, layers, wins = sends + pending, bufs + list(bufs_a), layers + [l + 1], win_b + win_a
        (dproj, dxs, dbm, dcm, ddt_g, dac_g, dsp), bufs = _ssd_bwd(
            dyn, s["xc"], s["proj"], s["dt"], s["ac"], sp8[l], s["states"], dproj, ng, z_col, f"ssd_bwd{l}",
            comm=_ExchangeComm(sends, bufs, layers, windows=wins))
        bufs_b = bufs[:len(rest_grads)]
        if pending is not None:
            bufs_a = bufs[len(rest_grads):]
        dsw = []
        for seg, (dseg, c0) in enumerate([(dxs, 0), (dbm, ds // d), (dcm, ds // d + gn // d)]):
            cols = slice(c0 * d, c0 * d + dseg.shape[1])
            dproj, part = _sconv_bwd(dseg, s["pre"], c0, s["proj"], sw8[l][:, cols], dproj, d, xbc_col + c0,
                                     dseg.shape[1] // d, f"sconv_bwd{l}_{seg}")
            dsw.append(part)
        dsw = jnp.concatenate(dsw, axis=1)
        dproj, ddtp = _dt_bwd(ddt_g, dac_g, s["dt"], s["proj"], dtp8[l], dproj, nh, dt_col, f"dt_bwd{l}")
        du1 = _matmul(dproj, wi[l], "nt", BF16, f"mm_in_dx{l}")

        if l > 0:
            pending = [_matmul(s["u1"], dproj, "tn", BF16, f"mm_in_dw{l}")]
        else:
            g_top = _matmul(s["u1"][:, :d // 2], dproj, "tn", BF16, "mm_in_dw0_top")
            g_bot, bufs_a = _matmul(s["u1"][:, d // 2:], dproj, "tn", BF16, "mm_in_dw0_bot",
                                    comm=_ExchangeComm([g_top], bufs_a, 0, row0=0, windows=win_a))
            pending = [g_bot]
        if l > 0:
            dx, dbr, acc1 = _resid_norm_bwd(dx_mid, du1, s["x_in"], modp[l], 0, f"resid_norm_a_bwd{l}",
                                            br=saved[l - 1]["mlp"], mp_gate=modp[l - 1], gate_row=5)
        else:
            dx, acc1 = _resid_norm_bwd(dx_mid, du1, s["x_in"], modp[0], 0, "norm_first_bwd")
        sc1, sc2 = mod[l, 1], mod[l, 4]
        small["mod"][l] = jnp.stack([acc1[0], acc1[1] * ln1[l], acc2[2], acc2[0], acc2[1] * ln2[l], dgate2])
        small["ln1"][l] = acc1[1] * (1.0 + sc1)
        small["ln2"][l] = acc2[1] * (1.0 + sc2)
        small["conv_w"][l] = dcw[:3]
        small["ssm_conv_w"][l] = dsw[:4]
        small["ssm_conv_b"][l] = dsw[4]
        small["dt_bias"][l] = ddtp[0, :nh]
        small["a_log"][l] = ddtp[1, :nh]
        small["d_skip"][l] = dsp[0].reshape(nh, HEAD_DIM).sum(axis=-1)
        small["ssm_norm_w"][l] = dsp[1]
        if l > 0:
            dgate2 = acc1[2]
    grad_x = dx[None]

    sm = {k: jnp.stack(v) for k, v in small.items()}
    rep_names = ["b_ada", "ln1", "ln2", "ssm_conv_b", "dt_bias", "a_log", "d_skip", "ssm_norm_w", "final_norm"]
    rep_grads = [sm["mod"].reshape(nl, 6 * d), sm["ln1"], sm["ln2"], sm["ssm_conv_b"], sm["dt_bias"], sm["a_log"],
                 sm["d_skip"], sm["ssm_norm_w"], g_final_norm]
    rep_pack = _pack(rep_grads, 8 * LANE)
    conv_pack = _pack([sm["conv_w"], sm["ssm_conv_w"]], 8 * LANE)
    g_rep, g_conv = _all_gather([rep_pack, conv_pack], "ag_small_grads")

    outs = {}

    def run_adamw(name, parts, tag, shard_stride=None):
        w, m, v = w_of[name], m_of[name], v_of[name]
        shp = w.shape
        r2 = (int(np.prod(shp[:-1])), shp[-1])
        res = _adamw(parts.reshape(parts.shape[0], r2[0], parts.shape[-1]), w.reshape(r2), m.reshape(r2),
                     v.reshape(r2), tag, shard_stride)
        outs[name] = [a.reshape(shp) for a in res]

    rep_shapes = [w_of[k].shape for k in rep_names]
    res = _adamw(g_rep, _pack([w_of[k] for k in rep_names], 8 * LANE), _pack([m_of[k] for k in rep_names], 8 * LANE),
                 _pack([v_of[k] for k in rep_names], 8 * LANE), "adamw_replicated")
    for k, vals in zip(rep_names, zip(*[_unpack(a.reshape(-1), rep_shapes) for a in res])):
        outs[k] = list(vals)

    conv_parts = [_unpack(g_conv[j].reshape(-1), [sm["conv_w"].shape, sm["ssm_conv_w"].shape]) for j in range(N_DEV)]
    for idx, name in enumerate(["conv_w", "ssm_conv_w"]):
        wsh = w_of[name].shape[-1]
        full = jnp.stack([p[idx] for p in conv_parts])
        run_adamw(name, lax.dynamic_slice_in_dim(full, me * wsh, wsh, axis=3), "adamw_" + name)

    dmod_all = g_rep.reshape(N_DEV, -1)[:, :nl * 6 * d].reshape(N_DEV, nl, 6 * d)
    dmod_mine = lax.dynamic_slice_in_dim(dmod_all, me * ada_w, ada_w, axis=2)
    g_ada = jnp.stack([_small_matmul(c_act_all, dmod_mine[:, l], "tn", f"ada_bwd{l}") for l in range(nl)])
    run_adamw("w_ada", g_ada[None], "adamw_w_ada")

    bufs_a = _exchange(pending, bufs_a, 0, "a2a_w_in0_bot", row0=d // 2, windows=win_a)
    run_adamw("w_in", bufs_a[0], "adamw_w_in", shard_stride=in_shard)
    for name, parts in zip(big[1:], bufs_b):
        run_adamw(name, parts, "adamw_" + name)

    result = [loss, grad_x]
    for i in range(4):
        result += [outs[k][i] for k in names]
    return tuple(result)
```

```python
import functools

import numpy as np
import jax
import jax.numpy as jnp
from jax import lax
from jax.experimental import pallas as pl
from jax.experimental.pallas import tpu as pltpu

F32 = jnp.float32
BF16 = jnp.bfloat16
EPS = 1e-6
N_STATE = 128
HEAD_DIM = 64
HEADS_PER_GROUP = 4
GROUP_W = HEAD_DIM * HEADS_PER_GROUP
SSD_GROUPS = 2
SSD_Q = 256
N_DEV = 8
ROW_TILE = 256
LANE = 128
VMEM_LIMIT = 56 * 1024 * 1024
MATMUL_VMEM_BUDGET = 40 * 1024 * 1024
MATMUL_TILE_CAP = 2048
MXU_FLOPS = 9.0e14
HBM_BYTES_PER_S = 3.0e12
STEP_OVERHEAD_S = 0.35e-6

ADAM_LR, ADAM_B1, ADAM_B2, ADAM_EPS, ADAM_WD, ADAM_STEP = 0.001, 0.9, 0.999, 1e-08, 0.01, 10

MESH = pl.DeviceIdType.MESH


def _params(sem=None):
    return pltpu.CompilerParams(dimension_semantics=sem, vmem_limit_bytes=VMEM_LIMIT)


def _divisors(n, cap):
    if n <= cap:
        return [n]
    return [d for d in range(cap - cap % LANE, 0, -LANE) if n % d == 0]


def _matmul_tiles(m, n, k, out_bytes, n_out, n_extra):
    best = None
    for tm in _divisors(m, MATMUL_TILE_CAP):
        for tn in _divisors(n, MATMUL_TILE_CAP):
            for tk in _divisors(k, MATMUL_TILE_CAP):
                out_tile = tm * tn * (out_bytes * n_out + 2 * n_extra)
                vmem = 2 * 2 * (tm * tk + tk * tn) + 2 * out_tile + (4 * tm * tn if tk < k else 0)
                if vmem > MATMUL_VMEM_BUDGET:
                    continue
                steps = (m // tm) * (n // tn) * (k // tk)
                a_reads = 1 if tk == k else n // tn
                traffic = 2 * (a_reads * m * k + (m // tm) * k * n) + m * n * (out_bytes * n_out + 2 * n_extra)
                est = (max(2.0 * m * n * k / MXU_FLOPS, traffic / HBM_BYTES_PER_S) + steps * STEP_OVERHEAD_S
                       + (2 * (tm * tk + tk * tn) + out_tile) / HBM_BYTES_PER_S)
                cand = (est, tm, tn, tk)
                if best is None or cand < best:
                    best = cand
    assert best is not None, (m, n, k)
    return best[1:]


def _tile(n, target):
    if n <= target:
        return n
    best = None
    for d in range(LANE, target + 1, LANE):
        if n % d == 0:
            best = d
    assert best is not None, (n, target)
    return best


def _slot(p):
    return 4 * p[0] + 2 * p[1] + p[2]


def _all_gather(xs, name):
    comm = _GatherComm(xs)

    def body(*refs):
        begin, end = comm.ops(refs[:comm.n_in], refs[comm.n_in:comm.n_in + comm.n_out], refs[-3:])
        begin()
        end()

    any_spec = pl.BlockSpec(memory_space=pl.ANY)
    return pl.pallas_call(
        body, name=name, out_shape=comm.out_shape,
        in_specs=[any_spec] * comm.n_in, out_specs=[any_spec] * comm.n_out, scratch_shapes=comm.sems,
    )(*comm.operands)


class _GatherComm:
    def __init__(self, xs, as_columns=None):
        n = len(xs)
        self.operands = list(xs)
        self.n_in = self.n_out = n
        self.aliases = {}
        self.as_columns = list(as_columns) if as_columns else [False] * n
        assert all(x.shape[1] % LANE == 0 for x, f in zip(xs, self.as_columns) if f)
        self.out_shape = [jax.ShapeDtypeStruct((x.shape[0], N_DEV * x.shape[1]) if f else (N_DEV,) + x.shape, x.dtype)
                          for x, f in zip(xs, self.as_columns)]
        self.widths = [x.shape[-1] for x in xs]
        self.sems = [pltpu.SemaphoreType.DMA((7 * n,)), pltpu.SemaphoreType.DMA((7 * n,)),
                     pltpu.SemaphoreType.DMA((n,))]

    def ops(self, x_refs, o_refs, sem_refs):
        n = self.n_in
        send, recv, loc = sem_refs
        x, y, c = lax.axis_index("x"), lax.axis_index("y"), lax.axis_index("c")
        me, sib = (x, y, c), (x, y, 1 - c)
        chips = [(1 - x, y), (x, 1 - y), (1 - x, 1 - y)]

        def place(a, block):
            if self.as_columns[a]:
                w = self.widths[a]
                return o_refs[a].at[:, pl.ds(pl.multiple_of(_slot(block) * w, LANE), w)]
            return o_refs[a].at[_slot(block)]

        def cp(a, k, block, to, src=None):
            dst = place(a, block)
            return pltpu.make_async_remote_copy(
                src_ref=dst if src is None else src, dst_ref=dst,
                send_sem=send.at[7 * a + k], recv_sem=recv.at[7 * a + k],
                device_id=to, device_id_type=MESH)

        mine = [pltpu.make_async_copy(x_refs[a], place(a, me), loc.at[a]) for a in range(n)]
        first = []
        for a in range(n):
            first.append(cp(a, 0, me, sib, src=x_refs[a]))
            first += [cp(a, 1 + j, me, (*chip, c), src=x_refs[a]) for j, chip in enumerate(chips)]

        def begin():
            for m in mine:
                m.start()
            for f in first:
                f.start()

        def end():
            passed = []
            for j, chip in enumerate(chips):
                for a in range(n):
                    cp(a, 1 + j, (*chip, c), me).wait_recv()
                    p = cp(a, 4 + j, (*chip, c), sib)
                    p.start()
                    passed.append(p)
            for a in range(n):
                cp(a, 0, sib, me).wait_recv()
                for j, chip in enumerate(chips):
                    cp(a, 4 + j, (*chip, 1 - c), me).wait_recv()
            for f in first + passed:
                f.wait_send()
            for m in mine:
                m.wait()

        return begin, end


class _ExchangeComm:
    def __init__(self, xs, bufs, layer, row0=0, windows=None):
        n = len(xs)
        self.layers = list(layer) if isinstance(layer, (list, tuple)) else [layer] * n
        self.row0 = row0
        self.windows = list(windows) if windows else [None] * n
        self.nrows = [x.shape[1] if wdw is None else x.shape[0] for x, wdw in zip(xs, self.windows)]
        self.operands = list(xs) + list(bufs)
        self.n_in, self.n_out = 2 * n, n
        self.aliases = {n + a: a for a in range(n)}
        self.out_shape = [jax.ShapeDtypeStruct(b.shape, b.dtype) for b in bufs]
        self.sems = [pltpu.SemaphoreType.DMA((7 * n,)), pltpu.SemaphoreType.DMA((7 * n,)),
                     pltpu.SemaphoreType.DMA((n,))]

    def ops(self, in_refs, o_refs, sem_refs):
        n = self.n_out
        x_refs = in_refs[:n]
        send, recv, loc = sem_refs
        x, y, c = lax.axis_index("x"), lax.axis_index("y"), lax.axis_index("c")
        me = (x, y, c)
        peers = []
        for k in range(1, 8):
            kx, ky, kc = (k >> 2) & 1, (k >> 1) & 1, k & 1
            peers.append((x + kx - 2 * x * kx, y + ky - 2 * y * ky, c + kc - 2 * c * kc))

        def land(a, slot):
            return o_refs[a].at[slot, self.layers[a], pl.ds(self.row0, self.nrows[a])]

        def block(a, slot):
            if self.windows[a] is None:
                return x_refs[a].at[slot]
            width, stride = self.windows[a]
            start = pl.multiple_of((stride * slot) // LANE * LANE, LANE)
            return x_refs[a].at[:, pl.ds(start, width)]

        def cp(a, k, src_slot, dst_slot, to):
            return pltpu.make_async_remote_copy(
                src_ref=block(a, src_slot), dst_ref=land(a, dst_slot),
                send_sem=send.at[7 * a + k], recv_sem=recv.at[7 * a + k],
                device_id=to, device_id_type=MESH)

        mine = [pltpu.make_async_copy(block(a, _slot(me)), land(a, _slot(me)), loc.at[a])
                for a in range(n)]
        sends = [cp(a, k, _slot(p), _slot(me), p) for a in range(n) for k, p in enumerate(peers)]

        def begin():
            for m in mine:
                m.start()
            for s in sends:
                s.start()

        def end():
            for a in range(n):
                for k, p in enumerate(peers):
                    cp(a, k, _slot(me), _slot(p), me).wait_recv()
            for s in sends:
                s.wait_send()
            for m in mine:
                m.wait()

        return begin, end


def _exchange(xs, bufs, layer, name, row0=0, windows=None):
    comm = _ExchangeComm(xs, bufs, layer, row0, windows)

    def body(*refs):
        begin, end = comm.ops(refs[:comm.n_in], refs[comm.n_in:comm.n_in + comm.n_out], refs[-3:])
        begin()
        end()

    any_spec = pl.BlockSpec(memory_space=pl.ANY)
    return pl.pallas_call(
        body, name=name, out_shape=comm.out_shape,
        in_specs=[any_spec] * comm.n_in, out_specs=[any_spec] * comm.n_out, scratch_shapes=comm.sems,
        input_output_aliases=dict(comm.aliases),
    )(*comm.operands)


_DIMS = {"nn": (((1,), (0,)), ((), ())), "nt": (((1,), (1,)), ((), ())), "tn": (((0,), (0,)), ((), ()))}


def _matmul(a, b, mode, out_dtype, name, comm=None, epi=None, extra=None):
    if mode == "nn":
        (m, k), (_, n) = a.shape, b.shape
    elif mode == "nt":
        (m, k), (n, _) = a.shape, b.shape
    else:
        (k, m), (_, n) = a.shape, b.shape
    n_ext = 1 if epi == "relu2_bwd" else 0
    n_main = 2 if epi == "relu2" else 1
    tm, tn, tk = _matmul_tiles(m, n, k, jnp.dtype(out_dtype).itemsize, n_main, n_ext)
    nk = k // tk
    grid = (m // tm, n // tn, nk)
    dn = _DIMS[mode]
    n_cin = comm.n_in if comm else 0
    n_cout = comm.n_out if comm else 0
    n_acc = 0 if nk == 1 else 1
    first_in, first_out = 2 + n_ext, 2 + n_ext + n_cin

    def finish(acc, refs):
        outs = refs[first_out:first_out + n_main]
        if epi == "relu2":
            outs[0][...] = acc.astype(outs[0].dtype)
            r = jnp.maximum(acc, 0.0)
            outs[1][...] = (r * r).astype(outs[1].dtype)
        elif epi == "relu2_bwd":
            outs[0][...] = (acc * (2.0 * jnp.maximum(refs[2][...].astype(F32), 0.0))).astype(outs[0].dtype)
        else:
            outs[0][...] = acc.astype(outs[0].dtype)

    def body(*refs):
        a_ref, b_ref = refs[:2]
        scratch = refs[first_out + n_main + n_cout:]
        pid = [pl.program_id(ax) for ax in range(3)]
        if comm:
            begin, end = comm.ops(refs[first_in:first_in + n_cin],
                                  refs[first_out + n_main:first_out + n_main + n_cout], scratch[n_acc:])
            pl.when((pid[0] == 0) & (pid[1] == 0) & (pid[2] == 0))(begin)
        part = lax.dot_general(a_ref[...], b_ref[...], dn, preferred_element_type=F32)
        if nk == 1:
            finish(part, refs)
        else:
            acc_ref = scratch[0]

            @pl.when(pid[2] == 0)
            def _():
                acc_ref[...] = part

            @pl.when(pid[2] > 0)
            def _():
                acc_ref[...] += part

            @pl.when(pid[2] == nk - 1)
            def _():
                finish(acc_ref[...], refs)
        if comm:
            pl.when((pid[0] == grid[0] - 1) & (pid[1] == grid[1] - 1) & (pid[2] == grid[2] - 1))(end)

    a_spec = pl.BlockSpec((tk, tm), lambda i, j, kk: (kk, i)) if mode == "tn" else \
        pl.BlockSpec((tm, tk), lambda i, j, kk: (i, kk))
    b_spec = pl.BlockSpec((tn, tk), lambda i, j, kk: (j, kk)) if mode == "nt" else \
        pl.BlockSpec((tk, tn), lambda i, j, kk: (kk, j))
    o_spec = pl.BlockSpec((tm, tn), lambda i, j, kk: (i, j))
    any_spec = pl.BlockSpec(memory_space=pl.ANY)
    res = pl.pallas_call(
        body, name=name,
        out_shape=[jax.ShapeDtypeStruct((m, n), out_dtype)] * n_main + (comm.out_shape if comm else []),
        grid=grid,
        in_specs=[a_spec, b_spec] + [o_spec] * n_ext + [any_spec] * n_cin,
        out_specs=[o_spec] * n_main + [any_spec] * n_cout,
        scratch_shapes=([] if nk == 1 else [pltpu.VMEM((tm, tn), F32)]) + (comm.sems if comm else []),
        input_output_aliases={first_in + i: n_main + o for i, o in comm.aliases.items()} if comm else {},
        compiler_params=_params(("arbitrary",) * 3 if comm else ("parallel", "parallel", "arbitrary")),
    )(a, b, *([extra] if n_ext else []), *(comm.operands if comm else []))
    main = res[0] if n_main == 1 else tuple(res[:n_main])
    return (main, list(res[n_main:])) if comm else main


def _dot(a, b, mode):
    return lax.dot_general(a.astype(BF16), b.astype(BF16), _DIMS[mode], preferred_element_type=F32)


@functools.partial(jax.custom_vjp, nondiff_argnums=(2,))
def _bdot(a, b, mode):
    return _dot(a, b, mode)


def _bdot_fwd(a, b, mode):
    return _dot(a, b, mode), (a, b)


def _bdot_bwd(mode, res, g):
    a, b = res
    if mode == "nn":
        return _dot(g, b, "nt"), _dot(a, g, "tn")
    if mode == "nt":
        return _dot(g, b, "nn"), _dot(g, a, "tn")
    return _dot(b, g, "nt"), _dot(a, g, "nn")


_bdot.defvjp(_bdot_fwd, _bdot_bwd)


def _split3(x):
    hi = x.astype(BF16)
    r1 = x - hi.astype(F32)
    mid = r1.astype(BF16)
    return hi, mid, (r1 - mid.astype(F32)).astype(BF16)


def _xdot01(m01, x):
    return sum(lax.dot_general(m01, p, _DIMS["nn"], preferred_element_type=F32) for p in _split3(x))


@jax.custom_vjp
def _spread(v, m01):
    return sum(lax.dot_general(p, m01, _DIMS["nn"], preferred_element_type=F32) for p in _split3(v))


def _spread_fwd(v, m01):
    return _spread(v, m01), m01


def _spread_bwd(m01, g):
    dv = sum(lax.dot_general(p, m01, _DIMS["nt"], preferred_element_type=F32) for p in _split3(g)[:2])
    return dv, jnp.zeros_like(m01)


_spread.defvjp(_spread_fwd, _spread_bwd)


def _chunk_tri(n, lower):
    r = lax.broadcasted_iota(jnp.int32, (n, n), 0)
    c = lax.broadcasted_iota(jnp.int32, (n, n), 1)
    tri = (r >= c) if lower else (r <= c)
    return jnp.where(tri & (r // SSD_Q == c // SSD_Q), 1.0, 0.0).astype(BF16)


def _shift_down(x, prev8, k):
    if k == 0:
        return x
    n = x.shape[0]
    r = pltpu.roll(x, k, 0)
    rp = pltpu.roll(prev8, k, 0)
    rows = lax.broadcasted_iota(jnp.int32, (8, x.shape[1]), 0)
    head = jnp.where(rows < k, rp, r[:8])
    return head if n == 8 else jnp.concatenate([head, r[8:]], axis=0)


def _shift_up(x, next8, k):
    if k == 0:
        return x
    n = x.shape[0]
    r = pltpu.roll(x, n - k, 0)
    rn = pltpu.roll(next8, 8 - k, 0)
    rows = lax.broadcasted_iota(jnp.int32, (8, x.shape[1]), 0)
    tail = jnp.where(rows >= 8 - k, rn, r[n - 8:])
    return tail if n == 8 else jnp.concatenate([r[:n - 8], tail], axis=0)


def _sigmoid(x):
    return 0.5 * jnp.tanh(0.5 * x) + 0.5


def _rows(ref, i):
    return ref[i:i + 1, :]


def _norm_parts(x, mp_ref, which):
    ln, sh, sc = _rows(mp_ref, 6 + which), _rows(mp_ref, 3 * which), _rows(mp_ref, 3 * which + 1)
    r = lax.rsqrt(jnp.mean(x * x, axis=1, keepdims=True) + EPS)
    return r, ln, sh, sc


def _norm_fwd(x, modp, which, name):
    t, d = x.shape

    def body(x_ref, mp_ref, u_ref):
        xv = x_ref[...]
        r, ln, sh, sc = _norm_parts(xv, mp_ref, which)
        u_ref[...] = (((xv * r) * ln) * (1.0 + sc) + sh).astype(BF16)

    row = pl.BlockSpec((ROW_TILE, d), lambda i: (i, 0))
    return pl.pallas_call(
        body, name=name, out_shape=jax.ShapeDtypeStruct((t, d), BF16), grid=(t // ROW_TILE,),
        in_specs=[row, pl.BlockSpec((8, d), lambda i: (0, 0))], out_specs=row,
        compiler_params=_params(("parallel",)))(x, modp)


def _resid_norm_fwd(x, br, mp_gate, gate_row, mp_norm, which, name):
    t, d = x.shape

    def body(x_ref, br_ref, mg_ref, mn_ref, xn_ref, u_ref):
        xv = x_ref[...] + _rows(mg_ref, gate_row) * br_ref[...].astype(F32)
        xn_ref[...] = xv
        r, ln, sh, sc = _norm_parts(xv, mn_ref, which)
        u_ref[...] = (((xv * r) * ln) * (1.0 + sc) + sh).astype(BF16)

    row = pl.BlockSpec((ROW_TILE, d), lambda i: (i, 0))
    mp = pl.BlockSpec((8, d), lambda i: (0, 0))
    return pl.pallas_call(
        body, name=name,
        out_shape=[jax.ShapeDtypeStruct((t, d), F32), jax.ShapeDtypeStruct((t, d), BF16)],
        grid=(t // ROW_TILE,), in_specs=[row, row, mp, mp], out_specs=[row, row],
        compiler_params=_params(("parallel",)))(x, br, mp_gate, mp_norm)


def _final_fwd_bwd(x, br, mp_gate, fnorm8, target, name):
    t, d = x.shape

    def body(x_ref, br_ref, mg_ref, fn_ref, tg_ref, dx_ref, dbr_ref, acc_ref):
        gate = _rows(mg_ref, 5)
        brv = br_ref[...].astype(F32)
        xv = x_ref[...] + gate * brv
        fn = _rows(fn_ref, 0)
        r = lax.rsqrt(jnp.mean(xv * xv, axis=1, keepdims=True) + EPS)
        nrm = xv * r
        err = nrm * fn - tg_ref[...]
        loss = 0.5 * jnp.sum(jnp.mean(err * err, axis=1, keepdims=True), axis=0, keepdims=True)
        dy = err * (1.0 / d)
        dn = dy * fn
        dx = r * (dn - nrm * jnp.mean(dn * nrm, axis=1, keepdims=True))
        dx_ref[...] = dx
        dbr_ref[...] = (dx * gate).astype(BF16)
        upd = jnp.concatenate([
            jnp.broadcast_to(loss, (1, d)),
            jnp.sum(dy * nrm, axis=0, keepdims=True),
            jnp.sum(dx * brv, axis=0, keepdims=True),
            jnp.zeros((5, d), F32)], axis=0)

        @pl.when(pl.program_id(0) == 0)
        def _():
            acc_ref[...] = upd

        @pl.when(pl.program_id(0) > 0)
        def _():
            acc_ref[...] += upd

    row = pl.BlockSpec((ROW_TILE, d), lambda i: (i, 0))
    mp = pl.BlockSpec((8, d), lambda i: (0, 0))
    return pl.pallas_call(
        body, name=name,
        out_shape=[jax.ShapeDtypeStruct((t, d), F32), jax.ShapeDtypeStruct((t, d), BF16),
                   jax.ShapeDtypeStruct((8, d), F32)],
        grid=(t // ROW_TILE,), in_specs=[row, row, mp, mp, row], out_specs=[row, row, mp],
        compiler_params=_params(("arbitrary",)))(x, br, mp_gate, fnorm8, target)


def _resid_norm_bwd(dx, du, x, mp_norm, which, name, br=None, mp_gate=None, gate_row=None):
    t, d = x.shape
    has_gate = br is not None

    def body(*refs):
        if has_gate:
            dx_ref, du_ref, x_ref, mn_ref, br_ref, mg_ref, dxn_ref, dbr_ref, acc_ref = refs
        else:
            dx_ref, du_ref, x_ref, mn_ref, dxn_ref, acc_ref = refs
        xv, duv = x_ref[...], du_ref[...].astype(F32)
        r, ln, _, sc = _norm_parts(xv, mn_ref, which)
        nrm = xv * r
        dn = duv * (ln * (1.0 + sc))
        dxn = dx_ref[...] + r * (dn - nrm * jnp.mean(dn * nrm, axis=1, keepdims=True))
        dxn_ref[...] = dxn
        rows = [jnp.sum(duv, axis=0, keepdims=True), jnp.sum(duv * nrm, axis=0, keepdims=True)]
        if has_gate:
            dbr_ref[...] = (dxn * _rows(mg_ref, gate_row)).astype(BF16)
            rows.append(jnp.sum(dxn * br_ref[...].astype(F32), axis=0, keepdims=True))
        upd = jnp.concatenate(rows + [jnp.zeros((8 - len(rows), d), F32)], axis=0)

        @pl.when(pl.program_id(0) == 0)
        def _():
            acc_ref[...] = upd

        @pl.when(pl.program_id(0) > 0)
        def _():
            acc_ref[...] += upd

    row = pl.BlockSpec((ROW_TILE, d), lambda i: (i, 0))
    mp = pl.BlockSpec((8, d), lambda i: (0, 0))
    ins, in_specs = [dx, du, x, mp_norm], [row, row, row, mp]
    outs = [jax.ShapeDtypeStruct((t, d), F32)]
    out_specs = [row]
    if has_gate:
        ins += [br, mp_gate]
        in_specs += [row, mp]
        outs.append(jax.ShapeDtypeStruct((t, d), BF16))
        out_specs.append(row)
    outs.append(jax.ShapeDtypeStruct((8, d), F32))
    out_specs.append(mp)
    return pl.pallas_call(
        body, name=name, out_shape=outs, grid=(t // ROW_TILE,), in_specs=in_specs, out_specs=out_specs,
        compiler_params=_params(("arbitrary",)))(*ins)


def _merge_fwd(proj, p_conv, p_ssm, name):
    t, d = p_conv.shape

    def body(gl_ref, pc_ref, ps_ref, o_ref):
        g = _sigmoid(gl_ref[...].astype(F32))
        o_ref[...] = (g[:, :d] * pc_ref[...].astype(F32) + g[:, d:] * ps_ref[...].astype(F32)).astype(BF16)

    row = pl.BlockSpec((ROW_TILE, d), lambda i: (i, 0))
    return pl.pallas_call(
        body, name=name, out_shape=jax.ShapeDtypeStruct((t, d), BF16), grid=(t // ROW_TILE,),
        in_specs=[pl.BlockSpec((ROW_TILE, 2 * d), lambda i: (i, 0)), row, row], out_specs=row,
        compiler_params=_params(("parallel",)))(proj, p_conv, p_ssm)


def _merge_bwd(dmerged, proj, p_conv, p_ssm, name):
    t, d = p_conv.shape

    def body(dm_ref, gl_ref, pc_ref, ps_ref, dpc_ref, dps_ref, dgl_ref):
        g = _sigmoid(gl_ref[...].astype(F32))
        gc, gs = g[:, :d], g[:, d:]
        dm = dm_ref[...].astype(F32)
        dpc_ref[...] = (dm * gc).astype(BF16)
        dps_ref[...] = (dm * gs).astype(BF16)
        dgl_ref[...] = jnp.concatenate(
            [dm * pc_ref[...].astype(F32) * gc * (1.0 - gc), dm * ps_ref[...].astype(F32) * gs * (1.0 - gs)], axis=1).astype(BF16)

    row = pl.BlockSpec((ROW_TILE, d), lambda i: (i, 0))
    wide = pl.BlockSpec((ROW_TILE, 2 * d), lambda i: (i, 0))
    return pl.pallas_call(
        body, name=name,
        out_shape=[jax.ShapeDtypeStruct((t, d), BF16), jax.ShapeDtypeStruct((t, d), BF16),
                   jax.ShapeDtypeStruct((t, 2 * d), BF16)],
        grid=(t // ROW_TILE,), in_specs=[row, wide, row, row], out_specs=[row, row, wide],
        compiler_params=_params(("parallel",)))(dmerged, proj, p_conv, p_ssm)


def _halo_specs(t, width, col):
    nb = t // 8
    step = ROW_TILE // 8
    prev = pl.BlockSpec((8, width), lambda i: (jnp.maximum(i * step - 1, 0), col))
    nxt = pl.BlockSpec((8, width), lambda i: (jnp.minimum((i + 1) * step, nb - 1), col))
    return prev, nxt


def _gconv_fwd(proj, conv_w8, d, name):
    t = proj.shape[0]

    def body(cb_ref, cc_ref, cx_ref, ccp_ref, cxp_ref, w_ref, o_ref):
        first = pl.program_id(0) == 0
        v = cc_ref[...].astype(F32) * cx_ref[...].astype(F32)
        vp = jnp.where(first, 0.0, ccp_ref[...].astype(F32) * cxp_ref[...].astype(F32))
        cv = sum(_rows(w_ref, k) * _shift_down(v, vp, 2 - k) for k in range(3))
        o_ref[...] = (cb_ref[...].astype(F32) * cv).astype(BF16)

    def win(col):
        return pl.BlockSpec((ROW_TILE, d), lambda i: (i, col))

    return pl.pallas_call(
        body, name=name, out_shape=jax.ShapeDtypeStruct((t, d), BF16), grid=(t // ROW_TILE,),
        in_specs=[win(2), win(3), win(4), _halo_specs(t, d, 3)[0], _halo_specs(t, d, 4)[0],
                  pl.BlockSpec((8, d), lambda i: (0, 0))],
        out_specs=pl.BlockSpec((ROW_TILE, d), lambda i: (i, 0)),
        compiler_params=_params(("parallel",)))(proj, proj, proj, proj, proj, conv_w8)


def _gconv_bwd(dy, proj, conv_w8, dgl, d, name):
    t = proj.shape[0]
    nt = t // ROW_TILE

    def body(dy_ref, dyn_ref, cb_ref, cc_ref, cx_ref, ccp_ref, cxp_ref, cbn_ref, w_ref, dgl_ref, dp_ref, dw_ref):
        i = pl.program_id(0)
        cb, cc, cx = cb_ref[...].astype(F32), cc_ref[...].astype(F32), cx_ref[...].astype(F32)
        v = cc * cx
        vp = jnp.where(i == 0, 0.0, ccp_ref[...].astype(F32) * cxp_ref[...].astype(F32))
        sh = [_shift_down(v, vp, 2 - k) for k in range(3)]
        cv = sum(_rows(w_ref, k) * sh[k] for k in range(3))
        dyv = dy_ref[...]
        dcv = dyv * cb
        dcvn = jnp.where(i == nt - 1, 0.0, dyn_ref[...] * cbn_ref[...].astype(F32))
        dv = sum(_rows(w_ref, k) * _shift_up(dcv, dcvn, 2 - k) for k in range(3))
        dp_ref[:, :2 * d] = dgl_ref[...]
        dp_ref[:, 2 * d:3 * d] = (dyv * cv).astype(BF16)
        dp_ref[:, 3 * d:4 * d] = (dv * cx).astype(BF16)
        dp_ref[:, 4 * d:] = (dv * cc).astype(BF16)
        upd = jnp.concatenate([jnp.sum(dcv * sh[k], axis=0, keepdims=True) for k in range(3)]
                              + [jnp.zeros((5, d), F32)], axis=0)

        @pl.when(i == 0)
        def _():
            dw_ref[...] = upd

        @pl.when(i > 0)
        def _():
            dw_ref[...] += upd

    def win(col):
        return pl.BlockSpec((ROW_TILE, d), lambda i: (i, col))

    row = pl.BlockSpec((ROW_TILE, d), lambda i: (i, 0))
    w8 = pl.BlockSpec((8, d), lambda i: (0, 0))
    return pl.pallas_call(
        body, name=name,
        out_shape=[jax.ShapeDtypeStruct(proj.shape, BF16), jax.ShapeDtypeStruct((8, d), F32)],
        grid=(nt,),
        in_specs=[row, _halo_specs(t, d, 0)[1], win(2), win(3), win(4), _halo_specs(t, d, 3)[0],
                  _halo_specs(t, d, 4)[0], _halo_specs(t, d, 2)[1], w8,
                  pl.BlockSpec((ROW_TILE, 2 * d), lambda i: (i, 0))],
        out_specs=[pl.BlockSpec((ROW_TILE, 5 * d), lambda i: (i, 0)), w8],
        compiler_params=_params(("arbitrary",)))(
            dy, dy, proj, proj, proj, proj, proj, proj, conv_w8, dgl)


def _dsilu(p):
    s = _sigmoid(p)
    return s * (1.0 + p * (1.0 - s))


def _sconv_specs(t, d, col0, nt):
    nb, step = t // 8, ROW_TILE // 8
    win = pl.BlockSpec((ROW_TILE, d), lambda j, i: (i, col0 + j))
    prev = pl.BlockSpec((8, d), lambda j, i: (jnp.maximum(i * step - 1, 0), col0 + j))
    nxt = pl.BlockSpec((8, d), lambda j, i: (jnp.minimum((i + 1) * step, nb - 1), col0 + j))
    return win, prev, nxt


def _sconv_fwd(proj, w8, b8, d, col0, ncol, name):
    t = proj.shape[0]
    nt = t // ROW_TILE

    def body(x_ref, xp_ref, w_ref, b_ref, o_ref, pre_ref):
        xv = x_ref[...].astype(F32)
        xp = jnp.where(pl.program_id(1) == 0, 0.0, xp_ref[...].astype(F32))
        pre = _rows(b_ref, 0) + sum(_rows(w_ref, k) * _shift_down(xv, xp, 3 - k) for k in range(4))
        o_ref[...] = (pre * _sigmoid(pre)).astype(BF16)
        pre_ref[...] = pre.astype(BF16)

    win, prev, _ = _sconv_specs(t, d, col0, nt)
    par = pl.BlockSpec((8, d), lambda j, i: (0, j))
    out = pl.BlockSpec((ROW_TILE, d), lambda j, i: (i, j))
    return pl.pallas_call(
        body, name=name, out_shape=[jax.ShapeDtypeStruct((t, ncol * d), BF16)] * 2, grid=(ncol, nt),
        in_specs=[win, prev, par, par], out_specs=[out, out],
        compiler_params=_params(("parallel", "parallel")))(proj, proj, w8, b8)


def _sconv_bwd(dxc, pre, pre_col0, proj, w8, dproj, d, col0, ncol, name):
    t = proj.shape[0]
    nt = t // ROW_TILE

    def body(d_ref, dn_ref, p_ref, pn_ref, x_ref, w_ref, _, dp_ref, dw_ref):
        i = pl.program_id(1)
        dpre = d_ref[...] * _dsilu(p_ref[...].astype(F32))
        dpre_n = jnp.where(i == nt - 1, 0.0, dn_ref[...] * _dsilu(pn_ref[...].astype(F32)))
        up = [_shift_up(dpre, dpre_n, 3 - k) for k in range(4)]
        dp_ref[...] = sum(_rows(w_ref, k) * up[k] for k in range(4)).astype(BF16)
        xv = x_ref[...].astype(F32)
        upd = jnp.concatenate([jnp.sum(xv * up[k], axis=0, keepdims=True) for k in range(4)]
                              + [jnp.sum(dpre, axis=0, keepdims=True), jnp.zeros((3, d), F32)], axis=0)

        @pl.when(i == 0)
        def _():
            dw_ref[...] = upd

        @pl.when(i > 0)
        def _():
            dw_ref[...] += upd

    win, _, _ = _sconv_specs(t, d, col0, nt)
    nb, step = t // 8, ROW_TILE // 8
    dwin = pl.BlockSpec((ROW_TILE, d), lambda j, i: (i, j))
    dnxt = pl.BlockSpec((8, d), lambda j, i: (jnp.minimum((i + 1) * step, nb - 1), j))
    par = pl.BlockSpec((8, d), lambda j, i: (0, j))
    return pl.pallas_call(
        body, name=name,
        out_shape=[jax.ShapeDtypeStruct(dproj.shape, BF16), jax.ShapeDtypeStruct((8, ncol * d), F32)],
        grid=(ncol, nt),
        in_specs=[dwin, dnxt,
                  pl.BlockSpec((ROW_TILE, d), lambda j, i: (i, pre_col0 + j)),
                  pl.BlockSpec((8, d), lambda j, i: (jnp.minimum((i + 1) * step, nb - 1), pre_col0 + j)),
                  win, par, pl.BlockSpec(memory_space=pl.ANY)],
        out_specs=[win, par], input_output_aliases={6: 0},
        compiler_params=_params(("arbitrary", "arbitrary")))(dxc, dxc, pre, pre, proj, w8, dproj)


def _softplus(x):
    return jnp.maximum(x, 0.0) + jnp.log(1.0 + jnp.exp(-jnp.abs(x)))


def _dt_fwd(proj, par8, nheads, dt_col, name):
    t = proj.shape[0]

    def body(r_ref, p_ref, dt_ref, ac_ref):
        lane = lax.broadcasted_iota(jnp.int32, (1, LANE), 1)
        valid = lane < nheads
        raw = r_ref[...][:, :LANE].astype(F32)
        dt = jnp.where(valid, _softplus(raw + _rows(p_ref, 0)), 0.0)
        a = jnp.where(valid, -jnp.exp(_rows(p_ref, 1)), 0.0)
        dt_ref[...] = dt
        ac_ref[...] = _xdot01(_chunk_tri(ROW_TILE, True), dt * a)

    out = pl.BlockSpec((ROW_TILE, LANE), lambda i: (i, 0))
    return pl.pallas_call(
        body, name=name,
        out_shape=[jax.ShapeDtypeStruct((t, LANE), F32), jax.ShapeDtypeStruct((t, LANE), F32)],
        grid=(t // ROW_TILE,),
        in_specs=[pl.BlockSpec((ROW_TILE, 2 * LANE), lambda i: (i, dt_col)), pl.BlockSpec((8, LANE), lambda i: (0, 0))],
        out_specs=[out, out], compiler_params=_params(("parallel",)))(proj, par8)


def _dt_bwd(ddt_g, dac_g, dt, proj, par8, dproj, nheads, dt_col, name):
    t = proj.shape[0]
    ng = ddt_g.shape[0]

    def body(dd_ref, da_ref, dt_ref, r_ref, p_ref, _, dp_ref, acc_ref):
        lane = lax.broadcasted_iota(jnp.int32, (1, LANE), 1)
        valid = lane < nheads
        ddt = sum(dd_ref[g] for g in range(ng))
        dac = sum(da_ref[g] for g in range(ng))
        a = jnp.where(valid, -jnp.exp(_rows(p_ref, 1)), 0.0)
        d_a = _xdot01(_chunk_tri(ROW_TILE, False), dac)
        ddt = ddt + d_a * a
        raw = r_ref[...][:, :LANE].astype(F32)
        draw = jnp.where(valid, ddt * _sigmoid(raw + _rows(p_ref, 0)), 0.0)
        dp_ref[...] = jnp.concatenate([draw, jnp.zeros_like(draw)], axis=1).astype(BF16)
        upd = jnp.concatenate([jnp.sum(draw, axis=0, keepdims=True),
                               jnp.sum(d_a * dt_ref[...], axis=0, keepdims=True) * a,
                               jnp.zeros((6, LANE), F32)], axis=0)

        @pl.when(pl.program_id(0) == 0)
        def _():
            acc_ref[...] = upd

        @pl.when(pl.program_id(0) > 0)
        def _():
            acc_ref[...] += upd

    row = pl.BlockSpec((ROW_TILE, LANE), lambda i: (i, 0))
    grp = pl.BlockSpec((ng, ROW_TILE, LANE), lambda i: (0, i, 0))
    win = pl.BlockSpec((ROW_TILE, 2 * LANE), lambda i: (i, dt_col))
    par = pl.BlockSpec((8, LANE), lambda i: (0, 0))
    return pl.pallas_call(
        body, name=name,
        out_shape=[jax.ShapeDtypeStruct(dproj.shape, BF16), jax.ShapeDtypeStruct((8, LANE), F32)],
        grid=(t // ROW_TILE,),
        in_specs=[grp, grp, row, win, par, pl.BlockSpec(memory_space=pl.ANY)],
        out_specs=[win, par], input_output_aliases={5: 0},
        compiler_params=_params(("arbitrary",)))(ddt_g, dac_g, dt, proj, par8, dproj)


def _ssd_chunk(xs, bm, cm, z, dt, ac, st, dsk, nw, g):
    q = xs.shape[0]
    lane_h = lax.broadcasted_iota(jnp.int32, (1, LANE), 1)
    sub_h = lax.broadcasted_iota(jnp.int32, (LANE, 1), 0)
    lane_c = lax.broadcasted_iota(jnp.int32, (1, GROUP_W), 1) // HEAD_DIM
    rr = lax.broadcasted_iota(jnp.int32, (q, q), 0)
    cc = lax.broadcasted_iota(jnp.int32, (q, q), 1)
    tril = rr >= cc
    last_row = jnp.where(lax.broadcasted_iota(jnp.int32, (q, 1), 0) == q - 1, 1.0, 0.0)
    act = ac.T
    col, row = [], []
    for r in range(HEADS_PER_GROUP):
        h = g * HEADS_PER_GROUP + r
        col.append(jnp.sum(ac * jnp.where(lane_h == h, 1.0, 0.0), axis=1, keepdims=True))
        row.append(jnp.sum(act * jnp.where(sub_h == h, 1.0, 0.0), axis=0, keepdims=True))

    def per_head(vals):
        out = jnp.broadcast_to(vals[0], (q, GROUP_W))
        for r in range(1, HEADS_PER_GROUP):
            out = jnp.where(lane_c == r, vals[r], out)
        return out

    spread = jnp.where(lax.broadcasted_iota(jnp.int32, (LANE, GROUP_W), 0)
                       == g * HEADS_PER_GROUP + lax.broadcasted_iota(jnp.int32, (LANE, GROUP_W), 1) // HEAD_DIM,
                       1.0, 0.0).astype(BF16)
    acx = _spread(ac, spread)
    x = xs * _spread(dt, spread)
    s = _bdot(cm, bm, "nt")
    y = per_head([_bdot(s * jnp.exp(jnp.where(tril, col[r] - row[r], -jnp.inf)), x, "nn")
                  for r in range(HEADS_PER_GROUP)])
    y = y + jnp.exp(acx) * _bdot(cm, st, "nn")
    last = jnp.sum(acx * last_row, axis=0, keepdims=True)
    new_st = st * jnp.exp(last) + _bdot(bm, x * jnp.exp(last - acx), "tn")
    y = y + dsk * xs
    y = y * (z * _sigmoid(z))
    yn = y * lax.rsqrt(jnp.mean(y * y, axis=1, keepdims=True) + EPS) * nw
    return yn, new_st


def _ssd_specs(nc, z_col, ngroups, rev):
    assert ngroups % SSD_GROUPS == 0 and z_col % SSD_GROUPS == 0
    z_col //= SSD_GROUPS
    b_col, c_col = 2 * ngroups // SSD_GROUPS, 3 * ngroups // SSD_GROUPS

    def ci(c):
        return nc - 1 - c if rev else c

    nx, nb = SSD_GROUPS * GROUP_W, SSD_GROUPS * N_STATE
    xs = pl.BlockSpec((SSD_Q, nx), lambda g, c: (ci(c), g))
    bm = pl.BlockSpec((SSD_Q, nb), lambda g, c: (ci(c), b_col + g))
    cm = pl.BlockSpec((SSD_Q, nb), lambda g, c: (ci(c), c_col + g))
    z = pl.BlockSpec((SSD_Q, nx), lambda g, c: (ci(c), z_col + g))
    hd = pl.BlockSpec((SSD_Q, LANE), lambda g, c: (ci(c), 0))
    par = pl.BlockSpec((8, nx), lambda g, c: (0, g))
    stb = pl.BlockSpec((None, N_STATE, nx), lambda g, c: (ci(c), 0, g))
    db = pl.BlockSpec((SSD_Q, nb), lambda g, c: (ci(c), g))
    return xs, bm, cm, z, hd, par, stb, db


def _ssd_fwd(xc, proj, dt, ac, par8, ngroups, z_col, name):
    t = xc.shape[0]
    nc = t // SSD_Q
    steps = ngroups // SSD_GROUPS

    def body(x_ref, b_ref, c_ref, z_ref, dt_ref, ac_ref, p_ref, y_ref, so_ref, st_ref):
        @pl.when(pl.program_id(1) == 0)
        def _():
            st_ref[...] = jnp.zeros_like(st_ref)

        so_ref[...] = st_ref[...]
        dtv, acv = dt_ref[...], ac_ref[...]
        for i in range(SSD_GROUPS):
            cx, cb = slice(i * GROUP_W, (i + 1) * GROUP_W), slice(i * N_STATE, (i + 1) * N_STATE)
            yn, new_st = _ssd_chunk(x_ref[:, cx].astype(F32), b_ref[:, cb].astype(F32), c_ref[:, cb].astype(F32),
                                    z_ref[:, cx].astype(F32), dtv, acv, st_ref[:, cx],
                                    p_ref[0:1, cx], p_ref[1:2, cx], pl.program_id(0) * SSD_GROUPS + i)
            y_ref[:, cx] = yn.astype(BF16)
            st_ref[:, cx] = new_st

    xs, bm, cm, zs, hd, par, stb, _ = _ssd_specs(nc, z_col, ngroups, False)
    return pl.pallas_call(
        body, name=name,
        out_shape=[jax.ShapeDtypeStruct((t, ngroups * GROUP_W), BF16),
                   jax.ShapeDtypeStruct((nc, N_STATE, ngroups * GROUP_W), F32)],
        grid=(steps, nc), in_specs=[xs, bm, cm, zs, hd, hd, par], out_specs=[xs, stb],
        scratch_shapes=[pltpu.VMEM((N_STATE, SSD_GROUPS * GROUP_W), F32)],
        compiler_params=_params(("parallel", "arbitrary")))(xc, xc, xc, proj, dt, ac, par8)


def _ssd_bwd(dyn, xc, proj, dt, ac, par8, states, dproj, ngroups, z_col, name, comm=None):
    t = xc.shape[0]
    nc = t // SSD_Q
    steps = ngroups // SSD_GROUPS

    n_cin = comm.n_in if comm else 0
    n_cout = comm.n_out if comm else 0

    def body(*refs):
        dy_ref, x_ref, b_ref, c_ref, z_ref, dt_ref, ac_ref, p_ref, s_ref = refs[:9]
        dz_ref, dx_ref, db_ref, dc_ref, ddt_ref, dac_ref, dp_ref = refs[10 + n_cin:17 + n_cin]
        ds_ref = refs[17 + n_cin + n_cout]
        c = pl.program_id(1)
        if comm:
            begin, end = comm.ops(refs[10:10 + n_cin], refs[17 + n_cin:17 + n_cin + n_cout], refs[-3:])
            pl.when((pl.program_id(0) == 0) & (c == 0))(begin)

        @pl.when(c == 0)
        def _():
            ds_ref[...] = jnp.zeros_like(ds_ref)

        dtv, acv = dt_ref[...], ac_ref[...]
        ddt_sum, dac_sum, upds = None, None, []
        for i in range(SSD_GROUPS):
            cx, cb = slice(i * GROUP_W, (i + 1) * GROUP_W), slice(i * N_STATE, (i + 1) * N_STATE)
            g = pl.program_id(0) * SSD_GROUPS + i
            args = (x_ref[:, cx].astype(F32), b_ref[:, cb].astype(F32), c_ref[:, cb].astype(F32),
                    z_ref[:, cx].astype(F32), dtv, acv, s_ref[:, cx], p_ref[0:1, cx], p_ref[1:2, cx])
            _, vjp = jax.vjp(lambda *a, g=g: _ssd_chunk(*a, g), *args)
            dxs, dbm, dcm, dz, ddt, dac, dst, ddsk, dnw = vjp((dy_ref[:, cx], ds_ref[:, cx]))
            ds_ref[:, cx] = dst
            dz_ref[:, cx] = dz.astype(BF16)
            dx_ref[:, cx] = dxs
            db_ref[:, cb] = dbm
            dc_ref[:, cb] = dcm
            ddt_sum = ddt if ddt_sum is None else ddt_sum + ddt
            dac_sum = dac if dac_sum is None else dac_sum + dac
            upds.append(jnp.concatenate([ddsk, dnw, jnp.zeros((6, GROUP_W), F32)], axis=0))
        ddt_ref[...] = ddt_sum
        dac_ref[...] = dac_sum
        upd = jnp.concatenate(upds, axis=1)

        @pl.when(c == 0)
        def _():
            dp_ref[...] = upd

        @pl.when(c > 0)
        def _():
            dp_ref[...] += upd

        if comm:
            pl.when((pl.program_id(0) == steps - 1) & (c == nc - 1))(end)

    xs, bm, cm, zs, hd, par, stb, db = _ssd_specs(nc, z_col, ngroups, True)
    hg = pl.BlockSpec((None, SSD_Q, LANE), lambda g, c: (g, nc - 1 - c, 0))
    any_spec = pl.BlockSpec(memory_space=pl.ANY)
    aliases = {9: 0}
    if comm:
        aliases.update({10 + i: 7 + o for i, o in comm.aliases.items()})
    res = pl.pallas_call(
        body, name=name,
        out_shape=[jax.ShapeDtypeStruct(dproj.shape, BF16),
                   jax.ShapeDtypeStruct((t, ngroups * GROUP_W), F32),
                   jax.ShapeDtypeStruct((t, ngroups * N_STATE), F32), jax.ShapeDtypeStruct((t, ngroups * N_STATE), F32),
                   jax.ShapeDtypeStruct((steps, t, LANE), F32), jax.ShapeDtypeStruct((steps, t, LANE), F32),
                   jax.ShapeDtypeStruct((8, ngroups * GROUP_W), F32)] + (comm.out_shape if comm else []),
        grid=(steps, nc),
        in_specs=[xs, xs, bm, cm, zs, hd, hd, par, stb, any_spec] + [any_spec] * n_cin,
        out_specs=[zs, xs, db, db, hg, hg, par] + [any_spec] * n_cout, input_output_aliases=aliases,
        scratch_shapes=[pltpu.VMEM((N_STATE, SSD_GROUPS * GROUP_W), F32)] + (comm.sems if comm else []),
        compiler_params=_params(("arbitrary", "arbitrary")))(
            dyn, xc, xc, xc, proj, dt, ac, par8, states, dproj, *(comm.operands if comm else []))
    return tuple(res[:7]), list(res[7:])


def _small_matmul(a, b, mode, name):
    if mode == "nn":
        shape = (a.shape[0], b.shape[1])
    else:
        shape = (a.shape[1], b.shape[1])

    def body(a_ref, b_ref, o_ref):
        o_ref[...] = _dot(a_ref[...], b_ref[...], mode)

    return pl.pallas_call(body, name=name, out_shape=jax.ShapeDtypeStruct(shape, F32),
                          compiler_params=_params())(a, b)


def _adamw(parts, w, m, v, name, shard_stride=None):
    npart, rows, pcols = parts.shape
    cols = w.shape[1]
    tr = rows if rows <= 128 else _tile8(rows, 128)

    def body(p_ref, w_ref, m_ref, v_ref, g_ref, d_ref, nm_ref, nv_ref):
        g = p_ref[0].astype(F32)
        for j in range(1, npart):
            g = g + p_ref[j].astype(F32)
        if shard_stride is not None:
            me = 4 * lax.axis_index("x") + 2 * lax.axis_index("y") + lax.axis_index("c")
            off = (shard_stride * me) % LANE
            g = pltpu.roll(g, (pcols - off) % pcols, 1)[:, :cols]
        mm = ADAM_B1 * m_ref[...] + (1.0 - ADAM_B1) * g
        vv = ADAM_B2 * v_ref[...] + (1.0 - ADAM_B2) * (g * g)
        m_hat = mm / (1.0 - ADAM_B1 ** ADAM_STEP)
        v_hat = vv / (1.0 - ADAM_B2 ** ADAM_STEP)
        g_ref[...] = g
        d_ref[...] = -ADAM_LR * (m_hat / (jnp.sqrt(v_hat) + ADAM_EPS) + ADAM_WD * w_ref[...])
        nm_ref[...] = mm
        nv_ref[...] = vv

    blk = pl.BlockSpec((tr, cols), lambda i: (i, 0))
    out = jax.ShapeDtypeStruct((rows, cols), F32)
    return pl.pallas_call(
        body, name=name, out_shape=[out] * 4, grid=(rows // tr,),
        in_specs=[pl.BlockSpec((npart, tr, pcols), lambda i: (0, i, 0)), blk, blk, blk], out_specs=[blk] * 4,
        compiler_params=_params(("parallel",)))(parts, w, m, v)


def _tile8(n, target):
    best = 8
    for d in range(8, target + 1, 8):
        if n % d == 0:
            best = d
    return best


def _pad_rows8(a):
    return jnp.concatenate([a, jnp.zeros((8 - a.shape[0], a.shape[1]), a.dtype)], axis=0)


def _pad_lanes(a, n):
    return jnp.concatenate([a, jnp.zeros(a.shape[:-1] + (n - a.shape[-1],), a.dtype)], axis=-1)


def _pack(vecs, mult):
    flat = jnp.concatenate([v.reshape(-1) for v in vecs])
    pad = (-flat.shape[0]) % mult
    flat = jnp.concatenate([flat, jnp.zeros((pad,), flat.dtype)])
    return flat.reshape(-1, LANE)


def _unpack(flat, shapes):
    out, o = [], 0
    for s in shapes:
        n = int(np.prod(s))
        out.append(flat[o:o + n].reshape(s))
        o += n
    return out


def kernel(x, c, w_ada, b_ada, ln1, ln2, w_in, conv_w, ssm_conv_w, ssm_conv_b, dt_bias, a_log, d_skip, ssm_norm_w, w_conv_out, w_ssm_out, w_o, w_up, w_down, final_norm, loss_target, m_w_ada, m_b_ada, m_ln1, m_ln2, m_w_in, m_conv_w, m_ssm_conv_w, m_ssm_conv_b, m_dt_bias, m_a_log, m_d_skip, m_ssm_norm_w, m_w_conv_out, m_w_ssm_out, m_w_o, m_w_up, m_w_down, m_final_norm, v_w_ada, v_b_ada, v_ln1, v_ln2, v_w_in, v_conv_w, v_ssm_conv_w, v_ssm_conv_b, v_dt_bias, v_a_log, v_d_skip, v_ssm_norm_w, v_w_conv_out, v_w_ssm_out, v_w_o, v_w_up, v_w_down, v_final_norm):
    names = ["w_ada", "b_ada", "ln1", "ln2", "w_in", "conv_w", "ssm_conv_w", "ssm_conv_b", "dt_bias", "a_log",
             "d_skip", "ssm_norm_w", "w_conv_out", "w_ssm_out", "w_o", "w_up", "w_down", "final_norm"]
    w_of = dict(zip(names, [w_ada, b_ada, ln1, ln2, w_in, conv_w, ssm_conv_w, ssm_conv_b, dt_bias, a_log, d_skip,
                            ssm_norm_w, w_conv_out, w_ssm_out, w_o, w_up, w_down, final_norm]))
    m_of = dict(zip(names, [m_w_ada, m_b_ada, m_ln1, m_ln2, m_w_in, m_conv_w, m_ssm_conv_w, m_ssm_conv_b, m_dt_bias,
                            m_a_log, m_d_skip, m_ssm_norm_w, m_w_conv_out, m_w_ssm_out, m_w_o, m_w_up, m_w_down,
                            m_final_norm]))
    v_of = dict(zip(names, [v_w_ada, v_b_ada, v_ln1, v_ln2, v_w_in, v_conv_w, v_ssm_conv_w, v_ssm_conv_b, v_dt_bias,
                            v_a_log, v_d_skip, v_ssm_norm_w, v_w_conv_out, v_w_ssm_out, v_w_o, v_w_up, v_w_down,
                            v_final_norm]))

    _, t, d = x.shape
    nl = w_ada.shape[0]
    ada_w = w_ada.shape[2]
    ds = ssm_norm_w.shape[1]
    nh = dt_bias.shape[1]
    ng = nh // HEADS_PER_GROUP
    gn = ng * N_STATE
    xbc_w = ds + 2 * gn
    z_off, xbc_off = 5 * d, 5 * d + ds
    dt_off = xbc_off + xbc_w
    proj_w = dt_off + nh
    pw = dt_off + 2 * LANE
    me = 4 * lax.axis_index("x") + 2 * lax.axis_index("y") + lax.axis_index("c")

    x2, tgt = x[0], loss_target[0]

    c_act = c * jax.nn.sigmoid(c)
    sizes1 = [(1, d), conv_w.shape, ssm_conv_w.shape]
    (g1,) = _all_gather([_pack([c_act, conv_w, ssm_conv_w], LANE)], "ag_small_in")
    parts1 = [_unpack(g1[j].reshape(-1), sizes1) for j in range(N_DEV)]
    c_act_all = jnp.concatenate([p[0] for p in parts1], axis=0)
    conv_w_full = jnp.concatenate([p[1] for p in parts1], axis=-1)
    sconv_w_full = jnp.concatenate([p[2] for p in parts1], axis=-1)

    mod_part = jnp.stack([_small_matmul(c_act_all, w_ada[l], "nn", f"ada_fwd{l}") for l in range(nl)])
    (gmod,) = _all_gather([mod_part], "ag_mod")
    mod = lax.dynamic_index_in_dim(gmod, me, axis=2, keepdims=False)
    mod = jnp.moveaxis(mod, 0, 1).reshape(nl, N_DEV * ada_w) + b_ada
    mod = mod.reshape(nl, 6, d)
    modp = [jnp.concatenate([mod[l], ln1[l][None], ln2[l][None]], axis=0) for l in range(nl)]

    big = ["w_in", "w_up", "w_conv_out", "w_ssm_out", "w_o", "w_down"]
    rest = big[1:]

    def blocks_of(l, keys):
        return [w_of[k][l].astype(BF16) for k in keys]

    def w_in_halves(l):
        w = w_of["w_in"][l].astype(BF16)
        return [w[:d // 2]], [w[d // 2:]]

    def full_cols(g):
        return jnp.moveaxis(g, 0, 1).reshape(g.shape[1], N_DEV * g.shape[2])

    def full_rows(g):
        return g.reshape(N_DEV * g.shape[1], g.shape[2])

    def pad_w_in(pieces):
        w = jnp.concatenate([full_cols(g) for g in pieces], axis=0)
        return jnp.concatenate([w, jnp.zeros((d, pw - proj_w), BF16)], axis=-1)

    wi = [None] * nl
    wi[0] = pad_w_in(_all_gather(blocks_of(0, ["w_in"]), "ag_w_in0"))

    cw8 = [_pad_rows8(conv_w_full[l]) for l in range(nl)]
    sw8 = [_pad_rows8(sconv_w_full[l]) for l in range(nl)]
    sb8 = [_pad_rows8(ssm_conv_b[l][None]) for l in range(nl)]
    dtp8 = [_pad_rows8(_pad_lanes(jnp.stack([dt_bias[l], a_log[l]]), LANE)) for l in range(nl)]
    sp8 = [_pad_rows8(jnp.stack([jnp.repeat(d_skip[l], HEAD_DIM), ssm_norm_w[l]])) for l in range(nl)]
    fn8 = _pad_rows8(final_norm[None])
    xbc_col, ncol, dt_col, z_col = xbc_off // d, xbc_w // d, dt_off // (2 * LANE), z_off // GROUP_W

    saved = []
    x_cur, br_prev = x2, None
    for l in range(nl):
        if l == 0:
            x_in, u1 = x_cur, _norm_fwd(x_cur, modp[0], 0, "norm_first")
        else:
            x_in, u1 = _resid_norm_fwd(x_cur, br_prev, modp[l - 1], 5, modp[l], 0, f"resid_norm_a{l}")
        proj, got = _matmul(u1, wi[l], "nn", BF16, f"mm_in{l}",
                            comm=_GatherComm(blocks_of(l, rest), as_columns=[k == "w_up" for k in rest]))
        got = dict(zip(rest, got))
        wup, wdown = got["w_up"], full_rows(got["w_down"])
        wco, wso, wo = full_rows(got["w_conv_out"]), full_rows(got["w_ssm_out"]), full_rows(got["w_o"])
        y_conv = _gconv_fwd(proj, cw8[l], d, f"gconv_fwd{l}")
        xc, pre = _sconv_fwd(proj, sw8[l], sb8[l], d, xbc_col, ncol, f"sconv_fwd{l}")
        dt, ac = _dt_fwd(proj, dtp8[l], nh, dt_col, f"dt_fwd{l}")
        yn, states = _ssd_fwd(xc, proj, dt, ac, sp8[l], ng, z_col, f"ssd_fwd{l}")
        p_conv = _matmul(y_conv, wco, "nn", BF16, f"mm_conv_out{l}")
        p_ssm = _matmul(yn, wso, "nn", BF16, f"mm_ssm_out{l}")
        merged = _merge_fwd(proj, p_conv, p_ssm, f"merge_fwd{l}")
        mix = _matmul(merged, wo, "nn", BF16, f"mm_o{l}")
        x_mid, u2 = _resid_norm_fwd(x_in, mix, modp[l], 2, modp[l], 1, f"resid_norm_b{l}")
        if l + 1 < nl:
            top, bot = w_in_halves(l + 1)
            (h, hid), g_top = _matmul(u2, wup, "nn", BF16, f"mm_up{l}", epi="relu2", comm=_GatherComm(top))
            mlp, g_bot = _matmul(hid, wdown, "nn", BF16, f"mm_down{l}", comm=_GatherComm(bot))
            wi[l + 1] = pad_w_in(g_top + g_bot)
        else:
            h, hid = _matmul(u2, wup, "nn", BF16, f"mm_up{l}", epi="relu2")
            mlp = _matmul(hid, wdown, "nn", BF16, f"mm_down{l}")
        saved.append(dict(x_in=x_in, u1=u1, proj=proj, y_conv=y_conv, xc=xc, pre=pre, dt=dt, ac=ac, yn=yn,
                          states=states,
                          p_conv=p_conv, p_ssm=p_ssm, merged=merged, mix=mix, x_mid=x_mid, u2=u2, h=h, hid=hid,
                          mlp=mlp, wup=wup, wdown=wdown, wco=wco, wso=wso, wo=wo))
        x_cur, br_prev = x_mid, mlp

    dx, dbr, acc = _final_fwd_bwd(x_cur, br_prev, modp[nl - 1], fn8, tgt, "final")
    loss = lax.psum(acc[0, 0], ("x", "y", "c"))
    g_final_norm = acc[1]
    dgate2 = acc[2]

    gw = {k: [None] * nl for k in big}
    small = {k: [None] * nl for k in ["mod", "ln1", "ln2", "conv_w", "ssm_conv_w", "ssm_conv_b", "dt_bias",
                                      "a_log", "d_skip", "ssm_norm_w"]}

    def grad_rows(a):
        return a.reshape(N_DEV, a.shape[0] // N_DEV, a.shape[1])

    def recv_buf(name):
        return lax.empty((N_DEV, nl) + w_of[name].shape[1:], BF16)

    in_shard, up_shard = w_in.shape[2], w_up.shape[2]
    in_win = -(-(in_shard + max((in_shard * j) % LANE for j in range(N_DEV))) // LANE) * LANE
    assert up_shard % LANE == 0 and (in_shard * (N_DEV - 1)) // LANE * LANE + in_win <= pw
    win_a, win_b = [(in_win, in_shard)], [(up_shard, up_shard)] + [None] * 4
    bufs_a = [lax.empty((N_DEV, nl, d, in_win), BF16)]
    bufs_b = [recv_buf(k) for k in big[1:]]
    pending = None
    for l in reversed(range(nl)):
        s = saved[l]
        dh = _matmul(dbr, s["wdown"], "nt", BF16, f"mm_down_dx{l}", epi="relu2_bwd", extra=s["h"])
        gw["w_down"][l] = _matmul(s["hid"], dbr, "tn", BF16, f"mm_down_dw{l}")
        du2 = _matmul(dh, s["wup"], "nt", BF16, f"mm_up_dx{l}")
        gw["w_up"][l] = _matmul(s["u2"], dh, "tn", BF16, f"mm_up_dw{l}")
        dx_mid, dmix, acc2 = _resid_norm_bwd(dx, du2, s["x_mid"], modp[l], 1, f"resid_norm_b_bwd{l}",
                                             br=s["mix"], mp_gate=modp[l], gate_row=2)
        dmerged = _matmul(dmix, s["wo"], "nt", BF16, f"mm_o_dx{l}")
        gw["w_o"][l] = _matmul(s["merged"], dmix, "tn", BF16, f"mm_o_dw{l}")
        dpc, dps, dgl = _merge_bwd(dmerged, s["proj"], s["p_conv"], s["p_ssm"], f"merge_bwd{l}")
        dyc = _matmul(dpc, s["wco"], "nt", F32, f"mm_conv_out_dx{l}")
        dproj, dcw = _gconv_bwd(dyc, s["proj"], cw8[l], dgl, d, f"gconv_bwd{l}")
        gw["w_conv_out"][l] = _matmul(s["y_conv"], dpc, "tn", BF16, f"mm_conv_out_dw{l}")
        dyn = _matmul(dps, s["wso"], "nt", F32, f"mm_ssm_out_dx{l}")
        gw["w_ssm_out"][l] = _matmul(s["yn"], dps, "tn", BF16, f"mm_ssm_out_dw{l}")
        rest_grads = [gw["w_up"][l]] + [grad_rows(gw[k][l]) for k in big[2:]]
        sends, bufs, layers, wins = rest_grads, list(bufs_b), [l] * len(rest_grads), win_b
        if pending is not None:
            sends, bufs, layers, wins = sends + pending, bufs + list(bufs_a), layers + [l + 1], win_b + win_a
        (dproj, dxs, dbm, dcm, ddt_g, dac_g, dsp), bufs = _ssd_bwd(
            dyn, s["xc"], s["proj"], s["dt"], s["ac"], sp8[l], s["states"], dproj, ng, z_col, f"ssd_bwd{l}",
            comm=_ExchangeComm(sends, bufs, layers, windows=wins))
        bufs_b = bufs[:len(rest_grads)]
        if pending is not None:
            bufs_a = bufs[len(rest_grads):]
        dsw = []
        for seg, (dseg, c0) in enumerate([(dxs, 0), (dbm, ds // d), (dcm, ds // d + gn // d)]):
            cols = slice(c0 * d, c0 * d + dseg.shape[1])
            dproj, part = _sconv_bwd(dseg, s["pre"], c0, s["proj"], sw8[l][:, cols], dproj, d, xbc_col + c0,
                                     dseg.shape[1] // d, f"sconv_bwd{l}_{seg}")
            dsw.append(part)
        dsw = jnp.concatenate(dsw, axis=1)
        dproj, ddtp = _dt_bwd(ddt_g, dac_g, s["dt"], s["proj"], dtp8[l], dproj, nh, dt_col, f"dt_bwd{l}")
        du1 = _matmul(dproj, wi[l], "nt", BF16, f"mm_in_dx{l}")

        if l > 0:
            pending = [_matmul(s["u1"], dproj, "tn", BF16, f"mm_in_dw{l}")]
        else:
            g_top = _matmul(s["u1"][:, :d // 2], dproj, "tn", BF16, "mm_in_dw0_top")
            g_bot, bufs_a = _matmul(s["u1"][:, d // 2:], dproj, "tn", BF16, "mm_in_dw0_bot",
                                    comm=_ExchangeComm([g_top], bufs_a, 0, row0=0, windows=win_a))
            pending = [g_bot]
        if l > 0:
            dx, dbr, acc1 = _resid_norm_bwd(dx_mid, du1, s["x_in"], modp[l], 0, f"resid_norm_a_bwd{l}",
                                            br=saved[l - 1]["mlp"], mp_gate=modp[l - 1], gate_row=5)
        else:
            dx, acc1 = _resid_norm_bwd(dx_mid, du1, s["x_in"], modp[0], 0, "norm_first_bwd")
        sc1, sc2 = mod[l, 1], mod[l, 4]
        small["mod"][l] = jnp.stack([acc1[0], acc1[1] * ln1[l], acc2[2], acc2[0], acc2[1] * ln2[l], dgate2])
        small["ln1"][l] = acc1[1] * (1.0 + sc1)
        small["ln2"][l] = acc2[1] * (1.0 + sc2)
        small["conv_w"][l] = dcw[:3]
        small["ssm_conv_w"][l] = dsw[:4]
        small["ssm_conv_b"][l] = dsw[4]
        small["dt_bias"][l] = ddtp[0, :nh]
        small["a_log"][l] = ddtp[1, :nh]
        small["d_skip"][l] = dsp[0].reshape(nh, HEAD_DIM).sum(axis=-1)
        small["ssm_norm_w"][l] = dsp[1]
        if l > 0:
            dgate2 = acc1[2]
    grad_x = dx[None]

    sm = {k: jnp.stack(v) for k, v in small.items()}
    rep_names = ["b_ada", "ln1", "ln2", "ssm_conv_b", "dt_bias", "a_log", "d_skip", "ssm_norm_w", "final_norm"]
    rep_grads = [sm["mod"].reshape(nl, 6 * d), sm["ln1"], sm["ln2"], sm["ssm_conv_b"], sm["dt_bias"], sm["a_log"],
                 sm["d_skip"], sm["ssm_norm_w"], g_final_norm]
    rep_pack = _pack(rep_grads, 8 * LANE)
    conv_pack = _pack([sm["conv_w"], sm["ssm_conv_w"]], 8 * LANE)
    g_rep, g_conv = _all_gather([rep_pack, conv_pack], "ag_small_grads")

    outs = {}

    def run_adamw(name, parts, tag, shard_stride=None):
        w, m, v = w_of[name], m_of[name], v_of[name]
        shp = w.shape
        r2 = (int(np.prod(shp[:-1])), shp[-1])
        res = _adamw(parts.reshape(parts.shape[0], r2[0], parts.shape[-1]), w.reshape(r2), m.reshape(r2),
                     v.reshape(r2), tag, shard_stride)
        outs[name] = [a.reshape(shp) for a in res]

    rep_shapes = [w_of[k].shape for k in rep_names]
    res = _adamw(g_rep, _pack([w_of[k] for k in rep_names], 8 * LANE), _pack([m_of[k] for k in rep_names], 8 * LANE),
                 _pack([v_of[k] for k in rep_names], 8 * LANE), "adamw_replicated")
    for k, vals in zip(rep_names, zip(*[_unpack(a.reshape(-1), rep_shapes) for a in res])):
        outs[k] = list(vals)

    conv_parts = [_unpack(g_conv[j].reshape(-1), [sm["conv_w"].shape, sm["ssm_conv_w"].shape]) for j in range(N_DEV)]
    for idx, name in enumerate(["conv_w", "ssm_conv_w"]):
        wsh = w_of[name].shape[-1]
        full = jnp.stack([p[idx] for p in conv_parts])
        run_adamw(name, lax.dynamic_slice_in_dim(full, me * wsh, wsh, axis=3), "adamw_" + name)

    dmod_all = g_rep.reshape(N_DEV, -1)[:, :nl * 6 * d].reshape(N_DEV, nl, 6 * d)
    dmod_mine = lax.dynamic_slice_in_dim(dmod_all, me * ada_w, ada_w, axis=2)
    g_ada = jnp.stack([_small_matmul(c_act_all, dmod_mine[:, l], "tn", f"ada_bwd{l}") for l in range(nl)])
    run_adamw("w_ada", g_ada[None], "adamw_w_ada")

    bufs_a = _exchange(pending, bufs_a, 0, "a2a_w_in0_bot", row0=d // 2, windows=win_a)
    run_adamw("w_in", bufs_a[0], "adamw_w_in", shard_stride=in_shard)
    for name, parts in zip(big[1:], bufs_b):
        run_adamw(name, parts, "adamw_" + name)

    result = [loss, grad_x]
    for i in range(4):
        result += [outs[k][i] for k in names]
    return tuple(result)
```

```python
import functools

import numpy as np
import jax
import jax.numpy as jnp
from jax import lax
from jax.experimental import pallas as pl
from jax.experimental.pallas import tpu as pltpu

F32 = jnp.float32
BF16 = jnp.bfloat16
EPS = 1e-6
N_STATE = 128
HEAD_DIM = 64
HEADS_PER_GROUP = 4
GROUP_W = HEAD_DIM * HEADS_PER_GROUP
SSD_GROUPS = 2
SSD_Q = 256
N_DEV = 8
ROW_TILE = 512
LANE = 128
VMEM_LIMIT = 56 * 1024 * 1024
MATMUL_VMEM_BUDGET = 40 * 1024 * 1024
MATMUL_TILE_CAP = 2048
MXU_FLOPS = 9.0e14
HBM_BYTES_PER_S = 3.0e12
STEP_OVERHEAD_S = 0.35e-6

ADAM_LR, ADAM_B1, ADAM_B2, ADAM_EPS, ADAM_WD, ADAM_STEP = 0.001, 0.9, 0.999, 1e-08, 0.01, 10

MESH = pl.DeviceIdType.MESH


def _params(sem=None):
    return pltpu.CompilerParams(dimension_semantics=sem, vmem_limit_bytes=VMEM_LIMIT)


def _divisors(n, cap):
    if n <= cap:
        return [n]
    return [d for d in range(cap - cap % LANE, 0, -LANE) if n % d == 0]


def _matmul_tiles(m, n, k, out_bytes, n_out, n_extra):
    best = None
    for tm in _divisors(m, MATMUL_TILE_CAP):
        for tn in _divisors(n, MATMUL_TILE_CAP):
            for tk in _divisors(k, MATMUL_TILE_CAP):
                out_tile = tm * tn * (out_bytes * n_out + 2 * n_extra)
                vmem = 2 * 2 * (tm * tk + tk * tn) + 2 * out_tile + (4 * tm * tn if tk < k else 0)
                if vmem > MATMUL_VMEM_BUDGET:
                    continue
                steps = (m // tm) * (n // tn) * (k // tk)
                a_reads = 1 if tk == k else n // tn
                traffic = 2 * (a_reads * m * k + (m // tm) * k * n) + m * n * (out_bytes * n_out + 2 * n_extra)
                est = (max(2.0 * m * n * k / MXU_FLOPS, traffic / HBM_BYTES_PER_S) + steps * STEP_OVERHEAD_S
                       + (2 * (tm * tk + tk * tn) + out_tile) / HBM_BYTES_PER_S)
                cand = (est, tm, tn, tk)
                if best is None or cand < best:
                    best = cand
    assert best is not None, (m, n, k)
    return best[1:]


def _tile(n, target):
    if n <= target:
        return n
    best = None
    for d in range(LANE, target + 1, LANE):
        if n % d == 0:
            best = d
    assert best is not None, (n, target)
    return best


def _slot(p):
    return 4 * p[0] + 2 * p[1] + p[2]


def _all_gather(xs, name):
    comm = _GatherComm(xs)

    def body(*refs):
        begin, end = comm.ops(refs[:comm.n_in], refs[comm.n_in:comm.n_in + comm.n_out], refs[-3:])
        begin()
        end()

    any_spec = pl.BlockSpec(memory_space=pl.ANY)
    return pl.pallas_call(
        body, name=name, out_shape=comm.out_shape,
        in_specs=[any_spec] * comm.n_in, out_specs=[any_spec] * comm.n_out, scratch_shapes=comm.sems,
    )(*comm.operands)


class _GatherComm:
    def __init__(self, xs, as_columns=None):
        n = len(xs)
        self.operands = list(xs)
        self.n_in = self.n_out = n
        self.aliases = {}
        self.as_columns = list(as_columns) if as_columns else [False] * n
        assert all(x.shape[1] % LANE == 0 for x, f in zip(xs, self.as_columns) if f)
        self.out_shape = [jax.ShapeDtypeStruct((x.shape[0], N_DEV * x.shape[1]) if f else (N_DEV,) + x.shape, x.dtype)
                          for x, f in zip(xs, self.as_columns)]
        self.widths = [x.shape[-1] for x in xs]
        self.sems = [pltpu.SemaphoreType.DMA((7 * n,)), pltpu.SemaphoreType.DMA((7 * n,)),
                     pltpu.SemaphoreType.DMA((n,))]

    def ops(self, x_refs, o_refs, sem_refs):
        n = self.n_in
        send, recv, loc = sem_refs
        x, y, c = lax.axis_index("x"), lax.axis_index("y"), lax.axis_index("c")
        me, sib = (x, y, c), (x, y, 1 - c)
        chips = [(1 - x, y), (x, 1 - y), (1 - x, 1 - y)]

        def place(a, block):
            if self.as_columns[a]:
                w = self.widths[a]
                return o_refs[a].at[:, pl.ds(pl.multiple_of(_slot(block) * w, LANE), w)]
            return o_refs[a].at[_slot(block)]

        def cp(a, k, block, to, src=None):
            dst = place(a, block)
            return pltpu.make_async_remote_copy(
                src_ref=dst if src is None else src, dst_ref=dst,
                send_sem=send.at[7 * a + k], recv_sem=recv.at[7 * a + k],
                device_id=to, device_id_type=MESH)

        mine = [pltpu.make_async_copy(x_refs[a], place(a, me), loc.at[a]) for a in range(n)]
        first = []
        for a in range(n):
            first.append(cp(a, 0, me, sib, src=x_refs[a]))
            first += [cp(a, 1 + j, me, (*chip, c), src=x_refs[a]) for j, chip in enumerate(chips)]

        def begin():
            for m in mine:
                m.start()
            for f in first:
                f.start()

        def end():
            passed = []
            for j, chip in enumerate(chips):
                for a in range(n):
                    cp(a, 1 + j, (*chip, c), me).wait_recv()
                    p = cp(a, 4 + j, (*chip, c), sib)
                    p.start()
                    passed.append(p)
            for a in range(n):
                cp(a, 0, sib, me).wait_recv()
                for j, chip in enumerate(chips):
                    cp(a, 4 + j, (*chip, 1 - c), me).wait_recv()
            for f in first + passed:
                f.wait_send()
            for m in mine:
                m.wait()

        return begin, end


class _ExchangeComm:
    def __init__(self, xs, bufs, layer, row0=0, windows=None):
        n = len(xs)
        self.layers = list(layer) if isinstance(layer, (list, tuple)) else [layer] * n
        self.row0 = row0
        self.windows = list(windows) if windows else [None] * n
        self.nrows = [x.shape[1] if wdw is None else x.shape[0] for x, wdw in zip(xs, self.windows)]
        self.operands = list(xs) + list(bufs)
        self.n_in, self.n_out = 2 * n, n
        self.aliases = {n + a: a for a in range(n)}
        self.out_shape = [jax.ShapeDtypeStruct(b.shape, b.dtype) for b in bufs]
        self.sems = [pltpu.SemaphoreType.DMA((7 * n,)), pltpu.SemaphoreType.DMA((7 * n,)),
                     pltpu.SemaphoreType.DMA((n,))]

    def ops(self, in_refs, o_refs, sem_refs):
        n = self.n_out
        x_refs = in_refs[:n]
        send, recv, loc = sem_refs
        x, y, c = lax.axis_index("x"), lax.axis_index("y"), lax.axis_index("c")
        me = (x, y, c)
        peers = []
        for k in range(1, 8):
            kx, ky, kc = (k >> 2) & 1, (k >> 1) & 1, k & 1
            peers.append((x + kx - 2 * x * kx, y + ky - 2 * y * ky, c + kc - 2 * c * kc))

        def land(a, slot):
            return o_refs[a].at[slot, self.layers[a], pl.ds(self.row0, self.nrows[a])]

        def block(a, slot):
            if self.windows[a] is None:
                return x_refs[a].at[slot]
            width, stride = self.windows[a]
            start = pl.multiple_of((stride * slot) // LANE * LANE, LANE)
            return x_refs[a].at[:, pl.ds(start, width)]

        def cp(a, k, src_slot, dst_slot, to):
            return pltpu.make_async_remote_copy(
                src_ref=block(a, src_slot), dst_ref=land(a, dst_slot),
                send_sem=send.at[7 * a + k], recv_sem=recv.at[7 * a + k],
                device_id=to, device_id_type=MESH)

        mine = [pltpu.make_async_copy(block(a, _slot(me)), land(a, _slot(me)), loc.at[a])
                for a in range(n)]
        sends = [cp(a, k, _slot(p), _slot(me), p) for a in range(n) for k, p in enumerate(peers)]

        def begin():
            for m in mine:
                m.start()
            for s in sends:
                s.start()

        def end():
            for a in range(n):
                for k, p in enumerate(peers):
                    cp(a, k, _slot(me), _slot(p), me).wait_recv()
            for s in sends:
                s.wait_send()
            for m in mine:
                m.wait()

        return begin, end


def _exchange(xs, bufs, layer, name, row0=0, windows=None):
    comm = _ExchangeComm(xs, bufs, layer, row0, windows)

    def body(*refs):
        begin, end = comm.ops(refs[:comm.n_in], refs[comm.n_in:comm.n_in + comm.n_out], refs[-3:])
        begin()
        end()

    any_spec = pl.BlockSpec(memory_space=pl.ANY)
    return pl.pallas_call(
        body, name=name, out_shape=comm.out_shape,
        in_specs=[any_spec] * comm.n_in, out_specs=[any_spec] * comm.n_out, scratch_shapes=comm.sems,
        input_output_aliases=dict(comm.aliases),
    )(*comm.operands)


_DIMS = {"nn": (((1,), (0,)), ((), ())), "nt": (((1,), (1,)), ((), ())), "tn": (((0,), (0,)), ((), ()))}


def _matmul(a, b, mode, out_dtype, name, comm=None, epi=None, extra=None):
    if mode == "nn":
        (m, k), (_, n) = a.shape, b.shape
    elif mode == "nt":
        (m, k), (n, _) = a.shape, b.shape
    else:
        (k, m), (_, n) = a.shape, b.shape
    n_ext = 1 if epi == "relu2_bwd" else 0
    n_main = 2 if epi == "relu2" else 1
    tm, tn, tk = _matmul_tiles(m, n, k, jnp.dtype(out_dtype).itemsize, n_main, n_ext)
    nk = k // tk
    grid = (m // tm, n // tn, nk)
    dn = _DIMS[mode]
    n_cin = comm.n_in if comm else 0
    n_cout = comm.n_out if comm else 0
    n_acc = 0 if nk == 1 else 1
    first_in, first_out = 2 + n_ext, 2 + n_ext + n_cin

    def finish(acc, refs):
        outs = refs[first_out:first_out + n_main]
        if epi == "relu2":
            outs[0][...] = acc.astype(outs[0].dtype)
            r = jnp.maximum(acc, 0.0)
            outs[1][...] = (r * r).astype(outs[1].dtype)
        elif epi == "relu2_bwd":
            outs[0][...] = (acc * (2.0 * jnp.maximum(refs[2][...].astype(F32), 0.0))).astype(outs[0].dtype)
        else:
            outs[0][...] = acc.astype(outs[0].dtype)

    def body(*refs):
        a_ref, b_ref = refs[:2]
        scratch = refs[first_out + n_main + n_cout:]
        pid = [pl.program_id(ax) for ax in range(3)]
        if comm:
            begin, end = comm.ops(refs[first_in:first_in + n_cin],
                                  refs[first_out + n_main:first_out + n_main + n_cout], scratch[n_acc:])
            pl.when((pid[0] == 0) & (pid[1] == 0) & (pid[2] == 0))(begin)
        part = lax.dot_general(a_ref[...], b_ref[...], dn, preferred_element_type=F32)
        if nk == 1:
            finish(part, refs)
        else:
            acc_ref = scratch[0]

            @pl.when(pid[2] == 0)
            def _():
                acc_ref[...] = part

            @pl.when(pid[2] > 0)
            def _():
                acc_ref[...] += part

            @pl.when(pid[2] == nk - 1)
            def _():
                finish(acc_ref[...], refs)
        if comm:
            pl.when((pid[0] == grid[0] - 1) & (pid[1] == grid[1] - 1) & (pid[2] == grid[2] - 1))(end)

    a_spec = pl.BlockSpec((tk, tm), lambda i, j, kk: (kk, i)) if mode == "tn" else \
        pl.BlockSpec((tm, tk), lambda i, j, kk: (i, kk))
    b_spec = pl.BlockSpec((tn, tk), lambda i, j, kk: (j, kk)) if mode == "nt" else \
        pl.BlockSpec((tk, tn), lambda i, j, kk: (kk, j))
    o_spec = pl.BlockSpec((tm, tn), lambda i, j, kk: (i, j))
    any_spec = pl.BlockSpec(memory_space=pl.ANY)
    res = pl.pallas_call(
        body, name=name,
        out_shape=[jax.ShapeDtypeStruct((m, n), out_dtype)] * n_main + (comm.out_shape if comm else []),
        grid=grid,
        in_specs=[a_spec, b_spec] + [o_spec] * n_ext + [any_spec] * n_cin,
        out_specs=[o_spec] * n_main + [any_spec] * n_cout,
        scratch_shapes=([] if nk == 1 else [pltpu.VMEM((tm, tn), F32)]) + (comm.sems if comm else []),
        input_output_aliases={first_in + i: n_main + o for i, o in comm.aliases.items()} if comm else {},
        compiler_params=_params(("arbitrary",) * 3 if comm else ("parallel", "parallel", "arbitrary")),
    )(a, b, *([extra] if n_ext else []), *(comm.operands if comm else []))
    main = res[0] if n_main == 1 else tuple(res[:n_main])
    return (main, list(res[n_main:])) if comm else main


def _dot(a, b, mode):
    return lax.dot_general(a.astype(BF16), b.astype(BF16), _DIMS[mode], preferred_element_type=F32)


@functools.partial(jax.custom_vjp, nondiff_argnums=(2,))
def _bdot(a, b, mode):
    return _dot(a, b, mode)


def _bdot_fwd(a, b, mode):
    return _dot(a, b, mode), (a, b)


def _bdot_bwd(mode, res, g):
    a, b = res
    if mode == "nn":
        return _dot(g, b, "nt"), _dot(a, g, "tn")
    if mode == "nt":
        return _dot(g, b, "nn"), _dot(g, a, "tn")
    return _dot(b, g, "nt"), _dot(a, g, "nn")


_bdot.defvjp(_bdot_fwd, _bdot_bwd)


def _split3(x):
    hi = x.astype(BF16)
    r1 = x - hi.astype(F32)
    mid = r1.astype(BF16)
    return hi, mid, (r1 - mid.astype(F32)).astype(BF16)


def _xdot01(m01, x):
    return sum(lax.dot_general(m01, p, _DIMS["nn"], preferred_element_type=F32) for p in _split3(x))


@jax.custom_vjp
def _spread(v, m01):
    return sum(lax.dot_general(p, m01, _DIMS["nn"], preferred_element_type=F32) for p in _split3(v))


def _spread_fwd(v, m01):
    return _spread(v, m01), m01


def _spread_bwd(m01, g):
    dv = sum(lax.dot_general(p, m01, _DIMS["nt"], preferred_element_type=F32) for p in _split3(g)[:2])
    return dv, jnp.zeros_like(m01)


_spread.defvjp(_spread_fwd, _spread_bwd)


def _chunk_tri(n, lower):
    r = lax.broadcasted_iota(jnp.int32, (n, n), 0)
    c = lax.broadcasted_iota(jnp.int32, (n, n), 1)
    tri = (r >= c) if lower else (r <= c)
    return jnp.where(tri & (r // SSD_Q == c // SSD_Q), 1.0, 0.0).astype(BF16)


def _shift_down(x, prev8, k):
    if k == 0:
        return x
    n = x.shape[0]
    r = pltpu.roll(x, k, 0)
    rp = pltpu.roll(prev8, k, 0)
    rows = lax.broadcasted_iota(jnp.int32, (8, x.shape[1]), 0)
    head = jnp.where(rows < k, rp, r[:8])
    return head if n == 8 else jnp.concatenate([head, r[8:]], axis=0)


def _shift_up(x, next8, k):
    if k == 0:
        return x
    n = x.shape[0]
    r = pltpu.roll(x, n - k, 0)
    rn = pltpu.roll(next8, 8 - k, 0)
    rows = lax.broadcasted_iota(jnp.int32, (8, x.shape[1]), 0)
    tail = jnp.where(rows >= 8 - k, rn, r[n - 8:])
    return tail if n == 8 else jnp.concatenate([r[:n - 8], tail], axis=0)


def _sigmoid(x):
    return 0.5 * jnp.tanh(0.5 * x) + 0.5


def _rows(ref, i):
    return ref[i:i + 1, :]


def _norm_parts(x, mp_ref, which):
    ln, sh, sc = _rows(mp_ref, 6 + which), _rows(mp_ref, 3 * which), _rows(mp_ref, 3 * which + 1)
    r = lax.rsqrt(jnp.mean(x * x, axis=1, keepdims=True) + EPS)
    return r, ln, sh, sc


def _norm_fwd(x, modp, which, name):
    t, d = x.shape

    def body(x_ref, mp_ref, u_ref):
        xv = x_ref[...]
        r, ln, sh, sc = _norm_parts(xv, mp_ref, which)
        u_ref[...] = (((xv * r) * ln) * (1.0 + sc) + sh).astype(BF16)

    row = pl.BlockSpec((ROW_TILE, d), lambda i: (i, 0))
    return pl.pallas_call(
        body, name=name, out_shape=jax.ShapeDtypeStruct((t, d), BF16), grid=(t // ROW_TILE,),
        in_specs=[row, pl.BlockSpec((8, d), lambda i: (0, 0))], out_specs=row,
        compiler_params=_params(("parallel",)))(x, modp)


def _resid_norm_fwd(x, br, mp_gate, gate_row, mp_norm, which, name):
    t, d = x.shape

    def body(x_ref, br_ref, mg_ref, mn_ref, xn_ref, u_ref):
        xv = x_ref[...] + _rows(mg_ref, gate_row) * br_ref[...].astype(F32)
        xn_ref[...] = xv
        r, ln, sh, sc = _norm_parts(xv, mn_ref, which)
        u_ref[...] = (((xv * r) * ln) * (1.0 + sc) + sh).astype(BF16)

    row = pl.BlockSpec((ROW_TILE, d), lambda i: (i, 0))
    mp = pl.BlockSpec((8, d), lambda i: (0, 0))
    return pl.pallas_call(
        body, name=name,
        out_shape=[jax.ShapeDtypeStruct((t, d), F32), jax.ShapeDtypeStruct((t, d), BF16)],
        grid=(t // ROW_TILE,), in_specs=[row, row, mp, mp], out_specs=[row, row],
        compiler_params=_params(("parallel",)))(x, br, mp_gate, mp_norm)


def _final_fwd_bwd(x, br, mp_gate, fnorm8, target, name):
    t, d = x.shape

    def body(x_ref, br_ref, mg_ref, fn_ref, tg_ref, dx_ref, dbr_ref, acc_ref):
        gate = _rows(mg_ref, 5)
        brv = br_ref[...].astype(F32)
        xv = x_ref[...] + gate * brv
        fn = _rows(fn_ref, 0)
        r = lax.rsqrt(jnp.mean(xv * xv, axis=1, keepdims=True) + EPS)
        nrm = xv * r
        err = nrm * fn - tg_ref[...]
        loss = 0.5 * jnp.sum(jnp.mean(err * err, axis=1, keepdims=True), axis=0, keepdims=True)
        dy = err * (1.0 / d)
        dn = dy * fn
        dx = r * (dn - nrm * jnp.mean(dn * nrm, axis=1, keepdims=True))
        dx_ref[...] = dx
        dbr_ref[...] = (dx * gate).astype(BF16)
        upd = jnp.concatenate([
            jnp.broadcast_to(loss, (1, d)),
            jnp.sum(dy * nrm, axis=0, keepdims=True),
            jnp.sum(dx * brv, axis=0, keepdims=True),
            jnp.zeros((5, d), F32)], axis=0)

        @pl.when(pl.program_id(0) == 0)
        def _():
            acc_ref[...] = upd

        @pl.when(pl.program_id(0) > 0)
        def _():
            acc_ref[...] += upd

    row = pl.BlockSpec((ROW_TILE, d), lambda i: (i, 0))
    mp = pl.BlockSpec((8, d), lambda i: (0, 0))
    return pl.pallas_call(
        body, name=name,
        out_shape=[jax.ShapeDtypeStruct((t, d), F32), jax.ShapeDtypeStruct((t, d), BF16),
                   jax.ShapeDtypeStruct((8, d), F32)],
        grid=(t // ROW_TILE,), in_specs=[row, row, mp, mp, row], out_specs=[row, row, mp],
        compiler_params=_params(("arbitrary",)))(x, br, mp_gate, fnorm8, target)


def _resid_norm_bwd(dx, du, x, mp_norm, which, name, br=None, mp_gate=None, gate_row=None):
    t, d = x.shape
    has_gate = br is not None

    def body(*refs):
        if has_gate:
            dx_ref, du_ref, x_ref, mn_ref, br_ref, mg_ref, dxn_ref, dbr_ref, acc_ref = refs
        else:
            dx_ref, du_ref, x_ref, mn_ref, dxn_ref, acc_ref = refs
        xv, duv = x_ref[...], du_ref[...].astype(F32)
        r, ln, _, sc = _norm_parts(xv, mn_ref, which)
        nrm = xv * r
        dn = duv * (ln * (1.0 + sc))
        dxn = dx_ref[...] + r * (dn - nrm * jnp.mean(dn * nrm, axis=1, keepdims=True))
        dxn_ref[...] = dxn
        rows = [jnp.sum(duv, axis=0, keepdims=True), jnp.sum(duv * nrm, axis=0, keepdims=True)]
        if has_gate:
            dbr_ref[...] = (dxn * _rows(mg_ref, gate_row)).astype(BF16)
            rows.append(jnp.sum(dxn * br_ref[...].astype(F32), axis=0, keepdims=True))
        upd = jnp.concatenate(rows + [jnp.zeros((8 - len(rows), d), F32)], axis=0)

        @pl.when(pl.program_id(0) == 0)
        def _():
            acc_ref[...] = upd

        @pl.when(pl.program_id(0) > 0)
        def _():
            acc_ref[...] += upd

    row = pl.BlockSpec((ROW_TILE, d), lambda i: (i, 0))
    mp = pl.BlockSpec((8, d), lambda i: (0, 0))
    ins, in_specs = [dx, du, x, mp_norm], [row, row, row, mp]
    outs = [jax.ShapeDtypeStruct((t, d), F32)]
    out_specs = [row]
    if has_gate:
        ins += [br, mp_gate]
        in_specs += [row, mp]
        outs.append(jax.ShapeDtypeStruct((t, d), BF16))
        out_specs.append(row)
    outs.append(jax.ShapeDtypeStruct((8, d), F32))
    out_specs.append(mp)
    return pl.pallas_call(
        body, name=name, out_shape=outs, grid=(t // ROW_TILE,), in_specs=in_specs, out_specs=out_specs,
        compiler_params=_params(("arbitrary",)))(*ins)


def _merge_fwd(proj, p_conv, p_ssm, name):
    t, d = p_conv.shape

    def body(gl_ref, pc_ref, ps_ref, o_ref):
        g = _sigmoid(gl_ref[...].astype(F32))
        o_ref[...] = (g[:, :d] * pc_ref[...].astype(F32) + g[:, d:] * ps_ref[...].astype(F32)).astype(BF16)

    row = pl.BlockSpec((ROW_TILE, d), lambda i: (i, 0))
    return pl.pallas_call(
        body, name=name, out_shape=jax.ShapeDtypeStruct((t, d), BF16), grid=(t // ROW_TILE,),
        in_specs=[pl.BlockSpec((ROW_TILE, 2 * d), lambda i: (i, 0)), row, row], out_specs=row,
        compiler_params=_params(("parallel",)))(proj, p_conv, p_ssm)


def _merge_bwd(dmerged, proj, p_conv, p_ssm, name):
    t, d = p_conv.shape

    def body(dm_ref, gl_ref, pc_ref, ps_ref, dpc_ref, dps_ref, dgl_ref):
        g = _sigmoid(gl_ref[...].astype(F32))
        gc, gs = g[:, :d], g[:, d:]
        dm = dm_ref[...].astype(F32)
        dpc_ref[...] = (dm * gc).astype(BF16)
        dps_ref[...] = (dm * gs).astype(BF16)
        dgl_ref[...] = jnp.concatenate(
            [dm * pc_ref[...].astype(F32) * gc * (1.0 - gc), dm * ps_ref[...].astype(F32) * gs * (1.0 - gs)], axis=1).astype(BF16)

    row = pl.BlockSpec((ROW_TILE, d), lambda i: (i, 0))
    wide = pl.BlockSpec((ROW_TILE, 2 * d), lambda i: (i, 0))
    return pl.pallas_call(
        body, name=name,
        out_shape=[jax.ShapeDtypeStruct((t, d), BF16), jax.ShapeDtypeStruct((t, d), BF16),
                   jax.ShapeDtypeStruct((t, 2 * d), BF16)],
        grid=(t // ROW_TILE,), in_specs=[row, wide, row, row], out_specs=[row, row, wide],
        compiler_params=_params(("parallel",)))(dmerged, proj, p_conv, p_ssm)


def _halo_specs(t, width, col):
    nb = t // 8
    step = ROW_TILE // 8
    prev = pl.BlockSpec((8, width), lambda i: (jnp.maximum(i * step - 1, 0), col))
    nxt = pl.BlockSpec((8, width), lambda i: (jnp.minimum((i + 1) * step, nb - 1), col))
    return prev, nxt


def _gconv_fwd(proj, conv_w8, d, name):
    t = proj.shape[0]

    def body(cb_ref, cc_ref, cx_ref, ccp_ref, cxp_ref, w_ref, o_ref):
        first = pl.program_id(0) == 0
        v = cc_ref[...].astype(F32) * cx_ref[...].astype(F32)
        vp = jnp.where(first, 0.0, ccp_ref[...].astype(F32) * cxp_ref[...].astype(F32))
        cv = sum(_rows(w_ref, k) * _shift_down(v, vp, 2 - k) for k in range(3))
        o_ref[...] = (cb_ref[...].astype(F32) * cv).astype(BF16)

    def win(col):
        return pl.BlockSpec((ROW_TILE, d), lambda i: (i, col))

    return pl.pallas_call(
        body, name=name, out_shape=jax.ShapeDtypeStruct((t, d), BF16), grid=(t // ROW_TILE,),
        in_specs=[win(2), win(3), win(4), _halo_specs(t, d, 3)[0], _halo_specs(t, d, 4)[0],
                  pl.BlockSpec((8, d), lambda i: (0, 0))],
        out_specs=pl.BlockSpec((ROW_TILE, d), lambda i: (i, 0)),
        compiler_params=_params(("parallel",)))(proj, proj, proj, proj, proj, conv_w8)


def _gconv_bwd(dy, proj, conv_w8, dgl, d, name):
    t = proj.shape[0]
    nt = t // ROW_TILE

    def body(dy_ref, dyn_ref, cb_ref, cc_ref, cx_ref, ccp_ref, cxp_ref, cbn_ref, w_ref, dgl_ref, dp_ref, dw_ref):
        i = pl.program_id(0)
        cb, cc, cx = cb_ref[...].astype(F32), cc_ref[...].astype(F32), cx_ref[...].astype(F32)
        v = cc * cx
        vp = jnp.where(i == 0, 0.0, ccp_ref[...].astype(F32) * cxp_ref[...].astype(F32))
        sh = [_shift_down(v, vp, 2 - k) for k in range(3)]
        cv = sum(_rows(w_ref, k) * sh[k] for k in range(3))
        dyv = dy_ref[...]
        dcv = dyv * cb
        dcvn = jnp.where(i == nt - 1, 0.0, dyn_ref[...] * cbn_ref[...].astype(F32))
        dv = sum(_rows(w_ref, k) * _shift_up(dcv, dcvn, 2 - k) for k in range(3))
        dp_ref[:, :2 * d] = dgl_ref[...]
        dp_ref[:, 2 * d:3 * d] = (dyv * cv).astype(BF16)
        dp_ref[:, 3 * d:4 * d] = (dv * cx).astype(BF16)
        dp_ref[:, 4 * d:] = (dv * cc).astype(BF16)
        upd = jnp.concatenate([jnp.sum(dcv * sh[k], axis=0, keepdims=True) for k in range(3)]
                              + [jnp.zeros((5, d), F32)], axis=0)

        @pl.when(i == 0)
        def _():
            dw_ref[...] = upd

        @pl.when(i > 0)
        def _():
            dw_ref[...] += upd

    def win(col):
        return pl.BlockSpec((ROW_TILE, d), lambda i: (i, col))

    row = pl.BlockSpec((ROW_TILE, d), lambda i: (i, 0))
    w8 = pl.BlockSpec((8, d), lambda i: (0, 0))
    return pl.pallas_call(
        body, name=name,
        out_shape=[jax.ShapeDtypeStruct(proj.shape, BF16), jax.ShapeDtypeStruct((8, d), F32)],
        grid=(nt,),
        in_specs=[row, _halo_specs(t, d, 0)[1], win(2), win(3), win(4), _halo_specs(t, d, 3)[0],
                  _halo_specs(t, d, 4)[0], _halo_specs(t, d, 2)[1], w8,
                  pl.BlockSpec((ROW_TILE, 2 * d), lambda i: (i, 0))],
        out_specs=[pl.BlockSpec((ROW_TILE, 5 * d), lambda i: (i, 0)), w8],
        compiler_params=_params(("arbitrary",)))(
            dy, dy, proj, proj, proj, proj, proj, proj, conv_w8, dgl)


def _dsilu(p):
    s = _sigmoid(p)
    return s * (1.0 + p * (1.0 - s))


def _sconv_specs(t, d, col0, nt):
    nb, step = t // 8, ROW_TILE // 8
    win = pl.BlockSpec((ROW_TILE, d), lambda j, i: (i, col0 + j))
    prev = pl.BlockSpec((8, d), lambda j, i: (jnp.maximum(i * step - 1, 0), col0 + j))
    nxt = pl.BlockSpec((8, d), lambda j, i: (jnp.minimum((i + 1) * step, nb - 1), col0 + j))
    return win, prev, nxt


def _sconv_fwd(proj, w8, b8, d, col0, ncol, name):
    t = proj.shape[0]
    nt = t // ROW_TILE

    def body(x_ref, xp_ref, w_ref, b_ref, o_ref, pre_ref):
        xv = x_ref[...].astype(F32)
        xp = jnp.where(pl.program_id(1) == 0, 0.0, xp_ref[...].astype(F32))
        pre = _rows(b_ref, 0) + sum(_rows(w_ref, k) * _shift_down(xv, xp, 3 - k) for k in range(4))
        o_ref[...] = (pre * _sigmoid(pre)).astype(BF16)
        pre_ref[...] = pre.astype(BF16)

    win, prev, _ = _sconv_specs(t, d, col0, nt)
    par = pl.BlockSpec((8, d), lambda j, i: (0, j))
    out = pl.BlockSpec((ROW_TILE, d), lambda j, i: (i, j))
    return pl.pallas_call(
        body, name=name, out_shape=[jax.ShapeDtypeStruct((t, ncol * d), BF16)] * 2, grid=(ncol, nt),
        in_specs=[win, prev, par, par], out_specs=[out, out],
        compiler_params=_params(("parallel", "parallel")))(proj, proj, w8, b8)


def _sconv_bwd(dxc, pre, pre_col0, proj, w8, dproj, d, col0, ncol, name):
    t = proj.shape[0]
    nt = t // ROW_TILE

    def body(d_ref, dn_ref, p_ref, pn_ref, x_ref, w_ref, _, dp_ref, dw_ref):
        i = pl.program_id(1)
        dpre = d_ref[...] * _dsilu(p_ref[...].astype(F32))
        dpre_n = jnp.where(i == nt - 1, 0.0, dn_ref[...] * _dsilu(pn_ref[...].astype(F32)))
        up = [_shift_up(dpre, dpre_n, 3 - k) for k in range(4)]
        dp_ref[...] = sum(_rows(w_ref, k) * up[k] for k in range(4)).astype(BF16)
        xv = x_ref[...].astype(F32)
        upd = jnp.concatenate([jnp.sum(xv * up[k], axis=0, keepdims=True) for k in range(4)]
                              + [jnp.sum(dpre, axis=0, keepdims=True), jnp.zeros((3, d), F32)], axis=0)

        @pl.when(i == 0)
        def _():
            dw_ref[...] = upd

        @pl.when(i > 0)
        def _():
            dw_ref[...] += upd

    win, _, _ = _sconv_specs(t, d, col0, nt)
    nb, step = t // 8, ROW_TILE // 8
    dwin = pl.BlockSpec((ROW_TILE, d), lambda j, i: (i, j))
    dnxt = pl.BlockSpec((8, d), lambda j, i: (jnp.minimum((i + 1) * step, nb - 1), j))
    par = pl.BlockSpec((8, d), lambda j, i: (0, j))
    return pl.pallas_call(
        body, name=name,
        out_shape=[jax.ShapeDtypeStruct(dproj.shape, BF16), jax.ShapeDtypeStruct((8, ncol * d), F32)],
        grid=(ncol, nt),
        in_specs=[dwin, dnxt,
                  pl.BlockSpec((ROW_TILE, d), lambda j, i: (i, pre_col0 + j)),
                  pl.BlockSpec((8, d), lambda j, i: (jnp.minimum((i + 1) * step, nb - 1), pre_col0 + j)),
                  win, par, pl.BlockSpec(memory_space=pl.ANY)],
        out_specs=[win, par], input_output_aliases={6: 0},
        compiler_params=_params(("arbitrary", "arbitrary")))(dxc, dxc, pre, pre, proj, w8, dproj)


def _softplus(x):
    return jnp.maximum(x, 0.0) + jnp.log(1.0 + jnp.exp(-jnp.abs(x)))


def _dt_fwd(proj, par8, nheads, dt_col, name):
    t = proj.shape[0]

    def body(r_ref, p_ref, dt_ref, ac_ref):
        lane = lax.broadcasted_iota(jnp.int32, (1, LANE), 1)
        valid = lane < nheads
        raw = r_ref[...][:, :LANE].astype(F32)
        dt = jnp.where(valid, _softplus(raw + _rows(p_ref, 0)), 0.0)
        a = jnp.where(valid, -jnp.exp(_rows(p_ref, 1)), 0.0)
        dt_ref[...] = dt
        ac_ref[...] = _xdot01(_chunk_tri(ROW_TILE, True), dt * a)

    out = pl.BlockSpec((ROW_TILE, LANE), lambda i: (i, 0))
    return pl.pallas_call(
        body, name=name,
        out_shape=[jax.ShapeDtypeStruct((t, LANE), F32), jax.ShapeDtypeStruct((t, LANE), F32)],
        grid=(t // ROW_TILE,),
        in_specs=[pl.BlockSpec((ROW_TILE, 2 * LANE), lambda i: (i, dt_col)), pl.BlockSpec((8, LANE), lambda i: (0, 0))],
        out_specs=[out, out], compiler_params=_params(("parallel",)))(proj, par8)


def _dt_bwd(ddt_g, dac_g, dt, proj, par8, dproj, nheads, dt_col, name):
    t = proj.shape[0]
    ng = ddt_g.shape[0]

    def body(dd_ref, da_ref, dt_ref, r_ref, p_ref, _, dp_ref, acc_ref):
        lane = lax.broadcasted_iota(jnp.int32, (1, LANE), 1)
        valid = lane < nheads
        ddt = sum(dd_ref[g] for g in range(ng))
        dac = sum(da_ref[g] for g in range(ng))
        a = jnp.where(valid, -jnp.exp(_rows(p_ref, 1)), 0.0)
        d_a = _xdot01(_chunk_tri(ROW_TILE, False), dac)
        ddt = ddt + d_a * a
        raw = r_ref[...][:, :LANE].astype(F32)
        draw = jnp.where(valid, ddt * _sigmoid(raw + _rows(p_ref, 0)), 0.0)
        dp_ref[...] = jnp.concatenate([draw, jnp.zeros_like(draw)], axis=1).astype(BF16)
        upd = jnp.concatenate([jnp.sum(draw, axis=0, keepdims=True),
                               jnp.sum(d_a * dt_ref[...], axis=0, keepdims=True) * a,
                               jnp.zeros((6, LANE), F32)], axis=0)

        @pl.when(pl.program_id(0) == 0)
        def _():
            acc_ref[...] = upd

        @pl.when(pl.program_id(0) > 0)
        def _():
            acc_ref[...] += upd

    row = pl.BlockSpec((ROW_TILE, LANE), lambda i: (i, 0))
    grp = pl.BlockSpec((ng, ROW_TILE, LANE), lambda i: (0, i, 0))
    win = pl.BlockSpec((ROW_TILE, 2 * LANE), lambda i: (i, dt_col))
    par = pl.BlockSpec((8, LANE), lambda i: (0, 0))
    return pl.pallas_call(
        body, name=name,
        out_shape=[jax.ShapeDtypeStruct(dproj.shape, BF16), jax.ShapeDtypeStruct((8, LANE), F32)],
        grid=(t // ROW_TILE,),
        in_specs=[grp, grp, row, win, par, pl.BlockSpec(memory_space=pl.ANY)],
        out_specs=[win, par], input_output_aliases={5: 0},
        compiler_params=_params(("arbitrary",)))(ddt_g, dac_g, dt, proj, par8, dproj)


def _ssd_chunk(xs, bm, cm, z, dt, ac, st, dsk, nw, g):
    q = xs.shape[0]
    lane_h = lax.broadcasted_iota(jnp.int32, (1, LANE), 1)
    sub_h = lax.broadcasted_iota(jnp.int32, (LANE, 1), 0)
    lane_c = lax.broadcasted_iota(jnp.int32, (1, GROUP_W), 1) // HEAD_DIM
    rr = lax.broadcasted_iota(jnp.int32, (q, q), 0)
    cc = lax.broadcasted_iota(jnp.int32, (q, q), 1)
    tril = rr >= cc
    last_row = jnp.where(lax.broadcasted_iota(jnp.int32, (q, 1), 0) == q - 1, 1.0, 0.0)
    act = ac.T
    col, row = [], []
    for r in range(HEADS_PER_GROUP):
        h = g * HEADS_PER_GROUP + r
        col.append(jnp.sum(ac * jnp.where(lane_h == h, 1.0, 0.0), axis=1, keepdims=True))
        row.append(jnp.sum(act * jnp.where(sub_h == h, 1.0, 0.0), axis=0, keepdims=True))

    def per_head(vals):
        out = jnp.broadcast_to(vals[0], (q, GROUP_W))
        for r in range(1, HEADS_PER_GROUP):
            out = jnp.where(lane_c == r, vals[r], out)
        return out

    spread = jnp.where(lax.broadcasted_iota(jnp.int32, (LANE, GROUP_W), 0)
                       == g * HEADS_PER_GROUP + lax.broadcasted_iota(jnp.int32, (LANE, GROUP_W), 1) // HEAD_DIM,
                       1.0, 0.0).astype(BF16)
    acx = _spread(ac, spread)
    x = xs * _spread(dt, spread)
    s = _bdot(cm, bm, "nt")
    y = per_head([_bdot(s * jnp.exp(jnp.where(tril, col[r] - row[r], -jnp.inf)), x, "nn")
                  for r in range(HEADS_PER_GROUP)])
    y = y + jnp.exp(acx) * _bdot(cm, st, "nn")
    last = jnp.sum(acx * last_row, axis=0, keepdims=True)
    new_st = st * jnp.exp(last) + _bdot(bm, x * jnp.exp(last - acx), "tn")
    y = y + dsk * xs
    y = y * (z * _sigmoid(z))
    yn = y * lax.rsqrt(jnp.mean(y * y, axis=1, keepdims=True) + EPS) * nw
    return yn, new_st


def _ssd_specs(nc, z_col, ngroups, rev):
    assert ngroups % SSD_GROUPS == 0 and z_col % SSD_GROUPS == 0
    z_col //= SSD_GROUPS
    b_col, c_col = 2 * ngroups // SSD_GROUPS, 3 * ngroups // SSD_GROUPS

    def ci(c):
        return nc - 1 - c if rev else c

    nx, nb = SSD_GROUPS * GROUP_W, SSD_GROUPS * N_STATE
    xs = pl.BlockSpec((SSD_Q, nx), lambda g, c: (ci(c), g))
    bm = pl.BlockSpec((SSD_Q, nb), lambda g, c: (ci(c), b_col + g))
    cm = pl.BlockSpec((SSD_Q, nb), lambda g, c: (ci(c), c_col + g))
    z = pl.BlockSpec((SSD_Q, nx), lambda g, c: (ci(c), z_col + g))
    hd = pl.BlockSpec((SSD_Q, LANE), lambda g, c: (ci(c), 0))
    par = pl.BlockSpec((8, nx), lambda g, c: (0, g))
    stb = pl.BlockSpec((None, N_STATE, nx), lambda g, c: (ci(c), 0, g))
    db = pl.BlockSpec((SSD_Q, nb), lambda g, c: (ci(c), g))
    return xs, bm, cm, z, hd, par, stb, db


def _ssd_fwd(xc, proj, dt, ac, par8, ngroups, z_col, name):
    t = xc.shape[0]
    nc = t // SSD_Q
    steps = ngroups // SSD_GROUPS

    def body(x_ref, b_ref, c_ref, z_ref, dt_ref, ac_ref, p_ref, y_ref, so_ref, st_ref):
        @pl.when(pl.program_id(1) == 0)
        def _():
            st_ref[...] = jnp.zeros_like(st_ref)

        so_ref[...] = st_ref[...]
        dtv, acv = dt_ref[...], ac_ref[...]
        for i in range(SSD_GROUPS):
            cx, cb = slice(i * GROUP_W, (i + 1) * GROUP_W), slice(i * N_STATE, (i + 1) * N_STATE)
            yn, new_st = _ssd_chunk(x_ref[:, cx].astype(F32), b_ref[:, cb].astype(F32), c_ref[:, cb].astype(F32),
                                    z_ref[:, cx].astype(F32), dtv, acv, st_ref[:, cx],
                                    p_ref[0:1, cx], p_ref[1:2, cx], pl.program_id(0) * SSD_GROUPS + i)
            y_ref[:, cx] = yn.astype(BF16)
            st_ref[:, cx] = new_st

    xs, bm, cm, zs, hd, par, stb, _ = _ssd_specs(nc, z_col, ngroups, False)
    return pl.pallas_call(
        body, name=name,
        out_shape=[jax.ShapeDtypeStruct((t, ngroups * GROUP_W), BF16),
                   jax.ShapeDtypeStruct((nc, N_STATE, ngroups * GROUP_W), F32)],
        grid=(steps, nc), in_specs=[xs, bm, cm, zs, hd, hd, par], out_specs=[xs, stb],
        scratch_shapes=[pltpu.VMEM((N_STATE, SSD_GROUPS * GROUP_W), F32)],
        compiler_params=_params(("parallel", "arbitrary")))(xc, xc, xc, proj, dt, ac, par8)


def _ssd_bwd(dyn, xc, proj, dt, ac, par8, states, dproj, ngroups, z_col, name, comm=None):
    t = xc.shape[0]
    nc = t // SSD_Q
    steps = ngroups // SSD_GROUPS

    n_cin = comm.n_in if comm else 0
    n_cout = comm.n_out if comm else 0

    def body(*refs):
        dy_ref, x_ref, b_ref, c_ref, z_ref, dt_ref, ac_ref, p_ref, s_ref = refs[:9]
        dz_ref, dx_ref, db_ref, dc_ref, ddt_ref, dac_ref, dp_ref = refs[10 + n_cin:17 + n_cin]
        ds_ref = refs[17 + n_cin + n_cout]
        c = pl.program_id(1)
        if comm:
            begin, end = comm.ops(refs[10:10 + n_cin], refs[17 + n_cin:17 + n_cin + n_cout], refs[-3:])
            pl.when((pl.program_id(0) == 0) & (c == 0))(begin)

        @pl.when(c == 0)
        def _():
            ds_ref[...] = jnp.zeros_like(ds_ref)

        dtv, acv = dt_ref[...], ac_ref[...]
        ddt_sum, dac_sum, upds = None, None, []
        for i in range(SSD_GROUPS):
            cx, cb = slice(i * GROUP_W, (i + 1) * GROUP_W), slice(i * N_STATE, (i + 1) * N_STATE)
            g = pl.program_id(0) * SSD_GROUPS + i
            args = (x_ref[:, cx].astype(F32), b_ref[:, cb].astype(F32), c_ref[:, cb].astype(F32),
                    z_ref[:, cx].astype(F32), dtv, acv, s_ref[:, cx], p_ref[0:1, cx], p_ref[1:2, cx])
            _, vjp = jax.vjp(lambda *a, g=g: _ssd_chunk(*a, g), *args)
            dxs, dbm, dcm, dz, ddt, dac, dst, ddsk, dnw = vjp((dy_ref[:, cx], ds_ref[:, cx]))
            ds_ref[:, cx] = dst
            dz_ref[:, cx] = dz.astype(BF16)
            dx_ref[:, cx] = dxs
            db_ref[:, cb] = dbm
            dc_ref[:, cb] = dcm
            ddt_sum = ddt if ddt_sum is None else ddt_sum + ddt
            dac_sum = dac if dac_sum is None else dac_sum + dac
            upds.append(jnp.concatenate([ddsk, dnw, jnp.zeros((6, GROUP_W), F32)], axis=0))
        ddt_ref[...] = ddt_sum
        dac_ref[...] = dac_sum
        upd = jnp.concatenate(upds, axis=1)

        @pl.when(c == 0)
        def _():
            dp_ref[...] = upd

        @pl.when(c > 0)
        def _():
            dp_ref[...] += upd

        if comm:
            pl.when((pl.program_id(0) == steps - 1) & (c == nc - 1))(end)

    xs, bm, cm, zs, hd, par, stb, db = _ssd_specs(nc, z_col, ngroups, True)
    hg = pl.BlockSpec((None, SSD_Q, LANE), lambda g, c: (g, nc - 1 - c, 0))
    any_spec = pl.BlockSpec(memory_space=pl.ANY)
    aliases = {9: 0}
    if comm:
        aliases.update({10 + i: 7 + o for i, o in comm.aliases.items()})
    res = pl.pallas_call(
        body, name=name,
        out_shape=[jax.ShapeDtypeStruct(dproj.shape, BF16),
                   jax.ShapeDtypeStruct((t, ngroups * GROUP_W), F32),
                   jax.ShapeDtypeStruct((t, ngroups * N_STATE), F32), jax.ShapeDtypeStruct((t, ngroups * N_STATE), F32),
                   jax.ShapeDtypeStruct((steps, t, LANE), F32), jax.ShapeDtypeStruct((steps, t, LANE), F32),
                   jax.ShapeDtypeStruct((8, ngroups * GROUP_W), F32)] + (comm.out_shape if comm else []),
        grid=(steps, nc),
        in_specs=[xs, xs, bm, cm, zs, hd, hd, par, stb, any_spec] + [any_spec] * n_cin,
        out_specs=[zs, xs, db, db, hg, hg, par] + [any_spec] * n_cout, input_output_aliases=aliases,
        scratch_shapes=[pltpu.VMEM((N_STATE, SSD_GROUPS * GROUP_W), F32)] + (comm.sems if comm else []),
        compiler_params=_params(("arbitrary", "arbitrary")))(
            dyn, xc, xc, xc, proj, dt, ac, par8, states, dproj, *(comm.operands if comm else []))
    return tuple(res[:7]), list(res[7:])


def _small_matmul(a, b, mode, name):
    if mode == "nn":
        shape = (a.shape[0], b.shape[1])
    else:
        shape = (a.shape[1], b.shape[1])

    def body(a_ref, b_ref, o_ref):
        o_ref[...] = _dot(a_ref[...], b_ref[...], mode)

    return pl.pallas_call(body, name=name, out_shape=jax.ShapeDtypeStruct(shape, F32),
                          compiler_params=_params())(a, b)


def _adamw(parts, w, m, v, name, shard_stride=None):
    npart, rows, pcols = parts.shape
    cols = w.shape[1]
    tr = rows if rows <= 128 else _tile8(rows, 128)

    def body(p_ref, w_ref, m_ref, v_ref, g_ref, d_ref, nm_ref, nv_ref):
        g = p_ref[0].astype(F32)
        for j in range(1, npart):
            g = g + p_ref[j].astype(F32)
        if shard_stride is not None:
            me = 4 * lax.axis_index("x") + 2 * lax.axis_index("y") + lax.axis_index("c")
            off = (shard_stride * me) % LANE
            g = pltpu.roll(g, (pcols - off) % pcols, 1)[:, :cols]
        mm = ADAM_B1 * m_ref[...] + (1.0 - ADAM_B1) * g
        vv = ADAM_B2 * v_ref[...] + (1.0 - ADAM_B2) * (g * g)
        m_hat = mm / (1.0 - ADAM_B1 ** ADAM_STEP)
        v_hat = vv / (1.0 - ADAM_B2 ** ADAM_STEP)
        g_ref[...] = g
        d_ref[...] = -ADAM_LR * (m_hat / (jnp.sqrt(v_hat) + ADAM_EPS) + ADAM_WD * w_ref[...])
        nm_ref[...] = mm
        nv_ref[...] = vv

    blk = pl.BlockSpec((tr, cols), lambda i: (i, 0))
    out = jax.ShapeDtypeStruct((rows, cols), F32)
    return pl.pallas_call(
        body, name=name, out_shape=[out] * 4, grid=(rows // tr,),
        in_specs=[pl.BlockSpec((npart, tr, pcols), lambda i: (0, i, 0)), blk, blk, blk], out_specs=[blk] * 4,
        compiler_params=_params(("parallel",)))(parts, w, m, v)


def _tile8(n, target):
    best = 8
    for d in range(8, target + 1, 8):
        if n % d == 0:
            best = d
    return best


def _pad_rows8(a):
    return jnp.concatenate([a, jnp.zeros((8 - a.shape[0], a.shape[1]), a.dtype)], axis=0)


def _pad_lanes(a, n):
    return jnp.concatenate([a, jnp.zeros(a.shape[:-1] + (n - a.shape[-1],), a.dtype)], axis=-1)


def _pack(vecs, mult):
    flat = jnp.concatenate([v.reshape(-1) for v in vecs])
    pad = (-flat.shape[0]) % mult
    flat = jnp.concatenate([flat, jnp.zeros((pad,), flat.dtype)])
    return flat.reshape(-1, LANE)


def _unpack(flat, shapes):
    out, o = [], 0
    for s in shapes:
        n = int(np.prod(s))
        out.append(flat[o:o + n].reshape(s))
        o += n
    return out


def kernel(x, c, w_ada, b_ada, ln1, ln2, w_in, conv_w, ssm_conv_w, ssm_conv_b, dt_bias, a_log, d_skip, ssm_norm_w, w_conv_out, w_ssm_out, w_o, w_up, w_down, final_norm, loss_target, m_w_ada, m_b_ada, m_ln1, m_ln2, m_w_in, m_conv_w, m_ssm_conv_w, m_ssm_conv_b, m_dt_bias, m_a_log, m_d_skip, m_ssm_norm_w, m_w_conv_out, m_w_ssm_out, m_w_o, m_w_up, m_w_down, m_final_norm, v_w_ada, v_b_ada, v_ln1, v_ln2, v_w_in, v_conv_w, v_ssm_conv_w, v_ssm_conv_b, v_dt_bias, v_a_log, v_d_skip, v_ssm_norm_w, v_w_conv_out, v_w_ssm_out, v_w_o, v_w_up, v_w_down, v_final_norm):
    names = ["w_ada", "b_ada", "ln1", "ln2", "w_in", "conv_w", "ssm_conv_w", "ssm_conv_b", "dt_bias", "a_log",
             "d_skip", "ssm_norm_w", "w_conv_out", "w_ssm_out", "w_o", "w_up", "w_down", "final_norm"]
    w_of = dict(zip(names, [w_ada, b_ada, ln1, ln2, w_in, conv_w, ssm_conv_w, ssm_conv_b, dt_bias, a_log, d_skip,
                            ssm_norm_w, w_conv_out, w_ssm_out, w_o, w_up, w_down, final_norm]))
    m_of = dict(zip(names, [m_w_ada, m_b_ada, m_ln1, m_ln2, m_w_in, m_conv_w, m_ssm_conv_w, m_ssm_conv_b, m_dt_bias,
                            m_a_log, m_d_skip, m_ssm_norm_w, m_w_conv_out, m_w_ssm_out, m_w_o, m_w_up, m_w_down,
                            m_final_norm]))
    v_of = dict(zip(names, [v_w_ada, v_b_ada, v_ln1, v_ln2, v_w_in, v_conv_w, v_ssm_conv_w, v_ssm_conv_b, v_dt_bias,
                            v_a_log, v_d_skip, v_ssm_norm_w, v_w_conv_out, v_w_ssm_out, v_w_o, v_w_up, v_w_down,
                            v_final_norm]))

    _, t, d = x.shape
    nl = w_ada.shape[0]
    ada_w = w_ada.shape[2]
    ds = ssm_norm_w.shape[1]
    nh = dt_bias.shape[1]
    ng = nh // HEADS_PER_GROUP
    gn = ng * N_STATE
    xbc_w = ds + 2 * gn
    z_off, xbc_off = 5 * d, 5 * d + ds
    dt_off = xbc_off + xbc_w
    proj_w = dt_off + nh
    pw = dt_off + 2 * LANE
    me = 4 * lax.axis_index("x") + 2 * lax.axis_index("y") + lax.axis_index("c")

    x2, tgt = x[0], loss_target[0]

    c_act = c * jax.nn.sigmoid(c)
    sizes1 = [(1, d), conv_w.shape, ssm_conv_w.shape]
    (g1,) = _all_gather([_pack([c_act, conv_w, ssm_conv_w], LANE)], "ag_small_in")
    parts1 = [_unpack(g1[j].reshape(-1), sizes1) for j in range(N_DEV)]
    c_act_all = jnp.concatenate([p[0] for p in parts1], axis=0)
    conv_w_full = jnp.concatenate([p[1] for p in parts1], axis=-1)
    sconv_w_full = jnp.concatenate([p[2] for p in parts1], axis=-1)

    mod_part = jnp.stack([_small_matmul(c_act_all, w_ada[l], "nn", f"ada_fwd{l}") for l in range(nl)])
    (gmod,) = _all_gather([mod_part], "ag_mod")
    mod = lax.dynamic_index_in_dim(gmod, me, axis=2, keepdims=False)
    mod = jnp.moveaxis(mod, 0, 1).reshape(nl, N_DEV * ada_w) + b_ada
    mod = mod.reshape(nl, 6, d)
    modp = [jnp.concatenate([mod[l], ln1[l][None], ln2[l][None]], axis=0) for l in range(nl)]

    big = ["w_in", "w_up", "w_conv_out", "w_ssm_out", "w_o", "w_down"]
    rest = big[1:]

    def blocks_of(l, keys):
        return [w_of[k][l].astype(BF16) for k in keys]

    def w_in_halves(l):
        w = w_of["w_in"][l].astype(BF16)
        return [w[:d // 2]], [w[d // 2:]]

    def full_cols(g):
        return jnp.moveaxis(g, 0, 1).reshape(g.shape[1], N_DEV * g.shape[2])

    def full_rows(g):
        return g.reshape(N_DEV * g.shape[1], g.shape[2])

    def pad_w_in(pieces):
        w = jnp.concatenate([full_cols(g) for g in pieces], axis=0)
        return jnp.concatenate([w, jnp.zeros((d, pw - proj_w), BF16)], axis=-1)

    wi = [None] * nl
    wi[0] = pad_w_in(_all_gather(blocks_of(0, ["w_in"]), "ag_w_in0"))

    cw8 = [_pad_rows8(conv_w_full[l]) for l in range(nl)]
    sw8 = [_pad_rows8(sconv_w_full[l]) for l in range(nl)]
    sb8 = [_pad_rows8(ssm_conv_b[l][None]) for l in range(nl)]
    dtp8 = [_pad_rows8(_pad_lanes(jnp.stack([dt_bias[l], a_log[l]]), LANE)) for l in range(nl)]
    sp8 = [_pad_rows8(jnp.stack([jnp.repeat(d_skip[l], HEAD_DIM), ssm_norm_w[l]])) for l in range(nl)]
    fn8 = _pad_rows8(final_norm[None])
    xbc_col, ncol, dt_col, z_col = xbc_off // d, xbc_w // d, dt_off // (2 * LANE), z_off // GROUP_W

    saved = []
    x_cur, br_prev = x2, None
    for l in range(nl):
        if l == 0:
            x_in, u1 = x_cur, _norm_fwd(x_cur, modp[0], 0, "norm_first")
        else:
            x_in, u1 = _resid_norm_fwd(x_cur, br_prev, modp[l - 1], 5, modp[l], 0, f"resid_norm_a{l}")
        proj, got = _matmul(u1, wi[l], "nn", BF16, f"mm_in{l}",
                            comm=_GatherComm(blocks_of(l, rest), as_columns=[k == "w_up" for k in rest]))
        got = dict(zip(rest, got))
        wup, wdown = got["w_up"], full_rows(got["w_down"])
        wco, wso, wo = full_rows(got["w_conv_out"]), full_rows(got["w_ssm_out"]), full_rows(got["w_o"])
        y_conv = _gconv_fwd(proj, cw8[l], d, f"gconv_fwd{l}")
        xc, pre = _sconv_fwd(proj, sw8[l], sb8[l], d, xbc_col, ncol, f"sconv_fwd{l}")
        dt, ac = _dt_fwd(proj, dtp8[l], nh, dt_col, f"dt_fwd{l}")
        yn, states = _ssd_fwd(xc, proj, dt, ac, sp8[l], ng, z_col, f"ssd_fwd{l}")
        p_conv = _matmul(y_conv, wco, "nn", BF16, f"mm_conv_out{l}")
        p_ssm = _matmul(yn, wso, "nn", BF16, f"mm_ssm_out{l}")
        merged = _merge_fwd(proj, p_conv, p_ssm, f"merge_fwd{l}")
        mix = _matmul(merged, wo, "nn", BF16, f"mm_o{l}")
        x_mid, u2 = _resid_norm_fwd(x_in, mix, modp[l], 2, modp[l], 1, f"resid_norm_b{l}")
        if l + 1 < nl:
            top, bot = w_in_halves(l + 1)
            (h, hid), g_top = _matmul(u2, wup, "nn", BF16, f"mm_up{l}", epi="relu2", comm=_GatherComm(top))
            mlp, g_bot = _matmul(hid, wdown, "nn", BF16, f"mm_down{l}", comm=_GatherComm(bot))
            wi[l + 1] = pad_w_in(g_top + g_bot)
        else:
            h, hid = _matmul(u2, wup, "nn", BF16, f"mm_up{l}", epi="relu2")
            mlp = _matmul(hid, wdown, "nn", BF16, f"mm_down{l}")
        saved.append(dict(x_in=x_in, u1=u1, proj=proj, y_conv=y_conv, xc=xc, pre=pre, dt=dt, ac=ac, yn=yn,
                          states=states,
                          p_conv=p_conv, p_ssm=p_ssm, merged=merged, mix=mix, x_mid=x_mid, u2=u2, h=h, hid=hid,
                          mlp=mlp, wup=wup, wdown=wdown, wco=wco, wso=wso, wo=wo))
        x_cur, br_prev = x_mid, mlp

    dx, dbr, acc = _final_fwd_bwd(x_cur, br_prev, modp[nl - 1], fn8, tgt, "final")
    loss = lax.psum(acc[0, 0], ("x", "y", "c"))
    g_final_norm = acc[1]
    dgate2 = acc[2]

    gw = {k: [None] * nl for k in big}
    small = {k: [None] * nl for k in ["mod", "ln1", "ln2", "conv_w", "ssm_conv_w", "ssm_conv_b", "dt_bias",
                                      "a_log", "d_skip", "ssm_norm_w"]}

    def grad_rows(a):
        return a.reshape(N_DEV, a.shape[0] // N_DEV, a.shape[1])

    def recv_buf(name):
        return lax.empty((N_DEV, nl) + w_of[name].shape[1:], BF16)

    in_shard, up_shard = w_in.shape[2], w_up.shape[2]
    in_win = -(-(in_shard + max((in_shard * j) % LANE for j in range(N_DEV))) // LANE) * LANE
    assert up_shard % LANE == 0 and (in_shard * (N_DEV - 1)) // LANE * LANE + in_win <= pw
    win_a, win_b = [(in_win, in_shard)], [(up_shard, up_shard)] + [None] * 4
    bufs_a = [lax.empty((N_DEV, nl, d, in_win), BF16)]
    bufs_b = [recv_buf(k) for k in big[1:]]
    pending = None
    for l in reversed(range(nl)):
        s = saved[l]
        dh = _matmul(dbr, s["wdown"], "nt", BF16, f"mm_down_dx{l}", epi="relu2_bwd", extra=s["h"])
        gw["w_down"][l] = _matmul(s["hid"], dbr, "tn", BF16, f"mm_down_dw{l}")
        du2 = _matmul(dh, s["wup"], "nt", BF16, f"mm_up_dx{l}")
        gw["w_up"][l] = _matmul(s["u2"], dh, "tn", BF16, f"mm_up_dw{l}")
        dx_mid, dmix, acc2 = _resid_norm_bwd(dx, du2, s["x_mid"], modp[l], 1, f"resid_norm_b_bwd{l}",
                                             br=s["mix"], mp_gate=modp[l], gate_row=2)
        dmerged = _matmul(dmix, s["wo"], "nt", BF16, f"mm_o_dx{l}")
        gw["w_o"][l] = _matmul(s["merged"], dmix, "tn", BF16, f"mm_o_dw{l}")
        dpc, dps, dgl = _merge_bwd(dmerged, s["proj"], s["p_conv"], s["p_ssm"], f"merge_bwd{l}")
        dyc = _matmul(dpc, s["wco"], "nt", F32, f"mm_conv_out_dx{l}")
        dproj, dcw = _gconv_bwd(dyc, s["proj"], cw8[l], dgl, d, f"gconv_bwd{l}")
        gw["w_conv_out"][l] = _matmul(s["y_conv"], dpc, "tn", BF16, f"mm_conv_out_dw{l}")
        dyn = _matmul(dps, s["wso"], "nt", F32, f"mm_ssm_out_dx{l}")
        gw["w_ssm_out"][l] = _matmul(s["yn"], dps, "tn", BF16, f"mm_ssm_out_dw{l}")
        rest_grads = [gw["w_up"][l]] + [grad_rows(gw[k][l]) for k in big[2:]]
        sends, bufs, layers, wins = rest_grads, list(bufs_b), [l] * len(rest_grads), win_b
        if pending is not None:
            sends, bufs, layers, wins = sends + pending, bufs + list(bufs_a), layers + [l + 1], win_b + win_a
        (dproj, dxs, dbm, dcm, ddt_g, dac_g, dsp), bufs = _ssd_bwd(
            dyn, s["xc"], s["proj"], s["dt"], s["ac"], sp8[l], s["states"], dproj, ng, z_col, f"ssd_bwd{l}",
            comm=_ExchangeComm(sends, bufs, layers, windows=wins))
        bufs_b = bufs[:len(rest_grads)]
        if pending is not None:
            bufs_a = bufs[len(rest_grads):]
        dsw = []
        for seg, (dseg, c0) in enumerate([(dxs, 0), (dbm, ds // d), (dcm, ds // d + gn // d)]):
            cols = slice(c0 * d, c0 * d + dseg.shape[1])
            dproj, part = _sconv_bwd(dseg, s["pre"], c0, s["proj"], sw8[l][:, cols], dproj, d, xbc_col + c0,
                                     dseg.shape[1] // d, f"sconv_bwd{l}_{seg}")
            dsw.append(part)
        dsw = jnp.concatenate(dsw, axis=1)
        dproj, ddtp = _dt_bwd(ddt_g, dac_g, s["dt"], s["proj"], dtp8[l], dproj, nh, dt_col, f"dt_bwd{l}")
        du1 = _matmul(dproj, wi[l], "nt", BF16, f"mm_in_dx{l}")

        if l > 0:
            pending = [_matmul(s["u1"], dproj, "tn", BF16, f"mm_in_dw{l}")]
        else:
            g_top = _matmul(s["u1"][:, :d // 2], dproj, "tn", BF16, "mm_in_dw0_top")
            g_bot, bufs_a = _matmul(s["u1"][:, d // 2:], dproj, "tn", BF16, "mm_in_dw0_bot",
                                    comm=_ExchangeComm([g_top], bufs_a, 0, row0=0, windows=win_a))
            pending = [g_bot]
        if l > 0:
            dx, dbr, acc1 = _resid_norm_bwd(dx_mid, du1, s["x_in"], modp[l], 0, f"resid_norm_a_bwd{l}",
                                            br=saved[l - 1]["mlp"], mp_gate=modp[l - 1], gate_row=5)
        else:
            dx, acc1 = _resid_norm_bwd(dx_mid, du1, s["x_in"], modp[0], 0, "norm_first_bwd")
        sc1, sc2 = mod[l, 1], mod[l, 4]
        small["mod"][l] = jnp.stack([acc1[0], acc1[1] * ln1[l], acc2[2], acc2[0], acc2[1] * ln2[l], dgate2])
        small["ln1"][l] = acc1[1] * (1.0 + sc1)
        small["ln2"][l] = acc2[1] * (1.0 + sc2)
        small["conv_w"][l] = dcw[:3]
        small["ssm_conv_w"][l] = dsw[:4]
        small["ssm_conv_b"][l] = dsw[4]
        small["dt_bias"][l] = ddtp[0, :nh]
        small["a_log"][l] = ddtp[1, :nh]
        small["d_skip"][l] = dsp[0].reshape(nh, HEAD_DIM).sum(axis=-1)
        small["ssm_norm_w"][l] = dsp[1]
        if l > 0:
            dgate2 = acc1[2]
    grad_x = dx[None]

    sm = {k: jnp.stack(v) for k, v in small.items()}
    rep_names = ["b_ada", "ln1", "ln2", "ssm_conv_b", "dt_bias", "a_log", "d_skip", "ssm_norm_w", "final_norm"]
    rep_grads = [sm["mod"].reshape(nl, 6 * d), sm["ln1"], sm["ln2"], sm["ssm_conv_b"], sm["dt_bias"], sm["a_log"],
                 sm["d_skip"], sm["ssm_norm_w"], g_final_norm]
    rep_pack = _pack(rep_grads, 8 * LANE)
    conv_pack = _pack([sm["conv_w"], sm["ssm_conv_w"]], 8 * LANE)
    g_rep, g_conv = _all_gather([rep_pack, conv_pack], "ag_small_grads")

    outs = {}

    def run_adamw(name, parts, tag, shard_stride=None):
        w, m, v = w_of[name], m_of[name], v_of[name]
        shp = w.shape
        r2 = (int(np.prod(shp[:-1])), shp[-1])
        res = _adamw(parts.reshape(parts.shape[0], r2[0], parts.shape[-1]), w.reshape(r2), m.reshape(r2),
                     v.reshape(r2), tag, shard_stride)
        outs[name] = [a.reshape(shp) for a in res]

    rep_shapes = [w_of[k].shape for k in rep_names]
    res = _adamw(g_rep, _pack([w_of[k] for k in rep_names], 8 * LANE), _pack([m_of[k] for k in rep_names], 8 * LANE),
                 _pack([v_of[k] for k in rep_names], 8 * LANE), "adamw_replicated")
    for k, vals in zip(rep_names, zip(*[_unpack(a.reshape(-1), rep_shapes) for a in res])):
        outs[k] = list(vals)

    conv_parts = [_unpack(g_conv[j].reshape(-1), [sm["conv_w"].shape, sm["ssm_conv_w"].shape]) for j in range(N_DEV)]
    for idx, name in enumerate(["conv_w", "ssm_conv_w"]):
        wsh = w_of[name].shape[-1]
        full = jnp.stack([p[idx] for p in conv_parts])
        run_adamw(name, lax.dynamic_slice_in_dim(full, me * wsh, wsh, axis=3), "adamw_" + name)

    dmod_all = g_rep.reshape(N_DEV, -1)[:, :nl * 6 * d].reshape(N_DEV, nl, 6 * d)
    dmod_mine = lax.dynamic_slice_in_dim(dmod_all, me * ada_w, ada_w, axis=2)
    g_ada = jnp.stack([_small_matmul(c_act_all, dmod_mine[:, l], "tn", f"ada_bwd{l}") for l in range(nl)])
    run_adamw("w_ada", g_ada[None], "adamw_w_ada")

    bufs_a = _exchange(pending, bufs_a, 0, "a2a_w_in0_bot", row0=d // 2, windows=win_a)
    run_adamw("w_in", bufs_a[0], "adamw_w_in", shard_stride=in_shard)
    for name, parts in zip(big[1:], bufs_b):
        run_adamw(name, parts, "adamw_" + name)

    result = [loss, grad_x]
    for i in range(4):
        result += [outs[k][i] for k in names]
    return tuple(result)
```

```python
import functools

import numpy as np
import jax
import jax.numpy as jnp
from jax import lax
from jax.experimental import pallas as pl
from jax.experimental.pallas import tpu as pltpu

F32 = jnp.float32
BF16 = jnp.bfloat16
EPS = 1e-6
N_STATE = 128
HEAD_DIM = 64
HEADS_PER_GROUP = 4
GROUP_W = HEAD_DIM * HEADS_PER_GROUP
SSD_GROUPS = 2
SSD_Q = 256
N_DEV = 8
ROW_TILE = 512
LANE = 128
VMEM_LIMIT = 56 * 1024 * 1024
MATMUL_VMEM_BUDGET = 40 * 1024 * 1024
MATMUL_TILE_CAP = 2048
MXU_FLOPS = 9.0e14
HBM_BYTES_PER_S = 3.0e12
STEP_OVERHEAD_S = 0.35e-6

ADAM_LR, ADAM_B1, ADAM_B2, ADAM_EPS, ADAM_WD, ADAM_STEP = 0.001, 0.9, 0.999, 1e-08, 0.01, 10

MESH = pl.DeviceIdType.MESH


def _params(sem=None):
    return pltpu.CompilerParams(dimension_semantics=sem, vmem_limit_bytes=VMEM_LIMIT)


def _divisors(n, cap):
    if n <= cap:
        return [n]
    return [d for d in range(cap - cap % LANE, 0, -LANE) if n % d == 0]


def _matmul_tiles(m, n, k, out_bytes, n_out, n_extra):
    best = None
    for tm in _divisors(m, MATMUL_TILE_CAP):
        for tn in _divisors(n, MATMUL_TILE_CAP):
            for tk in _divisors(k, MATMUL_TILE_CAP):
                out_tile = tm * tn * (out_bytes * n_out + 2 * n_extra)
                vmem = 2 * 2 * (tm * tk + tk * tn) + 2 * out_tile + (4 * tm * tn if tk < k else 0)
                if vmem > MATMUL_VMEM_BUDGET:
                    continue
                steps = (m // tm) * (n // tn) * (k // tk)
                a_reads = 1 if tk == k else n // tn
                traffic = 2 * (a_reads * m * k + (m // tm) * k * n) + m * n * (out_bytes * n_out + 2 * n_extra)
                est = (max(2.0 * m * n * k / MXU_FLOPS, traffic / HBM_BYTES_PER_S) + steps * STEP_OVERHEAD_S
                       + (2 * (tm * tk + tk * tn) + out_tile) / HBM_BYTES_PER_S)
                cand = (est, tm, tn, tk)
                if best is None or cand < best:
                    best = cand
    assert best is not None, (m, n, k)
    return best[1:]


def _tile(n, target):
    if n <= target:
        return n
    best = None
    for d in range(LANE, target + 1, LANE):
        if n % d == 0:
            best = d
    assert best is not None, (n, target)
    return best


def _slot(p):
    return 4 * p[0] + 2 * p[1] + p[2]


def _all_gather(xs, name):
    comm = _GatherComm(xs)

    def body(*refs):
        begin, end = comm.ops(refs[:comm.n_in], refs[comm.n_in:comm.n_in + comm.n_out], refs[-3:])
        begin()
        end()

    any_spec = pl.BlockSpec(memory_space=pl.ANY)
    return pl.pallas_call(
        body, name=name, out_shape=comm.out_shape,
        in_specs=[any_spec] * comm.n_in, out_specs=[any_spec] * comm.n_out, scratch_shapes=comm.sems,
    )(*comm.operands)


class _GatherComm:
    def __init__(self, xs, as_columns=None):
        n = len(xs)
        self.operands = list(xs)
        self.n_in = self.n_out = n
        self.aliases = {}
        self.as_columns = list(as_columns) if as_columns else [False] * n
        assert all(x.shape[1] % LANE == 0 for x, f in zip(xs, self.as_columns) if f)
        self.out_shape = [jax.ShapeDtypeStruct((x.shape[0], N_DEV * x.shape[1]) if f else (N_DEV,) + x.shape, x.dtype)
                          for x, f in zip(xs, self.as_columns)]
        self.widths = [x.shape[-1] for x in xs]
        self.sems = [pltpu.SemaphoreType.DMA((7 * n,)), pltpu.SemaphoreType.DMA((7 * n,)),
                     pltpu.SemaphoreType.DMA((n,))]

    def ops(self, x_refs, o_refs, sem_refs):
        n = self.n_in
        send, recv, loc = sem_refs
        x, y, c = lax.axis_index("x"), lax.axis_index("y"), lax.axis_index("c")
        me, sib = (x, y, c), (x, y, 1 - c)
        chips = [(1 - x, y), (x, 1 - y), (1 - x, 1 - y)]

        def place(a, block):
            if self.as_columns[a]:
                w = self.widths[a]
                return o_refs[a].at[:, pl.ds(pl.multiple_of(_slot(block) * w, LANE), w)]
            return o_refs[a].at[_slot(block)]

        def cp(a, k, block, to, src=None):
            dst = place(a, block)
            return pltpu.make_async_remote_copy(
                src_ref=dst if src is None else src, dst_ref=dst,
                send_sem=send.at[7 * a + k], recv_sem=recv.at[7 * a + k],
                device_id=to, device_id_type=MESH)

        mine = [pltpu.make_async_copy(x_refs[a], place(a, me), loc.at[a]) for a in range(n)]
        first = []
        for a in range(n):
            first.append(cp(a, 0, me, sib, src=x_refs[a]))
            first += [cp(a, 1 + j, me, (*chip, c), src=x_refs[a]) for j, chip in enumerate(chips)]

        def begin():
            for m in mine:
                m.start()
            for f in first:
                f.start()

        def end():
            passed = []
            for j, chip in enumerate(chips):
                for a in range(n):
                    cp(a, 1 + j, (*chip, c), me).wait_recv()
                    p = cp(a, 4 + j, (*chip, c), sib)
                    p.start()
                    passed.append(p)
            for a in range(n):
                cp(a, 0, sib, me).wait_recv()
                for j, chip in enumerate(chips):
                    cp(a, 4 + j, (*chip, 1 - c), me).wait_recv()
            for f in first + passed:
                f.wait_send()
            for m in mine:
                m.wait()

        return begin, end


class _ExchangeComm:
    def __init__(self, xs, bufs, layer, row0=0, windows=None):
        n = len(xs)
        self.layers = list(layer) if isinstance(layer, (list, tuple)) else [layer] * n
        self.row0 = row0
        self.windows = list(windows) if windows else [None] * n
        self.nrows = [x.shape[1] if wdw is None else x.shape[0] for x, wdw in zip(xs, self.windows)]
        self.operands = list(xs) + list(bufs)
        self.n_in, self.n_out = 2 * n, n
        self.aliases = {n + a: a for a in range(n)}
        self.out_shape = [jax.ShapeDtypeStruct(b.shape, b.dtype) for b in bufs]
        self.sems = [pltpu.SemaphoreType.DMA((7 * n,)), pltpu.SemaphoreType.DMA((7 * n,)),
                     pltpu.SemaphoreType.DMA((n,))]

    def ops(self, in_refs, o_refs, sem_refs):
        n = self.n_out
        x_refs = in_refs[:n]
        send, recv, loc = sem_refs
        x, y, c = lax.axis_index("x"), lax.axis_index("y"), lax.axis_index("c")
        me = (x, y, c)
        peers = []
        for k in range(1, 8):
            kx, ky, kc = (k >> 2) & 1, (k >> 1) & 1, k & 1
            peers.append((x + kx - 2 * x * kx, y + ky - 2 * y * ky, c + kc - 2 * c * kc))

        def land(a, slot):
            return o_refs[a].at[slot, self.layers[a], pl.ds(self.row0, self.nrows[a])]

        def block(a, slot):
            if self.windows[a] is None:
                return x_refs[a].at[slot]
            width, stride = self.windows[a]
            start = pl.multiple_of((stride * slot) // LANE * LANE, LANE)
            return x_refs[a].at[:, pl.ds(start, width)]

        def cp(a, k, src_slot, dst_slot, to):
            return pltpu.make_async_remote_copy(
                src_ref=block(a, src_slot), dst_ref=land(a, dst_slot),
                send_sem=send.at[7 * a + k], recv_sem=recv.at[7 * a + k],
                device_id=to, device_id_type=MESH)

        mine = [pltpu.make_async_copy(block(a, _slot(me)), land(a, _slot(me)), loc.at[a])
                for a in range(n)]
        sends = [cp(a, k, _slot(p), _slot(me), p) for a in range(n) for k, p in enumerate(peers)]

        def begin():
            for m in mine:
                m.start()
            for s in sends:
                s.start()

        def end():
            for a in range(n):
                for k, p in enumerate(peers):
                    cp(a, k, _slot(me), _slot(p), me).wait_recv()
            for s in sends:
                s.wait_send()
            for m in mine:
                m.wait()

        return begin, end


def _exchange(xs, bufs, layer, name, row0=0, windows=None):
    comm = _ExchangeComm(xs, bufs, layer, row0, windows)

    def body(*refs):
        begin, end = comm.ops(refs[:comm.n_in], refs[comm.n_in:comm.n_in + comm.n_out], refs[-3:])
        begin()
        end()

    any_spec = pl.BlockSpec(memory_space=pl.ANY)
    return pl.pallas_call(
        body, name=name, out_shape=comm.out_shape,
        in_specs=[any_spec] * comm.n_in, out_specs=[any_spec] * comm.n_out, scratch_shapes=comm.sems,
        input_output_aliases=dict(comm.aliases),
    )(*comm.operands)


_DIMS = {"nn": (((1,), (0,)), ((), ())), "nt": (((1,), (1,)), ((), ())), "tn": (((0,), (0,)), ((), ()))}


def _matmul(a, b, mode, out_dtype, name, comm=None, epi=None, extra=None):
    if mode == "nn":
        (m, k), (_, n) = a.shape, b.shape
    elif mode == "nt":
        (m, k), (n, _) = a.shape, b.shape
    else:
        (k, m), (_, n) = a.shape, b.shape
    n_ext = 1 if epi == "relu2_bwd" else 0
    n_main = 2 if epi == "relu2" else 1
    tm, tn, tk = _matmul_tiles(m, n, k, jnp.dtype(out_dtype).itemsize, n_main, n_ext)
    nk = k // tk
    grid = (m // tm, n // tn, nk)
    dn = _DIMS[mode]
    n_cin = comm.n_in if comm else 0
    n_cout = comm.n_out if comm else 0
    n_acc = 0 if nk == 1 else 1
    first_in, first_out = 2 + n_ext, 2 + n_ext + n_cin

    def finish(acc, refs):
        outs = refs[first_out:first_out + n_main]
        if epi == "relu2":
            outs[0][...] = acc.astype(outs[0].dtype)
            r = jnp.maximum(acc, 0.0)
            outs[1][...] = (r * r).astype(outs[1].dtype)
        elif epi == "relu2_bwd":
            outs[0][...] = (acc * (2.0 * jnp.maximum(refs[2][...].astype(F32), 0.0))).astype(outs[0].dtype)
        else:
            outs[0][...] = acc.astype(outs[0].dtype)

    def body(*refs):
        a_ref, b_ref = refs[:2]
        scratch = refs[first_out + n_main + n_cout:]
        pid = [pl.program_id(ax) for ax in range(3)]
        if comm:
            begin, end = comm.ops(refs[first_in:first_in + n_cin],
                                  refs[first_out + n_main:first_out + n_main + n_cout], scratch[n_acc:])
            pl.when((pid[0] == 0) & (pid[1] == 0) & (pid[2] == 0))(begin)
        part = lax.dot_general(a_ref[...], b_ref[...], dn, preferred_element_type=F32)
        if nk == 1:
            finish(part, refs)
        else:
            acc_ref = scratch[0]

            @pl.when(pid[2] == 0)
            def _():
                acc_ref[...] = part

            @pl.when(pid[2] > 0)
            def _():
                acc_ref[...] += part

            @pl.when(pid[2] == nk - 1)
            def _():
                finish(acc_ref[...], refs)
        if comm:
            pl.when((pid[0] == grid[0] - 1) & (pid[1] == grid[1] - 1) & (pid[2] == grid[2] - 1))(end)

    a_spec = pl.BlockSpec((tk, tm), lambda i, j, kk: (kk, i)) if mode == "tn" else \
        pl.BlockSpec((tm, tk), lambda i, j, kk: (i, kk))
    b_spec = pl.BlockSpec((tn, tk), lambda i, j, kk: (j, kk)) if mode == "nt" else \
        pl.BlockSpec((tk, tn), lambda i, j, kk: (kk, j))
    o_spec = pl.BlockSpec((tm, tn), lambda i, j, kk: (i, j))
    any_spec = pl.BlockSpec(memory_space=pl.ANY)
    res = pl.pallas_call(
        body, name=name,
        out_shape=[jax.ShapeDtypeStruct((m, n), out_dtype)] * n_main + (comm.out_shape if comm else []),
        grid=grid,
        in_specs=[a_spec, b_spec] + [o_spec] * n_ext + [any_spec] * n_cin,
        out_specs=[o_spec] * n_main + [any_spec] * n_cout,
        scratch_shapes=([] if nk == 1 else [pltpu.VMEM((tm, tn), F32)]) + (comm.sems if comm else []),
        input_output_aliases={first_in + i: n_main + o for i, o in comm.aliases.items()} if comm else {},
        compiler_params=_params(("arbitrary",) * 3 if comm else ("parallel", "parallel", "arbitrary")),
    )(a, b, *([extra] if n_ext else []), *(comm.operands if comm else []))
    main = res[0] if n_main == 1 else tuple(res[:n_main])
    return (main, list(res[n_main:])) if comm else main


def _dot(a, b, mode):
    return lax.dot_general(a.astype(BF16), b.astype(BF16), _DIMS[mode], preferred_element_type=F32)


@functools.partial(jax.custom_vjp, nondiff_argnums=(2,))
def _bdot(a, b, mode):
    return _dot(a, b, mode)


def _bdot_fwd(a, b, mode):
    return _dot(a, b, mode), (a, b)


def _bdot_bwd(mode, res, g):
    a, b = res
    if mode == "nn":
        return _dot(g, b, "nt"), _dot(a, g, "tn")
    if mode == "nt":
        return _dot(g, b, "nn"), _dot(g, a, "tn")
    return _dot(b, g, "nt"), _dot(a, g, "nn")


_bdot.defvjp(_bdot_fwd, _bdot_bwd)


def _split3(x):
    hi = x.astype(BF16)
    r1 = x - hi.astype(F32)
    mid = r1.astype(BF16)
    return hi, mid, (r1 - mid.astype(F32)).astype(BF16)


def _xdot01(m01, x):
    return sum(lax.dot_general(m01, p, _DIMS["nn"], preferred_element_type=F32) for p in _split3(x))


@jax.custom_vjp
def _spread(v, m01):
    return sum(lax.dot_general(p, m01, _DIMS["nn"], preferred_element_type=F32) for p in _split3(v))


def _spread_fwd(v, m01):
    return _spread(v, m01), m01


def _spread_bwd(m01, g):
    dv = sum(lax.dot_general(p, m01, _DIMS["nt"], preferred_element_type=F32) for p in _split3(g)[:2])
    return dv, jnp.zeros_like(m01)


_spread.defvjp(_spread_fwd, _spread_bwd)


def _chunk_tri(n, lower):
    r = lax.broadcasted_iota(jnp.int32, (n, n), 0)
    c = lax.broadcasted_iota(jnp.int32, (n, n), 1)
    tri = (r >= c) if lower else (r <= c)
    return jnp.where(tri & (r // SSD_Q == c // SSD_Q), 1.0, 0.0).astype(BF16)


def _shift_down(x, prev8, k):
    if k == 0:
        return x
    n = x.shape[0]
    r = pltpu.roll(x, k, 0)
    rp = pltpu.roll(prev8, k, 0)
    rows = lax.broadcasted_iota(jnp.int32, (8, x.shape[1]), 0)
    head = jnp.where(rows < k, rp, r[:8])
    return head if n == 8 else jnp.concatenate([head, r[8:]], axis=0)


def _shift_up(x, next8, k):
    if k == 0:
        return x
    n = x.shape[0]
    r = pltpu.roll(x, n - k, 0)
    rn = pltpu.roll(next8, 8 - k, 0)
    rows = lax.broadcasted_iota(jnp.int32, (8, x.shape[1]), 0)
    tail = jnp.where(rows >= 8 - k, rn, r[n - 8:])
    return tail if n == 8 else jnp.concatenate([r[:n - 8], tail], axis=0)


def _sigmoid(x):
    return 0.5 * jnp.tanh(0.5 * x) + 0.5


def _rows(ref, i):
    return ref[i:i + 1, :]


def _norm_parts(x, mp_ref, which):
    ln, sh, sc = _rows(mp_ref, 6 + which), _rows(mp_ref, 3 * which), _rows(mp_ref, 3 * which + 1)
    r = lax.rsqrt(jnp.mean(x * x, axis=1, keepdims=True) + EPS)
    return r, ln, sh, sc


def _norm_fwd(x, modp, which, name):
    t, d = x.shape

    def body(x_ref, mp_ref, u_ref):
        xv = x_ref[...]
        r, ln, sh, sc = _norm_parts(xv, mp_ref, which)
        u_ref[...] = (((xv * r) * ln) * (1.0 + sc) + sh).astype(BF16)

    row = pl.BlockSpec((ROW_TILE, d), lambda i: (i, 0))
    return pl.pallas_call(
        body, name=name, out_shape=jax.ShapeDtypeStruct((t, d), BF16), grid=(t // ROW_TILE,),
        in_specs=[row, pl.BlockSpec((8, d), lambda i: (0, 0))], out_specs=row,
        compiler_params=_params(("parallel",)))(x, modp)


def _resid_norm_fwd(x, br, mp_gate, gate_row, mp_norm, which, name):
    t, d = x.shape

    def body(x_ref, br_ref, mg_ref, mn_ref, xn_ref, u_ref):
        xv = x_ref[...] + _rows(mg_ref, gate_row) * br_ref[...].astype(F32)
        xn_ref[...] = xv
        r, ln, sh, sc = _norm_parts(xv, mn_ref, which)
        u_ref[...] = (((xv * r) * ln) * (1.0 + sc) + sh).astype(BF16)

    row = pl.BlockSpec((ROW_TILE, d), lambda i: (i, 0))
    mp = pl.BlockSpec((8, d), lambda i: (0, 0))
    return pl.pallas_call(
        body, name=name,
        out_shape=[jax.ShapeDtypeStruct((t, d), F32), jax.ShapeDtypeStruct((t, d), BF16)],
        grid=(t // ROW_TILE,), in_specs=[row, row, mp, mp], out_specs=[row, row],
        compiler_params=_params(("parallel",)))(x, br, mp_gate, mp_norm)


def _final_fwd_bwd(x, br, mp_gate, fnorm8, target, name):
    t, d = x.shape

    def body(x_ref, br_ref, mg_ref, fn_ref, tg_ref, dx_ref, dbr_ref, acc_ref):
        gate = _rows(mg_ref, 5)
        brv = br_ref[...].astype(F32)
        xv = x_ref[...] + gate * brv
        fn = _rows(fn_ref, 0)
        r = lax.rsqrt(jnp.mean(xv * xv, axis=1, keepdims=True) + EPS)
        nrm = xv * r
        err = nrm * fn - tg_ref[...]
        loss = 0.5 * jnp.sum(jnp.mean(err * err, axis=1, keepdims=True), axis=0, keepdims=True)
        dy = err * (1.0 / d)
        dn = dy * fn
        dx = r * (dn - nrm * jnp.mean(dn * nrm, axis=1, keepdims=True))
        dx_ref[...] = dx
        dbr_ref[...] = (dx * gate).astype(BF16)
        upd = jnp.concatenate([
            jnp.broadcast_to(loss, (1, d)),
            jnp.sum(dy * nrm, axis=0, keepdims=True),
            jnp.sum(dx * brv, axis=0, keepdims=True),
            jnp.zeros((5, d), F32)], axis=0)

        @pl.when(pl.program_id(0) == 0)
        def _():
            acc_ref[...] = upd

        @pl.when(pl.program_id(0) > 0)
        def _():
            acc_ref[...] += upd

    row = pl.BlockSpec((ROW_TILE, d), lambda i: (i, 0))
    mp = pl.BlockSpec((8, d), lambda i: (0, 0))
    return pl.pallas_call(
        body, name=name,
        out_shape=[jax.ShapeDtypeStruct((t, d), F32), jax.ShapeDtypeStruct((t, d), BF16),
                   jax.ShapeDtypeStruct((8, d), F32)],
        grid=(t // ROW_TILE,), in_specs=[row, row, mp, mp, row], out_specs=[row, row, mp],
        compiler_params=_params(("arbitrary",)))(x, br, mp_gate, fnorm8, target)


def _resid_norm_bwd(dx, du, x, mp_norm, which, name, br=None, mp_gate=None, gate_row=None):
    t, d = x.shape
    has_gate = br is not None

    def body(*refs):
        if has_gate:
            dx_ref, du_ref, x_ref, mn_ref, br_ref, mg_ref, dxn_ref, dbr_ref, acc_ref = refs
        else:
            dx_ref, du_ref, x_ref, mn_ref, dxn_ref, acc_ref = refs
        xv, duv = x_ref[...], du_ref[...].astype(F32)
        r, ln, _, sc = _norm_parts(xv, mn_ref, which)
        nrm = xv * r
        dn = duv * (ln * (1.0 + sc))
        dxn = dx_ref[...] + r * (dn - nrm * jnp.mean(dn * nrm, axis=1, keepdims=True))
        dxn_ref[...] = dxn
        rows = [jnp.sum(duv, axis=0, keepdims=True), jnp.sum(duv * nrm, axis=0, keepdims=True)]
        if has_gate:
            dbr_ref[...] = (dxn * _rows(mg_ref, gate_row)).astype(BF16)
            rows.append(jnp.sum(dxn * br_ref[...].astype(F32), axis=0, keepdims=True))
        upd = jnp.concatenate(rows + [jnp.zeros((8 - len(rows), d), F32)], axis=0)

        @pl.when(pl.program_id(0) == 0)
        def _():
            acc_ref[...] = upd

        @pl.when(pl.program_id(0) > 0)
        def _():
            acc_ref[...] += upd

    row = pl.BlockSpec((ROW_TILE, d), lambda i: (i, 0))
    mp = pl.BlockSpec((8, d), lambda i: (0, 0))
    ins, in_specs = [dx, du, x, mp_norm], [row, row, row, mp]
    outs = [jax.ShapeDtypeStruct((t, d), F32)]
    out_specs = [row]
    if has_gate:
        ins += [br, mp_gate]
        in_specs += [row, mp]
        outs.append(jax.ShapeDtypeStruct((t, d), BF16))
        out_specs.append(row)
    outs.append(jax.ShapeDtypeStruct((8, d), F32))
    out_specs.append(mp)
    return pl.pallas_call(
        body, name=name, out_shape=outs, grid=(t // ROW_TILE,), in_specs=in_specs, out_specs=out_specs,
        compiler_params=_params(("arbitrary",)))(*ins)


def _merge_fwd(proj, p_conv, p_ssm, name):
    t, d = p_conv.shape

    def body(gl_ref, pc_ref, ps_ref, o_ref):
        g = _sigmoid(gl_ref[...].astype(F32))
        o_ref[...] = (g[:, :d] * pc_ref[...].astype(F32) + g[:, d:] * ps_ref[...].astype(F32)).astype(BF16)

    row = pl.BlockSpec((ROW_TILE, d), lambda i: (i, 0))
    return pl.pallas_call(
        body, name=name, out_shape=jax.ShapeDtypeStruct((t, d), BF16), grid=(t // ROW_TILE,),
        in_specs=[pl.BlockSpec((ROW_TILE, 2 * d), lambda i: (i, 0)), row, row], out_specs=row,
        compiler_params=_params(("parallel",)))(proj, p_conv, p_ssm)


def _merge_bwd(dmerged, proj, p_conv, p_ssm, name):
    t, d = p_conv.shape

    def body(dm_ref, gl_ref, pc_ref, ps_ref, dpc_ref, dps_ref, dgl_ref):
        g = _sigmoid(gl_ref[...].astype(F32))
        gc, gs = g[:, :d], g[:, d:]
        dm = dm_ref[...].astype(F32)
        dpc_ref[...] = (dm * gc).astype(BF16)
        dps_ref[...] = (dm * gs).astype(BF16)
        dgl_ref[...] = jnp.concatenate(
            [dm * pc_ref[...].astype(F32) * gc * (1.0 - gc), dm * ps_ref[...].astype(F32) * gs * (1.0 - gs)], axis=1).astype(BF16)

    row = pl.BlockSpec((ROW_TILE, d), lambda i: (i, 0))
    wide = pl.BlockSpec((ROW_TILE, 2 * d), lambda i: (i, 0))
    return pl.pallas_call(
        body, name=name,
        out_shape=[jax.ShapeDtypeStruct((t, d), BF16), jax.ShapeDtypeStruct((t, d), BF16),
                   jax.ShapeDtypeStruct((t, 2 * d), BF16)],
        grid=(t // ROW_TILE,), in_specs=[row, wide, row, row], out_specs=[row, row, wide],
        compiler_params=_params(("parallel",)))(dmerged, proj, p_conv, p_ssm)


def _halo_specs(t, width, col):
    nb = t // 8
    step = ROW_TILE // 8
    prev = pl.BlockSpec((8, width), lambda i: (jnp.maximum(i * step - 1, 0), col))
    nxt = pl.BlockSpec((8, width), lambda i: (jnp.minimum((i + 1) * step, nb - 1), col))
    return prev, nxt


def _gconv_fwd(proj, conv_w8, d, name):
    t = proj.shape[0]

    def body(cb_ref, cc_ref, cx_ref, ccp_ref, cxp_ref, w_ref, o_ref):
        first = pl.program_id(0) == 0
        v = cc_ref[...].astype(F32) * cx_ref[...].astype(F32)
        vp = jnp.where(first, 0.0, ccp_ref[...].astype(F32) * cxp_ref[...].astype(F32))
        cv = sum(_rows(w_ref, k) * _shift_down(v, vp, 2 - k) for k in range(3))
        o_ref[...] = (cb_ref[...].astype(F32) * cv).astype(BF16)

    def win(col):
        return pl.BlockSpec((ROW_TILE, d), lambda i: (i, col))

    return pl.pallas_call(
        body, name=name, out_shape=jax.ShapeDtypeStruct((t, d), BF16), grid=(t // ROW_TILE,),
        in_specs=[win(2), win(3), win(4), _halo_specs(t, d, 3)[0], _halo_specs(t, d, 4)[0],
                  pl.BlockSpec((8, d), lambda i: (0, 0))],
        out_specs=pl.BlockSpec((ROW_TILE, d), lambda i: (i, 0)),
        compiler_params=_params(("parallel",)))(proj, proj, proj, proj, proj, conv_w8)


def _gconv_bwd(dy, proj, conv_w8, dgl, d, name):
    t = proj.shape[0]
    nt = t // ROW_TILE

    def body(dy_ref, dyn_ref, cb_ref, cc_ref, cx_ref, ccp_ref, cxp_ref, cbn_ref, w_ref, dgl_ref, dp_ref, dw_ref):
        i = pl.program_id(0)
        cb, cc, cx = cb_ref[...].astype(F32), cc_ref[...].astype(F32), cx_ref[...].astype(F32)
        v = cc * cx
        vp = jnp.where(i == 0, 0.0, ccp_ref[...].astype(F32) * cxp_ref[...].astype(F32))
        sh = [_shift_down(v, vp, 2 - k) for k in range(3)]
        cv = sum(_rows(w_ref, k) * sh[k] for k in range(3))
        dyv = dy_ref[...]
        dcv = dyv * cb
        dcvn = jnp.where(i == nt - 1, 0.0, dyn_ref[...] * cbn_ref[...].astype(F32))
        dv = sum(_rows(w_ref, k) * _shift_up(dcv, dcvn, 2 - k) for k in range(3))
        dp_ref[:, :2 * d] = dgl_ref[...]
        dp_ref[:, 2 * d:3 * d] = (dyv * cv).astype(BF16)
        dp_ref[:, 3 * d:4 * d] = (dv * cx).astype(BF16)
        dp_ref[:, 4 * d:] = (dv * cc).astype(BF16)
        upd = jnp.concatenate([jnp.sum(dcv * sh[k], axis=0, keepdims=True) for k in range(3)]
                              + [jnp.zeros((5, d), F32)], axis=0)

        @pl.when(i == 0)
        def _():
            dw_ref[...] = upd

        @pl.when(i > 0)
        def _():
            dw_ref[...] += upd

    def win(col):
        return pl.BlockSpec((ROW_TILE, d), lambda i: (i, col))

    row = pl.BlockSpec((ROW_TILE, d), lambda i: (i, 0))
    w8 = pl.BlockSpec((8, d), lambda i: (0, 0))
    return pl.pallas_call(
        body, name=name,
        out_shape=[jax.ShapeDtypeStruct(proj.shape, BF16), jax.ShapeDtypeStruct((8, d), F32)],
        grid=(nt,),
        in_specs=[row, _halo_specs(t, d, 0)[1], win(2), win(3), win(4), _halo_specs(t, d, 3)[0],
                  _halo_specs(t, d, 4)[0], _halo_specs(t, d, 2)[1], w8,
                  pl.BlockSpec((ROW_TILE, 2 * d), lambda i: (i, 0))],
        out_specs=[pl.BlockSpec((ROW_TILE, 5 * d), lambda i: (i, 0)), w8],
        compiler_params=_params(("arbitrary",)))(
            dy, dy, proj, proj, proj, proj, proj, proj, conv_w8, dgl)


def _dsilu(p):
    s = _sigmoid(p)
    return s * (1.0 + p * (1.0 - s))


def _sconv_specs(t, d, col0, nt):
    nb, step = t // 8, ROW_TILE // 8
    win = pl.BlockSpec((ROW_TILE, d), lambda j, i: (i, col0 + j))
    prev = pl.BlockSpec((8, d), lambda j, i: (jnp.maximum(i * step - 1, 0), col0 + j))
    nxt = pl.BlockSpec((8, d), lambda j, i: (jnp.minimum((i + 1) * step, nb - 1), col0 + j))
    return win, prev, nxt


def _sconv_fwd(proj, w8, b8, d, col0, ncol, name):
    t = proj.shape[0]
    nt = t // ROW_TILE

    def body(x_ref, xp_ref, w_ref, b_ref, o_ref, pre_ref):
        xv = x_ref[...].astype(F32)
        xp = jnp.where(pl.program_id(1) == 0, 0.0, xp_ref[...].astype(F32))
        pre = _rows(b_ref, 0) + sum(_rows(w_ref, k) * _shift_down(xv, xp, 3 - k) for k in range(4))
        o_ref[...] = (pre * _sigmoid(pre)).astype(BF16)
        pre_ref[...] = pre.astype(BF16)

    win, prev, _ = _sconv_specs(t, d, col0, nt)
    par = pl.BlockSpec((8, d), lambda j, i: (0, j))
    out = pl.BlockSpec((ROW_TILE, d), lambda j, i: (i, j))
    return pl.pallas_call(
        body, name=name, out_shape=[jax.ShapeDtypeStruct((t, ncol * d), BF16)] * 2, grid=(ncol, nt),
        in_specs=[win, prev, par, par], out_specs=[out, out],
        compiler_params=_params(("parallel", "parallel")))(proj, proj, w8, b8)


def _sconv_bwd(dxc, pre, pre_col0, proj, w8, dproj, d, col0, ncol, name):
    t = proj.shape[0]
    nt = t // ROW_TILE

    def body(d_ref, dn_ref, p_ref, pn_ref, x_ref, w_ref, _, dp_ref, dw_ref):
        i = pl.program_id(1)
        dpre = d_ref[...] * _dsilu(p_ref[...].astype(F32))
        dpre_n = jnp.where(i == nt - 1, 0.0, dn_ref[...] * _dsilu(pn_ref[...].astype(F32)))
        up = [_shift_up(dpre, dpre_n, 3 - k) for k in range(4)]
        dp_ref[...] = sum(_rows(w_ref, k) * up[k] for k in range(4)).astype(BF16)
        xv = x_ref[...].astype(F32)
        upd = jnp.concatenate([jnp.sum(xv * up[k], axis=0, keepdims=True) for k in range(4)]
                              + [jnp.sum(dpre, axis=0, keepdims=True), jnp.zeros((3, d), F32)], axis=0)

        @pl.when(i == 0)
        def _():
            dw_ref[...] = upd

        @pl.when(i > 0)
        def _():
            dw_ref[...] += upd

    win, _, _ = _sconv_specs(t, d, col0, nt)
    nb, step = t // 8, ROW_TILE // 8
    dwin = pl.BlockSpec((ROW_TILE, d), lambda j, i: (i, j))
    dnxt = pl.BlockSpec((8, d), lambda j, i: (jnp.minimum((i + 1) * step, nb - 1), j))
    par = pl.BlockSpec((8, d), lambda j, i: (0, j))
    return pl.pallas_call(
        body, name=name,
        out_shape=[jax.ShapeDtypeStruct(dproj.shape, BF16), jax.ShapeDtypeStruct((8, ncol * d), F32)],
        grid=(ncol, nt),
        in_specs=[dwin, dnxt,
                  pl.BlockSpec((ROW_TILE, d), lambda j, i: (i, pre_col0 + j)),
                  pl.BlockSpec((8, d), lambda j, i: (jnp.minimum((i + 1) * step, nb - 1), pre_col0 + j)),
                  win, par, pl.BlockSpec(memory_space=pl.ANY)],
        out_specs=[win, par], input_output_aliases={6: 0},
        compiler_params=_params(("arbitrary", "arbitrary")))(dxc, dxc, pre, pre, proj, w8, dproj)


def _softplus(x):
    return jnp.maximum(x, 0.0) + jnp.log(1.0 + jnp.exp(-jnp.abs(x)))


def _dt_fwd(proj, par8, nheads, dt_col, name):
    t = proj.shape[0]

    def body(r_ref, p_ref, dt_ref, ac_ref):
        lane = lax.broadcasted_iota(jnp.int32, (1, LANE), 1)
        valid = lane < nheads
        raw = r_ref[...][:, :LANE].astype(F32)
        dt = jnp.where(valid, _softplus(raw + _rows(p_ref, 0)), 0.0)
        a = jnp.where(valid, -jnp.exp(_rows(p_ref, 1)), 0.0)
        dt_ref[...] = dt
        ac_ref[...] = _xdot01(_chunk_tri(ROW_TILE, True), dt * a)

    out = pl.BlockSpec((ROW_TILE, LANE), lambda i: (i, 0))
    return pl.pallas_call(
        body, name=name,
        out_shape=[jax.ShapeDtypeStruct((t, LANE), F32), jax.ShapeDtypeStruct((t, LANE), F32)],
        grid=(t // ROW_TILE,),
        in_specs=[pl.BlockSpec((ROW_TILE, 2 * LANE), lambda i: (i, dt_col)), pl.BlockSpec((8, LANE), lambda i: (0, 0))],
        out_specs=[out, out], compiler_params=_params(("parallel",)))(proj, par8)


def _dt_bwd(ddt_g, dac_g, dt, proj, par8, dproj, nheads, dt_col, name):
    t = proj.shape[0]
    ng = ddt_g.shape[0]

    def body(dd_ref, da_ref, dt_ref, r_ref, p_ref, _, dp_ref, acc_ref):
        lane = lax.broadcasted_iota(jnp.int32, (1, LANE), 1)
        valid = lane < nheads
        ddt = sum(dd_ref[g] for g in range(ng))
        dac = sum(da_ref[g] for g in range(ng))
        a = jnp.where(valid, -jnp.exp(_rows(p_ref, 1)), 0.0)
        d_a = _xdot01(_chunk_tri(ROW_TILE, False), dac)
        ddt = ddt + d_a * a
        raw = r_ref[...][:, :LANE].astype(F32)
        draw = jnp.where(valid, ddt * _sigmoid(raw + _rows(p_ref, 0)), 0.0)
        dp_ref[...] = jnp.concatenate([draw, jnp.zeros_like(draw)], axis=1).astype(BF16)
        upd = jnp.concatenate([jnp.sum(draw, axis=0, keepdims=True),
                               jnp.sum(d_a * dt_ref[...], axis=0, keepdims=True) * a,
                               jnp.zeros((6, LANE), F32)], axis=0)

        @pl.when(pl.program_id(0) == 0)
        def _():
            acc_ref[...] = upd

        @pl.when(pl.program_id(0) > 0)
        def _():
            acc_ref[...] += upd

    row = pl.BlockSpec((ROW_TILE, LANE), lambda i: (i, 0))
    grp = pl.BlockSpec((ng, ROW_TILE, LANE), lambda i: (0, i, 0))
    win = pl.BlockSpec((ROW_TILE, 2 * LANE), lambda i: (i, dt_col))
    par = pl.BlockSpec((8, LANE), lambda i: (0, 0))
    return pl.pallas_call(
        body, name=name,
        out_shape=[jax.ShapeDtypeStruct(dproj.shape, BF16), jax.ShapeDtypeStruct((8, LANE), F32)],
        grid=(t // ROW_TILE,),
        in_specs=[grp, grp, row, win, par, pl.BlockSpec(memory_space=pl.ANY)],
        out_specs=[win, par], input_output_aliases={5: 0},
        compiler_params=_params(("arbitrary",)))(ddt_g, dac_g, dt, proj, par8, dproj)


def _ssd_chunk(xs, bm, cm, z, dt, ac, st, dsk, nw, g):
    q = xs.shape[0]
    lane_h = lax.broadcasted_iota(jnp.int32, (1, LANE), 1)
    sub_h = lax.broadcasted_iota(jnp.int32, (LANE, 1), 0)
    lane_c = lax.broadcasted_iota(jnp.int32, (1, GROUP_W), 1) // HEAD_DIM
    rr = lax.broadcasted_iota(jnp.int32, (q, q), 0)
    cc = lax.broadcasted_iota(jnp.int32, (q, q), 1)
    tril = rr >= cc
    last_row = jnp.where(lax.broadcasted_iota(jnp.int32, (q, 1), 0) == q - 1, 1.0, 0.0)
    act = ac.T
    col, row = [], []
    for r in range(HEADS_PER_GROUP):
        h = g * HEADS_PER_GROUP + r
        col.append(jnp.sum(ac * jnp.where(lane_h == h, 1.0, 0.0), axis=1, keepdims=True))
        row.append(jnp.sum(act * jnp.where(sub_h == h, 1.0, 0.0), axis=0, keepdims=True))

    def per_head(vals):
        out = jnp.broadcast_to(vals[0], (q, GROUP_W))
        for r in range(1, HEADS_PER_GROUP):
            out = jnp.where(lane_c == r, vals[r], out)
        return out

    spread = jnp.where(lax.broadcasted_iota(jnp.int32, (LANE, GROUP_W), 0)
                       == g * HEADS_PER_GROUP + lax.broadcasted_iota(jnp.int32, (LANE, GROUP_W), 1) // HEAD_DIM,
                       1.0, 0.0).astype(BF16)
    acx = _spread(ac, spread)
    x = xs * _spread(dt, spread)
    s = _bdot(cm, bm, "nt")
    y = per_head([_bdot(s * jnp.exp(jnp.where(tril, col[r] - row[r], -jnp.inf)), x, "nn")
                  for r in range(HEADS_PER_GROUP)])
    y = y + jnp.exp(acx) * _bdot(cm, st, "nn")
    last = jnp.sum(acx * last_row, axis=0, keepdims=True)
    new_st = st * jnp.exp(last) + _bdot(bm, x * jnp.exp(last - acx), "tn")
    y = y + dsk * xs
    y = y * (z * _sigmoid(z))
    yn = y * lax.rsqrt(jnp.mean(y * y, axis=1, keepdims=True) + EPS) * nw
    return yn, new_st


def _ssd_specs(nc, z_col, ngroups, rev):
    assert ngroups % SSD_GROUPS == 0 and z_col % SSD_GROUPS == 0
    z_col //= SSD_GROUPS
    b_col, c_col = 2 * ngroups // SSD_GROUPS, 3 * ngroups // SSD_GROUPS

    def ci(c):
        return nc - 1 - c if rev else c

    nx, nb = SSD_GROUPS * GROUP_W, SSD_GROUPS * N_STATE
    xs = pl.BlockSpec((SSD_Q, nx), lambda g, c: (ci(c), g))
    bm = pl.BlockSpec((SSD_Q, nb), lambda g, c: (ci(c), b_col + g))
    cm = pl.BlockSpec((SSD_Q, nb), lambda g, c: (ci(c), c_col + g))
    z = pl.BlockSpec((SSD_Q, nx), lambda g, c: (ci(c), z_col + g))
    hd = pl.BlockSpec((SSD_Q, LANE), lambda g, c: (ci(c), 0))
    par = pl.BlockSpec((8, nx), lambda g, c: (0, g))
    stb = pl.BlockSpec((None, N_STATE, nx), lambda g, c: (ci(c), 0, g))
    db = pl.BlockSpec((SSD_Q, nb), lambda g, c: (ci(c), g))
    return xs, bm, cm, z, hd, par, stb, db


def _ssd_fwd(xc, proj, dt, ac, par8, ngroups, z_col, name):
    t = xc.shape[0]
    nc = t // SSD_Q
    steps = ngroups // SSD_GROUPS

    def body(x_ref, b_ref, c_ref, z_ref, dt_ref, ac_ref, p_ref, y_ref, so_ref, st_ref):
        @pl.when(pl.program_id(1) == 0)
        def _():
            st_ref[...] = jnp.zeros_like(st_ref)

        so_ref[...] = st_ref[...]
        dtv, acv = dt_ref[...], ac_ref[...]
        for i in range(SSD_GROUPS):
            cx, cb = slice(i * GROUP_W, (i + 1) * GROUP_W), slice(i * N_STATE, (i + 1) * N_STATE)
            yn, new_st = _ssd_chunk(x_ref[:, cx].astype(F32), b_ref[:, cb].astype(F32), c_ref[:, cb].astype(F32),
                                    z_ref[:, cx].astype(F32), dtv, acv, st_ref[:, cx],
                                    p_ref[0:1, cx], p_ref[1:2, cx], pl.program_id(0) * SSD_GROUPS + i)
            y_ref[:, cx] = yn.astype(BF16)
            st_ref[:, cx] = new_st

    xs, bm, cm, zs, hd, par, stb, _ = _ssd_specs(nc, z_col, ngroups, False)
    return pl.pallas_call(
        body, name=name,
        out_shape=[jax.ShapeDtypeStruct((t, ngroups * GROUP_W), BF16),
                   jax.ShapeDtypeStruct((nc, N_STATE, ngroups * GROUP_W), F32)],
        grid=(steps, nc), in_specs=[xs, bm, cm, zs, hd, hd, par], out_specs=[xs, stb],
        scratch_shapes=[pltpu.VMEM((N_STATE, SSD_GROUPS * GROUP_W), F32)],
        compiler_params=_params(("parallel", "arbitrary")))(xc, xc, xc, proj, dt, ac, par8)


def _ssd_bwd(dyn, xc, proj, dt, ac, par8, states, dproj, ngroups, z_col, name, comm=None):
    t = xc.shape[0]
    nc = t // SSD_Q
    steps = ngroups // SSD_GROUPS

    n_cin = comm.n_in if comm else 0
    n_cout = comm.n_out if comm else 0

    def body(*refs):
        dy_ref, x_ref, b_ref, c_ref, z_ref, dt_ref, ac_ref, p_ref, s_ref = refs[:9]
        dz_ref, dx_ref, db_ref, dc_ref, ddt_ref, dac_ref, dp_ref = refs[10 + n_cin:17 + n_cin]
        ds_ref = refs[17 + n_cin + n_cout]
        c = pl.program_id(1)
        if comm:
            begin, end = comm.ops(refs[10:10 + n_cin], refs[17 + n_cin:17 + n_cin + n_cout], refs[-3:])
            pl.when((pl.program_id(0) == 0) & (c == 0))(begin)

        @pl.when(c == 0)
        def _():
            ds_ref[...] = jnp.zeros_like(ds_ref)

        dtv, acv = dt_ref[...], ac_ref[...]
        ddt_sum, dac_sum, upds = None, None, []
        for i in range(SSD_GROUPS):
            cx, cb = slice(i * GROUP_W, (i + 1) * GROUP_W), slice(i * N_STATE, (i + 1) * N_STATE)
            g = pl.program_id(0) * SSD_GROUPS + i
            args = (x_ref[:, cx].astype(F32), b_ref[:, cb].astype(F32), c_ref[:, cb].astype(F32),
                    z_ref[:, cx].astype(F32), dtv, acv, s_ref[:, cx], p_ref[0:1, cx], p_ref[1:2, cx])
            _, vjp = jax.vjp(lambda *a, g=g: _ssd_chunk(*a, g), *args)
            dxs, dbm, dcm, dz, ddt, dac, dst, ddsk, dnw = vjp((dy_ref[:, cx], ds_ref[:, cx]))
            ds_ref[:, cx] = dst
            dz_ref[:, cx] = dz.astype(BF16)
            dx_ref[:, cx] = dxs
            db_ref[:, cb] = dbm
            dc_ref[:, cb] = dcm
            ddt_sum = ddt if ddt_sum is None else ddt_sum + ddt
            dac_sum = dac if dac_sum is None else dac_sum + dac
            upds.append(jnp.concatenate([ddsk, dnw, jnp.zeros((6, GROUP_W), F32)], axis=0))
        ddt_ref[...] = ddt_sum
        dac_ref[...] = dac_sum
        upd = jnp.concatenate(upds, axis=1)

        @pl.when(c == 0)
        def _():
            dp_ref[...] = upd

        @pl.when(c > 0)
        def _():
            dp_ref[...] += upd

        if comm:
            pl.when((pl.program_id(0) == steps - 1) & (c == nc - 1))(end)

    xs, bm, cm, zs, hd, par, stb, db = _ssd_specs(nc, z_col, ngroups, True)
    hg = pl.BlockSpec((None, SSD_Q, LANE), lambda g, c: (g, nc - 1 - c, 0))
    any_spec = pl.BlockSpec(memory_space=pl.ANY)
    aliases = {9: 0}
    if comm:
        aliases.update({10 + i: 7 + o for i, o in comm.aliases.items()})
    res = pl.pallas_call(
        body, name=name,
        out_shape=[jax.ShapeDtypeStruct(dproj.shape, BF16),
                   jax.ShapeDtypeStruct((t, ngroups * GROUP_W), F32),
                   jax.ShapeDtypeStruct((t, ngroups * N_STATE), F32), jax.ShapeDtypeStruct((t, ngroups * N_STATE), F32),
                   jax.ShapeDtypeStruct((steps, t, LANE), F32), jax.ShapeDtypeStruct((steps, t, LANE), F32),
                   jax.ShapeDtypeStruct((8, ngroups * GROUP_W), F32)] + (comm.out_shape if comm else []),
        grid=(steps, nc),
        in_specs=[xs, xs, bm, cm, zs, hd, hd, par, stb, any_spec] + [any_spec] * n_cin,
        out_specs=[zs, xs, db, db, hg, hg, par] + [any_spec] * n_cout, input_output_aliases=aliases,
        scratch_shapes=[pltpu.VMEM((N_STATE, SSD_GROUPS * GROUP_W), F32)] + (comm.sems if comm else []),
        compiler_params=_params(("arbitrary", "arbitrary")))(
            dyn, xc, xc, xc, proj, dt, ac, par8, states, dproj, *(comm.operands if comm else []))
    return tuple(res[:7]), list(res[7:])


def _small_matmul(a, b, mode, name):
    if mode == "nn":
        shape = (a.shape[0], b.shape[1])
    else:
        shape = (a.shape[1], b.shape[1])

    def body(a_ref, b_ref, o_ref):
        o_ref[...] = _dot(a_ref[...], b_ref[...], mode)

    return pl.pallas_call(body, name=name, out_shape=jax.ShapeDtypeStruct(shape, F32),
                          compiler_params=_params())(a, b)


def _adamw(parts, w, m, v, name, shard_stride=None):
    npart, rows, pcols = parts.shape
    cols = w.shape[1]
    tr = rows if rows <= 128 else _tile8(rows, 128)

    def body(p_ref, w_ref, m_ref, v_ref, g_ref, d_ref, nm_ref, nv_ref):
        g = p_ref[0].astype(F32)
        for j in range(1, npart):
            g = g + p_ref[j].astype(F32)
        if shard_stride is not None:
            me = 4 * lax.axis_index("x") + 2 * lax.axis_index("y") + lax.axis_index("c")
            off = (shard_stride * me) % LANE
            g = pltpu.roll(g, (pcols - off) % pcols, 1)[:, :cols]
        mm = ADAM_B1 * m_ref[...] + (1.0 - ADAM_B1) * g
        vv = ADAM_B2 * v_ref[...] + (1.0 - ADAM_B2) * (g * g)
        m_hat = mm / (1.0 - ADAM_B1 ** ADAM_STEP)
        v_hat = vv / (1.0 - ADAM_B2 ** ADAM_STEP)
        g_ref[...] = g
        d_ref[...] = -ADAM_LR * (m_hat / (jnp.sqrt(v_hat) + ADAM_EPS) + ADAM_WD * w_ref[...])
        nm_ref[...] = mm
        nv_ref[...] = vv

    blk = pl.BlockSpec((tr, cols), lambda i: (i, 0))
    out = jax.ShapeDtypeStruct((rows, cols), F32)
    return pl.pallas_call(
        body, name=name, out_shape=[out] * 4, grid=(rows // tr,),
        in_specs=[pl.BlockSpec((npart, tr, pcols), lambda i: (0, i, 0)), blk, blk, blk], out_specs=[blk] * 4,
        compiler_params=_params(("parallel",)))(parts, w, m, v)


def _tile8(n, target):
    best = 8
    for d in range(8, target + 1, 8):
        if n % d == 0:
            best = d
    return best


def _pad_rows8(a):
    return jnp.concatenate([a, jnp.zeros((8 - a.shape[0], a.shape[1]), a.dtype)], axis=0)


def _pad_lanes(a, n):
    return jnp.concatenate([a, jnp.zeros(a.shape[:-1] + (n - a.shape[-1],), a.dtype)], axis=-1)


def _pack(vecs, mult):
    flat = jnp.concatenate([v.reshape(-1) for v in vecs])
    pad = (-flat.shape[0]) % mult
    flat = jnp.concatenate([flat, jnp.zeros((pad,), flat.dtype)])
    return flat.reshape(-1, LANE)


def _unpack(flat, shapes):
    out, o = [], 0
    for s in shapes:
        n = int(np.prod(s))
        out.append(flat[o:o + n].reshape(s))
        o += n
    return out


def kernel(x, c, w_ada, b_ada, ln1, ln2, w_in, conv_w, ssm_conv_w, ssm_conv_b, dt_bias, a_log, d_skip, ssm_norm_w, w_conv_out, w_ssm_out, w_o, w_up, w_down, final_norm, loss_target, m_w_ada, m_b_ada, m_ln1, m_ln2, m_w_in, m_conv_w, m_ssm_conv_w, m_ssm_conv_b, m_dt_bias, m_a_log, m_d_skip, m_ssm_norm_w, m_w_conv_out, m_w_ssm_out, m_w_o, m_w_up, m_w_down, m_final_norm, v_w_ada, v_b_ada, v_ln1, v_ln2, v_w_in, v_conv_w, v_ssm_conv_w, v_ssm_conv_b, v_dt_bias, v_a_log, v_d_skip, v_ssm_norm_w, v_w_conv_out, v_w_ssm_out, v_w_o, v_w_up, v_w_down, v_final_norm):
    names = ["w_ada", "b_ada", "ln1", "ln2", "w_in", "conv_w", "ssm_conv_w", "ssm_conv_b", "dt_bias", "a_log",
             "d_skip", "ssm_norm_w", "w_conv_out", "w_ssm_out", "w_o", "w_up", "w_down", "final_norm"]
    w_of = dict(zip(names, [w_ada, b_ada, ln1, ln2, w_in, conv_w, ssm_conv_w, ssm_conv_b, dt_bias, a_log, d_skip,
                            ssm_norm_w, w_conv_out, w_ssm_out, w_o, w_up, w_down, final_norm]))
    m_of = dict(zip(names, [m_w_ada, m_b_ada, m_ln1, m_ln2, m_w_in, m_conv_w, m_ssm_conv_w, m_ssm_conv_b, m_dt_bias,
                            m_a_log, m_d_skip, m_ssm_norm_w, m_w_conv_out, m_w_ssm_out, m_w_o, m_w_up, m_w_down,
                            m_final_norm]))
    v_of = dict(zip(names, [v_w_ada, v_b_ada, v_ln1, v_ln2, v_w_in, v_conv_w, v_ssm_conv_w, v_ssm_conv_b, v_dt_bias,
                            v_a_log, v_d_skip, v_ssm_norm_w, v_w_conv_out, v_w_ssm_out, v_w_o, v_w_up, v_w_down,
                            v_final_norm]))

    _, t, d = x.shape
    nl = w_ada.shape[0]
    ada_w = w_ada.shape[2]
    ds = ssm_norm_w.shape[1]
    nh = dt_bias.shape[1]
    ng = nh // HEADS_PER_GROUP
    gn = ng * N_STATE
    xbc_w = ds + 2 * gn
    z_off, xbc_off = 5 * d, 5 * d + ds
    dt_off = xbc_off + xbc_w
    proj_w = dt_off + nh
    pw = dt_off + 2 * LANE
    me = 4 * lax.axis_index("x") + 2 * lax.axis_index("y") + lax.axis_index("c")

    x2, tgt = x[0], loss_target[0]

    c_act = c * jax.nn.sigmoid(c)
    sizes1 = [(1, d), conv_w.shape, ssm_conv_w.shape]
    (g1,) = _all_gather([_pack([c_act, conv_w, ssm_conv_w], LANE)], "ag_small_in")
    parts1 = [_unpack(g1[j].reshape(-1), sizes1) for j in range(N_DEV)]
    c_act_all = jnp.concatenate([p[0] for p in parts1], axis=0)
    conv_w_full = jnp.concatenate([p[1] for p in parts1], axis=-1)
    sconv_w_full = jnp.concatenate([p[2] for p in parts1], axis=-1)

    mod_part = jnp.stack([_small_matmul(c_act_all, w_ada[l], "nn", f"ada_fwd{l}") for l in range(nl)])
    (gmod,) = _all_gather([mod_part], "ag_mod")
    mod = lax.dynamic_index_in_dim(gmod, me, axis=2, keepdims=False)
    mod = jnp.moveaxis(mod, 0, 1).reshape(nl, N_DEV * ada_w) + b_ada
    mod = mod.reshape(nl, 6, d)
    modp = [jnp.concatenate([mod[l], ln1[l][None], ln2[l][None]], axis=0) for l in range(nl)]

    big = ["w_in", "w_up", "w_conv_out", "w_ssm_out", "w_o", "w_down"]
    rest = big[1:]

    def blocks_of(l, keys):
        return [w_of[k][l].astype(BF16) for k in keys]

    def w_in_halves(l):
        w = w_of["w_in"][l].astype(BF16)
        return [w[:d // 2]], [w[d // 2:]]

    def full_cols(g):
        return jnp.moveaxis(g, 0, 1).reshape(g.shape[1], N_DEV * g.shape[2])

    def full_rows(g):
        return g.reshape(N_DEV * g.shape[1], g.shape[2])

    def pad_w_in(pieces):
        w = jnp.concatenate([full_cols(g) for g in pieces], axis=0)
        return jnp.concatenate([w, jnp.zeros((d, pw - proj_w), BF16)], axis=-1)

    wi = [None] * nl
    wi[0] = pad_w_in(_all_gather(blocks_of(0, ["w_in"]), "ag_w_in0"))

    cw8 = [_pad_rows8(conv_w_full[l]) for l in range(nl)]
    sw8 = [_pad_rows8(sconv_w_full[l]) for l in range(nl)]
    sb8 = [_pad_rows8(ssm_conv_b[l][None]) for l in range(nl)]
    dtp8 = [_pad_rows8(_pad_lanes(jnp.stack([dt_bias[l], a_log[l]]), LANE)) for l in range(nl)]
    sp8 = [_pad_rows8(jnp.stack([jnp.repeat(d_skip[l], HEAD_DIM), ssm_norm_w[l]])) for l in range(nl)]
    fn8 = _pad_rows8(final_norm[None])
    xbc_col, ncol, dt_col, z_col = xbc_off // d, xbc_w // d, dt_off // (2 * LANE), z_off // GROUP_W

    saved = []
    x_cur, br_prev = x2, None
    for l in range(nl):
        if l == 0:
            x_in, u1 = x_cur, _norm_fwd(x_cur, modp[0], 0, "norm_first")
        else:
            x_in, u1 = _resid_norm_fwd(x_cur, br_prev, modp[l - 1], 5, modp[l], 0, f"resid_norm_a{l}")
        proj, got = _matmul(u1, wi[l], "nn", BF16, f"mm_in{l}",
                            comm=_GatherComm(blocks_of(l, rest), as_columns=[k == "w_up" for k in rest]))
        got = dict(zip(rest, got))
        wup, wdown = got["w_up"], full_rows(got["w_down"])
        wco, wso, wo = full_rows(got["w_conv_out"]), full_rows(got["w_ssm_out"]), full_rows(got["w_o"])
        y_conv = _gconv_fwd(proj, cw8[l], d, f"gconv_fwd{l}")
        xc, pre = _sconv_fwd(proj, sw8[l], sb8[l], d, xbc_col, ncol, f"sconv_fwd{l}")
        dt, ac = _dt_fwd(proj, dtp8[l], nh, dt_col, f"dt_fwd{l}")
        yn, states = _ssd_fwd(xc, proj, dt, ac, sp8[l], ng, z_col, f"ssd_fwd{l}")
        p_conv = _matmul(y_conv, wco, "nn", BF16, f"mm_conv_out{l}")
        p_ssm = _matmul(yn, wso, "nn", BF16, f"mm_ssm_out{l}")
        merged = _merge_fwd(proj, p_conv, p_ssm, f"merge_fwd{l}")
        mix = _matmul(merged, wo, "nn", BF16, f"mm_o{l}")
        x_mid, u2 = _resid_norm_fwd(x_in, mix, modp[l], 2, modp[l], 1, f"resid_norm_b{l}")
        if l + 1 < nl:
            top, bot = w_in_halves(l + 1)
            (h, hid), g_top = _matmul(u2, wup, "nn", BF16, f"mm_up{l}", epi="relu2", comm=_GatherComm(top))
            mlp, g_bot = _matmul(hid, wdown, "nn", BF16, f"mm_down{l}", comm=_GatherComm(bot))
            wi[l + 1] = pad_w_in(g_top + g_bot)
        else:
            h, hid = _matmul(u2, wup, "nn", BF16, f"mm_up{l}", epi="relu2")
            mlp = _matmul(hid, wdown, "nn", BF16, f"mm_down{l}")
        saved.append(dict(x_in=x_in, u1=u1, proj=proj, y_conv=y_conv, xc=xc, pre=pre, dt=dt, ac=ac, yn=yn,
                          states=states,
                          p_conv=p_conv, p_ssm=p_ssm, merged=merged, mix=mix, x_mid=x_mid, u2=u2, h=h, hid=hid,
                          mlp=mlp, wup=wup, wdown=wdown, wco=wco, wso=wso, wo=wo))
        x_cur, br_prev = x_mid, mlp

    dx, dbr, acc = _final_fwd_bwd(x_cur, br_prev, modp[nl - 1], fn8, tgt, "final")
    loss = lax.psum(acc[0, 0], ("x", "y", "c"))
    g_final_norm = acc[1]
    dgate2 = acc[2]

    gw = {k: [None] * nl for k in big}
    small = {k: [None] * nl for k in ["mod", "ln1", "ln2", "conv_w", "ssm_conv_w", "ssm_conv_b", "dt_bias",
                                      "a_log", "d_skip", "ssm_norm_w"]}

    def grad_rows(a):
        return a.reshape(N_DEV, a.shape[0] // N_DEV, a.shape[1])

    def recv_buf(name):
        return lax.empty((N_DEV, nl) + w_of[name].shape[1:], BF16)

    in_shard, up_shard = w_in.shape[2], w_up.shape[2]
    in_win = -(-(in_shard + max((in_shard * j) % LANE for j in range(N_DEV))) // LANE) * LANE
    assert up_shard % LANE == 0 and (in_shard * (N_DEV - 1)) // LANE * LANE + in_win <= pw
    win_a, win_b = [(in_win, in_shard)], [(up_shard, up_shard)] + [None] * 4
    bufs_a = [lax.empty((N_DEV, nl, d, in_win), BF16)]
    bufs_b = [recv_buf(k) for k in big[1:]]
    pending = None
    for l in reversed(range(nl)):
        s = saved[l]
        dh = _matmul(dbr, s["wdown"], "nt", BF16, f"mm_down_dx{l}", epi="relu2_bwd", extra=s["h"])
        gw["w_down"][l] = _matmul(s["hid"], dbr, "tn", BF16, f"mm_down_dw{l}")
        du2 = _matmul(dh, s["wup"], "nt", BF16, f"mm_up_dx{l}")
        gw["w_up"][l] = _matmul(s["u2"], dh, "tn", BF16, f"mm_up_dw{l}")
        dx_mid, dmix, acc2 = _resid_norm_bwd(dx, du2, s["x_mid"], modp[l], 1, f"resid_norm_b_bwd{l}",
                                             br=s["mix"], mp_gate=modp[l], gate_row=2)
        dmerged = _matmul(dmix, s["wo"], "nt", BF16, f"mm_o_dx{l}")
        gw["w_o"][l] = _matmul(s["merged"], dmix, "tn", BF16, f"mm_o_dw{l}")
        dpc, dps, dgl = _merge_bwd(dmerged, s["proj"], s["p_conv"], s["p_ssm"], f"merge_bwd{l}")
        dyc = _matmul(dpc, s["wco"], "nt", F32, f"mm_conv_out_dx{l}")
        dproj, dcw = _gconv_bwd(dyc, s["proj"], cw8[l], dgl, d, f"gconv_bwd{l}")
        gw["w_conv_out"][l] = _matmul(s["y_conv"], dpc, "tn", BF16, f"mm_conv_out_dw{l}")
        dyn = _matmul(dps, s["wso"], "nt", F32, f"mm_ssm_out_dx{l}")
        gw["w_ssm_out"][l] = _matmul(s["yn"], dps, "tn", BF16, f"mm_ssm_out_dw{l}")
        rest_grads = [gw["w_up"][l]] + [grad_rows(gw[k][l]) for k in big[2:]]
        sends, bufs, layers, wins = rest_grads, list(bufs_b), [l] * len(rest_grads), win_b
        if pending is not None:
            sends, bufs, layers, wins = sends + pending, bufs + list(bufs_a), layers + [l + 1], win_b + win_a
        (dproj, dxs, dbm, dcm, ddt_g, dac_g, dsp), bufs = _ssd_bwd(
            dyn, s["xc"], s["proj"], s["dt"], s["ac"], sp8[l], s["states"], dproj, ng, z_col, f"ssd_bwd{l}",
            comm=_ExchangeComm(sends, bufs, layers, windows=wins))
        bufs_b = bufs[:len(rest_grads)]
        if pending is not None:
            bufs_a = bufs[len(rest_grads):]
        dsw = []
        for seg, (dseg, c0) in enumerate([(dxs, 0), (dbm, ds // d), (dcm, ds // d + gn // d)]):
            cols = slice(c0 * d, c0 * d + dseg.shape[1])
            dproj, part = _sconv_bwd(dseg, s["pre"], c0, s["proj"], sw8[l][:, cols], dproj, d, xbc_col + c0,
                                     dseg.shape[1] // d, f"sconv_bwd{l}_{seg}")
            dsw.append(part)
        dsw = jnp.concatenate(dsw, axis=1)
        dproj, ddtp = _dt_bwd(ddt_g, dac_g, s["dt"], s["proj"], dtp8[l], dproj, nh, dt_col, f"dt_bwd{l}")
        if l > 0:
            du1 = _matmul(dproj, wi[l], "nt", BF16, f"mm_in_dx{l}")
            pending = [_matmul(s["u1"], dproj, "tn", BF16, f"mm_in_dw{l}")]
        else:
            g_top = _matmul(s["u1"][:, :d // 2], dproj, "tn", BF16, "mm_in_dw0_top")
            g_bot, bufs_a = _matmul(s["u1"][:, d // 2:], dproj, "tn", BF16, "mm_in_dw0_bot",
                                    comm=_ExchangeComm([g_top], bufs_a, 0, row0=0, windows=win_a))
            du1, bufs_a = _matmul(dproj, wi[l], "nt", BF16, f"mm_in_dx{l}",
                                  comm=_ExchangeComm([g_bot], bufs_a, 0, row0=d // 2, windows=win_a))
        if l > 0:
            dx, dbr, acc1 = _resid_norm_bwd(dx_mid, du1, s["x_in"], modp[l], 0, f"resid_norm_a_bwd{l}",
                                            br=saved[l - 1]["mlp"], mp_gate=modp[l - 1], gate_row=5)
        else:
            dx, acc1 = _resid_norm_bwd(dx_mid, du1, s["x_in"], modp[0], 0, "norm_first_bwd")
        sc1, sc2 = mod[l, 1], mod[l, 4]
        small["mod"][l] = jnp.stack([acc1[0], acc1[1] * ln1[l], acc2[2], acc2[0], acc2[1] * ln2[l], dgate2])
        small["ln1"][l] = acc1[1] * (1.0 + sc1)
        small["ln2"][l] = acc2[1] * (1.0 + sc2)
        small["conv_w"][l] = dcw[:3]
        small["ssm_conv_w"][l] = dsw[:4]
        small["ssm_conv_b"][l] = dsw[4]
        small["dt_bias"][l] = ddtp[0, :nh]
        small["a_log"][l] = ddtp[1, :nh]
        small["d_skip"][l] = dsp[0].reshape(nh, HEAD_DIM).sum(axis=-1)
        small["ssm_norm_w"][l] = dsp[1]
        if l > 0:
            dgate2 = acc1[2]
    grad_x = dx[None]

    sm = {k: jnp.stack(v) for k, v in small.items()}
    rep_names = ["b_ada", "ln1", "ln2", "ssm_conv_b", "dt_bias", "a_log", "d_skip", "ssm_norm_w", "final_norm"]
    rep_grads = [sm["mod"].reshape(nl, 6 * d), sm["ln1"], sm["ln2"], sm["ssm_conv_b"], sm["dt_bias"], sm["a_log"],
                 sm["d_skip"], sm["ssm_norm_w"], g_final_norm]
    rep_pack = _pack(rep_grads, 8 * LANE)
    conv_pack = _pack([sm["conv_w"], sm["ssm_conv_w"]], 8 * LANE)
    g_rep, g_conv = _all_gather([rep_pack, conv_pack], "ag_small_grads")

    outs = {}

    def run_adamw(name, parts, tag, shard_stride=None):
        w, m, v = w_of[name], m_of[name], v_of[name]
        shp = w.shape
        r2 = (int(np.prod(shp[:-1])), shp[-1])
        res = _adamw(parts.reshape(parts.shape[0], r2[0], parts.shape[-1]), w.reshape(r2), m.reshape(r2),
                     v.reshape(r2), tag, shard_stride)
        outs[name] = [a.reshape(shp) for a in res]

    rep_shapes = [w_of[k].shape for k in rep_names]
    res = _adamw(g_rep, _pack([w_of[k] for k in rep_names], 8 * LANE), _pack([m_of[k] for k in rep_names], 8 * LANE),
                 _pack([v_of[k] for k in rep_names], 8 * LANE), "adamw_replicated")
    for k, vals in zip(rep_names, zip(*[_unpack(a.reshape(-1), rep_shapes) for a in res])):
        outs[k] = list(vals)

    conv_parts = [_unpack(g_conv[j].reshape(-1), [sm["conv_w"].shape, sm["ssm_conv_w"].shape]) for j in range(N_DEV)]
    for idx, name in enumerate(["conv_w", "ssm_conv_w"]):
        wsh = w_of[name].shape[-1]
        full = jnp.stack([p[idx] for p in conv_parts])
        run_adamw(name, lax.dynamic_slice_in_dim(full, me * wsh, wsh, axis=3), "adamw_" + name)

    dmod_all = g_rep.reshape(N_DEV, -1)[:, :nl * 6 * d].reshape(N_DEV, nl, 6 * d)
    dmod_mine = lax.dynamic_slice_in_dim(dmod_all, me * ada_w, ada_w, axis=2)
    g_ada = jnp.stack([_small_matmul(c_act_all, dmod_mine[:, l], "tn", f"ada_bwd{l}") for l in range(nl)])
    run_adamw("w_ada", g_ada[None], "adamw_w_ada")

    run_adamw("w_in", bufs_a[0], "adamw_w_in", shard_stride=in_shard)
    for name, parts in zip(big[1:], bufs_b):
        run_adamw(name, parts, "adamw_" + name)

    result = [loss, grad_x]
    for i in range(4):
        result += [outs[k][i] for k in names]
    return tuple(result)
```
